```python
import jax, jax.numpy as jnp
from jax import lax
import numpy as np

D_MODEL = 2048
BATCH = 8
SEQ = 4096
DEPTH = 1

N_ATTN_HEADS = 8
HEAD_DIM = 128
ATTN_WIDTH = N_ATTN_HEADS * HEAD_DIM
POOL_WINDOWS = (2, 4, 8, 16)
N_POOL_GROUPS = len(POOL_WINDOWS)
POOL_GROUP_WIDTH = 256
POOL_WIDTH = N_POOL_GROUPS * POOL_GROUP_WIDTH
N_BRANCHES = 2
IN_WIDTH = 3 * ATTN_WIDTH + POOL_WIDTH + N_BRANCHES * D_MODEL
D_FF = 5632
CONV_WIDTH = 3
PLE_DIM = 256
Q_BLOCK = 128
EPS = 1e-6

kernel_name = "hybrid_stickbreak_pool_convffn_layer"


def rmsnorm(x, gain):
    xf = x.astype(jnp.float32)
    y = xf * lax.rsqrt(jnp.mean(xf * xf, axis=-1, keepdims=True) + EPS)
    return (y * gain.astype(jnp.float32)).astype(x.dtype)


def stick_breaking_attention(q, k, v):
    B, S, H, Dh = q.shape
    nb = S // Q_BLOCK
    scale = Dh ** -0.5
    qb = q.reshape(B, nb, Q_BLOCK, H, Dh).transpose(1, 0, 3, 2, 4)
    starts = jnp.arange(nb, dtype=jnp.int32) * Q_BLOCK
    key_pos = jnp.arange(S, dtype=jnp.int32)

    def block(args):
        q_i, start = args
        z = jnp.einsum('bhqd,bkhd->bhqk', q_i, k).astype(jnp.float32) * scale
        q_pos = start + jnp.arange(Q_BLOCK, dtype=jnp.int32)
        causal = key_pos[None, :] < q_pos[:, None]
        log_1m_beta = jnp.where(causal, jax.nn.log_sigmoid(-z), 0.0)
        suffix = lax.cumsum(log_1m_beta, axis=3, reverse=True) - log_1m_beta
        log_a = jax.nn.log_sigmoid(z) + suffix
        a = jnp.where(causal, jnp.exp(log_a), 0.0)
        return jnp.einsum('bhqk,bkhd->bqhd', a.astype(v.dtype), v)

    out = lax.map(block, (qb, starts))
    return out.transpose(1, 0, 2, 3, 4).reshape(B, S, H * Dh)


def multiscale_causal_pool(u):
    B, S, _ = u.shape
    groups = u.astype(jnp.float32).reshape(B, S, N_POOL_GROUPS, POOL_GROUP_WIDTH)
    csum = jnp.pad(jnp.cumsum(groups, axis=1), ((0, 0), (1, 0), (0, 0), (0, 0)))
    pos = jnp.arange(S, dtype=jnp.int32)
    means = []
    for g, w in enumerate(POOL_WINDOWS):
        upper = csum[:, 1:, g]
        lower = jnp.pad(csum[:, :S + 1 - w, g], ((0, 0), (w - 1, 0), (0, 0)))
        count = jnp.minimum(pos + 1, w).astype(jnp.float32)[None, :, None]
        means.append((upper - lower) / count)
    return jnp.stack(means, axis=2) - groups


def causal_depthwise_conv(h, w, b):
    S = h.shape[1]
    hp = jnp.pad(h, ((0, 0), (CONV_WIDTH - 1, 0), (0, 0)))
    out = b
    for j in range(CONV_WIDTH):
        out = out + hp[:, j:j + S] * w[j]
    return out


def _fwd_setup_inputs(seed: int = 0) -> dict:
    key = jax.random.key(seed)
    ks = jax.random.split(key, 20)
    f32 = jnp.float32

    def w(k, shape, fan_in):
        return jax.random.normal(k, shape, f32) * (fan_in ** -0.5)

    def gain(k, shape):
        return 1.0 + 0.02 * jax.random.normal(k, shape, f32)

    return {
        "x": jax.random.normal(ks[0], (BATCH, SEQ, D_MODEL), f32),
        "p": jax.random.normal(ks[1], (DEPTH, BATCH, SEQ, PLE_DIM), f32),
        "norm_mix_pre": gain(ks[2], (DEPTH, D_MODEL)),
        "w_in": w(ks[3], (DEPTH, D_MODEL, IN_WIDTH), D_MODEL),
        "w_attn_branch": w(ks[4], (DEPTH, ATTN_WIDTH, D_MODEL), ATTN_WIDTH),
        "w_pool_group": w(ks[5], (DEPTH, N_POOL_GROUPS, POOL_GROUP_WIDTH, POOL_GROUP_WIDTH), POOL_GROUP_WIDTH),
        "pool_scale": gain(ks[6], (DEPTH, POOL_WIDTH)),
        "w_pool_branch": w(ks[7], (DEPTH, POOL_WIDTH, D_MODEL), POOL_WIDTH),
        "w_out": w(ks[8], (DEPTH, D_MODEL, D_MODEL), D_MODEL),
        "norm_mix_post": gain(ks[9], (DEPTH, D_MODEL)),
        "norm_ffn_pre": gain(ks[10], (DEPTH, D_MODEL)),
        "w_up": w(ks[11], (DEPTH, D_MODEL, 2 * D_FF), D_MODEL),
        "conv_w": w(ks[12], (DEPTH, CONV_WIDTH, 2 * D_FF), CONV_WIDTH),
        "conv_b": 0.01 * jax.random.normal(ks[13], (DEPTH, 2 * D_FF), f32),
        "w_down": w(ks[14], (DEPTH, D_FF, D_MODEL), D_FF),
        "norm_ffn_post": gain(ks[15], (DEPTH, D_MODEL)),
        "w_ple": w(ks[16], (DEPTH, PLE_DIM, D_MODEL), PLE_DIM),
        "w_ple_gate": w(ks[17], (DEPTH, D_MODEL, D_MODEL), D_MODEL),
        "norm_ple_post": gain(ks[18], (DEPTH, D_MODEL)),
    }


def _fwd_reference(x, p, norm_mix_pre, w_in, w_attn_branch, w_pool_group, pool_scale, w_pool_branch, w_out,
              norm_mix_post, norm_ffn_pre, w_up, conv_w, conv_b, w_down, norm_ffn_post, w_ple, w_ple_gate,
              norm_ple_post):
    B, S, _ = x.shape
    splits = [ATTN_WIDTH, 2 * ATTN_WIDTH, 3 * ATTN_WIDTH, 3 * ATTN_WIDTH + POOL_WIDTH,
              3 * ATTN_WIDTH + POOL_WIDTH + D_MODEL]
    for i in range(DEPTH):
        h = rmsnorm(x, norm_mix_pre[i])
        proj = h @ w_in[i]
        q, k, v, u, g_attn, g_pool = jnp.split(proj, splits, axis=-1)
        q = q.reshape(B, S, N_ATTN_HEADS, HEAD_DIM)
        k = k.reshape(B, S, N_ATTN_HEADS, HEAD_DIM)
        v = v.reshape(B, S, N_ATTN_HEADS, HEAD_DIM)
        y_attn = stick_breaking_attention(q, k, v) @ w_attn_branch[i]

        pooled = multiscale_causal_pool(u).astype(u.dtype)
        pooled = jnp.einsum('bsgc,gcd->bsgd', pooled, w_pool_group[i]).reshape(B, S, POOL_WIDTH)
        y_pool = (pooled * pool_scale[i]) @ w_pool_branch[i]

        mixed = jax.nn.sigmoid(g_attn) * y_attn + jax.nn.sigmoid(g_pool) * y_pool
        x = x + rmsnorm(mixed @ w_out[i], norm_mix_post[i])

        h = rmsnorm(x, norm_ffn_pre[i])
        up = causal_depthwise_conv(h @ w_up[i], conv_w[i], conv_b[i])
        gate, val = jnp.split(up, 2, axis=-1)
        y_ffn = (jax.nn.gelu(gate, approximate=True) * val) @ w_down[i]
        x = x + rmsnorm(y_ffn, norm_ffn_post[i])

        e = p[i] @ w_ple[i]
        x = x + rmsnorm(jax.nn.sigmoid(x @ w_ple_gate[i]) * e, norm_ple_post[i])
    return x


import jax as _jax
import jax.numpy as _jnp

TWIN_FORMAT = 'train_step'
FWD_PARAMS = ['x', 'p', 'norm_mix_pre', 'w_in', 'w_attn_branch', 'w_pool_group', 'pool_scale', 'w_pool_branch', 'w_out', 'norm_mix_post', 'norm_ffn_pre', 'w_up', 'conv_w', 'conv_b', 'w_down', 'norm_ffn_post', 'w_ple', 'w_ple_gate', 'norm_ple_post']
TWIN_WEIGHTS = ['norm_mix_pre', 'w_in', 'w_attn_branch', 'w_pool_group', 'pool_scale', 'w_pool_branch', 'w_out', 'norm_mix_post', 'norm_ffn_pre', 'w_up', 'conv_w', 'conv_b', 'w_down', 'norm_ffn_post', 'w_ple', 'w_ple_gate', 'norm_ple_post']
TWIN_DIFF_INPUT = 'x'
TWIN_INPUTS = ['x', 'p', 'norm_mix_pre', 'w_in', 'w_attn_branch', 'w_pool_group', 'pool_scale', 'w_pool_branch', 'w_out', 'norm_mix_post', 'norm_ffn_pre', 'w_up', 'conv_w', 'conv_b', 'w_down', 'norm_ffn_post', 'w_ple', 'w_ple_gate', 'norm_ple_post', 'loss_target', 'm_norm_mix_pre', 'm_w_in', 'm_w_attn_branch', 'm_w_pool_group', 'm_pool_scale', 'm_w_pool_branch', 'm_w_out', 'm_norm_mix_post', 'm_norm_ffn_pre', 'm_w_up', 'm_conv_w', 'm_conv_b', 'm_w_down', 'm_norm_ffn_post', 'm_w_ple', 'm_w_ple_gate', 'm_norm_ple_post', 'v_norm_mix_pre', 'v_w_in', 'v_w_attn_branch', 'v_w_pool_group', 'v_pool_scale', 'v_w_pool_branch', 'v_w_out', 'v_norm_mix_post', 'v_norm_ffn_pre', 'v_w_up', 'v_conv_w', 'v_conv_b', 'v_w_down', 'v_norm_ffn_post', 'v_w_ple', 'v_w_ple_gate', 'v_norm_ple_post']
TWIN_OUTPUTS = ['loss', 'grad_x', 'grad_norm_mix_pre', 'grad_w_in', 'grad_w_attn_branch', 'grad_w_pool_group', 'grad_pool_scale', 'grad_w_pool_branch', 'grad_w_out', 'grad_norm_mix_post', 'grad_norm_ffn_pre', 'grad_w_up', 'grad_conv_w', 'grad_conv_b', 'grad_w_down', 'grad_norm_ffn_post', 'grad_w_ple', 'grad_w_ple_gate', 'grad_norm_ple_post', 'delta_norm_mix_pre', 'delta_w_in', 'delta_w_attn_branch', 'delta_w_pool_group', 'delta_pool_scale', 'delta_w_pool_branch', 'delta_w_out', 'delta_norm_mix_post', 'delta_norm_ffn_pre', 'delta_w_up', 'delta_conv_w', 'delta_conv_b', 'delta_w_down', 'delta_norm_ffn_post', 'delta_w_ple', 'delta_w_ple_gate', 'delta_norm_ple_post', 'new_m_norm_mix_pre', 'new_m_w_in', 'new_m_w_attn_branch', 'new_m_w_pool_group', 'new_m_pool_scale', 'new_m_w_pool_branch', 'new_m_w_out', 'new_m_norm_mix_post', 'new_m_norm_ffn_pre', 'new_m_w_up', 'new_m_conv_w', 'new_m_conv_b', 'new_m_w_down', 'new_m_norm_ffn_post', 'new_m_w_ple', 'new_m_w_ple_gate', 'new_m_norm_ple_post', 'new_v_norm_mix_pre', 'new_v_w_in', 'new_v_w_attn_branch', 'new_v_w_pool_group', 'new_v_pool_scale', 'new_v_w_pool_branch', 'new_v_w_out', 'new_v_norm_mix_post', 'new_v_norm_ffn_pre', 'new_v_w_up', 'new_v_conv_w', 'new_v_conv_b', 'new_v_w_down', 'new_v_norm_ffn_post', 'new_v_w_ple', 'new_v_w_ple_gate', 'new_v_norm_ple_post']
TWIN_LEAF_KINDS = {'loss': 'loss', 'grad_x': 'grad_x', 'grad_norm_mix_pre': 'grad_w', 'grad_w_in': 'grad_w', 'grad_w_attn_branch': 'grad_w', 'grad_w_pool_group': 'grad_w', 'grad_pool_scale': 'grad_w', 'grad_w_pool_branch': 'grad_w', 'grad_w_out': 'grad_w', 'grad_norm_mix_post': 'grad_w', 'grad_norm_ffn_pre': 'grad_w', 'grad_w_up': 'grad_w', 'grad_conv_w': 'grad_w', 'grad_conv_b': 'grad_w', 'grad_w_down': 'grad_w', 'grad_norm_ffn_post': 'grad_w', 'grad_w_ple': 'grad_w', 'grad_w_ple_gate': 'grad_w', 'grad_norm_ple_post': 'grad_w', 'delta_norm_mix_pre': 'delta_w', 'delta_w_in': 'delta_w', 'delta_w_attn_branch': 'delta_w', 'delta_w_pool_group': 'delta_w', 'delta_pool_scale': 'delta_w', 'delta_w_pool_branch': 'delta_w', 'delta_w_out': 'delta_w', 'delta_norm_mix_post': 'delta_w', 'delta_norm_ffn_pre': 'delta_w', 'delta_w_up': 'delta_w', 'delta_conv_w': 'delta_w', 'delta_conv_b': 'delta_w', 'delta_w_down': 'delta_w', 'delta_norm_ffn_post': 'delta_w', 'delta_w_ple': 'delta_w', 'delta_w_ple_gate': 'delta_w', 'delta_norm_ple_post': 'delta_w', 'new_m_norm_mix_pre': 'new_m', 'new_m_w_in': 'new_m', 'new_m_w_attn_branch': 'new_m', 'new_m_w_pool_group': 'new_m', 'new_m_pool_scale': 'new_m', 'new_m_w_pool_branch': 'new_m', 'new_m_w_out': 'new_m', 'new_m_norm_mix_post': 'new_m', 'new_m_norm_ffn_pre': 'new_m', 'new_m_w_up': 'new_m', 'new_m_conv_w': 'new_m', 'new_m_conv_b': 'new_m', 'new_m_w_down': 'new_m', 'new_m_norm_ffn_post': 'new_m', 'new_m_w_ple': 'new_m', 'new_m_w_ple_gate': 'new_m', 'new_m_norm_ple_post': 'new_m', 'new_v_norm_mix_pre': 'new_v', 'new_v_w_in': 'new_v', 'new_v_w_attn_branch': 'new_v', 'new_v_w_pool_group': 'new_v', 'new_v_pool_scale': 'new_v', 'new_v_w_pool_branch': 'new_v', 'new_v_w_out': 'new_v', 'new_v_norm_mix_post': 'new_v', 'new_v_norm_ffn_pre': 'new_v', 'new_v_w_up': 'new_v', 'new_v_conv_w': 'new_v', 'new_v_conv_b': 'new_v', 'new_v_w_down': 'new_v', 'new_v_norm_ffn_post': 'new_v', 'new_v_w_ple': 'new_v', 'new_v_w_ple_gate': 'new_v', 'new_v_norm_ple_post': 'new_v'}


def _forward(args):
    return _fwd_reference(*[args[k] for k in FWD_PARAMS])


def _output_shape():
    def fwd():
        inp = _fwd_setup_inputs(0)
        return _fwd_reference(*[inp[k] for k in FWD_PARAMS])
    out = _jax.eval_shape(fwd)
    return out.shape, out.dtype

N_MICROBATCH = 1
ADAM_LR = 0.001
ADAM_B1 = 0.9
ADAM_B2 = 0.999
ADAM_EPS = 1e-08
ADAM_WD = 0.01
ADAM_STEP = 10
PER_EXAMPLE_BATCH_AXIS = {'x': 0, 'p': 1, 'loss_target': 0}
SHARED_INPUTS = []
_WEIGHT_DTYPES = {'norm_mix_pre': _jnp.float32, 'w_in': _jnp.float32, 'w_attn_branch': _jnp.float32, 'w_pool_group': _jnp.float32, 'pool_scale': _jnp.float32, 'w_pool_branch': _jnp.float32, 'w_out': _jnp.float32, 'norm_mix_post': _jnp.float32, 'norm_ffn_pre': _jnp.float32, 'w_up': _jnp.float32, 'conv_w': _jnp.float32, 'conv_b': _jnp.float32, 'w_down': _jnp.float32, 'norm_ffn_post': _jnp.float32, 'w_ple': _jnp.float32, 'w_ple_gate': _jnp.float32, 'norm_ple_post': _jnp.float32}
MOMENT_SCALE = {'norm_mix_pre': 3.696447e-01, 'w_in': 1.740225e-01, 'w_attn_branch': 1.629865e-01, 'w_pool_group': 3.940019e-01, 'pool_scale': 4.086880e-01, 'w_pool_branch': 2.847715e-01, 'w_out': 3.342885e-01, 'norm_mix_post': 1.598804e+01, 'norm_ffn_pre': 3.005071e-01, 'w_up': 1.261157e-01, 'conv_w': 1.342247e-01, 'conv_b': 1.755939e-01, 'w_down': 2.254433e-01, 'norm_ffn_post': 1.599538e+01, 'w_ple': 1.789301e-01, 'w_ple_gate': 9.522313e-02, 'norm_ple_post': 1.607253e+01}


def _to_microbatches(a, axis):
    t = _jnp.moveaxis(a, axis, 0)
    t = t.reshape((N_MICROBATCH, t.shape[0] // N_MICROBATCH) + t.shape[1:])
    return _jnp.moveaxis(t, 1, axis + 1)


def setup_inputs(seed: int = 0) -> dict:
    inp = _fwd_setup_inputs(seed)
    key = _jax.random.fold_in(_jax.random.key(seed), 7919)
    shape, _ = _output_shape()
    out = dict(inp)
    out["loss_target"] = _jax.random.normal(_jax.random.fold_in(key, 0), shape, _jnp.float32)
    for i, name in enumerate(TWIN_WEIGHTS):
        w = inp[name].astype(_jnp.float32)
        if MOMENT_SCALE is None:
            s = _jnp.sqrt(_jnp.mean(_jnp.square(w)) + 1e-30)
        else:
            s = MOMENT_SCALE[name]
        km, kv = _jax.random.split(_jax.random.fold_in(key, i + 1))
        out[name] = w
        out["m_" + name] = s * _jax.random.normal(km, w.shape, _jnp.float32)
        out["v_" + name] = (s * s) * _jax.random.uniform(kv, w.shape, _jnp.float32, 0.5, 1.5)
    if N_MICROBATCH > 1:
        for name, axis in PER_EXAMPLE_BATCH_AXIS.items():
            out[name] = _to_microbatches(out[name], axis)
    return {'x': out['x'], 'p': out['p'], 'norm_mix_pre': out['norm_mix_pre'], 'w_in': out['w_in'], 'w_attn_branch': out['w_attn_branch'], 'w_pool_group': out['w_pool_group'], 'pool_scale': out['pool_scale'], 'w_pool_branch': out['w_pool_branch'], 'w_out': out['w_out'], 'norm_mix_post': out['norm_mix_post'], 'norm_ffn_pre': out['norm_ffn_pre'], 'w_up': out['w_up'], 'conv_w': out['conv_w'], 'conv_b': out['conv_b'], 'w_down': out['w_down'], 'norm_ffn_post': out['norm_ffn_post'], 'w_ple': out['w_ple'], 'w_ple_gate': out['w_ple_gate'], 'norm_ple_post': out['norm_ple_post'], 'loss_target': out['loss_target'], 'm_norm_mix_pre': out['m_norm_mix_pre'], 'm_w_in': out['m_w_in'], 'm_w_attn_branch': out['m_w_attn_branch'], 'm_w_pool_group': out['m_w_pool_group'], 'm_pool_scale': out['m_pool_scale'], 'm_w_pool_branch': out['m_w_pool_branch'], 'm_w_out': out['m_w_out'], 'm_norm_mix_post': out['m_norm_mix_post'], 'm_norm_ffn_pre': out['m_norm_ffn_pre'], 'm_w_up': out['m_w_up'], 'm_conv_w': out['m_conv_w'], 'm_conv_b': out['m_conv_b'], 'm_w_down': out['m_w_down'], 'm_norm_ffn_post': out['m_norm_ffn_post'], 'm_w_ple': out['m_w_ple'], 'm_w_ple_gate': out['m_w_ple_gate'], 'm_norm_ple_post': out['m_norm_ple_post'], 'v_norm_mix_pre': out['v_norm_mix_pre'], 'v_w_in': out['v_w_in'], 'v_w_attn_branch': out['v_w_attn_branch'], 'v_w_pool_group': out['v_w_pool_group'], 'v_pool_scale': out['v_pool_scale'], 'v_w_pool_branch': out['v_w_pool_branch'], 'v_w_out': out['v_w_out'], 'v_norm_mix_post': out['v_norm_mix_post'], 'v_norm_ffn_pre': out['v_norm_ffn_pre'], 'v_w_up': out['v_w_up'], 'v_conv_w': out['v_conv_w'], 'v_conv_b': out['v_conv_b'], 'v_w_down': out['v_w_down'], 'v_norm_ffn_post': out['v_norm_ffn_post'], 'v_w_ple': out['v_w_ple'], 'v_w_ple_gate': out['v_w_ple_gate'], 'v_norm_ple_post': out['v_norm_ple_post']}


def _loss(weights, diff, rest, loss_target):
    with _jax.named_scope("forward"):
        args = {**rest, TWIN_DIFF_INPUT: diff, **{k: w.astype(_WEIGHT_DTYPES[k]) for k, w in weights.items()}}
        y = _forward(args)
    with _jax.named_scope("loss_head"):
        err = _jnp.square(y.astype(_jnp.float32) - loss_target)
        return 0.5 * _jnp.sum(_jnp.mean(err, axis=-1)) if err.ndim else 0.5 * err


def _adamw(w, g, m, v):
    m = ADAM_B1 * m + (1.0 - ADAM_B1) * g
    v = ADAM_B2 * v + (1.0 - ADAM_B2) * _jnp.square(g)
    m_hat = m / (1.0 - ADAM_B1 ** ADAM_STEP)
    v_hat = v / (1.0 - ADAM_B2 ** ADAM_STEP)
    delta = -ADAM_LR * (m_hat / (_jnp.sqrt(v_hat) + ADAM_EPS) + ADAM_WD * w)
    return delta, m, v


def reference(x, p, norm_mix_pre, w_in, w_attn_branch, w_pool_group, pool_scale, w_pool_branch, w_out, norm_mix_post, norm_ffn_pre, w_up, conv_w, conv_b, w_down, norm_ffn_post, w_ple, w_ple_gate, norm_ple_post, loss_target, m_norm_mix_pre, m_w_in, m_w_attn_branch, m_w_pool_group, m_pool_scale, m_w_pool_branch, m_w_out, m_norm_mix_post, m_norm_ffn_pre, m_w_up, m_conv_w, m_conv_b, m_w_down, m_norm_ffn_post, m_w_ple, m_w_ple_gate, m_norm_ple_post, v_norm_mix_pre, v_w_in, v_w_attn_branch, v_w_pool_group, v_pool_scale, v_w_pool_branch, v_w_out, v_norm_mix_post, v_norm_ffn_pre, v_w_up, v_conv_w, v_conv_b, v_w_down, v_norm_ffn_post, v_w_ple, v_w_ple_gate, v_norm_ple_post):
    given = dict(x=x, p=p, norm_mix_pre=norm_mix_pre, w_in=w_in, w_attn_branch=w_attn_branch, w_pool_group=w_pool_group, pool_scale=pool_scale, w_pool_branch=w_pool_branch, w_out=w_out, norm_mix_post=norm_mix_post, norm_ffn_pre=norm_ffn_pre, w_up=w_up, conv_w=conv_w, conv_b=conv_b, w_down=w_down, norm_ffn_post=norm_ffn_post, w_ple=w_ple, w_ple_gate=w_ple_gate, norm_ple_post=norm_ple_post, loss_target=loss_target, m_norm_mix_pre=m_norm_mix_pre, m_w_in=m_w_in, m_w_attn_branch=m_w_attn_branch, m_w_pool_group=m_w_pool_group, m_pool_scale=m_pool_scale, m_w_pool_branch=m_w_pool_branch, m_w_out=m_w_out, m_norm_mix_post=m_norm_mix_post, m_norm_ffn_pre=m_norm_ffn_pre, m_w_up=m_w_up, m_conv_w=m_conv_w, m_conv_b=m_conv_b, m_w_down=m_w_down, m_norm_ffn_post=m_norm_ffn_post, m_w_ple=m_w_ple, m_w_ple_gate=m_w_ple_gate, m_norm_ple_post=m_norm_ple_post, v_norm_mix_pre=v_norm_mix_pre, v_w_in=v_w_in, v_w_attn_branch=v_w_attn_branch, v_w_pool_group=v_w_pool_group, v_pool_scale=v_pool_scale, v_w_pool_branch=v_w_pool_branch, v_w_out=v_w_out, v_norm_mix_post=v_norm_mix_post, v_norm_ffn_pre=v_norm_ffn_pre, v_w_up=v_w_up, v_conv_w=v_conv_w, v_conv_b=v_conv_b, v_w_down=v_w_down, v_norm_ffn_post=v_norm_ffn_post, v_w_ple=v_w_ple, v_w_ple_gate=v_w_ple_gate, v_norm_ple_post=v_norm_ple_post)
    weights = {n: given[n] for n in TWIN_WEIGHTS}
    shared = {n: given[n] for n in SHARED_INPUTS}
    per_example = {n: given[n] for n in ['x', 'p']}
    grad_fn = _jax.value_and_grad(_loss, argnums=(0, 1))

    def one_microbatch(ex, loss_target):
        ex = dict(ex)
        diff = ex.pop(TWIN_DIFF_INPUT)
        return grad_fn(weights, diff, {**shared, **ex}, loss_target)

    if N_MICROBATCH == 1:
        loss, (grad_w, grad_x) = one_microbatch(per_example, given["loss_target"])
    else:
        def body(carry, xs):
            loss_sum, grad_sum = carry
            l_k, (gw_k, gx_k) = one_microbatch(xs[0], xs[1])
            with _jax.named_scope("update"):
                return (loss_sum + l_k, _jax.tree.map(_jnp.add, grad_sum, gw_k)), gx_k

        init = (_jnp.zeros((), _jnp.float32), _jax.tree.map(_jnp.zeros_like, weights))
        (loss, grad_w), grad_x = _jax.lax.scan(body, init, (per_example, given["loss_target"]))
    with _jax.named_scope("update"):
        delta_w, new_m, new_v = {}, {}, {}
        for n in TWIN_WEIGHTS:
            delta_w[n], new_m[n], new_v[n] = _adamw(weights[n], grad_w[n], given["m_" + n], given["v_" + n])
    return (loss, grad_x, *[grad_w[n] for n in TWIN_WEIGHTS], *[delta_w[n] for n in TWIN_WEIGHTS],
            *[new_m[n] for n in TWIN_WEIGHTS], *[new_v[n] for n in TWIN_WEIGHTS])
```

```python
import functools

import jax
import jax.numpy as jnp
from jax import lax
from jax.experimental import pallas as pl
from jax.experimental.pallas import tpu as pltpu

F32 = jnp.float32
BF16 = jnp.bfloat16
MESH = pl.DeviceIdType.MESH

EPS = 1e-6
HEAD_DIM = 128
POOL_WINDOWS = (2, 4, 8, 16)
POOL_HALO = 16
CONV_HALO = 8
GELU_C0 = 0.7978845608028654
GELU_C1 = 0.044715
ADAM_LR = 0.001
ADAM_B1 = 0.9
ADAM_B2 = 0.999
ADAM_EPS = 1e-08
ADAM_WD = 0.01
ADAM_STEP = 10
N_DEV = 8
PACK_W = 1024
VMEM_LIMIT = 56 * 2**20
ANY = pl.BlockSpec(memory_space=pl.ANY)


def _params(*sem):
    return pltpu.CompilerParams(dimension_semantics=sem, vmem_limit_bytes=VMEM_LIMIT)


def _tile(dim, target, align):
    if dim <= target:
        return dim
    t = (target // align) * align
    while t >= align:
        if dim % t == 0:
            return t
        t -= align
    return dim


def _sigmoid(x):
    return 1.0 / (1.0 + jnp.exp(-x))


def _mm(a, b, *, ta=False, tb=False, out_dtype=F32, name, tm=1024, tn=1024, tk=512):
    M, K = (a.shape[1], a.shape[0]) if ta else a.shape
    N = b.shape[0] if tb else b.shape[1]
    tm = _tile(M, tm, 128)
    tn = _tile(N, tn, 128)
    tk = _tile(K, tk, 128)
    nk = K // tk
    a_spec = pl.BlockSpec((tk, tm), lambda i, j, k: (k, i)) if ta else pl.BlockSpec((tm, tk), lambda i, j, k: (i, k))
    b_spec = pl.BlockSpec((tn, tk), lambda i, j, k: (j, k)) if tb else pl.BlockSpec((tk, tn), lambda i, j, k: (k, j))
    dims = (((0 if ta else 1,), (1 if tb else 0,)), ((), ()))

    def body(a_ref, b_ref, o_ref, acc_ref):
        k = pl.program_id(2)

        @pl.when(k == 0)
        def _():
            acc_ref[...] = jnp.zeros_like(acc_ref)

        acc_ref[...] += lax.dot_general(a_ref[...].astype(BF16), b_ref[...].astype(BF16), dims,
                                        preferred_element_type=F32)

        @pl.when(k == nk - 1)
        def _():
            o_ref[...] = acc_ref[...].astype(o_ref.dtype)

    return pl.pallas_call(
        body, name=name, grid=(M // tm, N // tn, nk), in_specs=[a_spec, b_spec],
        out_specs=pl.BlockSpec((tm, tn), lambda i, j, k: (i, j)), out_shape=jax.ShapeDtypeStruct((M, N), out_dtype),
        scratch_shapes=[pltpu.VMEM((tm, tn), F32)], compiler_params=_params("parallel", "parallel", "arbitrary"),
    )(a, b)


def _pool_group_fwd(pooled, w_pg, scale):
    S = pooled.shape[0]
    G, C, C2 = w_pg.shape
    tm = _tile(S, 1024, 16)

    def body(a_ref, w_ref, s_ref, pg_ref, ps_ref):
        pg = jnp.dot(a_ref[...], w_ref[...], preferred_element_type=F32)
        pg_ref[...] = pg
        ps_ref[...] = (pg * s_ref[...]).astype(BF16)

    return pl.pallas_call(
        body, name="pool_group_fwd", grid=(G, S // tm),
        in_specs=[pl.BlockSpec((tm, C), lambda g, i: (i, g)), pl.BlockSpec((None, C, C2), lambda g, i: (g, 0, 0)),
                  pl.BlockSpec((1, C2), lambda g, i: (0, g))],
        out_specs=[pl.BlockSpec((tm, C2), lambda g, i: (i, g)), pl.BlockSpec((tm, C2), lambda g, i: (i, g))],
        out_shape=[jax.ShapeDtypeStruct((S, G * C2), F32), jax.ShapeDtypeStruct((S, G * C2), BF16)],
        compiler_params=_params("parallel", "parallel"),
    )(pooled, w_pg, scale)


def _pool_group_bwd_x(dpg, w_pg):
    S = dpg.shape[0]
    G, C, C2 = w_pg.shape
    tm = _tile(S, 1024, 16)

    def body(d_ref, w_ref, o_ref):
        o_ref[...] = lax.dot_general(d_ref[...], w_ref[...], (((1,), (1,)), ((), ())), preferred_element_type=F32)

    return pl.pallas_call(
        body, name="pool_group_bwd_x", grid=(G, S // tm),
        in_specs=[pl.BlockSpec((tm, C2), lambda g, i: (i, g)), pl.BlockSpec((None, C, C2), lambda g, i: (g, 0, 0))],
        out_specs=pl.BlockSpec((tm, C), lambda g, i: (i, g)), out_shape=jax.ShapeDtypeStruct((S, G * C), F32),
        compiler_params=_params("parallel", "parallel"),
    )(dpg, w_pg)


def _pool_group_bwd_w(pooled, dpg, G):
    S = pooled.shape[0]
    C, C2 = pooled.shape[1] // G, dpg.shape[1] // G
    tk = _tile(S, 1024, 16)

    def body(a_ref, d_ref, o_ref):
        @pl.when(pl.program_id(1) == 0)
        def _():
            o_ref[...] = jnp.zeros_like(o_ref)

        o_ref[...] += lax.dot_general(a_ref[...], d_ref[...], (((0,), (0,)), ((), ())), preferred_element_type=F32)

    return pl.pallas_call(
        body, name="pool_group_bwd_w", grid=(G, S // tk),
        in_specs=[pl.BlockSpec((tk, C), lambda g, k: (k, g)), pl.BlockSpec((tk, C2), lambda g, k: (k, g))],
        out_specs=pl.BlockSpec((None, C, C2), lambda g, k: (g, 0, 0)), out_shape=jax.ShapeDtypeStruct((G, C, C2), F32),
        compiler_params=_params("parallel", "arbitrary"),
    )(pooled, dpg)


def _rms(x, gain):
    r = lax.rsqrt(jnp.mean(x * x, axis=-1, keepdims=True) + EPS)
    return x * r * gain


def _rms_bwd(x, gain, dy):
    r = lax.rsqrt(jnp.mean(x * x, axis=-1, keepdims=True) + EPS)
    xh = x * r
    dgain = jnp.sum(dy * xh, axis=0, keepdims=True)
    dxh = dy * gain
    dx = r * (dxh - xh * jnp.mean(dxh * xh, axis=-1, keepdims=True))
    return dx, dgain


def _row_call(body, name, ins, outs, tr, *, accs=()):
    S = None
    in_specs, args = [], []
    for it in ins:
        arr, kind = it[0], it[1]
        if kind == "row":
            S = arr.shape[0]
            if len(it) == 4:
                width, cb = it[2], it[3]
                in_specs.append(pl.BlockSpec((tr, width), functools.partial(lambda i, cb: (i, cb), cb=cb)))
            else:
                in_specs.append(pl.BlockSpec((tr, arr.shape[1]), lambda i: (i, 0)))
        else:
            assert arr.ndim == 2
            in_specs.append(pl.BlockSpec(arr.shape, lambda i: (0, 0)))
        args.append(arr)
    out_specs, out_shape = [], []
    for sds, kind in outs:
        if kind == "row":
            out_specs.append(pl.BlockSpec((tr, sds.shape[1]), lambda i: (i, 0)))
        else:
            assert len(sds.shape) == 2
            out_specs.append(pl.BlockSpec(sds.shape, lambda i: (0, 0)))
        out_shape.append(sds)
    return pl.pallas_call(body, name=name, grid=(S // tr,), in_specs=in_specs, out_specs=out_specs, out_shape=out_shape,
                          compiler_params=_params("arbitrary"))(*args)


def _sds(shape, dtype):
    return jax.ShapeDtypeStruct(shape, dtype)


def _first_step_zero(*refs):
    @pl.when(pl.program_id(0) == 0)
    def _():
        for r in refs:
            r[...] = jnp.zeros_like(r)


def _rms_fwd(x, gain, tr):
    def body(x_ref, g_ref, o_ref):
        o_ref[...] = _rms(x_ref[...], g_ref[...]).astype(BF16)

    S, D = x.shape
    return _row_call(body, "rms_fwd", [(x, "row"), (gain, "full")], [(_sds((S, D), BF16), "row")], tr)[0]


def _gate_mix(proj, ya, yp, gate_cb, tr):
    S, D = ya.shape

    def body(ga_ref, gp_ref, ya_ref, yp_ref, o_ref):
        o_ref[...] = (_sigmoid(ga_ref[...]) * ya_ref[...] + _sigmoid(gp_ref[...]) * yp_ref[...]).astype(BF16)

    return _row_call(body, "gate_mix", [(proj, "row", D, gate_cb), (proj, "row", D, gate_cb + 1), (ya, "row"), (yp, "row")],
                     [(_sds((S, D), BF16), "row")], tr)[0]


def _resid_rms2(x, mo, g2, g3, tr):
    S, D = x.shape

    def body(x_ref, mo_ref, g2_ref, g3_ref, x1_ref, h2_ref):
        x1 = x_ref[...] + _rms(mo_ref[...], g2_ref[...])
        x1_ref[...] = x1
        h2_ref[...] = _rms(x1, g3_ref[...]).astype(BF16)

    return _row_call(body, "resid_rms2", [(x, "row"), (mo, "row"), (g2, "full"), (g3, "full")],
                     [(_sds((S, D), F32), "row"), (_sds((S, D), BF16), "row")], tr)


def _resid_rms(x1, yf, g4, tr):
    S, D = x1.shape

    def body(x_ref, y_ref, g_ref, o_ref, ob_ref):
        x2 = x_ref[...] + _rms(y_ref[...], g_ref[...])
        o_ref[...] = x2
        ob_ref[...] = x2.astype(BF16)

    return _row_call(body, "resid_rms", [(x1, "row"), (yf, "row"), (g4, "full")],
                     [(_sds((S, D), F32), "row"), (_sds((S, D), BF16), "row")], tr)


def _ple_loss(gl, e, x2, tgt, g5, tr):
    S, D = x2.shape

    def body(gl_ref, e_ref, x2_ref, t_ref, g_ref, loss_ref, dx3_ref, de_ref, dgl_ref, dg_ref):
        _first_step_zero(loss_ref, dg_ref)
        s = _sigmoid(gl_ref[...])
        e_ = e_ref[...]
        t = s * e_
        gain = g_ref[...]
        err = x2_ref[...] + _rms(t, gain) - t_ref[...]
        row_loss = jnp.mean(err * err, axis=-1, keepdims=True)
        loss_ref[...] += 0.5 * jnp.sum(row_loss, axis=0, keepdims=True)
        dx3 = err * (1.0 / D)
        dx3_ref[...] = dx3
        dt, dgain = _rms_bwd(t, gain, dx3)
        dg_ref[...] += dgain
        de_ref[...] = (dt * s).astype(BF16)
        dgl_ref[...] = (dt * e_ * s * (1.0 - s)).astype(BF16)

    return _row_call(body, "ple_loss", [(gl, "row"), (e, "row"), (x2, "row"), (tgt, "row"), (g5, "full")],
                     [(_sds((1, 1), F32), "acc"), (_sds((S, D), F32), "row"), (_sds((S, D), BF16), "row"),
                      (_sds((S, D), BF16), "row"), (_sds((1, D), F32), "acc")], tr)


def _rms_bwd_a(dx3, dx2g, yf, g4, tr):
    S, D = yf.shape

    def body(a_ref, b_ref, y_ref, g_ref, dx_ref, dy_ref, dg_ref):
        _first_step_zero(dg_ref)
        dx2 = a_ref[...] + b_ref[...]
        dx_ref[...] = dx2
        dy, dgain = _rms_bwd(y_ref[...], g_ref[...], dx2)
        dy_ref[...] = dy.astype(BF16)
        dg_ref[...] += dgain

    return _row_call(body, "rms_bwd_a", [(dx3, "row"), (dx2g, "row"), (yf, "row"), (g4, "full")],
                     [(_sds((S, D), F32), "row"), (_sds((S, D), BF16), "row"), (_sds((1, D), F32), "acc")], tr)


def _rms_bwd_b(dx2, dh2, x1, g3, mo, g2, tr):
    S, D = x1.shape

    def body(dx2_ref, dh2_ref, x1_ref, g3_ref, mo_ref, g2_ref, dx1_ref, dmo_ref, dg3_ref, dg2_ref):
        _first_step_zero(dg3_ref, dg2_ref)
        d, dgain3 = _rms_bwd(x1_ref[...], g3_ref[...], dh2_ref[...])
        dx1 = dx2_ref[...] + d
        dx1_ref[...] = dx1
        dg3_ref[...] += dgain3
        dmo, dgain2 = _rms_bwd(mo_ref[...], g2_ref[...], dx1)
        dmo_ref[...] = dmo.astype(BF16)
        dg2_ref[...] += dgain2

    return _row_call(body, "rms_bwd_b", [(dx2, "row"), (dh2, "row"), (x1, "row"), (g3, "full"), (mo, "row"), (g2, "full")],
                     [(_sds((S, D), F32), "row"), (_sds((S, D), BF16), "row"), (_sds((1, D), F32), "acc"),
                      (_sds((1, D), F32), "acc")], tr)


def _rms_bwd_c(dx1, dh, x, g1, tr):
    S, D = x.shape

    def body(dx1_ref, dh_ref, x_ref, g_ref, o_ref, dg_ref):
        _first_step_zero(dg_ref)
        d, dgain = _rms_bwd(x_ref[...], g_ref[...], dh_ref[...])
        o_ref[...] = dx1_ref[...] + d
        dg_ref[...] += dgain

    return _row_call(body, "rms_bwd_c", [(dx1, "row"), (dh, "row"), (x, "row"), (g1, "full")],
                     [(_sds((S, D), F32), "row"), (_sds((1, D), F32), "acc")], tr)


def _gate_bwd(dmixed, proj, ya, yp, gate_cb, tr):
    S, D = ya.shape

    def body(dm_ref, ga_ref, gp_ref, ya_ref, yp_ref, dya_ref, dyp_ref, dga_ref, dgp_ref):
        dm = dm_ref[...]
        sa = _sigmoid(ga_ref[...])
        sp = _sigmoid(gp_ref[...])
        dya_ref[...] = (dm * sa).astype(BF16)
        dyp_ref[...] = (dm * sp).astype(BF16)
        dga_ref[...] = (dm * ya_ref[...] * sa * (1.0 - sa)).astype(BF16)
        dgp_ref[...] = (dm * yp_ref[...] * sp * (1.0 - sp)).astype(BF16)

    return _row_call(body, "gate_bwd",
                     [(dmixed, "row"), (proj, "row", D, gate_cb), (proj, "row", D, gate_cb + 1), (ya, "row"), (yp, "row")],
                     [(_sds((S, D), BF16), "row")] * 4, tr)


def _scale_bwd(dps, pg, scale, tr):
    S, W = dps.shape

    def body(d_ref, pg_ref, s_ref, o_ref, ds_ref):
        _first_step_zero(ds_ref)
        d = d_ref[...]
        o_ref[...] = (d * s_ref[...]).astype(BF16)
        ds_ref[...] += jnp.sum(d * pg_ref[...], axis=0, keepdims=True)

    return _row_call(body, "scale_bwd", [(dps, "row"), (pg, "row"), (scale, "full")],
                     [(_sds((S, W), BF16), "row"), (_sds((1, W), F32), "acc")], tr)


def _tri(blk, cmp):
    j = lax.broadcasted_iota(jnp.int32, (blk, blk), 0)
    s = lax.broadcasted_iota(jnp.int32, (blk, blk), 1)
    return jnp.concatenate([cmp(j, s).astype(BF16), jnp.ones((blk, 128), BF16)], axis=1)


def _split_dot(x, u):
    hi = x.astype(BF16)
    lo = (x - hi.astype(F32)).astype(BF16)
    return jnp.dot(hi, u, preferred_element_type=F32) + jnp.dot(lo, u, preferred_element_type=F32)


def _scores(q, kj, scale, causal):
    z = lax.dot_general(q, kj, (((1,), (1,)), ((), ())), preferred_element_type=F32) * scale
    l1p = jnp.log(1.0 + jnp.exp(-jnp.abs(z)))
    lb = -(jnp.maximum(z, 0.0) + l1p)
    if causal is not None:
        lb = jnp.where(causal, lb, 0.0)
    return z, lb, jnp.minimum(z, 0.0) - l1p


def _attn_fwd(proj, n_heads, blk):
    S = proj.shape[0]
    nq = S // blk
    scale = HEAD_DIM ** -0.5
    lanes = blk // 128
    u_incl = _tri(blk, lambda j, s: j >= s)

    def body(q_ref, k_ref, v_ref, u_ref, o_ref, tot_ref):
        i = pl.program_id(1)
        q = q_ref[...].astype(BF16)
        u = u_ref[...]

        def step(j, carry, masked):
            acc, run = carry
            ks = pl.multiple_of(j * blk, blk)
            kj = k_ref[pl.ds(ks, blk), :].astype(BF16)
            vj = v_ref[pl.ds(ks, blk), :].astype(BF16)
            causal = None
            if masked:
                causal = lax.broadcasted_iota(jnp.int32, (blk, blk), 1) < lax.broadcasted_iota(jnp.int32, (blk, blk), 0)
            z, lb, _ = _scores(q, kj, scale, causal)
            ct = _split_dot(lb, u)
            a = jnp.exp(z + ct[:, :blk] + jnp.tile(run, (1, lanes)))
            if masked:
                a = jnp.where(causal, a, 0.0)
            acc = acc + jnp.dot(a.astype(BF16), vj, preferred_element_type=F32)
            return acc, run + ct[:, blk:]

        zero = jnp.zeros((blk, HEAD_DIM), F32)
        carry = step(i, (zero, zero), True)
        carry = lax.fori_loop(0, i, lambda t, c: step(i - 1 - t, c, False), carry)
        o_ref[...] = carry[0]
        tot_ref[...] = carry[1]

    H = n_heads
    return pl.pallas_call(
        body, name="attn_fwd", grid=(H, nq),
        in_specs=[pl.BlockSpec((blk, HEAD_DIM), lambda h, i: (i, h)),
                  pl.BlockSpec((S, HEAD_DIM), lambda h, i: (0, H + h)),
                  pl.BlockSpec((S, HEAD_DIM), lambda h, i: (0, 2 * H + h)),
                  pl.BlockSpec(u_incl.shape, lambda h, i: (0, 0))],
        out_specs=[pl.BlockSpec((blk, HEAD_DIM), lambda h, i: (i, h))] * 2,
        out_shape=[_sds((S, H * HEAD_DIM), F32)] * 2,
        compiler_params=_params("parallel", "arbitrary"),
    )(proj, proj, proj, u_incl)


def _attn_bwd(proj, tot, do, n_heads, blk):
    S = proj.shape[0]
    nq = S // blk
    scale = HEAD_DIM ** -0.5
    lanes = blk // 128
    l_strict = _tri(blk, lambda j, s: j < s)
    l_incl = _tri(blk, lambda j, s: j <= s)

    def body(q_ref, k_ref, v_ref, tot_ref, do_ref, ls_ref, li_ref, dq_ref, dk_ref, dv_ref):
        i = pl.program_id(1)

        @pl.when(i == 0)
        def _():
            dk_ref[...] = jnp.zeros_like(dk_ref)
            dv_ref[...] = jnp.zeros_like(dv_ref)

        q = q_ref[...].astype(BF16)
        dob = do_ref[...].astype(BF16)
        total = jnp.tile(tot_ref[...], (1, lanes))
        ls = ls_ref[...]
        li = li_ref[...]

        def step(j, carry, masked):
            dq, run_lb, run_g = carry
            ks = pl.multiple_of(j * blk, blk)
            kj = k_ref[pl.ds(ks, blk), :].astype(BF16)
            vj = v_ref[pl.ds(ks, blk), :].astype(BF16)
            causal = None
            if masked:
                causal = lax.broadcasted_iota(jnp.int32, (blk, blk), 1) < lax.broadcasted_iota(jnp.int32, (blk, blk), 0)
            z, lb, log_beta = _scores(q, kj, scale, causal)
            beta = jnp.exp(log_beta)
            pt = _split_dot(lb, ls)
            a = jnp.exp(z + total - (pt[:, :blk] + jnp.tile(run_lb, (1, lanes))))
            if masked:
                a = jnp.where(causal, a, 0.0)
            da = lax.dot_general(dob, vj, (((1,), (1,)), ((), ())), preferred_element_type=F32)
            g = a * da
            gt = _split_dot(g, li)
            dz = g - beta * (gt[:, :blk] + jnp.tile(run_g, (1, lanes)))
            if masked:
                dz = jnp.where(causal, dz, 0.0)
            dzs = (dz * scale).astype(BF16)
            dq = dq + jnp.dot(dzs, kj, preferred_element_type=F32)
            dk_ref[pl.ds(ks, blk), :] += lax.dot_general(dzs, q, (((0,), (0,)), ((), ())), preferred_element_type=F32)
            dv_ref[pl.ds(ks, blk), :] += lax.dot_general(a.astype(BF16), dob, (((0,), (0,)), ((), ())),
                                                         preferred_element_type=F32)
            return dq, run_lb + pt[:, blk:], run_g + gt[:, blk:]

        zero = jnp.zeros((blk, HEAD_DIM), F32)
        carry = lax.fori_loop(0, i, lambda j, c: step(j, c, False), (zero, zero, zero))
        carry = step(i, carry, True)
        dq_ref[...] = carry[0].astype(BF16)

    H = n_heads
    AW = H * HEAD_DIM
    return pl.pallas_call(
        body, name="attn_bwd", grid=(H, nq),
        in_specs=[pl.BlockSpec((blk, HEAD_DIM), lambda h, i: (i, h)),
                  pl.BlockSpec((S, HEAD_DIM), lambda h, i: (0, H + h)),
                  pl.BlockSpec((S, HEAD_DIM), lambda h, i: (0, 2 * H + h)),
                  pl.BlockSpec((blk, HEAD_DIM), lambda h, i: (i, h)),
                  pl.BlockSpec((blk, HEAD_DIM), lambda h, i: (i, h)),
                  pl.BlockSpec(l_strict.shape, lambda h, i: (0, 0)),
                  pl.BlockSpec(l_incl.shape, lambda h, i: (0, 0))],
        out_specs=[pl.BlockSpec((blk, HEAD_DIM), lambda h, i: (i, h)),
                   pl.BlockSpec((S, HEAD_DIM), lambda h, i: (0, h)),
                   pl.BlockSpec((S, HEAD_DIM), lambda h, i: (0, h))],
        out_shape=[_sds((S, AW), BF16), _sds((S, AW), F32), _sds((S, AW), F32)],
        compiler_params=_params("parallel", "arbitrary"),
    )(proj, proj, proj, tot, do, l_strict, l_incl)


def _pool_count(r0, rows, w):
    t = r0 + lax.broadcasted_iota(jnp.int32, (rows, 1), 0)
    return jnp.minimum(t + 1, w).astype(F32)


def _pool_fwd(proj, col_blk, n_groups, width, chunk):
    S = proj.shape[0]
    H = POOL_HALO

    def body(u_ref, o_ref, pad_ref):
        g = pl.program_id(0)
        pad_ref[0:H, :] = jnp.zeros((H, width), F32)
        pad_ref[H:, :] = u_ref[...]
        for gi, w in enumerate(POOL_WINDOWS[:n_groups]):
            @pl.when(g == gi)
            def _(w=w):
                def one(c, _):
                    r0 = pl.multiple_of(c * chunk, chunk)
                    ext = pad_ref[pl.ds(r0, chunk + H), :]
                    s = ext
                    k = 1
                    while k < w:
                        s = s + pltpu.roll(s, k, 0)
                        k *= 2
                    o_ref[pl.ds(r0, chunk), :] = (s[H:] / _pool_count(r0, chunk, w) - ext[H:]).astype(BF16)
                    return 0

                lax.fori_loop(0, S // chunk, one, 0)

    return pl.pallas_call(
        body, name="pool_fwd", grid=(n_groups,),
        in_specs=[pl.BlockSpec((S, width), lambda g: (0, col_blk + g))],
        out_specs=pl.BlockSpec((S, width), lambda g: (0, g)), out_shape=_sds((S, n_groups * width), BF16),
        scratch_shapes=[pltpu.VMEM((S + H, width), F32)], compiler_params=_params("parallel"),
    )(proj)


def _pool_bwd(dpooled, n_groups, chunk):
    S = dpooled.shape[0]
    width = dpooled.shape[1] // n_groups
    H = POOL_HALO

    def body(d_ref, o_ref, pad_ref):
        g = pl.program_id(0)
        pad_ref[S:, :] = jnp.zeros((H, width), F32)
        for gi, w in enumerate(POOL_WINDOWS[:n_groups]):
            @pl.when(g == gi)
            def _(w=w):
                def fill(c, _):
                    r0 = pl.multiple_of(c * chunk, chunk)
                    pad_ref[pl.ds(r0, chunk), :] = d_ref[pl.ds(r0, chunk), :] / _pool_count(r0, chunk, w)
                    return 0

                lax.fori_loop(0, S // chunk, fill, 0)

                def one(c, _):
                    r0 = pl.multiple_of(c * chunk, chunk)
                    s = pad_ref[pl.ds(r0, chunk + H), :]
                    k = 1
                    while k < w:
                        s = s + pltpu.roll(s, chunk + H - k, 0)
                        k *= 2
                    o_ref[pl.ds(r0, chunk), :] = (s[:chunk] - d_ref[pl.ds(r0, chunk), :]).astype(BF16)
                    return 0

                lax.fori_loop(0, S // chunk, one, 0)

    return pl.pallas_call(
        body, name="pool_bwd", grid=(n_groups,),
        in_specs=[pl.BlockSpec((S, width), lambda g: (0, g))],
        out_specs=pl.BlockSpec((S, width), lambda g: (0, g)), out_shape=_sds((S, n_groups * width), BF16),
        scratch_shapes=[pltpu.VMEM((S + H, width), F32)], compiler_params=_params("parallel"),
    )(dpooled)


def _conv3(x_ext, w, b):
    return b + pltpu.roll(x_ext, 2, 0) * w[0:1, :] + pltpu.roll(x_ext, 1, 0) * w[1:2, :] + x_ext * w[2:3, :]


def _gelu_parts(x):
    th = jnp.tanh(GELU_C0 * (x + GELU_C1 * (x * x * x)))
    return th, 0.5 * (1.0 + th)


def _conv_specs(S, F, cb):
    nb = F // cb
    return [pl.BlockSpec((S, cb), lambda j: (0, j)), pl.BlockSpec((S, cb), lambda j: (0, nb + j)),
            pl.BlockSpec((3, cb), lambda j: (0, j)), pl.BlockSpec((3, cb), lambda j: (0, nb + j)),
            pl.BlockSpec((1, cb), lambda j: (0, j)), pl.BlockSpec((1, cb), lambda j: (0, nb + j))]


def _conv_fwd(upre, cw, cb_, chunk):
    S, F2 = upre.shape
    F = F2 // 2
    cb = 128
    H = CONV_HALO

    def body(g_ref, v_ref, wg_ref, wv_ref, bg_ref, bv_ref, o_ref, pg_ref, pv_ref):
        pg_ref[0:H, :] = jnp.zeros((H, cb), F32)
        pv_ref[0:H, :] = jnp.zeros((H, cb), F32)
        pg_ref[H:, :] = g_ref[...]
        pv_ref[H:, :] = v_ref[...]
        wg, wv, bg, bv = wg_ref[...], wv_ref[...], bg_ref[...], bv_ref[...]

        def one(c, _):
            r0 = pl.multiple_of(c * chunk, chunk)
            up_g = _conv3(pg_ref[pl.ds(r0, chunk + H), :], wg, bg)[H:]
            up_v = _conv3(pv_ref[pl.ds(r0, chunk + H), :], wv, bv)[H:]
            _, cdf = _gelu_parts(up_g)
            o_ref[pl.ds(r0, chunk), :] = (up_g * cdf * up_v).astype(BF16)
            return 0

        lax.fori_loop(0, S // chunk, one, 0)

    return pl.pallas_call(
        body, name="conv_fwd", grid=(F // cb,), in_specs=_conv_specs(S, F, cb),
        out_specs=pl.BlockSpec((S, cb), lambda j: (0, j)), out_shape=_sds((S, F), BF16),
        scratch_shapes=[pltpu.VMEM((S + H, cb), F32)] * 2, compiler_params=_params("parallel"),
    )(upre, upre, cw, cw, cb_, cb_)


def _conv_bwd(upre, dact, cw, cb_, chunk):
    S, F2 = upre.shape
    F = F2 // 2
    cb = 128
    H = CONV_HALO
    E = chunk + 2 * H

    def body(g_ref, v_ref, wg_ref, wv_ref, bg_ref, bv_ref, d_ref, dg_ref, dv_ref, dwg_ref, dwv_ref, dbg_ref, dbv_ref,
             pg_ref, pv_ref, pd_ref):
        for p, src in ((pg_ref, g_ref), (pv_ref, v_ref), (pd_ref, d_ref)):
            p[0:H, :] = jnp.zeros((H, cb), F32)
            p[H:S + H, :] = src[...]
            p[S + H:, :] = jnp.zeros((H, cb), F32)
        wg, wv, bg, bv = wg_ref[...], wv_ref[...], bg_ref[...], bv_ref[...]

        def taps_bwd(d, w):
            return d * w[2:3, :] + pltpu.roll(d, E - 1, 0) * w[1:2, :] + pltpu.roll(d, E - 2, 0) * w[0:1, :]

        def wsum(d, x):
            dc = d[H:H + chunk]
            return [jnp.sum(dc * pltpu.roll(x, 2, 0)[H:H + chunk], axis=0, keepdims=True),
                    jnp.sum(dc * pltpu.roll(x, 1, 0)[H:H + chunk], axis=0, keepdims=True),
                    jnp.sum(dc * x[H:H + chunk], axis=0, keepdims=True),
                    jnp.sum(dc, axis=0, keepdims=True)]

        def one(c, acc):
            r0 = pl.multiple_of(c * chunk, chunk)
            xg = pg_ref[pl.ds(r0, E), :]
            xv = pv_ref[pl.ds(r0, E), :]
            d = pd_ref[pl.ds(r0, E), :]
            up_g = _conv3(xg, wg, bg)
            up_v = _conv3(xv, wv, bv)
            th, cdf = _gelu_parts(up_g)
            dgelu = cdf + 0.5 * up_g * (1.0 - th * th) * (GELU_C0 * (1.0 + 3.0 * GELU_C1 * (up_g * up_g)))
            dgate = d * up_v * dgelu
            dval = d * (up_g * cdf)
            dg_ref[pl.ds(r0, chunk), :] = taps_bwd(dgate, wg)[H:H + chunk].astype(BF16)
            dv_ref[pl.ds(r0, chunk), :] = taps_bwd(dval, wv)[H:H + chunk].astype(BF16)
            return tuple(a + b for a, b in zip(acc, wsum(dgate, xg) + wsum(dval, xv)))

        zero = jnp.zeros((1, cb), F32)
        acc = lax.fori_loop(0, S // chunk, one, (zero,) * 8)
        dwg_ref[...] = jnp.concatenate(acc[0:3], axis=0)
        dbg_ref[...] = acc[3]
        dwv_ref[...] = jnp.concatenate(acc[4:7], axis=0)
        dbv_ref[...] = acc[7]

    col = lambda rows: pl.BlockSpec((rows, cb), lambda j: (0, j))
    return pl.pallas_call(
        body, name="conv_bwd", grid=(F // cb,), in_specs=_conv_specs(S, F, cb) + [col(S)],
        out_specs=[col(S), col(S), col(3), col(3), col(1), col(1)],
        out_shape=[_sds((S, F), BF16), _sds((S, F), BF16), _sds((3, F), F32), _sds((3, F), F32), _sds((1, F), F32),
                   _sds((1, F), F32)],
        scratch_shapes=[pltpu.VMEM((S + 2 * H, cb), F32)] * 3, compiler_params=_params("parallel"),
    )(upre, upre, cw, cw, cb_, cb_, dact)


def _position():
    x, y, c = lax.axis_index("x"), lax.axis_index("y"), lax.axis_index("c")
    return x, y, c, 4 * x + 2 * y + c


def _peer(x, y, c, d):
    px = 1 - x if d & 4 else x
    py = 1 - y if d & 2 else y
    pc = 1 - c if d & 1 else c
    return (px, py, pc), 4 * px + 2 * py + pc


def _all_gather(flat, small):
    def body(x_ref, s_ref, out_ref, sout_ref, send_sems, recv_sems, ssend, srecv, local_sems):
        x, y, c, me = _position()
        sibling = (x, y, 1 - c)
        chips = [(1 - x, y), (x, 1 - y), (1 - x, 1 - y)]

        def slot(px, py, pc):
            return out_ref.at[4 * px + 2 * py + pc]

        def copy(k, block, to, src=None):
            return pltpu.make_async_remote_copy(src_ref=slot(*block) if src is None else src, dst_ref=slot(*block),
                                                send_sem=send_sems.at[k], recv_sem=recv_sems.at[k], device_id=to,
                                                device_id_type=MESH)

        def small_copy(d):
            peer, peer_idx = _peer(x, y, c, d)
            return pltpu.make_async_remote_copy(src_ref=s_ref, dst_ref=sout_ref.at[me], send_sem=ssend.at[d - 1],
                                                recv_sem=srecv.at[d - 1], device_id=peer, device_id_type=MESH), peer_idx

        mine = pltpu.make_async_copy(x_ref, slot(x, y, c), local_sems.at[0])
        mine_small = pltpu.make_async_copy(s_ref, sout_ref.at[me], local_sems.at[1])
        mine.start()
        mine_small.start()
        first = [copy(0, (x, y, c), sibling, src=x_ref)]
        first += [copy(1 + j, (x, y, c), (*chip, c), src=x_ref) for j, chip in enumerate(chips)]
        for cp in first:
            cp.start()
        smalls = [small_copy(d) for d in range(1, N_DEV)]
        for cp, _ in smalls:
            cp.start()
        passed = [copy(4 + j, (*chip, c), sibling) for j, chip in enumerate(chips)]
        for j, chip in enumerate(chips):
            copy(1 + j, (*chip, c), (x, y, c)).wait_recv()
            passed[j].start()
        copy(0, sibling, (x, y, c)).wait_recv()
        for j, chip in enumerate(chips):
            copy(4 + j, (*chip, 1 - c), (x, y, c)).wait_recv()
        for d in range(1, N_DEV):
            _, peer_idx = _peer(x, y, c, d)
            pltpu.make_async_remote_copy(src_ref=s_ref, dst_ref=sout_ref.at[peer_idx], send_sem=ssend.at[d - 1],
                                         recv_sem=srecv.at[d - 1], device_id=(x, y, c), device_id_type=MESH).wait_recv()
        for cp in first + passed:
            cp.wait_send()
        for cp, _ in smalls:
            cp.wait_send()
        mine.wait()
        mine_small.wait()

    return pl.pallas_call(
        body, name="all_gather", in_specs=[ANY, ANY], out_specs=[ANY, ANY],
        out_shape=[_sds((N_DEV,) + flat.shape, flat.dtype), _sds((N_DEV,) + small.shape, small.dtype)],
        scratch_shapes=[pltpu.SemaphoreType.DMA((7,)), pltpu.SemaphoreType.DMA((7,)), pltpu.SemaphoreType.DMA((7,)),
                        pltpu.SemaphoreType.DMA((7,)), pltpu.SemaphoreType.DMA((2,))],
        compiler_params=pltpu.CompilerParams(has_side_effects=True),
    )(flat, small)


def _reduce_scatter_exchange(packed):
    def body(p_ref, r_ref, send_sems, recv_sems, local_sem):
        x, y, c, me = _position()
        mine = pltpu.make_async_copy(p_ref.at[me], r_ref.at[me], local_sem)
        mine.start()
        sends = []
        for d in range(1, N_DEV):
            peer, peer_idx = _peer(x, y, c, d)
            cp = pltpu.make_async_remote_copy(src_ref=p_ref.at[peer_idx], dst_ref=r_ref.at[me], send_sem=send_sems.at[d - 1],
                                              recv_sem=recv_sems.at[d - 1], device_id=peer, device_id_type=MESH)
            cp.start()
            sends.append(cp)
        for d in range(1, N_DEV):
            _, peer_idx = _peer(x, y, c, d)
            pltpu.make_async_remote_copy(src_ref=p_ref.at[me], dst_ref=r_ref.at[peer_idx], send_sem=send_sems.at[d - 1],
                                         recv_sem=recv_sems.at[d - 1], device_id=(x, y, c), device_id_type=MESH).wait_recv()
        for cp in sends:
            cp.wait_send()
        mine.wait()

    return pl.pallas_call(
        body, name="reduce_scatter_exchange", in_specs=[ANY], out_specs=ANY, out_shape=_sds(packed.shape, packed.dtype),
        scratch_shapes=[pltpu.SemaphoreType.DMA((7,)), pltpu.SemaphoreType.DMA((7,)), pltpu.SemaphoreType.DMA(())],
        compiler_params=pltpu.CompilerParams(has_side_effects=True),
    )(packed)


def _sum_blocks(recv):
    n, R, W = recv.shape
    tr = _tile(R, 256, 16)

    def body(r_ref, o_ref):
        acc = r_ref[0].astype(F32)
        for i in range(1, n):
            acc = acc + r_ref[i].astype(F32)
        o_ref[...] = acc

    return pl.pallas_call(
        body, name="sum_blocks", grid=(R // tr,), in_specs=[pl.BlockSpec((n, tr, W), lambda i: (0, i, 0))],
        out_specs=pl.BlockSpec((tr, W), lambda i: (i, 0)), out_shape=_sds((R, W), F32), compiler_params=_params("parallel"),
    )(recv)


def _all_reduce_small(part):
    r, W = part.shape

    def body(p_ref, o_ref, g_ref, send_sems, recv_sems):
        x, y, c, me = _position()
        sends = []
        for d in range(1, N_DEV):
            peer, _ = _peer(x, y, c, d)
            cp = pltpu.make_async_remote_copy(src_ref=p_ref, dst_ref=g_ref.at[me], send_sem=send_sems.at[d - 1],
                                              recv_sem=recv_sems.at[d - 1], device_id=peer, device_id_type=MESH)
            cp.start()
            sends.append(cp)
        g_ref[me] = p_ref[...]
        for d in range(1, N_DEV):
            _, peer_idx = _peer(x, y, c, d)
            pltpu.make_async_remote_copy(src_ref=p_ref, dst_ref=g_ref.at[peer_idx], send_sem=send_sems.at[d - 1],
                                         recv_sem=recv_sems.at[d - 1], device_id=(x, y, c), device_id_type=MESH).wait_recv()
        for cp in sends:
            cp.wait_send()
        acc = g_ref[0]
        for i in range(1, N_DEV):
            acc = acc + g_ref[i]
        o_ref[...] = acc

    vmem = pl.BlockSpec(memory_space=pltpu.VMEM)
    return pl.pallas_call(
        body, name="all_reduce_small", in_specs=[vmem], out_specs=[vmem, vmem],
        out_shape=[_sds((r, W), F32), _sds((N_DEV, r, W), F32)],
        scratch_shapes=[pltpu.SemaphoreType.DMA((7,)), pltpu.SemaphoreType.DMA((7,))],
        compiler_params=pltpu.CompilerParams(has_side_effects=True, vmem_limit_bytes=VMEM_LIMIT),
    )(part)[0]


def _adamw(w, g, m, v, name):
    rows, cols = w.shape
    tr = _tile(rows, max(8, (2**18 // cols) // 8 * 8), 8)

    def body(w_ref, g_ref, m_ref, v_ref, d_ref, nm_ref, nv_ref):
        g_ = g_ref[...]
        m_ = ADAM_B1 * m_ref[...] + (1.0 - ADAM_B1) * g_
        v_ = ADAM_B2 * v_ref[...] + (1.0 - ADAM_B2) * (g_ * g_)
        m_hat = m_ / (1.0 - ADAM_B1 ** ADAM_STEP)
        v_hat = v_ / (1.0 - ADAM_B2 ** ADAM_STEP)
        d_ref[...] = -ADAM_LR * (m_hat / (jnp.sqrt(v_hat) + ADAM_EPS) + ADAM_WD * w_ref[...])
        nm_ref[...] = m_
        nv_ref[...] = v_

    spec = pl.BlockSpec((tr, cols), lambda i: (i, 0))
    return pl.pallas_call(
        body, name=name, grid=(rows // tr,), in_specs=[spec] * 4, out_specs=[spec] * 3,
        out_shape=[_sds((rows, cols), F32)] * 3, compiler_params=_params("parallel"),
    )(w, g, m, v)


SHARDED = (("w_in", 1), ("w_attn_branch", 1), ("w_pool_group", 1), ("w_pool_branch", 1), ("w_out", 0), ("w_up", 1),
           ("w_down", 0), ("w_ple", 1), ("w_ple_gate", 0))
REPLICATED = ("norm_mix_pre", "pool_scale", "norm_mix_post", "norm_ffn_pre", "conv_b", "norm_ffn_post", "norm_ple_post")
WEIGHTS = ("norm_mix_pre", "w_in", "w_attn_branch", "w_pool_group", "pool_scale", "w_pool_branch", "w_out", "norm_mix_post",
           "norm_ffn_pre", "w_up", "conv_w", "conv_b", "w_down", "norm_ffn_post", "w_ple", "w_ple_gate", "norm_ple_post")


def _size(shape):
    n = 1
    for s in shape:
        n *= s
    return n


def _pad_rows(flat, row_align):
    n = flat.shape[-1]
    per = PACK_W * row_align
    total = -(-n // per) * per
    pad = [(0, 0)] * (flat.ndim - 1) + [(0, total - n)]
    return jnp.pad(flat, pad).reshape(flat.shape[:-1] + (total // PACK_W, PACK_W))


def _to_shards(full, axis):
    shp = full.shape
    split = full.reshape(shp[:axis] + (N_DEV, shp[axis] // N_DEV) + shp[axis + 1:])
    return jnp.moveaxis(split, axis, 0)


def _from_shards(shards, axis):
    moved = jnp.moveaxis(shards, 0, axis)
    shp = moved.shape
    return moved.reshape(shp[:axis] + (shp[axis] * shp[axis + 1],) + shp[axis + 2:])


def kernel(x, p, norm_mix_pre, w_in, w_attn_branch, w_pool_group, pool_scale, w_pool_branch, w_out, norm_mix_post, norm_ffn_pre, w_up, conv_w, conv_b, w_down, norm_ffn_post, w_ple, w_ple_gate, norm_ple_post, loss_target, m_norm_mix_pre, m_w_in, m_w_attn_branch, m_w_pool_group, m_pool_scale, m_w_pool_branch, m_w_out, m_norm_mix_post, m_norm_ffn_pre, m_w_up, m_conv_w, m_conv_b, m_w_down, m_norm_ffn_post, m_w_ple, m_w_ple_gate, m_norm_ple_post, v_norm_mix_pre, v_w_in, v_w_attn_branch, v_w_pool_group, v_pool_scale, v_w_pool_branch, v_w_out, v_norm_mix_post, v_norm_ffn_pre, v_w_up, v_conv_w, v_conv_b, v_w_down, v_norm_ffn_post, v_w_ple, v_w_ple_gate, v_norm_ple_post):
    given = dict(locals())
    wts = {n: given[n][0] for n in WEIGHTS}
    mom = {n: given["m_" + n][0] for n in WEIGHTS}
    var = {n: given["v_" + n][0] for n in WEIGHTS}
    xs = x[0]
    ps_in = p[0, 0]
    tgt = loss_target[0]
    S, D = xs.shape
    AW = wts["w_attn_branch"].shape[0]
    PW = wts["w_pool_branch"].shape[0]
    G = wts["w_pool_group"].shape[0]
    PGW = PW // G
    H = AW // HEAD_DIM
    F = wts["w_down"].shape[0] * N_DEV
    assert (3 * AW) % PGW == 0 and (3 * AW + PW) % D == 0 and PGW % 128 == 0 and F % 128 == 0
    tr = _tile(S, 256, 16)
    blk = _tile(S, 256, 128)
    chunk = _tile(S, 256, 8)
    me = 4 * lax.axis_index("x") + 2 * lax.axis_index("y") + lax.axis_index("c")

    shard_shapes = [wts[n].shape for n, _ in SHARDED]
    sizes = [_size(s) for s in shard_shapes]
    flat = _pad_rows(jnp.concatenate([wts[n].astype(BF16).reshape(-1) for n, _ in SHARDED]), 16)
    cw_small = _pad_rows(wts["conv_w"].reshape(-1), 8)
    gathered, cw_all = _all_gather(flat, cw_small)
    gflat = gathered.reshape(N_DEV, -1)
    full = {}
    off = 0
    for (n, axis), shp, sz in zip(SHARDED, shard_shapes, sizes):
        full[n] = _from_shards(gflat[:, off:off + sz].reshape((N_DEV,) + shp), axis)
        off += sz
    cw_shape = wts["conv_w"].shape
    conv_w_full = _from_shards(cw_all.reshape(N_DEV, -1)[:, :_size(cw_shape)].reshape((N_DEV,) + cw_shape), 1)
    conv_b_row = wts["conv_b"].reshape(1, -1)
    g1, g2, g3, g4, g5 = (wts[n].reshape(1, D) for n in
                          ("norm_mix_pre", "norm_mix_post", "norm_ffn_pre", "norm_ffn_post", "norm_ple_post"))
    pscale = wts["pool_scale"].reshape(1, PW)

    h = _rms_fwd(xs, g1, tr)
    proj = _mm(h, full["w_in"], name="mm_in")
    attn, tot = _attn_fwd(proj, H, blk)
    y_attn = _mm(attn, full["w_attn_branch"], name="mm_attn_branch")
    pooled = _pool_fwd(proj, 3 * AW // PGW, G, PGW, chunk)
    pg, ps = _pool_group_fwd(pooled, full["w_pool_group"], pscale)
    y_pool = _mm(ps, full["w_pool_branch"], name="mm_pool_branch")
    gate_cb = (3 * AW + PW) // D
    mixed = _gate_mix(proj, y_attn, y_pool, gate_cb, tr)
    mo = _mm(mixed, full["w_out"], name="mm_out")
    x1, h2 = _resid_rms2(xs, mo, g2, g3, tr)
    upre = _mm(h2, full["w_up"], name="mm_up")
    act = _conv_fwd(upre, conv_w_full, conv_b_row, chunk)
    yf = _mm(act, full["w_down"], name="mm_down")
    x2, x2b = _resid_rms(x1, yf, g4, tr)
    e = _mm(ps_in, full["w_ple"], name="mm_ple")
    gl = _mm(x2b, full["w_ple_gate"], name="mm_ple_gate")
    loss_part, dx3, de, dgl, dg5 = _ple_loss(gl, e, x2, tgt, g5, tr)

    grads = {}
    grads["w_ple"] = _mm(ps_in, de, ta=True, name="mm_d_w_ple")
    grads["w_ple_gate"] = _mm(x2b, dgl, ta=True, name="mm_d_w_ple_gate")
    dx2g = _mm(dgl, full["w_ple_gate"], tb=True, name="mm_d_x2")
    dx2, dyf, dg4 = _rms_bwd_a(dx3, dx2g, yf, g4, tr)
    dact = _mm(dyf, full["w_down"], tb=True, name="mm_d_act")
    grads["w_down"] = _mm(act, dyf, ta=True, name="mm_d_w_down")
    dup_g, dup_v, dcw_g, dcw_v, dcb_g, dcb_v = _conv_bwd(upre, dact, conv_w_full, conv_b_row, chunk)
    dupre = jnp.concatenate([dup_g, dup_v], axis=1)
    dh2 = _mm(dupre, full["w_up"], tb=True, name="mm_d_h2")
    grads["w_up"] = _mm(h2, dupre, ta=True, name="mm_d_w_up")
    dx1, dmo, dg3, dg2 = _rms_bwd_b(dx2, dh2, x1, g3, mo, g2, tr)
    dmixed = _mm(dmo, full["w_out"], tb=True, name="mm_d_mixed")
    grads["w_out"] = _mm(mixed, dmo, ta=True, name="mm_d_w_out")
    dya, dyp, dga, dgp = _gate_bwd(dmixed, proj, y_attn, y_pool, gate_cb, tr)
    dps = _mm(dyp, full["w_pool_branch"], tb=True, name="mm_d_ps")
    grads["w_pool_branch"] = _mm(ps, dyp, ta=True, name="mm_d_w_pool_branch")
    dpg, dscale = _scale_bwd(dps, pg, pscale, tr)
    dpooled = _pool_group_bwd_x(dpg, full["w_pool_group"])
    grads["w_pool_group"] = _pool_group_bwd_w(pooled, dpg, G)
    du = _pool_bwd(dpooled, G, chunk)
    dattn = _mm(dya, full["w_attn_branch"], tb=True, name="mm_d_attn")
    grads["w_attn_branch"] = _mm(attn, dya, ta=True, name="mm_d_w_attn_branch")
    dq, dk, dv = _attn_bwd(proj, tot, dattn, H, blk)
    dproj = jnp.concatenate([dq, dk.astype(BF16), dv.astype(BF16), du, dga, dgp], axis=1)
    dh = _mm(dproj, full["w_in"], tb=True, name="mm_d_h")
    grads["w_in"] = _mm(h, dproj, ta=True, name="mm_d_w_in")
    grad_x, dg1 = _rms_bwd_c(dx1, dh, xs, g1, tr)

    packed = _pad_rows(jnp.concatenate(
        [_to_shards(grads[n], axis).astype(BF16).reshape(N_DEV, -1) for n, axis in SHARDED], axis=1), 16)
    gsum = _sum_blocks(_reduce_scatter_exchange(packed)).reshape(-1)
    gshard = {}
    off = 0
    for (n, _), shp, sz in zip(SHARDED, shard_shapes, sizes):
        gshard[n] = gsum[off:off + sz].reshape(shp)
        off += sz

    dconv_w = jnp.concatenate([dcw_g, dcw_v], axis=1)
    dconv_b = jnp.concatenate([dcb_g, dcb_v], axis=1).reshape(-1)
    rep_parts = {"norm_mix_pre": dg1, "pool_scale": dscale, "norm_mix_post": dg2, "norm_ffn_pre": dg3, "conv_b": dconv_b,
                 "norm_ffn_post": dg4, "norm_ple_post": dg5}
    small = jnp.concatenate([rep_parts[n].reshape(-1) for n in REPLICATED] + [dconv_w.reshape(-1)])
    n_small = small.shape[0]
    small_sum = _all_reduce_small(_pad_rows(small, 8)).reshape(-1)[:n_small]
    off = 0
    for n in REPLICATED:
        sz = _size(wts[n].shape)
        gshard[n] = small_sum[off:off + sz].reshape(wts[n].shape)
        off += sz
    dconv_w_sum = small_sum[off:off + 3 * 2 * F].reshape(3, 2 * F)
    gshard["conv_w"] = lax.dynamic_slice_in_dim(dconv_w_sum, me * cw_shape[1], cw_shape[1], axis=1)

    delta, new_m, new_v = {}, {}, {}
    for n, _ in SHARDED + (("conv_w", 1),):
        shp = wts[n].shape
        two_d = (_size(shp[:-1]), shp[-1])
        d_, m_, v_ = _adamw(wts[n].reshape(two_d), gshard[n].reshape(two_d), mom[n].reshape(two_d), var[n].reshape(two_d),
                            "adamw_" + n)
        delta[n], new_m[n], new_v[n] = d_.reshape(shp), m_.reshape(shp), v_.reshape(shp)
    rep_sizes = [_size(wts[n].shape) for n in REPLICATED]
    n_rep = sum(rep_sizes)
    cat = lambda t: _pad_rows(jnp.concatenate([t[n].reshape(-1) for n in REPLICATED]), 8)
    d_, m_, v_ = _adamw(cat(wts), cat(gshard), cat(mom), cat(var), "adamw_replicated")
    off = 0
    for n, sz in zip(REPLICATED, rep_sizes):
        shp = wts[n].shape
        delta[n], new_m[n], new_v[n] = (t.reshape(-1)[off:off + sz].reshape(shp) for t in (d_, m_, v_))
        off += sz
    assert off == n_rep

    loss = lax.psum(loss_part[0, 0], ("x", "y", "c"))
    lead = lambda t: t[None]
    return (loss, grad_x[None], *[lead(gshard[n]) for n in WEIGHTS], *[lead(delta[n]) for n in WEIGHTS],
            *[lead(new_m[n]) for n in WEIGHTS], *[lead(new_v[n]) for n in WEIGHTS])
```

```python
import functools

import jax
import jax.numpy as jnp
from jax import lax
from jax.experimental import pallas as pl
from jax.experimental.pallas import tpu as pltpu

F32 = jnp.float32
BF16 = jnp.bfloat16
MESH = pl.DeviceIdType.MESH

EPS = 1e-6
HEAD_DIM = 128
POOL_WINDOWS = (2, 4, 8, 16)
POOL_HALO = 16
CONV_HALO = 8
GELU_C0 = 0.7978845608028654
GELU_C1 = 0.044715
ADAM_LR = 0.001
ADAM_B1 = 0.9
ADAM_B2 = 0.999
ADAM_EPS = 1e-08
ADAM_WD = 0.01
ADAM_STEP = 10
N_DEV = 8
PACK_W = 1024
VMEM_LIMIT = 56 * 2**20
ANY = pl.BlockSpec(memory_space=pl.ANY)


def _params(*sem):
    return pltpu.CompilerParams(dimension_semantics=sem, vmem_limit_bytes=VMEM_LIMIT)


def _sds(shape, dtype):
    return jax.ShapeDtypeStruct(shape, dtype)


def _tile(dim, target, align):
    if dim <= target:
        return dim
    t = (target // align) * align
    while t >= align:
        if dim % t == 0:
            return t
        t -= align
    return dim


def _sigmoid(x):
    return 1.0 / (1.0 + jnp.exp(-x))


class _Comm:
    def __init__(self, ins, outs, sems, start, finish):
        self.ins, self.outs, self.sems, self.start, self.finish = list(ins), list(outs), list(sems), start, finish


def _call(body, *, name, grid, in_specs, out_specs, out_shape, args, scratch=(), sem=(), comm=None):
    in_specs, out_specs, out_shape, scratch = list(in_specs), list(out_specs), list(out_shape), list(scratch)
    if comm is None:
        return pl.pallas_call(body, name=name, grid=grid, in_specs=in_specs, out_specs=out_specs, out_shape=out_shape,
                              scratch_shapes=scratch, compiler_params=_params(*sem))(*args)
    n_in, n_out, n_scr, n_ci, n_co = len(in_specs), len(out_specs), len(scratch), len(comm.ins), len(comm.outs)

    def wrapped(*refs):
        ins, refs = refs[:n_in], refs[n_in:]
        c_ins, refs = refs[:n_ci], refs[n_ci:]
        outs, refs = refs[:n_out], refs[n_out:]
        c_outs, refs = refs[:n_co], refs[n_co:]
        scr, c_sems = refs[:n_scr], refs[n_scr:]
        first = last = None
        for axis, size in enumerate(grid):
            at_start, at_end = pl.program_id(axis) == 0, pl.program_id(axis) == size - 1
            first = at_start if first is None else jnp.logical_and(first, at_start)
            last = at_end if last is None else jnp.logical_and(last, at_end)
        if grid:
            pl.when(first)(lambda: comm.start(c_ins, c_outs, c_sems))
            body(*ins, *outs, *scr)
            pl.when(last)(lambda: comm.finish(c_ins, c_outs, c_sems))
        else:
            comm.start(c_ins, c_outs, c_sems)
            body(*ins, *outs, *scr)
            comm.finish(c_ins, c_outs, c_sems)

    return pl.pallas_call(
        wrapped, name=name, grid=grid, in_specs=in_specs + [ANY] * n_ci, out_specs=out_specs + [ANY] * n_co,
        out_shape=out_shape + comm.outs, scratch_shapes=scratch + comm.sems,
        compiler_params=pltpu.CompilerParams(dimension_semantics=("arbitrary",) * len(grid), vmem_limit_bytes=VMEM_LIMIT,
                                             has_side_effects=True),
    )(*args, *comm.ins)


def _mm(a, b, *, ta=False, tb=False, b_sm=False, out_sm=False, out_dtype=F32, name, tm=1024, tn=1024, tk=512, comm=None):
    M, K = (a.shape[1], a.shape[0]) if ta else a.shape
    if b_sm:
        n_sl, rows, per = b.shape
        N = rows if tb else n_sl * per
        assert K == (n_sl * per if tb else rows)
    else:
        N = b.shape[0] if tb else b.shape[1]
    tm = _tile(M, tm, 128)
    tn = _tile(per if (b_sm and not tb) else N // N_DEV if out_sm else N, tn, 128)
    tk = _tile(per if (b_sm and tb) else K, tk, 128)
    nk = K // tk
    a_spec = pl.BlockSpec((tk, tm), lambda i, j, k: (k, i)) if ta else pl.BlockSpec((tm, tk), lambda i, j, k: (i, k))
    if not b_sm:
        b_spec = pl.BlockSpec((tn, tk), lambda i, j, k: (j, k)) if tb else pl.BlockSpec((tk, tn), lambda i, j, k: (k, j))
    elif tb:
        kp = per // tk
        b_spec = pl.BlockSpec((None, tn, tk), lambda i, j, k: (k // kp, j, k % kp))
    else:
        jp = per // tn
        b_spec = pl.BlockSpec((None, tk, tn), lambda i, j, k: (j // jp, k, j % jp))
    if out_sm:
        jo = (N // N_DEV) // tn
        o_spec = pl.BlockSpec((None, tm, tn), lambda i, j, k: (j // jo, i, j % jo))
        o_shape = _sds((N_DEV, M, N // N_DEV), out_dtype)
    else:
        o_spec = pl.BlockSpec((tm, tn), lambda i, j, k: (i, j))
        o_shape = _sds((M, N), out_dtype)
    dims = (((0 if ta else 1,), (1 if tb else 0,)), ((), ()))

    def body(a_ref, b_ref, o_ref, acc_ref):
        k = pl.program_id(2)

        @pl.when(k == 0)
        def _():
            acc_ref[...] = jnp.zeros_like(acc_ref)

        acc_ref[...] += lax.dot_general(a_ref[...].astype(BF16), b_ref[...].astype(BF16), dims,
                                        preferred_element_type=F32)

        @pl.when(k == nk - 1)
        def _():
            o_ref[...] = acc_ref[...].astype(o_ref.dtype)

    res = _call(body, name=name, grid=(M // tm, N // tn, nk), in_specs=[a_spec, b_spec], out_specs=[o_spec],
                out_shape=[o_shape], args=(a, b), scratch=[pltpu.VMEM((tm, tn), F32)],
                sem=("parallel", "parallel", "arbitrary"), comm=comm)
    return res[0] if comm is None else res


def _pool_group_fwd(pooled, w_pg, scale):
    S = pooled.shape[0]
    G, C, C2 = w_pg.shape
    tm = _tile(S, 1024, 16)

    def body(a_ref, w_ref, s_ref, pg_ref, ps_ref):
        pg = jnp.dot(a_ref[...], w_ref[...], preferred_element_type=F32)
        pg_ref[...] = pg
        ps_ref[...] = (pg * s_ref[...]).astype(BF16)

    return pl.pallas_call(
        body, name="pool_group_fwd", grid=(G, S // tm),
        in_specs=[pl.BlockSpec((tm, C), lambda g, i: (i, g)), pl.BlockSpec((None, C, C2), lambda g, i: (g, 0, 0)),
                  pl.BlockSpec((1, C2), lambda g, i: (0, g))],
        out_specs=[pl.BlockSpec((tm, C2), lambda g, i: (i, g)), pl.BlockSpec((tm, C2), lambda g, i: (i, g))],
        out_shape=[jax.ShapeDtypeStruct((S, G * C2), F32), jax.ShapeDtypeStruct((S, G * C2), BF16)],
        compiler_params=_params("parallel", "parallel"),
    )(pooled, w_pg, scale)


def _pool_group_bwd_x(dpg, w_pg):
    S = dpg.shape[0]
    G, C, C2 = w_pg.shape
    tm = _tile(S, 1024, 16)

    def body(d_ref, w_ref, o_ref):
        o_ref[...] = lax.dot_general(d_ref[...], w_ref[...], (((1,), (1,)), ((), ())), preferred_element_type=F32)

    return pl.pallas_call(
        body, name="pool_group_bwd_x", grid=(G, S // tm),
        in_specs=[pl.BlockSpec((tm, C2), lambda g, i: (i, g)), pl.BlockSpec((None, C, C2), lambda g, i: (g, 0, 0))],
        out_specs=pl.BlockSpec((tm, C), lambda g, i: (i, g)), out_shape=jax.ShapeDtypeStruct((S, G * C), F32),
        compiler_params=_params("parallel", "parallel"),
    )(dpg, w_pg)


def _pool_group_bwd_w(pooled, dpg, G):
    S = pooled.shape[0]
    C, C2 = pooled.shape[1] // G, dpg.shape[1] // G
    tk = _tile(S, 1024, 16)

    def body(a_ref, d_ref, o_ref):
        @pl.when(pl.program_id(1) == 0)
        def _():
            o_ref[...] = jnp.zeros_like(o_ref)

        o_ref[...] += lax.dot_general(a_ref[...], d_ref[...], (((0,), (0,)), ((), ())), preferred_element_type=F32)

    return pl.pallas_call(
        body, name="pool_group_bwd_w", grid=(G, S // tk),
        in_specs=[pl.BlockSpec((tk, C), lambda g, k: (k, g)), pl.BlockSpec((tk, C2), lambda g, k: (k, g))],
        out_specs=pl.BlockSpec((None, C, C2), lambda g, k: (g, 0, 0)), out_shape=jax.ShapeDtypeStruct((G, C, C2), F32),
        compiler_params=_params("parallel", "arbitrary"),
    )(pooled, dpg)


def _rms(x, gain):
    r = lax.rsqrt(jnp.mean(x * x, axis=-1, keepdims=True) + EPS)
    return x * r * gain


def _rms_bwd(x, gain, dy):
    r = lax.rsqrt(jnp.mean(x * x, axis=-1, keepdims=True) + EPS)
    xh = x * r
    dgain = jnp.sum(dy * xh, axis=0, keepdims=True)
    dxh = dy * gain
    dx = r * (dxh - xh * jnp.mean(dxh * xh, axis=-1, keepdims=True))
    return dx, dgain


def _row_call(body, name, ins, outs, tr, *, comm=None):
    S = None
    in_specs, args = [], []
    for it in ins:
        arr, kind = it[0], it[1]
        if kind == "row":
            S = arr.shape[0]
            if len(it) == 4:
                width, cb = it[2], it[3]
                in_specs.append(pl.BlockSpec((tr, width), functools.partial(lambda i, cb: (i, cb), cb=cb)))
            else:
                in_specs.append(pl.BlockSpec((tr, arr.shape[1]), lambda i: (i, 0)))
        else:
            assert arr.ndim == 2
            in_specs.append(pl.BlockSpec(arr.shape, lambda i: (0, 0)))
        args.append(arr)
    out_specs, out_shape = [], []
    for sds, kind in outs:
        if kind == "row":
            out_specs.append(pl.BlockSpec((tr, sds.shape[1]), lambda i: (i, 0)))
        else:
            assert len(sds.shape) == 2
            out_specs.append(pl.BlockSpec(sds.shape, lambda i: (0, 0)))
        out_shape.append(sds)
    return _call(body, name=name, grid=(S // tr,), in_specs=in_specs, out_specs=out_specs, out_shape=out_shape, args=args,
                 sem=("arbitrary",), comm=comm)


def _first_step_zero(*refs):
    @pl.when(pl.program_id(0) == 0)
    def _():
        for r in refs:
            r[...] = jnp.zeros_like(r)


def _rms_fwd(x, gain, tr, comm):
    def body(x_ref, g_ref, o_ref):
        o_ref[...] = _rms(x_ref[...], g_ref[...]).astype(BF16)

    S, D = x.shape
    return _row_call(body, "rms_fwd", [(x, "row"), (gain, "full")], [(_sds((S, D), BF16), "row")], tr, comm=comm)


def _gate_mix(proj, ya, yp, gate_cb, tr):
    S, D = ya.shape

    def body(ga_ref, gp_ref, ya_ref, yp_ref, o_ref):
        o_ref[...] = (_sigmoid(ga_ref[...]) * ya_ref[...] + _sigmoid(gp_ref[...]) * yp_ref[...]).astype(BF16)

    return _row_call(body, "gate_mix", [(proj, "row", D, gate_cb), (proj, "row", D, gate_cb + 1), (ya, "row"), (yp, "row")],
                     [(_sds((S, D), BF16), "row")], tr)[0]


def _resid_rms2(x, mo, g2, g3, tr):
    S, D = x.shape

    def body(x_ref, mo_ref, g2_ref, g3_ref, x1_ref, h2_ref):
        x1 = x_ref[...] + _rms(mo_ref[...], g2_ref[...])
        x1_ref[...] = x1
        h2_ref[...] = _rms(x1, g3_ref[...]).astype(BF16)

    return _row_call(body, "resid_rms2", [(x, "row"), (mo, "row"), (g2, "full"), (g3, "full")],
                     [(_sds((S, D), F32), "row"), (_sds((S, D), BF16), "row")], tr)


def _resid_rms(x1, yf, g4, tr):
    S, D = x1.shape

    def body(x_ref, y_ref, g_ref, o_ref, ob_ref):
        x2 = x_ref[...] + _rms(y_ref[...], g_ref[...])
        o_ref[...] = x2
        ob_ref[...] = x2.astype(BF16)

    return _row_call(body, "resid_rms", [(x1, "row"), (yf, "row"), (g4, "full")],
                     [(_sds((S, D), F32), "row"), (_sds((S, D), BF16), "row")], tr)


def _ple_loss(gl, e, x2, tgt, g5, tr):
    S, D = x2.shape

    def body(gl_ref, e_ref, x2_ref, t_ref, g_ref, loss_ref, dx3_ref, de_ref, dgl_ref, dg_ref):
        _first_step_zero(loss_ref, dg_ref)
        s = _sigmoid(gl_ref[...])
        e_ = e_ref[...]
        t = s * e_
        gain = g_ref[...]
        err = x2_ref[...] + _rms(t, gain) - t_ref[...]
        row_loss = jnp.mean(err * err, axis=-1, keepdims=True)
        loss_ref[...] += 0.5 * jnp.sum(row_loss, axis=0, keepdims=True)
        dx3 = err * (1.0 / D)
        dx3_ref[...] = dx3
        dt, dgain = _rms_bwd(t, gain, dx3)
        dg_ref[...] += dgain
        de_ref[...] = (dt * s).astype(BF16)
        dgl_ref[...] = (dt * e_ * s * (1.0 - s)).astype(BF16)

    return _row_call(body, "ple_loss", [(gl, "row"), (e, "row"), (x2, "row"), (tgt, "row"), (g5, "full")],
                     [(_sds((1, 1), F32), "acc"), (_sds((S, D), F32), "row"), (_sds((S, D), BF16), "row"),
                      (_sds((S, D), BF16), "row"), (_sds((1, D), F32), "acc")], tr)


def _rms_bwd_a(dx3, dx2g, yf, g4, tr):
    S, D = yf.shape

    def body(a_ref, b_ref, y_ref, g_ref, dx_ref, dy_ref, dg_ref):
        _first_step_zero(dg_ref)
        dx2 = a_ref[...] + b_ref[...]
        dx_ref[...] = dx2
        dy, dgain = _rms_bwd(y_ref[...], g_ref[...], dx2)
        dy_ref[...] = dy.astype(BF16)
        dg_ref[...] += dgain

    return _row_call(body, "rms_bwd_a", [(dx3, "row"), (dx2g, "row"), (yf, "row"), (g4, "full")],
                     [(_sds((S, D), F32), "row"), (_sds((S, D), BF16), "row"), (_sds((1, D), F32), "acc")], tr)


def _rms_bwd_b(dx2, dh2, x1, g3, mo, g2, tr):
    S, D = x1.shape

    def body(dx2_ref, dh2_ref, x1_ref, g3_ref, mo_ref, g2_ref, dx1_ref, dmo_ref, dg3_ref, dg2_ref):
        _first_step_zero(dg3_ref, dg2_ref)
        d, dgain3 = _rms_bwd(x1_ref[...], g3_ref[...], dh2_ref[...])
        dx1 = dx2_ref[...] + d
        dx1_ref[...] = dx1
        dg3_ref[...] += dgain3
        dmo, dgain2 = _rms_bwd(mo_ref[...], g2_ref[...], dx1)
        dmo_ref[...] = dmo.astype(BF16)
        dg2_ref[...] += dgain2

    return _row_call(body, "rms_bwd_b", [(dx2, "row"), (dh2, "row"), (x1, "row"), (g3, "full"), (mo, "row"), (g2, "full")],
                     [(_sds((S, D), F32), "row"), (_sds((S, D), BF16), "row"), (_sds((1, D), F32), "acc"),
                      (_sds((1, D), F32), "acc")], tr)


def _rms_bwd_c(dx1, dh, x, g1, tr):
    S, D = x.shape

    def body(dx1_ref, dh_ref, x_ref, g_ref, o_ref, dg_ref):
        _first_step_zero(dg_ref)
        d, dgain = _rms_bwd(x_ref[...], g_ref[...], dh_ref[...])
        o_ref[...] = dx1_ref[...] + d
        dg_ref[...] += dgain

    return _row_call(body, "rms_bwd_c", [(dx1, "row"), (dh, "row"), (x, "row"), (g1, "full")],
                     [(_sds((S, D), F32), "row"), (_sds((1, D), F32), "acc")], tr)


def _gate_bwd(dmixed, proj, ya, yp, gate_cb, tr):
    S, D = ya.shape

    def body(dm_ref, ga_ref, gp_ref, ya_ref, yp_ref, dya_ref, dyp_ref, dga_ref, dgp_ref):
        dm = dm_ref[...]
        sa = _sigmoid(ga_ref[...])
        sp = _sigmoid(gp_ref[...])
        dya_ref[...] = (dm * sa).astype(BF16)
        dyp_ref[...] = (dm * sp).astype(BF16)
        dga_ref[...] = (dm * ya_ref[...] * sa * (1.0 - sa)).astype(BF16)
        dgp_ref[...] = (dm * yp_ref[...] * sp * (1.0 - sp)).astype(BF16)

    return _row_call(body, "gate_bwd",
                     [(dmixed, "row"), (proj, "row", D, gate_cb), (proj, "row", D, gate_cb + 1), (ya, "row"), (yp, "row")],
                     [(_sds((S, D), BF16), "row")] * 4, tr)


def _scale_bwd(dps, pg, scale, tr):
    S, W = dps.shape

    def body(d_ref, pg_ref, s_ref, o_ref, ds_ref):
        _first_step_zero(ds_ref)
        d = d_ref[...]
        o_ref[...] = (d * s_ref[...]).astype(BF16)
        ds_ref[...] += jnp.sum(d * pg_ref[...], axis=0, keepdims=True)

    return _row_call(body, "scale_bwd", [(dps, "row"), (pg, "row"), (scale, "full")],
                     [(_sds((S, W), BF16), "row"), (_sds((1, W), F32), "acc")], tr)


def _tri(blk, cmp):
    j = lax.broadcasted_iota(jnp.int32, (blk, blk), 0)
    s = lax.broadcasted_iota(jnp.int32, (blk, blk), 1)
    one = jnp.concatenate([cmp(j, s).astype(BF16), jnp.ones((blk, 128), BF16)], axis=1)
    return jnp.concatenate([one, one], axis=0)


def _split_dot(x, u2):
    hi = x.astype(BF16)
    lo = (x - hi.astype(F32)).astype(BF16)
    return jnp.dot(jnp.concatenate([hi, lo], axis=1), u2, preferred_element_type=F32)


def _causal(blk):
    return lax.broadcasted_iota(jnp.int32, (blk, blk), 1) < lax.broadcasted_iota(jnp.int32, (blk, blk), 0)


def _scores(q, kj, scale, causal):
    z = lax.dot_general(q, kj, (((1,), (1,)), ((), ())), preferred_element_type=F32) * scale
    l1p = jnp.log(1.0 + jnp.exp(-jnp.abs(z)))
    lb = -(jnp.maximum(z, 0.0) + l1p)
    if causal is not None:
        lb = jnp.where(causal, lb, 0.0)
    return z, lb, jnp.minimum(z, 0.0) - l1p


def _attn_fwd(proj, n_heads, blk, comm=None):
    S = proj.shape[0]
    nq = S // blk
    scale = HEAD_DIM ** -0.5
    lanes = blk // 128
    hp = 2 if n_heads % 2 == 0 else 1
    cols = [slice(h * HEAD_DIM, (h + 1) * HEAD_DIM) for h in range(hp)]
    u_incl = _tri(blk, lambda j, s: j >= s)

    def body(q_ref, k_ref, v_ref, u_ref, o_ref, tot_ref):
        i = pl.program_id(1)
        qs = [q_ref[:, c].astype(BF16) for c in cols]
        u = u_ref[...]

        def step(j, carry, masked):
            ks = pl.multiple_of(j * blk, blk)
            causal = _causal(blk) if masked else None
            out = []
            for h in range(hp):
                acc, run = carry[h]
                kj = k_ref[pl.ds(ks, blk), cols[h]].astype(BF16)
                vj = v_ref[pl.ds(ks, blk), cols[h]].astype(BF16)
                z, lb, _ = _scores(qs[h], kj, scale, causal)
                ct = _split_dot(lb, u)
                a = jnp.exp(z + ct[:, :blk] + jnp.tile(run, (1, lanes)))
                if masked:
                    a = jnp.where(causal, a, 0.0)
                out.append((acc + jnp.dot(a.astype(BF16), vj, preferred_element_type=F32), run + ct[:, blk:]))
            return tuple(out)

        zero = jnp.zeros((blk, HEAD_DIM), F32)
        carry = step(i, ((zero, zero),) * hp, True)
        carry = lax.fori_loop(0, i, lambda t, c: step(i - 1 - t, c, False), carry)
        for h in range(hp):
            o_ref[:, cols[h]] = carry[h][0].astype(BF16)
            tot_ref[:, cols[h]] = carry[h][1]

    G = n_heads // hp
    W = hp * HEAD_DIM
    return _call(
        body, name="attn_fwd", grid=(G, nq),
        in_specs=[pl.BlockSpec((blk, W), lambda h, i: (i, h)),
                  pl.BlockSpec((S, W), lambda h, i: (0, G + h)),
                  pl.BlockSpec((S, W), lambda h, i: (0, 2 * G + h)),
                  pl.BlockSpec(u_incl.shape, lambda h, i: (0, 0))],
        out_specs=[pl.BlockSpec((blk, W), lambda h, i: (i, h))] * 2,
        out_shape=[_sds((S, n_heads * HEAD_DIM), BF16), _sds((S, n_heads * HEAD_DIM), F32)],
        args=(proj, proj, proj, u_incl), sem=("parallel", "arbitrary"), comm=comm)


def _attn_bwd(proj, tot, do, n_heads, blk, comm=None):
    S = proj.shape[0]
    nq = S // blk
    scale = HEAD_DIM ** -0.5
    lanes = blk // 128
    hp = 2 if n_heads % 2 == 0 else 1
    cols = [slice(h * HEAD_DIM, (h + 1) * HEAD_DIM) for h in range(hp)]
    l_strict = _tri(blk, lambda j, s: j < s)
    l_incl = _tri(blk, lambda j, s: j <= s)

    def body(q_ref, k_ref, v_ref, tot_ref, do_ref, ls_ref, li_ref, dq_ref, dk_ref, dv_ref, dk_acc, dv_acc):
        i = pl.program_id(1)

        @pl.when(i == 0)
        def _():
            dk_acc[...] = jnp.zeros_like(dk_acc)
            dv_acc[...] = jnp.zeros_like(dv_acc)

        qs = [q_ref[:, c].astype(BF16) for c in cols]
        dobs = [do_ref[:, c].astype(BF16) for c in cols]
        totals = [jnp.tile(tot_ref[:, c], (1, lanes)) for c in cols]
        ls = ls_ref[...]
        li = li_ref[...]

        def step(j, carry, masked):
            ks = pl.multiple_of(j * blk, blk)
            causal = _causal(blk) if masked else None
            out = []
            for h in range(hp):
                dq, run_lb, run_g = carry[h]
                kj = k_ref[pl.ds(ks, blk), cols[h]].astype(BF16)
                vj = v_ref[pl.ds(ks, blk), cols[h]].astype(BF16)
                z, lb, log_beta = _scores(qs[h], kj, scale, causal)
                beta = jnp.exp(log_beta)
                pt = _split_dot(lb, ls)
                a = jnp.exp(z + totals[h] - (pt[:, :blk] + jnp.tile(run_lb, (1, lanes))))
                if masked:
                    a = jnp.where(causal, a, 0.0)
                da = lax.dot_general(dobs[h], vj, (((1,), (1,)), ((), ())), preferred_element_type=F32)
                g = a * da
                gt = _split_dot(g, li)
                dz = g - beta * (gt[:, :blk] + jnp.tile(run_g, (1, lanes)))
                if masked:
                    dz = jnp.where(causal, dz, 0.0)
                dzs = (dz * scale).astype(BF16)
                dq = dq + jnp.dot(dzs, kj, preferred_element_type=F32)
                dk_acc[pl.ds(ks, blk), cols[h]] += lax.dot_general(dzs, qs[h], (((0,), (0,)), ((), ())),
                                                                   preferred_element_type=F32)
                dv_acc[pl.ds(ks, blk), cols[h]] += lax.dot_general(a.astype(BF16), dobs[h], (((0,), (0,)), ((), ())),
                                                                   preferred_element_type=F32)
                out.append((dq, run_lb + pt[:, blk:], run_g + gt[:, blk:]))
            return tuple(out)

        zero = jnp.zeros((blk, HEAD_DIM), F32)
        carry = lax.fori_loop(0, i, lambda j, c: step(j, c, False), ((zero, zero, zero),) * hp)
        carry = step(i, carry, True)
        for h in range(hp):
            dq_ref[:, cols[h]] = carry[h][0].astype(BF16)

        @pl.when(i == nq - 1)
        def _():
            dk_ref[...] = dk_acc[...].astype(BF16)
            dv_ref[...] = dv_acc[...].astype(BF16)

    G = n_heads // hp
    W = hp * HEAD_DIM
    AW = n_heads * HEAD_DIM
    return _call(
        body, name="attn_bwd", grid=(G, nq),
        in_specs=[pl.BlockSpec((blk, W), lambda h, i: (i, h)),
                  pl.BlockSpec((S, W), lambda h, i: (0, G + h)),
                  pl.BlockSpec((S, W), lambda h, i: (0, 2 * G + h)),
                  pl.BlockSpec((blk, W), lambda h, i: (i, h)),
                  pl.BlockSpec((blk, W), lambda h, i: (i, h)),
                  pl.BlockSpec(l_strict.shape, lambda h, i: (0, 0)),
                  pl.BlockSpec(l_incl.shape, lambda h, i: (0, 0))],
        out_specs=[pl.BlockSpec((blk, W), lambda h, i: (i, h)),
                   pl.BlockSpec((S, W), lambda h, i: (0, h)),
                   pl.BlockSpec((S, W), lambda h, i: (0, h))],
        out_shape=[_sds((S, AW), BF16)] * 3, args=(proj, proj, proj, tot, do, l_strict, l_incl),
        scratch=[pltpu.VMEM((S, W), F32)] * 2, sem=("parallel", "arbitrary"), comm=comm)


def _pool_count(r0, rows, w):
    t = r0 + lax.broadcasted_iota(jnp.int32, (rows, 1), 0)
    return jnp.minimum(t + 1, w).astype(F32)


def _pool_fwd(proj, col_blk, n_groups, width, chunk):
    S = proj.shape[0]
    H = POOL_HALO

    def body(u_ref, o_ref, pad_ref):
        g = pl.program_id(0)
        pad_ref[0:H, :] = jnp.zeros((H, width), F32)
        pad_ref[H:, :] = u_ref[...]
        for gi, w in enumerate(POOL_WINDOWS[:n_groups]):
            @pl.when(g == gi)
            def _(w=w):
                def one(c, _):
                    r0 = pl.multiple_of(c * chunk, chunk)
                    ext = pad_ref[pl.ds(r0, chunk + H), :]
                    s = ext
                    k = 1
                    while k < w:
                        s = s + pltpu.roll(s, k, 0)
                        k *= 2
                    o_ref[pl.ds(r0, chunk), :] = (s[H:] / _pool_count(r0, chunk, w) - ext[H:]).astype(BF16)
                    return 0

                lax.fori_loop(0, S // chunk, one, 0)

    return pl.pallas_call(
        body, name="pool_fwd", grid=(n_groups,),
        in_specs=[pl.BlockSpec((S, width), lambda g: (0, col_blk + g))],
        out_specs=pl.BlockSpec((S, width), lambda g: (0, g)), out_shape=_sds((S, n_groups * width), BF16),
        scratch_shapes=[pltpu.VMEM((S + H, width), F32)], compiler_params=_params("parallel"),
    )(proj)


def _pool_bwd(dpooled, n_groups, chunk):
    S = dpooled.shape[0]
    width = dpooled.shape[1] // n_groups
    H = POOL_HALO

    def body(d_ref, o_ref, pad_ref):
        g = pl.program_id(0)
        pad_ref[S:, :] = jnp.zeros((H, width), F32)
        for gi, w in enumerate(POOL_WINDOWS[:n_groups]):
            @pl.when(g == gi)
            def _(w=w):
                def fill(c, _):
                    r0 = pl.multiple_of(c * chunk, chunk)
                    pad_ref[pl.ds(r0, chunk), :] = d_ref[pl.ds(r0, chunk), :] / _pool_count(r0, chunk, w)
                    return 0

                lax.fori_loop(0, S // chunk, fill, 0)

                def one(c, _):
                    r0 = pl.multiple_of(c * chunk, chunk)
                    s = pad_ref[pl.ds(r0, chunk + H), :]
                    k = 1
                    while k < w:
                        s = s + pltpu.roll(s, chunk + H - k, 0)
                        k *= 2
                    o_ref[pl.ds(r0, chunk), :] = (s[:chunk] - d_ref[pl.ds(r0, chunk), :]).astype(BF16)
                    return 0

                lax.fori_loop(0, S // chunk, one, 0)

    return pl.pallas_call(
        body, name="pool_bwd", grid=(n_groups,),
        in_specs=[pl.BlockSpec((S, width), lambda g: (0, g))],
        out_specs=pl.BlockSpec((S, width), lambda g: (0, g)), out_shape=_sds((S, n_groups * width), BF16),
        scratch_shapes=[pltpu.VMEM((S + H, width), F32)], compiler_params=_params("parallel"),
    )(dpooled)


def _conv3(x_ext, w, b):
    return b + pltpu.roll(x_ext, 2, 0) * w[0:1, :] + pltpu.roll(x_ext, 1, 0) * w[1:2, :] + x_ext * w[2:3, :]


def _gelu_parts(x):
    th = jnp.tanh(GELU_C0 * (x + GELU_C1 * (x * x * x)))
    return th, 0.5 * (1.0 + th)


def _conv_specs(S, F, cb):
    nb = F // cb
    return [pl.BlockSpec((S, cb), lambda j: (0, j)), pl.BlockSpec((S, cb), lambda j: (0, nb + j)),
            pl.BlockSpec((3, cb), lambda j: (0, j)), pl.BlockSpec((3, cb), lambda j: (0, nb + j)),
            pl.BlockSpec((1, cb), lambda j: (0, j)), pl.BlockSpec((1, cb), lambda j: (0, nb + j))]


def _conv_fwd(upre, cw, cb_, chunk):
    S, F2 = upre.shape
    F = F2 // 2
    cb = 128
    H = CONV_HALO

    def body(g_ref, v_ref, wg_ref, wv_ref, bg_ref, bv_ref, o_ref, pg_ref, pv_ref):
        pg_ref[0:H, :] = jnp.zeros((H, cb), F32)
        pv_ref[0:H, :] = jnp.zeros((H, cb), F32)
        pg_ref[H:, :] = g_ref[...]
        pv_ref[H:, :] = v_ref[...]
        wg, wv, bg, bv = wg_ref[...], wv_ref[...], bg_ref[...], bv_ref[...]

        def one(c, _):
            r0 = pl.multiple_of(c * chunk, chunk)
            up_g = _conv3(pg_ref[pl.ds(r0, chunk + H), :], wg, bg)[H:]
            up_v = _conv3(pv_ref[pl.ds(r0, chunk + H), :], wv, bv)[H:]
            _, cdf = _gelu_parts(up_g)
            o_ref[pl.ds(r0, chunk), :] = (up_g * cdf * up_v).astype(BF16)
            return 0

        lax.fori_loop(0, S // chunk, one, 0)

    return pl.pallas_call(
        body, name="conv_fwd", grid=(F // cb,), in_specs=_conv_specs(S, F, cb),
        out_specs=pl.BlockSpec((S, cb), lambda j: (0, j)), out_shape=_sds((S, F), BF16),
        scratch_shapes=[pltpu.VMEM((S + H, cb), F32)] * 2, compiler_params=_params("parallel"),
    )(upre, upre, cw, cw, cb_, cb_)


def _conv_bwd(upre, dact, cw, cb_, chunk, comm=None):
    S, F2 = upre.shape
    F = F2 // 2
    cb = 128
    H = CONV_HALO
    E = chunk + 2 * H

    def body(g_ref, v_ref, wg_ref, wv_ref, bg_ref, bv_ref, d_ref, dg_ref, dv_ref, dwg_ref, dwv_ref, dbg_ref, dbv_ref,
             pg_ref, pv_ref, pd_ref):
        for p, src in ((pg_ref, g_ref), (pv_ref, v_ref), (pd_ref, d_ref)):
            p[0:H, :] = jnp.zeros((H, cb), F32)
            p[H:S + H, :] = src[...]
            p[S + H:, :] = jnp.zeros((H, cb), F32)
        wg, wv, bg, bv = wg_ref[...], wv_ref[...], bg_ref[...], bv_ref[...]

        def taps_bwd(d, w):
            return d * w[2:3, :] + pltpu.roll(d, E - 1, 0) * w[1:2, :] + pltpu.roll(d, E - 2, 0) * w[0:1, :]

        def wsum(d, x):
            dc = d[H:H + chunk]
            return [jnp.sum(dc * pltpu.roll(x, 2, 0)[H:H + chunk], axis=0, keepdims=True),
                    jnp.sum(dc * pltpu.roll(x, 1, 0)[H:H + chunk], axis=0, keepdims=True),
                    jnp.sum(dc * x[H:H + chunk], axis=0, keepdims=True),
                    jnp.sum(dc, axis=0, keepdims=True)]

        def one(c, acc):
            r0 = pl.multiple_of(c * chunk, chunk)
            xg = pg_ref[pl.ds(r0, E), :]
            xv = pv_ref[pl.ds(r0, E), :]
            d = pd_ref[pl.ds(r0, E), :]
            up_g = _conv3(xg, wg, bg)
            up_v = _conv3(xv, wv, bv)
            th, cdf = _gelu_parts(up_g)
            dgelu = cdf + 0.5 * up_g * (1.0 - th * th) * (GELU_C0 * (1.0 + 3.0 * GELU_C1 * (up_g * up_g)))
            dgate = d * up_v * dgelu
            dval = d * (up_g * cdf)
            dg_ref[pl.ds(r0, chunk), :] = taps_bwd(dgate, wg)[H:H + chunk].astype(BF16)
            dv_ref[pl.ds(r0, chunk), :] = taps_bwd(dval, wv)[H:H + chunk].astype(BF16)
            return tuple(a + b for a, b in zip(acc, wsum(dgate, xg) + wsum(dval, xv)))

        zero = jnp.zeros((1, cb), F32)
        acc = lax.fori_loop(0, S // chunk, one, (zero,) * 8)
        dwg_ref[...] = jnp.concatenate(acc[0:3], axis=0)
        dbg_ref[...] = acc[3]
        dwv_ref[...] = jnp.concatenate(acc[4:7], axis=0)
        dbv_ref[...] = acc[7]

    col = lambda rows: pl.BlockSpec((rows, cb), lambda j: (0, j))
    return _call(
        body, name="conv_bwd", grid=(F // cb,), in_specs=_conv_specs(S, F, cb) + [col(S)],
        out_specs=[col(S), col(S), col(3), col(3), col(1), col(1)],
        out_shape=[_sds((S, F), BF16), _sds((S, F), BF16), _sds((3, F), F32), _sds((3, F), F32), _sds((1, F), F32),
                   _sds((1, F), F32)],
        args=(upre, upre, cw, cw, cb_, cb_, dact), scratch=[pltpu.VMEM((S + 2 * H, cb), F32)] * 3, sem=("parallel",),
        comm=comm)


def _position():
    x, y, c = lax.axis_index("x"), lax.axis_index("y"), lax.axis_index("c")
    return x, y, c, 4 * x + 2 * y + c


def _peer(x, y, c, d):
    px = 1 - x if d & 4 else x
    py = 1 - y if d & 2 else y
    pc = 1 - c if d & 1 else c
    return (px, py, pc), 4 * px + 2 * py + pc


def _all_gather_comm(tensors):
    nt = len(tensors)
    outs = [_sds((N_DEV,) + t.shape, t.dtype) for t in tensors]
    sems = [pltpu.SemaphoreType.DMA((7 * nt,)), pltpu.SemaphoreType.DMA((7 * nt,)), pltpu.SemaphoreType.DMA((nt,))]

    def parts(ins, outs_, sem_refs):
        send, recv, loc = sem_refs
        x, y, c, me = _position()
        chips = [(1 - x, y), (x, 1 - y), (1 - x, 1 - y)]

        def copy(t, k, block, to, src=None):
            slot = outs_[t].at[4 * block[0] + 2 * block[1] + block[2]]
            return pltpu.make_async_remote_copy(src_ref=slot if src is None else src, dst_ref=slot,
                                                send_sem=send.at[7 * t + k], recv_sem=recv.at[7 * t + k], device_id=to,
                                                device_id_type=MESH)

        def mine(t):
            return pltpu.make_async_copy(ins[t], outs_[t].at[me], loc.at[t])

        return (x, y, c), (x, y, 1 - c), chips, copy, mine

    def start(ins, outs_, sem_refs):
        me, sibling, chips, copy, mine = parts(ins, outs_, sem_refs)
        for t in range(nt):
            mine(t).start()
            copy(t, 0, me, sibling, src=ins[t]).start()
            for j, chip in enumerate(chips):
                copy(t, 1 + j, me, (*chip, me[2]), src=ins[t]).start()

    def finish(ins, outs_, sem_refs):
        me, sibling, chips, copy, mine = parts(ins, outs_, sem_refs)
        c = me[2]
        for t in range(nt):
            for j, chip in enumerate(chips):
                copy(t, 1 + j, (*chip, c), me).wait_recv()
                copy(t, 4 + j, (*chip, c), sibling).start()
        for t in range(nt):
            copy(t, 0, sibling, me).wait_recv()
            for j, chip in enumerate(chips):
                copy(t, 4 + j, (*chip, 1 - c), me).wait_recv()
        for t in range(nt):
            copy(t, 0, me, sibling, src=ins[t]).wait_send()
            for j, chip in enumerate(chips):
                copy(t, 1 + j, me, (*chip, c), src=ins[t]).wait_send()
                copy(t, 4 + j, (*chip, c), sibling).wait_send()
            mine(t).wait()

    return _Comm(tensors, outs, sems, start, finish)


def _reduce_scatter_comm(tensors):
    nt = len(tensors)
    outs = [_sds(t.shape, t.dtype) for t in tensors]
    sems = [pltpu.SemaphoreType.DMA((7 * nt,)), pltpu.SemaphoreType.DMA((7 * nt,)), pltpu.SemaphoreType.DMA((nt,))]

    def local(ins, outs_, sem_refs, t, me):
        return pltpu.make_async_copy(ins[t].at[me], outs_[t].at[me], sem_refs[2].at[t])

    def remote(ins, outs_, sem_refs, t, d, inbound):
        x, y, c, me = _position()
        peer, peer_idx = _peer(x, y, c, d)
        k = 7 * t + d - 1
        src, dst, to = (ins[t].at[me], outs_[t].at[peer_idx], (x, y, c)) if inbound else (ins[t].at[peer_idx], outs_[t].at[me], peer)
        return pltpu.make_async_remote_copy(src_ref=src, dst_ref=dst, send_sem=sem_refs[0].at[k], recv_sem=sem_refs[1].at[k],
                                            device_id=to, device_id_type=MESH)

    def start(ins, outs_, sem_refs):
        me = _position()[3]
        for t in range(nt):
            local(ins, outs_, sem_refs, t, me).start()
            for d in range(1, N_DEV):
                remote(ins, outs_, sem_refs, t, d, False).start()

    def finish(ins, outs_, sem_refs):
        me = _position()[3]
        for t in range(nt):
            for d in range(1, N_DEV):
                remote(ins, outs_, sem_refs, t, d, True).wait_recv()
        for t in range(nt):
            for d in range(1, N_DEV):
                remote(ins, outs_, sem_refs, t, d, False).wait_send()
            local(ins, outs_, sem_refs, t, me).wait()

    return _Comm(tensors, outs, sems, start, finish)


def _all_reduce_small(part):
    r, W = part.shape

    def body(p_ref, o_ref, g_ref, send_sems, recv_sems):
        x, y, c, me = _position()
        sends = []
        for d in range(1, N_DEV):
            peer, _ = _peer(x, y, c, d)
            cp = pltpu.make_async_remote_copy(src_ref=p_ref, dst_ref=g_ref.at[me], send_sem=send_sems.at[d - 1],
                                              recv_sem=recv_sems.at[d - 1], device_id=peer, device_id_type=MESH)
            cp.start()
            sends.append(cp)
        g_ref[me] = p_ref[...]
        for d in range(1, N_DEV):
            _, peer_idx = _peer(x, y, c, d)
            pltpu.make_async_remote_copy(src_ref=p_ref, dst_ref=g_ref.at[peer_idx], send_sem=send_sems.at[d - 1],
                                         recv_sem=recv_sems.at[d - 1], device_id=(x, y, c), device_id_type=MESH).wait_recv()
        for cp in sends:
            cp.wait_send()
        acc = g_ref[0]
        for i in range(1, N_DEV):
            acc = acc + g_ref[i]
        o_ref[...] = acc

    vmem = pl.BlockSpec(memory_space=pltpu.VMEM)
    return pl.pallas_call(
        body, name="all_reduce_small", in_specs=[vmem], out_specs=[vmem, vmem],
        out_shape=[_sds((r, W), F32), _sds((N_DEV, r, W), F32)],
        scratch_shapes=[pltpu.SemaphoreType.DMA((7,)), pltpu.SemaphoreType.DMA((7,))],
        compiler_params=pltpu.CompilerParams(has_side_effects=True, vmem_limit_bytes=VMEM_LIMIT),
    )(part)[0]


def _adamw_math(w, g, m, v):
    m = ADAM_B1 * m + (1.0 - ADAM_B1) * g
    v = ADAM_B2 * v + (1.0 - ADAM_B2) * (g * g)
    m_hat = m / (1.0 - ADAM_B1 ** ADAM_STEP)
    v_hat = v / (1.0 - ADAM_B2 ** ADAM_STEP)
    return -ADAM_LR * (m_hat / (jnp.sqrt(v_hat) + ADAM_EPS) + ADAM_WD * w), m, v


def _adamw(w, g, m, v, name):
    rows, cols = w.shape
    tr = _tile(rows, max(8, (2**18 // cols) // 8 * 8), 8)

    def body(w_ref, g_ref, m_ref, v_ref, d_ref, nm_ref, nv_ref):
        d_ref[...], nm_ref[...], nv_ref[...] = _adamw_math(w_ref[...], g_ref[...], m_ref[...], v_ref[...])

    spec = pl.BlockSpec((tr, cols), lambda i: (i, 0))
    return pl.pallas_call(
        body, name=name, grid=(rows // tr,), in_specs=[spec] * 4, out_specs=[spec] * 3,
        out_shape=[_sds((rows, cols), F32)] * 3, compiler_params=_params("parallel"),
    )(w, g, m, v)


def _adamw_sum(recv, w, m, v, name):
    n, rows, cols = recv.shape
    tr = _tile(rows, max(16, (2**17 // cols) // 16 * 16), 16)

    def body(r_ref, w_ref, m_ref, v_ref, g_ref, d_ref, nm_ref, nv_ref):
        g = r_ref[0].astype(F32)
        for i in range(1, n):
            g = g + r_ref[i].astype(F32)
        g_ref[...] = g
        d_ref[...], nm_ref[...], nv_ref[...] = _adamw_math(w_ref[...], g, m_ref[...], v_ref[...])

    spec = pl.BlockSpec((tr, cols), lambda i: (i, 0))
    return pl.pallas_call(
        body, name=name, grid=(rows // tr,), in_specs=[pl.BlockSpec((n, tr, cols), lambda i: (0, i, 0))] + [spec] * 3,
        out_specs=[spec] * 4, out_shape=[_sds((rows, cols), F32)] * 4, compiler_params=_params("parallel"),
    )(recv, w, m, v)


COLUMN_CUT = ("w_in", "w_attn_branch", "w_pool_branch", "w_up", "w_ple")
ROW_CUT = ("w_out", "w_down", "w_ple_gate")
REPLICATED = ("norm_mix_pre", "pool_scale", "norm_mix_post", "norm_ffn_pre", "conv_b", "norm_ffn_post", "norm_ple_post")
WEIGHTS = ("norm_mix_pre", "w_in", "w_attn_branch", "w_pool_group", "pool_scale", "w_pool_branch", "w_out", "norm_mix_post",
           "norm_ffn_pre", "w_up", "conv_w", "conv_b", "w_down", "norm_ffn_post", "w_ple", "w_ple_gate", "norm_ple_post")


def _size(shape):
    n = 1
    for s in shape:
        n *= s
    return n


def _pad_rows(flat, row_align):
    n = flat.shape[-1]
    per = PACK_W * row_align
    total = -(-n // per) * per
    return jnp.pad(flat, [(0, total - n)]).reshape(total // PACK_W, PACK_W)


def _natural(shard_major):
    n, r, c = shard_major.shape
    return shard_major.reshape(n * r, c)


def kernel(x, p, norm_mix_pre, w_in, w_attn_branch, w_pool_group, pool_scale, w_pool_branch, w_out, norm_mix_post, norm_ffn_pre, w_up, conv_w, conv_b, w_down, norm_ffn_post, w_ple, w_ple_gate, norm_ple_post, loss_target, m_norm_mix_pre, m_w_in, m_w_attn_branch, m_w_pool_group, m_pool_scale, m_w_pool_branch, m_w_out, m_norm_mix_post, m_norm_ffn_pre, m_w_up, m_conv_w, m_conv_b, m_w_down, m_norm_ffn_post, m_w_ple, m_w_ple_gate, m_norm_ple_post, v_norm_mix_pre, v_w_in, v_w_attn_branch, v_w_pool_group, v_pool_scale, v_w_pool_branch, v_w_out, v_norm_mix_post, v_norm_ffn_pre, v_w_up, v_conv_w, v_conv_b, v_w_down, v_norm_ffn_post, v_w_ple, v_w_ple_gate, v_norm_ple_post):
    given = dict(locals())
    wts = {n: given[n][0] for n in WEIGHTS}
    mom = {n: given["m_" + n][0] for n in WEIGHTS}
    var = {n: given["v_" + n][0] for n in WEIGHTS}
    xs = x[0]
    ps_in = p[0, 0]
    tgt = loss_target[0]
    S, D = xs.shape
    AW = wts["w_attn_branch"].shape[0]
    PW = wts["w_pool_branch"].shape[0]
    G = wts["w_pool_group"].shape[0]
    PGW = PW // G
    H = AW // HEAD_DIM
    F = wts["w_down"].shape[0] * N_DEV
    assert (3 * AW) % PGW == 0 and (3 * AW + PW) % D == 0 and PGW % 128 == 0 and F % 128 == 0
    tr = _tile(S, 256, 16)
    blk = _tile(S, 256, 128)
    chunk = _tile(S, 256, 8)
    me = 4 * lax.axis_index("x") + 2 * lax.axis_index("y") + lax.axis_index("c")

    cw_shape = wts["conv_w"].shape
    conv_b_row = wts["conv_b"].reshape(1, -1)
    g1, g2, g3, g4, g5 = (wts[n].reshape(1, D) for n in
                          ("norm_mix_pre", "norm_mix_post", "norm_ffn_pre", "norm_ffn_post", "norm_ple_post"))
    pscale = wts["pool_scale"].reshape(1, PW)
    big = 4096
    wb = {n: wts[n].astype(BF16) for n in COLUMN_CUT + ROW_CUT}
    wb["w_pool_group"] = wts["w_pool_group"].astype(BF16).reshape(G * PGW // N_DEV, PGW)

    h, w_in = _rms_fwd(xs, g1, tr, _all_gather_comm([wb["w_in"]]))
    proj, w_ab, w_pg, w_pb, w_out = _mm(
        h, w_in, b_sm=True, name="mm_in",
        comm=_all_gather_comm([wb["w_attn_branch"], wb["w_pool_group"], wb["w_pool_branch"], wb["w_out"]]))
    w_pg = jnp.moveaxis(w_pg.reshape(N_DEV, G, PGW // N_DEV, PGW), 0, 1).reshape(G, PGW, PGW)
    w_out = _natural(w_out)
    attn, tot, w_up, conv_w_all = _attn_fwd(proj, H, blk, _all_gather_comm([wb["w_up"], wts["conv_w"]]))
    conv_w_full = jnp.moveaxis(conv_w_all, 0, 1).reshape(cw_shape[0], N_DEV * cw_shape[1])
    y_attn = _mm(attn, w_ab, b_sm=True, name="mm_attn_branch", tm=big, tn=256, tk=big)
    pooled = _pool_fwd(proj, 3 * AW // PGW, G, PGW, chunk)
    pg, ps = _pool_group_fwd(pooled, w_pg, pscale)
    y_pool = _mm(ps, w_pb, b_sm=True, name="mm_pool_branch", tm=big, tn=256, tk=big)
    gate_cb = (3 * AW + PW) // D
    mixed = _gate_mix(proj, y_attn, y_pool, gate_cb, tr)
    mo = _mm(mixed, w_out, name="mm_out")
    x1, h2 = _resid_rms2(xs, mo, g2, g3, tr)
    upre, w_down, w_ple, w_pleg = _mm(h2, w_up, b_sm=True, name="mm_up", tn=2048,
                                      comm=_all_gather_comm([wb["w_down"], wb["w_ple"], wb["w_ple_gate"]]))
    w_down, w_pleg = _natural(w_down), _natural(w_pleg)
    act = _conv_fwd(upre, conv_w_full, conv_b_row, chunk)
    yf = _mm(act, w_down, name="mm_down")
    x2, x2b = _resid_rms(x1, yf, g4, tr)
    e = _mm(ps_in, w_ple, b_sm=True, name="mm_ple", tm=big, tn=256, tk=big)
    gl = _mm(x2b, w_pleg, name="mm_ple_gate")
    loss_part, dx3, de, dgl, dg5 = _ple_loss(gl, e, x2, tgt, g5, tr)

    shards = lambda natural: natural.reshape((N_DEV, natural.shape[0] // N_DEV) + natural.shape[1:])
    recv = {}
    dw_ple = _mm(ps_in, de, ta=True, out_sm=True, out_dtype=BF16, name="mm_d_w_ple", tm=256, tn=256, tk=big)
    dw_pleg = shards(_mm(x2b, dgl, ta=True, out_dtype=BF16, name="mm_d_w_ple_gate"))
    dx2g = _mm(dgl, w_pleg, tb=True, name="mm_d_x2")
    dx2, dyf, dg4 = _rms_bwd_a(dx3, dx2g, yf, g4, tr)
    dw_down = shards(_mm(act, dyf, ta=True, out_dtype=BF16, name="mm_d_w_down"))
    dact, recv["w_ple"], recv["w_ple_gate"] = _mm(dyf, w_down, tb=True, name="mm_d_act",
                                                  comm=_reduce_scatter_comm([dw_ple, dw_pleg]))
    dup_g, dup_v, dcw_g, dcw_v, dcb_g, dcb_v, recv["w_down"] = _conv_bwd(upre, dact, conv_w_full, conv_b_row, chunk,
                                                                          _reduce_scatter_comm([dw_down]))
    dupre = jnp.concatenate([dup_g, dup_v], axis=1)
    dw_up = _mm(h2, dupre, ta=True, out_sm=True, out_dtype=BF16, name="mm_d_w_up", tn=2048)
    dh2 = _mm(dupre, w_up, tb=True, b_sm=True, name="mm_d_h2", tk=2048)
    dx1, dmo, dg3, dg2 = _rms_bwd_b(dx2, dh2, x1, g3, mo, g2, tr)
    dmixed = _mm(dmo, w_out, tb=True, name="mm_d_mixed")
    dw_out = shards(_mm(mixed, dmo, ta=True, out_dtype=BF16, name="mm_d_w_out"))
    dya, dyp, dga, dgp = _gate_bwd(dmixed, proj, y_attn, y_pool, gate_cb, tr)
    dps = _mm(dyp, w_pb, tb=True, b_sm=True, name="mm_d_ps", tm=2048, tk=256)
    dw_pb = _mm(ps, dyp, ta=True, out_sm=True, out_dtype=BF16, name="mm_d_w_pool_branch", tn=256, tk=big)
    dpg, dscale = _scale_bwd(dps, pg, pscale, tr)
    dpooled = _pool_group_bwd_x(dpg, w_pg)
    dw_pg = _pool_group_bwd_w(pooled, dpg, G)
    dw_pg = jnp.moveaxis(dw_pg.astype(BF16).reshape(G, N_DEV, PGW // N_DEV, PGW), 1, 0).reshape(N_DEV, G * PGW // N_DEV, PGW)
    du = _pool_bwd(dpooled, G, chunk)
    dattn = _mm(dya, w_ab, tb=True, b_sm=True, name="mm_d_attn", tm=2048, tk=256)
    dw_ab = _mm(attn, dya, ta=True, out_sm=True, out_dtype=BF16, name="mm_d_w_attn_branch", tn=256, tk=big)
    dq, dk, dv, recv["w_up"], recv["w_out"], recv["w_pool_branch"], recv["w_pool_group"], recv["w_attn_branch"] = _attn_bwd(
        proj, tot, dattn, H, blk, _reduce_scatter_comm([dw_up, dw_out, dw_pb, dw_pg, dw_ab]))
    dproj = jnp.concatenate([dq, dk, dv, du, dga, dgp], axis=1)
    dw_in = _mm(h, dproj, ta=True, out_sm=True, out_dtype=BF16, name="mm_d_w_in")
    dh, recv["w_in"] = _mm(dproj, w_in, tb=True, b_sm=True, name="mm_d_h", tk=1024, comm=_reduce_scatter_comm([dw_in]))
    grad_x, dg1 = _rms_bwd_c(dx1, dh, xs, g1, tr)

    gshard, delta, new_m, new_v = {}, {}, {}, {}
    for n in COLUMN_CUT + ROW_CUT + ("w_pool_group",):
        shp = wts[n].shape
        two_d = (_size(shp[:-1]), shp[-1])
        g_, d_, m_, v_ = _adamw_sum(recv[n].reshape((N_DEV,) + two_d), wts[n].reshape(two_d), mom[n].reshape(two_d),
                                    var[n].reshape(two_d), "adamw_" + n)
        gshard[n], delta[n], new_m[n], new_v[n] = g_.reshape(shp), d_.reshape(shp), m_.reshape(shp), v_.reshape(shp)

    dconv_w = jnp.concatenate([dcw_g, dcw_v], axis=1)
    dconv_b = jnp.concatenate([dcb_g, dcb_v], axis=1).reshape(-1)
    rep_parts = {"norm_mix_pre": dg1, "pool_scale": dscale, "norm_mix_post": dg2, "norm_ffn_pre": dg3, "conv_b": dconv_b,
                 "norm_ffn_post": dg4, "norm_ple_post": dg5}
    small = jnp.concatenate([rep_parts[n].reshape(-1) for n in REPLICATED] + [dconv_w.reshape(-1)])
    n_small = small.shape[0]
    small_sum = _all_reduce_small(_pad_rows(small, 8)).reshape(-1)[:n_small]
    off = 0
    for n in REPLICATED:
        sz = _size(wts[n].shape)
        gshard[n] = small_sum[off:off + sz].reshape(wts[n].shape)
        off += sz
    dconv_w_sum = small_sum[off:off + 3 * 2 * F].reshape(3, 2 * F)
    gshard["conv_w"] = lax.dynamic_slice_in_dim(dconv_w_sum, me * cw_shape[1], cw_shape[1], axis=1)

    delta["conv_w"], new_m["conv_w"], new_v["conv_w"] = _adamw(wts["conv_w"], gshard["conv_w"], mom["conv_w"], var["conv_w"],
                                                               "adamw_conv_w")
    rep_sizes = [_size(wts[n].shape) for n in REPLICATED]
    n_rep = sum(rep_sizes)
    cat = lambda t: _pad_rows(jnp.concatenate([t[n].reshape(-1) for n in REPLICATED]), 8)
    d_, m_, v_ = _adamw(cat(wts), cat(gshard), cat(mom), cat(var), "adamw_replicated")
    off = 0
    for n, sz in zip(REPLICATED, rep_sizes):
        shp = wts[n].shape
        delta[n], new_m[n], new_v[n] = (t.reshape(-1)[off:off + sz].reshape(shp) for t in (d_, m_, v_))
        off += sz
    assert off == n_rep

    loss = lax.psum(loss_part[0, 0], ("x", "y", "c"))
    lead = lambda t: t[None]
    return (loss, grad_x[None], *[lead(gshard[n]) for n in WEIGHTS], *[lead(delta[n]) for n in WEIGHTS],
            *[lead(new_m[n]) for n in WEIGHTS], *[lead(new_v[n]) for n in WEIGHTS])
```

```python
import functools

import jax
import jax.numpy as jnp
from jax import lax
from jax.experimental import pallas as pl
from jax.experimental.pallas import tpu as pltpu

F32 = jnp.float32
BF16 = jnp.bfloat16
MESH = pl.DeviceIdType.MESH

EPS = 1e-6
HEAD_DIM = 128
POOL_WINDOWS = (2, 4, 8, 16)
POOL_HALO = 16
CONV_HALO = 8
GELU_C0 = 0.7978845608028654
GELU_C1 = 0.044715
ADAM_LR = 0.001
ADAM_B1 = 0.9
ADAM_B2 = 0.999
ADAM_EPS = 1e-08
ADAM_WD = 0.01
ADAM_STEP = 10
N_DEV = 8
PACK_W = 1024
VMEM_LIMIT = 56 * 2**20
ANY = pl.BlockSpec(memory_space=pl.ANY)


def _params(*sem):
    return pltpu.CompilerParams(dimension_semantics=sem, vmem_limit_bytes=VMEM_LIMIT)


def _sds(shape, dtype):
    return jax.ShapeDtypeStruct(shape, dtype)


def _tile(dim, target, align):
    if dim <= target:
        return dim
    t = (target // align) * align
    while t >= align:
        if dim % t == 0:
            return t
        t -= align
    return dim


def _sigmoid(x):
    return 1.0 / (1.0 + jnp.exp(-x))


class _Comm:
    def __init__(self, ins, outs, sems, start, finish):
        self.ins, self.outs, self.sems, self.start, self.finish = list(ins), list(outs), list(sems), start, finish


def _call(body, *, name, grid, in_specs, out_specs, out_shape, args, scratch=(), sem=(), comm=None):
    in_specs, out_specs, out_shape, scratch = list(in_specs), list(out_specs), list(out_shape), list(scratch)
    if comm is None:
        return pl.pallas_call(body, name=name, grid=grid, in_specs=in_specs, out_specs=out_specs, out_shape=out_shape,
                              scratch_shapes=scratch, compiler_params=_params(*sem))(*args)
    n_in, n_out, n_scr, n_ci, n_co = len(in_specs), len(out_specs), len(scratch), len(comm.ins), len(comm.outs)

    def wrapped(*refs):
        ins, refs = refs[:n_in], refs[n_in:]
        c_ins, refs = refs[:n_ci], refs[n_ci:]
        outs, refs = refs[:n_out], refs[n_out:]
        c_outs, refs = refs[:n_co], refs[n_co:]
        scr, c_sems = refs[:n_scr], refs[n_scr:]
        first = last = None
        for axis, size in enumerate(grid):
            at_start, at_end = pl.program_id(axis) == 0, pl.program_id(axis) == size - 1
            first = at_start if first is None else jnp.logical_and(first, at_start)
            last = at_end if last is None else jnp.logical_and(last, at_end)
        if grid:
            pl.when(first)(lambda: comm.start(c_ins, c_outs, c_sems))
            body(*ins, *outs, *scr)
            pl.when(last)(lambda: comm.finish(c_ins, c_outs, c_sems))
        else:
            comm.start(c_ins, c_outs, c_sems)
            body(*ins, *outs, *scr)
            comm.finish(c_ins, c_outs, c_sems)

    return pl.pallas_call(
        wrapped, name=name, grid=grid, in_specs=in_specs + [ANY] * n_ci, out_specs=out_specs + [ANY] * n_co,
        out_shape=out_shape + comm.outs, scratch_shapes=scratch + comm.sems,
        compiler_params=pltpu.CompilerParams(dimension_semantics=("arbitrary",) * len(grid), vmem_limit_bytes=VMEM_LIMIT,
                                             has_side_effects=True),
    )(*args, *comm.ins)


def _mm(a, b, *, ta=False, tb=False, b_sm=False, out_sm=False, out_dtype=F32, name, tm=2048, tn=1024, tk=1024, comm=None):
    M, K = (a.shape[1], a.shape[0]) if ta else a.shape
    if b_sm:
        n_sl, rows, per = b.shape
        N = rows if tb else n_sl * per
        assert K == (n_sl * per if tb else rows)
    else:
        N = b.shape[0] if tb else b.shape[1]
    tm = _tile(M, tm, 128)
    tn = _tile(per if (b_sm and not tb) else N // N_DEV if out_sm else N, tn, 128)
    tk = _tile(per if (b_sm and tb) else K, tk, 128)
    nk = K // tk
    a_spec = pl.BlockSpec((tk, tm), lambda i, j, k: (k, i)) if ta else pl.BlockSpec((tm, tk), lambda i, j, k: (i, k))
    if not b_sm:
        b_spec = pl.BlockSpec((tn, tk), lambda i, j, k: (j, k)) if tb else pl.BlockSpec((tk, tn), lambda i, j, k: (k, j))
    elif tb:
        kp = per // tk
        b_spec = pl.BlockSpec((None, tn, tk), lambda i, j, k: (k // kp, j, k % kp))
    else:
        jp = per // tn
        b_spec = pl.BlockSpec((None, tk, tn), lambda i, j, k: (j // jp, k, j % jp))
    if out_sm:
        jo = (N // N_DEV) // tn
        o_spec = pl.BlockSpec((None, tm, tn), lambda i, j, k: (j // jo, i, j % jo))
        o_shape = _sds((N_DEV, M, N // N_DEV), out_dtype)
    else:
        o_spec = pl.BlockSpec((tm, tn), lambda i, j, k: (i, j))
        o_shape = _sds((M, N), out_dtype)
    dims = (((0 if ta else 1,), (1 if tb else 0,)), ((), ()))

    def product(a_ref, b_ref):
        return lax.dot_general(a_ref[...].astype(BF16), b_ref[...].astype(BF16), dims, preferred_element_type=F32)

    def body(a_ref, b_ref, o_ref, acc_ref):
        k = pl.program_id(2)

        @pl.when(k == 0)
        def _():
            acc_ref[...] = jnp.zeros_like(acc_ref)

        acc_ref[...] += product(a_ref, b_ref)

        @pl.when(k == nk - 1)
        def _():
            o_ref[...] = acc_ref[...].astype(o_ref.dtype)

    def body_one_step(a_ref, b_ref, o_ref):
        o_ref[...] = product(a_ref, b_ref).astype(o_ref.dtype)

    res = _call(body if nk > 1 else body_one_step, name=name, grid=(M // tm, N // tn, nk), in_specs=[a_spec, b_spec],
                out_specs=[o_spec], out_shape=[o_shape], args=(a, b), scratch=[pltpu.VMEM((tm, tn), F32)] if nk > 1 else [],
                sem=("parallel", "parallel", "arbitrary"), comm=comm)
    return res[0] if comm is None else res


def _pool_group_fwd(pooled, w_pg, scale):
    S = pooled.shape[0]
    G, C, C2 = w_pg.shape
    tm = _tile(S, 1024, 16)

    def body(a_ref, w_ref, s_ref, pg_ref, ps_ref):
        pg = jnp.dot(a_ref[...], w_ref[...], preferred_element_type=F32)
        pg_ref[...] = pg
        ps_ref[...] = (pg * s_ref[...]).astype(BF16)

    return pl.pallas_call(
        body, name="pool_group_fwd", grid=(G, S // tm),
        in_specs=[pl.BlockSpec((tm, C), lambda g, i: (i, g)), pl.BlockSpec((None, C, C2), lambda g, i: (g, 0, 0)),
                  pl.BlockSpec((1, C2), lambda g, i: (0, g))],
        out_specs=[pl.BlockSpec((tm, C2), lambda g, i: (i, g)), pl.BlockSpec((tm, C2), lambda g, i: (i, g))],
        out_shape=[jax.ShapeDtypeStruct((S, G * C2), F32), jax.ShapeDtypeStruct((S, G * C2), BF16)],
        compiler_params=_params("parallel", "parallel"),
    )(pooled, w_pg, scale)


def _pool_group_bwd_x(dpg, w_pg):
    S = dpg.shape[0]
    G, C, C2 = w_pg.shape
    tm = _tile(S, 1024, 16)

    def body(d_ref, w_ref, o_ref):
        o_ref[...] = lax.dot_general(d_ref[...], w_ref[...], (((1,), (1,)), ((), ())), preferred_element_type=F32)

    return pl.pallas_call(
        body, name="pool_group_bwd_x", grid=(G, S // tm),
        in_specs=[pl.BlockSpec((tm, C2), lambda g, i: (i, g)), pl.BlockSpec((None, C, C2), lambda g, i: (g, 0, 0))],
        out_specs=pl.BlockSpec((tm, C), lambda g, i: (i, g)), out_shape=jax.ShapeDtypeStruct((S, G * C), F32),
        compiler_params=_params("parallel", "parallel"),
    )(dpg, w_pg)


def _pool_group_bwd_w(pooled, dpg, G):
    S = pooled.shape[0]
    C, C2 = pooled.shape[1] // G, dpg.shape[1] // G
    tk = _tile(S, 1024, 16)

    def body(a_ref, d_ref, o_ref):
        @pl.when(pl.program_id(1) == 0)
        def _():
            o_ref[...] = jnp.zeros_like(o_ref)

        o_ref[...] += lax.dot_general(a_ref[...], d_ref[...], (((0,), (0,)), ((), ())), preferred_element_type=F32)

    return pl.pallas_call(
        body, name="pool_group_bwd_w", grid=(G, S // tk),
        in_specs=[pl.BlockSpec((tk, C), lambda g, k: (k, g)), pl.BlockSpec((tk, C2), lambda g, k: (k, g))],
        out_specs=pl.BlockSpec((None, C, C2), lambda g, k: (g, 0, 0)), out_shape=jax.ShapeDtypeStruct((G, C, C2), F32),
        compiler_params=_params("parallel", "arbitrary"),
    )(pooled, dpg)


def _rms(x, gain):
    r = lax.rsqrt(jnp.mean(x * x, axis=-1, keepdims=True) + EPS)
    return x * r * gain


def _rms_bwd(x, gain, dy):
    r = lax.rsqrt(jnp.mean(x * x, axis=-1, keepdims=True) + EPS)
    xh = x * r
    dgain = jnp.sum(dy * xh, axis=0, keepdims=True)
    dxh = dy * gain
    dx = r * (dxh - xh * jnp.mean(dxh * xh, axis=-1, keepdims=True))
    return dx, dgain


def _row_call(body, name, ins, outs, tr, *, comm=None):
    S = None
    in_specs, args = [], []
    for it in ins:
        arr, kind = it[0], it[1]
        if kind == "row":
            S = arr.shape[0]
            if len(it) == 4:
                width, cb = it[2], it[3]
                in_specs.append(pl.BlockSpec((tr, width), functools.partial(lambda i, cb: (i, cb), cb=cb)))
            else:
                in_specs.append(pl.BlockSpec((tr, arr.shape[1]), lambda i: (i, 0)))
        else:
            assert arr.ndim == 2
            in_specs.append(pl.BlockSpec(arr.shape, lambda i: (0, 0)))
        args.append(arr)
    out_specs, out_shape = [], []
    for sds, kind in outs:
        if kind == "row":
            out_specs.append(pl.BlockSpec((tr, sds.shape[1]), lambda i: (i, 0)))
        else:
            assert len(sds.shape) == 2
            out_specs.append(pl.BlockSpec(sds.shape, lambda i: (0, 0)))
        out_shape.append(sds)
    return _call(body, name=name, grid=(S // tr,), in_specs=in_specs, out_specs=out_specs, out_shape=out_shape, args=args,
                 sem=("arbitrary",), comm=comm)


def _first_step_zero(*refs):
    @pl.when(pl.program_id(0) == 0)
    def _():
        for r in refs:
            r[...] = jnp.zeros_like(r)


def _rms_fwd(x, gain, tr, comm):
    def body(x_ref, g_ref, o_ref):
        o_ref[...] = _rms(x_ref[...], g_ref[...]).astype(BF16)

    S, D = x.shape
    return _row_call(body, "rms_fwd", [(x, "row"), (gain, "full")], [(_sds((S, D), BF16), "row")], tr, comm=comm)


def _gate_mix(proj, ya, yp, gate_cb, tr):
    S, D = ya.shape

    def body(ga_ref, gp_ref, ya_ref, yp_ref, o_ref):
        o_ref[...] = (_sigmoid(ga_ref[...]) * ya_ref[...] + _sigmoid(gp_ref[...]) * yp_ref[...]).astype(BF16)

    return _row_call(body, "gate_mix", [(proj, "row", D, gate_cb), (proj, "row", D, gate_cb + 1), (ya, "row"), (yp, "row")],
                     [(_sds((S, D), BF16), "row")], tr)[0]


def _resid_rms2(x, mo, g2, g3, tr):
    S, D = x.shape

    def body(x_ref, mo_ref, g2_ref, g3_ref, x1_ref, h2_ref):
        x1 = x_ref[...] + _rms(mo_ref[...], g2_ref[...])
        x1_ref[...] = x1
        h2_ref[...] = _rms(x1, g3_ref[...]).astype(BF16)

    return _row_call(body, "resid_rms2", [(x, "row"), (mo, "row"), (g2, "full"), (g3, "full")],
                     [(_sds((S, D), F32), "row"), (_sds((S, D), BF16), "row")], tr)


def _resid_rms(x1, yf, g4, tr):
    S, D = x1.shape

    def body(x_ref, y_ref, g_ref, o_ref, ob_ref):
        x2 = x_ref[...] + _rms(y_ref[...], g_ref[...])
        o_ref[...] = x2
        ob_ref[...] = x2.astype(BF16)

    return _row_call(body, "resid_rms", [(x1, "row"), (yf, "row"), (g4, "full")],
                     [(_sds((S, D), F32), "row"), (_sds((S, D), BF16), "row")], tr)


def _ple_loss(gl, e, x2, tgt, g5, tr):
    S, D = x2.shape

    def body(gl_ref, e_ref, x2_ref, t_ref, g_ref, loss_ref, dx3_ref, de_ref, dgl_ref, dg_ref):
        _first_step_zero(loss_ref, dg_ref)
        s = _sigmoid(gl_ref[...])
        e_ = e_ref[...]
        t = s * e_
        gain = g_ref[...]
        err = x2_ref[...] + _rms(t, gain) - t_ref[...]
        row_loss = jnp.mean(err * err, axis=-1, keepdims=True)
        loss_ref[...] += 0.5 * jnp.sum(row_loss, axis=0, keepdims=True)
        dx3 = err * (1.0 / D)
        dx3_ref[...] = dx3
        dt, dgain = _rms_bwd(t, gain, dx3)
        dg_ref[...] += dgain
        de_ref[...] = (dt * s).astype(BF16)
        dgl_ref[...] = (dt * e_ * s * (1.0 - s)).astype(BF16)

    return _row_call(body, "ple_loss", [(gl, "row"), (e, "row"), (x2, "row"), (tgt, "row"), (g5, "full")],
                     [(_sds((1, 1), F32), "acc"), (_sds((S, D), F32), "row"), (_sds((S, D), BF16), "row"),
                      (_sds((S, D), BF16), "row"), (_sds((1, D), F32), "acc")], tr)


def _rms_bwd_a(dx3, dx2g, yf, g4, tr):
    S, D = yf.shape

    def body(a_ref, b_ref, y_ref, g_ref, dx_ref, dy_ref, dg_ref):
        _first_step_zero(dg_ref)
        dx2 = a_ref[...] + b_ref[...]
        dx_ref[...] = dx2
        dy, dgain = _rms_bwd(y_ref[...], g_ref[...], dx2)
        dy_ref[...] = dy.astype(BF16)
        dg_ref[...] += dgain

    return _row_call(body, "rms_bwd_a", [(dx3, "row"), (dx2g, "row"), (yf, "row"), (g4, "full")],
                     [(_sds((S, D), F32), "row"), (_sds((S, D), BF16), "row"), (_sds((1, D), F32), "acc")], tr)


def _rms_bwd_b(dx2, dh2, x1, g3, mo, g2, tr):
    S, D = x1.shape

    def body(dx2_ref, dh2_ref, x1_ref, g3_ref, mo_ref, g2_ref, dx1_ref, dmo_ref, dg3_ref, dg2_ref):
        _first_step_zero(dg3_ref, dg2_ref)
        d, dgain3 = _rms_bwd(x1_ref[...], g3_ref[...], dh2_ref[...])
        dx1 = dx2_ref[...] + d
        dx1_ref[...] = dx1
        dg3_ref[...] += dgain3
        dmo, dgain2 = _rms_bwd(mo_ref[...], g2_ref[...], dx1)
        dmo_ref[...] = dmo.astype(BF16)
        dg2_ref[...] += dgain2

    return _row_call(body, "rms_bwd_b", [(dx2, "row"), (dh2, "row"), (x1, "row"), (g3, "full"), (mo, "row"), (g2, "full")],
                     [(_sds((S, D), F32), "row"), (_sds((S, D), BF16), "row"), (_sds((1, D), F32), "acc"),
                      (_sds((1, D), F32), "acc")], tr)


def _rms_bwd_c(dx1, dh, x, g1, tr):
    S, D = x.shape

    def body(dx1_ref, dh_ref, x_ref, g_ref, o_ref, dg_ref):
        _first_step_zero(dg_ref)
        d, dgain = _rms_bwd(x_ref[...], g_ref[...], dh_ref[...])
        o_ref[...] = dx1_ref[...] + d
        dg_ref[...] += dgain

    return _row_call(body, "rms_bwd_c", [(dx1, "row"), (dh, "row"), (x, "row"), (g1, "full")],
                     [(_sds((S, D), F32), "row"), (_sds((1, D), F32), "acc")], tr)


def _gate_bwd(dmixed, proj, ya, yp, gate_cb, tr):
    S, D = ya.shape

    def body(dm_ref, ga_ref, gp_ref, ya_ref, yp_ref, dya_ref, dyp_ref, dga_ref, dgp_ref):
        dm = dm_ref[...]
        sa = _sigmoid(ga_ref[...])
        sp = _sigmoid(gp_ref[...])
        dya_ref[...] = (dm * sa).astype(BF16)
        dyp_ref[...] = (dm * sp).astype(BF16)
        dga_ref[...] = (dm * ya_ref[...] * sa * (1.0 - sa)).astype(BF16)
        dgp_ref[...] = (dm * yp_ref[...] * sp * (1.0 - sp)).astype(BF16)

    return _row_call(body, "gate_bwd",
                     [(dmixed, "row"), (proj, "row", D, gate_cb), (proj, "row", D, gate_cb + 1), (ya, "row"), (yp, "row")],
                     [(_sds((S, D), BF16), "row")] * 4, tr)


def _scale_bwd(dps, pg, scale, tr):
    S, W = dps.shape

    def body(d_ref, pg_ref, s_ref, o_ref, ds_ref):
        _first_step_zero(ds_ref)
        d = d_ref[...]
        o_ref[...] = (d * s_ref[...]).astype(BF16)
        ds_ref[...] += jnp.sum(d * pg_ref[...], axis=0, keepdims=True)

    return _row_call(body, "scale_bwd", [(dps, "row"), (pg, "row"), (scale, "full")],
                     [(_sds((S, W), BF16), "row"), (_sds((1, W), F32), "acc")], tr)


def _tri(blk, cmp):
    j = lax.broadcasted_iota(jnp.int32, (blk, blk), 0)
    s = lax.broadcasted_iota(jnp.int32, (blk, blk), 1)
    one = jnp.concatenate([cmp(j, s).astype(BF16), jnp.ones((blk, 128), BF16)], axis=1)
    return jnp.concatenate([one, one], axis=0)


def _split_dot(x, u2):
    hi = x.astype(BF16)
    lo = (x - hi.astype(F32)).astype(BF16)
    return jnp.dot(jnp.concatenate([hi, lo], axis=1), u2, preferred_element_type=F32)


def _causal(blk):
    return lax.broadcasted_iota(jnp.int32, (blk, blk), 1) < lax.broadcasted_iota(jnp.int32, (blk, blk), 0)


def _scores(q, kj, scale, causal):
    z = lax.dot_general(q, kj, (((1,), (1,)), ((), ())), preferred_element_type=F32) * scale
    l1p = jnp.log(1.0 + jnp.exp(-jnp.abs(z)))
    lb = -(jnp.maximum(z, 0.0) + l1p)
    if causal is not None:
        lb = jnp.where(causal, lb, 0.0)
    return z, lb, jnp.minimum(z, 0.0) - l1p


def _attn_fwd(proj, n_heads, blk, comm=None):
    S = proj.shape[0]
    nq = S // blk
    scale = HEAD_DIM ** -0.5
    lanes = blk // 128
    hp = 2 if n_heads % 2 == 0 else 1
    cols = [slice(h * HEAD_DIM, (h + 1) * HEAD_DIM) for h in range(hp)]
    u_incl = _tri(blk, lambda j, s: j >= s)

    def body(q_ref, k_ref, v_ref, u_ref, o_ref, tot_ref):
        i = pl.program_id(1)
        qs = [q_ref[:, c].astype(BF16) for c in cols]
        u = u_ref[...]

        def step(j, carry, masked):
            ks = pl.multiple_of(j * blk, blk)
            causal = _causal(blk) if masked else None
            out = []
            for h in range(hp):
                acc, run = carry[h]
                kj = k_ref[pl.ds(ks, blk), cols[h]].astype(BF16)
                vj = v_ref[pl.ds(ks, blk), cols[h]].astype(BF16)
                z, lb, _ = _scores(qs[h], kj, scale, causal)
                ct = _split_dot(lb, u)
                a = jnp.exp(z + ct[:, :blk] + jnp.tile(run, (1, lanes)))
                if masked:
                    a = jnp.where(causal, a, 0.0)
                out.append((acc + jnp.dot(a.astype(BF16), vj, preferred_element_type=F32), run + ct[:, blk:]))
            return tuple(out)

        zero = jnp.zeros((blk, HEAD_DIM), F32)
        carry = step(i, ((zero, zero),) * hp, True)
        carry = lax.fori_loop(0, i, lambda t, c: step(i - 1 - t, c, False), carry)
        for h in range(hp):
            o_ref[:, cols[h]] = carry[h][0].astype(BF16)
            tot_ref[:, cols[h]] = carry[h][1]

    G = n_heads // hp
    W = hp * HEAD_DIM
    return _call(
        body, name="attn_fwd", grid=(G, nq),
        in_specs=[pl.BlockSpec((blk, W), lambda h, i: (i, h)),
                  pl.BlockSpec((S, W), lambda h, i: (0, G + h)),
                  pl.BlockSpec((S, W), lambda h, i: (0, 2 * G + h)),
                  pl.BlockSpec(u_incl.shape, lambda h, i: (0, 0))],
        out_specs=[pl.BlockSpec((blk, W), lambda h, i: (i, h))] * 2,
        out_shape=[_sds((S, n_heads * HEAD_DIM), BF16), _sds((S, n_heads * HEAD_DIM), F32)],
        args=(proj, proj, proj, u_incl), sem=("parallel", "arbitrary"), comm=comm)


def _attn_bwd(proj, tot, do, n_heads, blk, comm=None):
    S = proj.shape[0]
    nq = S // blk
    scale = HEAD_DIM ** -0.5
    lanes = blk // 128
    hp = 2 if n_heads % 2 == 0 else 1
    cols = [slice(h * HEAD_DIM, (h + 1) * HEAD_DIM) for h in range(hp)]
    l_strict = _tri(blk, lambda j, s: j < s)
    l_incl = _tri(blk, lambda j, s: j <= s)

    def body(q_ref, k_ref, v_ref, tot_ref, do_ref, ls_ref, li_ref, dq_ref, dk_ref, dv_ref, dk_acc, dv_acc):
        i = pl.program_id(1)

        @pl.when(i == 0)
        def _():
            dk_acc[...] = jnp.zeros_like(dk_acc)
            dv_acc[...] = jnp.zeros_like(dv_acc)

        qs = [q_ref[:, c].astype(BF16) for c in cols]
        dobs = [do_ref[:, c].astype(BF16) for c in cols]
        totals = [jnp.tile(tot_ref[:, c], (1, lanes)) for c in cols]
        ls = ls_ref[...]
        li = li_ref[...]

        def step(j, carry, masked):
            ks = pl.multiple_of(j * blk, blk)
            causal = _causal(blk) if masked else None
            out = []
            for h in range(hp):
                dq, run_lb, run_g = carry[h]
                kj = k_ref[pl.ds(ks, blk), cols[h]].astype(BF16)
                vj = v_ref[pl.ds(ks, blk), cols[h]].astype(BF16)
                z, lb, log_beta = _scores(qs[h], kj, scale, causal)
                beta = jnp.exp(log_beta)
                pt = _split_dot(lb, ls)
                a = jnp.exp(z + totals[h] - (pt[:, :blk] + jnp.tile(run_lb, (1, lanes))))
                if masked:
                    a = jnp.where(causal, a, 0.0)
                da = lax.dot_general(dobs[h], vj, (((1,), (1,)), ((), ())), preferred_element_type=F32)
                g = a * da
                gt = _split_dot(g, li)
                dz = g - beta * (gt[:, :blk] + jnp.tile(run_g, (1, lanes)))
                if masked:
                    dz = jnp.where(causal, dz, 0.0)
                dzs = (dz * scale).astype(BF16)
                dq = dq + jnp.dot(dzs, kj, preferred_element_type=F32)
                dk_acc[pl.ds(ks, blk), cols[h]] += lax.dot_general(dzs, qs[h], (((0,), (0,)), ((), ())),
                                                                   preferred_element_type=F32)
                dv_acc[pl.ds(ks, blk), cols[h]] += lax.dot_general(a.astype(BF16), dobs[h], (((0,), (0,)), ((), ())),
                                                                   preferred_element_type=F32)
                out.append((dq, run_lb + pt[:, blk:], run_g + gt[:, blk:]))
            return tuple(out)

        zero = jnp.zeros((blk, HEAD_DIM), F32)
        carry = lax.fori_loop(0, i, lambda j, c: step(j, c, False), ((zero, zero, zero),) * hp)
        carry = step(i, carry, True)
        for h in range(hp):
            dq_ref[:, cols[h]] = carry[h][0].astype(BF16)

        @pl.when(i == nq - 1)
        def _():
            dk_ref[...] = dk_acc[...].astype(BF16)
            dv_ref[...] = dv_acc[...].astype(BF16)

    G = n_heads // hp
    W = hp * HEAD_DIM
    AW = n_heads * HEAD_DIM
    return _call(
        body, name="attn_bwd", grid=(G, nq),
        in_specs=[pl.BlockSpec((blk, W), lambda h, i: (i, h)),
                  pl.BlockSpec((S, W), lambda h, i: (0, G + h)),
                  pl.BlockSpec((S, W), lambda h, i: (0, 2 * G + h)),
                  pl.BlockSpec((blk, W), lambda h, i: (i, h)),
                  pl.BlockSpec((blk, W), lambda h, i: (i, h)),
                  pl.BlockSpec(l_strict.shape, lambda h, i: (0, 0)),
                  pl.BlockSpec(l_incl.shape, lambda h, i: (0, 0))],
        out_specs=[pl.BlockSpec((blk, W), lambda h, i: (i, h)),
                   pl.BlockSpec((S, W), lambda h, i: (0, h)),
                   pl.BlockSpec((S, W), lambda h, i: (0, h))],
        out_shape=[_sds((S, AW), BF16)] * 3, args=(proj, proj, proj, tot, do, l_strict, l_incl),
        scratch=[pltpu.VMEM((S, W), F32)] * 2, sem=("parallel", "arbitrary"), comm=comm)


def _pool_count(r0, rows, w):
    t = r0 + lax.broadcasted_iota(jnp.int32, (rows, 1), 0)
    return jnp.minimum(t + 1, w).astype(F32)


def _pool_fwd(proj, col_blk, n_groups, width, chunk):
    S = proj.shape[0]
    H = POOL_HALO

    def body(u_ref, o_ref, pad_ref):
        g = pl.program_id(0)
        pad_ref[0:H, :] = jnp.zeros((H, width), F32)
        pad_ref[H:, :] = u_ref[...]
        for gi, w in enumerate(POOL_WINDOWS[:n_groups]):
            @pl.when(g == gi)
            def _(w=w):
                def one(c, _):
                    r0 = pl.multiple_of(c * chunk, chunk)
                    ext = pad_ref[pl.ds(r0, chunk + H), :]
                    s = ext
                    k = 1
                    while k < w:
                        s = s + pltpu.roll(s, k, 0)
                        k *= 2
                    o_ref[pl.ds(r0, chunk), :] = (s[H:] / _pool_count(r0, chunk, w) - ext[H:]).astype(BF16)
                    return 0

                lax.fori_loop(0, S // chunk, one, 0)

    return pl.pallas_call(
        body, name="pool_fwd", grid=(n_groups,),
        in_specs=[pl.BlockSpec((S, width), lambda g: (0, col_blk + g))],
        out_specs=pl.BlockSpec((S, width), lambda g: (0, g)), out_shape=_sds((S, n_groups * width), BF16),
        scratch_shapes=[pltpu.VMEM((S + H, width), F32)], compiler_params=_params("parallel"),
    )(proj)


def _pool_bwd(dpooled, n_groups, chunk):
    S = dpooled.shape[0]
    width = dpooled.shape[1] // n_groups
    H = POOL_HALO

    def body(d_ref, o_ref, pad_ref):
        g = pl.program_id(0)
        pad_ref[S:, :] = jnp.zeros((H, width), F32)
        for gi, w in enumerate(POOL_WINDOWS[:n_groups]):
            @pl.when(g == gi)
            def _(w=w):
                def fill(c, _):
                    r0 = pl.multiple_of(c * chunk, chunk)
                    pad_ref[pl.ds(r0, chunk), :] = d_ref[pl.ds(r0, chunk), :] / _pool_count(r0, chunk, w)
                    return 0

                lax.fori_loop(0, S // chunk, fill, 0)

                def one(c, _):
                    r0 = pl.multiple_of(c * chunk, chunk)
                    s = pad_ref[pl.ds(r0, chunk + H), :]
                    k = 1
                    while k < w:
                        s = s + pltpu.roll(s, chunk + H - k, 0)
                        k *= 2
                    o_ref[pl.ds(r0, chunk), :] = (s[:chunk] - d_ref[pl.ds(r0, chunk), :]).astype(BF16)
                    return 0

                lax.fori_loop(0, S // chunk, one, 0)

    return pl.pallas_call(
        body, name="pool_bwd", grid=(n_groups,),
        in_specs=[pl.BlockSpec((S, width), lambda g: (0, g))],
        out_specs=pl.BlockSpec((S, width), lambda g: (0, g)), out_shape=_sds((S, n_groups * width), BF16),
        scratch_shapes=[pltpu.VMEM((S + H, width), F32)], compiler_params=_params("parallel"),
    )(dpooled)


def _conv3(x_ext, w, b):
    return b + pltpu.roll(x_ext, 2, 0) * w[0:1, :] + pltpu.roll(x_ext, 1, 0) * w[1:2, :] + x_ext * w[2:3, :]


def _gelu_parts(x):
    th = jnp.tanh(GELU_C0 * (x + GELU_C1 * (x * x * x)))
    return th, 0.5 * (1.0 + th)


def _conv_specs(S, F, cb):
    nb = F // cb
    return [pl.BlockSpec((S, cb), lambda j: (0, j)), pl.BlockSpec((S, cb), lambda j: (0, nb + j)),
            pl.BlockSpec((3, cb), lambda j: (0, j)), pl.BlockSpec((3, cb), lambda j: (0, nb + j)),
            pl.BlockSpec((1, cb), lambda j: (0, j)), pl.BlockSpec((1, cb), lambda j: (0, nb + j))]


def _conv_fwd(upre, cw, cb_, chunk):
    S, F2 = upre.shape
    F = F2 // 2
    cb = 128
    H = CONV_HALO

    def body(g_ref, v_ref, wg_ref, wv_ref, bg_ref, bv_ref, o_ref, pg_ref, pv_ref):
        pg_ref[0:H, :] = jnp.zeros((H, cb), F32)
        pv_ref[0:H, :] = jnp.zeros((H, cb), F32)
        pg_ref[H:, :] = g_ref[...]
        pv_ref[H:, :] = v_ref[...]
        wg, wv, bg, bv = wg_ref[...], wv_ref[...], bg_ref[...], bv_ref[...]

        def one(c, _):
            r0 = pl.multiple_of(c * chunk, chunk)
            up_g = _conv3(pg_ref[pl.ds(r0, chunk + H), :], wg, bg)[H:]
            up_v = _conv3(pv_ref[pl.ds(r0, chunk + H), :], wv, bv)[H:]
            _, cdf = _gelu_parts(up_g)
            o_ref[pl.ds(r0, chunk), :] = (up_g * cdf * up_v).astype(BF16)
            return 0

        lax.fori_loop(0, S // chunk, one, 0)

    return pl.pallas_call(
        body, name="conv_fwd", grid=(F // cb,), in_specs=_conv_specs(S, F, cb),
        out_specs=pl.BlockSpec((S, cb), lambda j: (0, j)), out_shape=_sds((S, F), BF16),
        scratch_shapes=[pltpu.VMEM((S + H, cb), F32)] * 2, compiler_params=_params("parallel"),
    )(upre, upre, cw, cw, cb_, cb_)


def _conv_bwd(upre, dact, cw, cb_, chunk, comm=None):
    S, F2 = upre.shape
    F = F2 // 2
    cb = 128
    H = CONV_HALO
    E = chunk + 2 * H

    def body(g_ref, v_ref, wg_ref, wv_ref, bg_ref, bv_ref, d_ref, dg_ref, dv_ref, dwg_ref, dwv_ref, dbg_ref, dbv_ref,
             pg_ref, pv_ref, pd_ref):
        for p, src in ((pg_ref, g_ref), (pv_ref, v_ref), (pd_ref, d_ref)):
            p[0:H, :] = jnp.zeros((H, cb), F32)
            p[H:S + H, :] = src[...]
            p[S + H:, :] = jnp.zeros((H, cb), F32)
        wg, wv, bg, bv = wg_ref[...], wv_ref[...], bg_ref[...], bv_ref[...]

        def taps_bwd(d, w):
            return d * w[2:3, :] + pltpu.roll(d, E - 1, 0) * w[1:2, :] + pltpu.roll(d, E - 2, 0) * w[0:1, :]

        def wsum(d, x):
            dc = d[H:H + chunk]
            return [jnp.sum(dc * pltpu.roll(x, 2, 0)[H:H + chunk], axis=0, keepdims=True),
                    jnp.sum(dc * pltpu.roll(x, 1, 0)[H:H + chunk], axis=0, keepdims=True),
                    jnp.sum(dc * x[H:H + chunk], axis=0, keepdims=True),
                    jnp.sum(dc, axis=0, keepdims=True)]

        def one(c, acc):
            r0 = pl.multiple_of(c * chunk, chunk)
            xg = pg_ref[pl.ds(r0, E), :]
            xv = pv_ref[pl.ds(r0, E), :]
            d = pd_ref[pl.ds(r0, E), :]
            up_g = _conv3(xg, wg, bg)
            up_v = _conv3(xv, wv, bv)
            th, cdf = _gelu_parts(up_g)
            dgelu = cdf + 0.5 * up_g * (1.0 - th * th) * (GELU_C0 * (1.0 + 3.0 * GELU_C1 * (up_g * up_g)))
            dgate = d * up_v * dgelu
            dval = d * (up_g * cdf)
            dg_ref[pl.ds(r0, chunk), :] = taps_bwd(dgate, wg)[H:H + chunk].astype(BF16)
            dv_ref[pl.ds(r0, chunk), :] = taps_bwd(dval, wv)[H:H + chunk].astype(BF16)
            return tuple(a + b for a, b in zip(acc, wsum(dgate, xg) + wsum(dval, xv)))

        zero = jnp.zeros((1, cb), F32)
        acc = lax.fori_loop(0, S // chunk, one, (zero,) * 8)
        dwg_ref[...] = jnp.concatenate(acc[0:3], axis=0)
        dbg_ref[...] = acc[3]
        dwv_ref[...] = jnp.concatenate(acc[4:7], axis=0)
        dbv_ref[...] = acc[7]

    col = lambda rows: pl.BlockSpec((rows, cb), lambda j: (0, j))
    return _call(
        body, name="conv_bwd", grid=(F // cb,), in_specs=_conv_specs(S, F, cb) + [col(S)],
        out_specs=[col(S), col(S), col(3), col(3), col(1), col(1)],
        out_shape=[_sds((S, F), BF16), _sds((S, F), BF16), _sds((3, F), F32), _sds((3, F), F32), _sds((1, F), F32),
                   _sds((1, F), F32)],
        args=(upre, upre, cw, cw, cb_, cb_, dact), scratch=[pltpu.VMEM((S + 2 * H, cb), F32)] * 3, sem=("parallel",),
        comm=comm)


def _position():
    x, y, c = lax.axis_index("x"), lax.axis_index("y"), lax.axis_index("c")
    return x, y, c, 4 * x + 2 * y + c


def _peer(x, y, c, d):
    px = 1 - x if d & 4 else x
    py = 1 - y if d & 2 else y
    pc = 1 - c if d & 1 else c
    return (px, py, pc), 4 * px + 2 * py + pc


def _all_gather_comm(tensors):
    nt = len(tensors)
    outs = [_sds((N_DEV,) + t.shape, t.dtype) for t in tensors]
    sems = [pltpu.SemaphoreType.DMA((7 * nt,)), pltpu.SemaphoreType.DMA((7 * nt,)), pltpu.SemaphoreType.DMA((nt,))]

    def parts(ins, outs_, sem_refs):
        send, recv, loc = sem_refs
        x, y, c, me = _position()
        chips = [(1 - x, y), (x, 1 - y), (1 - x, 1 - y)]

        def copy(t, k, block, to, src=None):
            slot = outs_[t].at[4 * block[0] + 2 * block[1] + block[2]]
            return pltpu.make_async_remote_copy(src_ref=slot if src is None else src, dst_ref=slot,
                                                send_sem=send.at[7 * t + k], recv_sem=recv.at[7 * t + k], device_id=to,
                                                device_id_type=MESH)

        def mine(t):
            return pltpu.make_async_copy(ins[t], outs_[t].at[me], loc.at[t])

        return (x, y, c), (x, y, 1 - c), chips, copy, mine

    def start(ins, outs_, sem_refs):
        me, sibling, chips, copy, mine = parts(ins, outs_, sem_refs)
        for t in range(nt):
            mine(t).start()
            copy(t, 0, me, sibling, src=ins[t]).start()
            for j, chip in enumerate(chips):
                copy(t, 1 + j, me, (*chip, me[2]), src=ins[t]).start()

    def finish(ins, outs_, sem_refs):
        me, sibling, chips, copy, mine = parts(ins, outs_, sem_refs)
        c = me[2]
        for t in range(nt):
            for j, chip in enumerate(chips):
                copy(t, 1 + j, (*chip, c), me).wait_recv()
                copy(t, 4 + j, (*chip, c), sibling).start()
        for t in range(nt):
            copy(t, 0, sibling, me).wait_recv()
            for j, chip in enumerate(chips):
                copy(t, 4 + j, (*chip, 1 - c), me).wait_recv()
        for t in range(nt):
            copy(t, 0, me, sibling, src=ins[t]).wait_send()
            for j, chip in enumerate(chips):
                copy(t, 1 + j, me, (*chip, c), src=ins[t]).wait_send()
                copy(t, 4 + j, (*chip, c), sibling).wait_send()
            mine(t).wait()

    return _Comm(tensors, outs, sems, start, finish)


def _reduce_scatter_comm(tensors):
    nt = len(tensors)
    outs = [_sds(t.shape, t.dtype) for t in tensors]
    sems = [pltpu.SemaphoreType.DMA((7 * nt,)), pltpu.SemaphoreType.DMA((7 * nt,)), pltpu.SemaphoreType.DMA((nt,))]

    def local(ins, outs_, sem_refs, t, me):
        return pltpu.make_async_copy(ins[t].at[me], outs_[t].at[me], sem_refs[2].at[t])

    def remote(ins, outs_, sem_refs, t, d, inbound):
        x, y, c, me = _position()
        peer, peer_idx = _peer(x, y, c, d)
        k = 7 * t + d - 1
        src, dst, to = (ins[t].at[me], outs_[t].at[peer_idx], (x, y, c)) if inbound else (ins[t].at[peer_idx], outs_[t].at[me], peer)
        return pltpu.make_async_remote_copy(src_ref=src, dst_ref=dst, send_sem=sem_refs[0].at[k], recv_sem=sem_refs[1].at[k],
                                            device_id=to, device_id_type=MESH)

    def start(ins, outs_, sem_refs):
        me = _position()[3]
        for t in range(nt):
            local(ins, outs_, sem_refs, t, me).start()
            for d in range(1, N_DEV):
                remote(ins, outs_, sem_refs, t, d, False).start()

    def finish(ins, outs_, sem_refs):
        me = _position()[3]
        for t in range(nt):
            for d in range(1, N_DEV):
                remote(ins, outs_, sem_refs, t, d, True).wait_recv()
        for t in range(nt):
            for d in range(1, N_DEV):
                remote(ins, outs_, sem_refs, t, d, False).wait_send()
            local(ins, outs_, sem_refs, t, me).wait()

    return _Comm(tensors, outs, sems, start, finish)


def _all_reduce_small(part):
    r, W = part.shape

    def body(p_ref, o_ref, g_ref, send_sems, recv_sems):
        x, y, c, me = _position()
        sends = []
        for d in range(1, N_DEV):
            peer, _ = _peer(x, y, c, d)
            cp = pltpu.make_async_remote_copy(src_ref=p_ref, dst_ref=g_ref.at[me], send_sem=send_sems.at[d - 1],
                                              recv_sem=recv_sems.at[d - 1], device_id=peer, device_id_type=MESH)
            cp.start()
            sends.append(cp)
        g_ref[me] = p_ref[...]
        for d in range(1, N_DEV):
            _, peer_idx = _peer(x, y, c, d)
            pltpu.make_async_remote_copy(src_ref=p_ref, dst_ref=g_ref.at[peer_idx], send_sem=send_sems.at[d - 1],
                                         recv_sem=recv_sems.at[d - 1], device_id=(x, y, c), device_id_type=MESH).wait_recv()
        for cp in sends:
            cp.wait_send()
        acc = g_ref[0]
        for i in range(1, N_DEV):
            acc = acc + g_ref[i]
        o_ref[...] = acc

    vmem = pl.BlockSpec(memory_space=pltpu.VMEM)
    return pl.pallas_call(
        body, name="all_reduce_small", in_specs=[vmem], out_specs=[vmem, vmem],
        out_shape=[_sds((r, W), F32), _sds((N_DEV, r, W), F32)],
        scratch_shapes=[pltpu.SemaphoreType.DMA((7,)), pltpu.SemaphoreType.DMA((7,))],
        compiler_params=pltpu.CompilerParams(has_side_effects=True, vmem_limit_bytes=VMEM_LIMIT),
    )(part)[0]


def _adamw_math(w, g, m, v):
    m = ADAM_B1 * m + (1.0 - ADAM_B1) * g
    v = ADAM_B2 * v + (1.0 - ADAM_B2) * (g * g)
    m_hat = m / (1.0 - ADAM_B1 ** ADAM_STEP)
    v_hat = v / (1.0 - ADAM_B2 ** ADAM_STEP)
    return -ADAM_LR * (m_hat / (jnp.sqrt(v_hat) + ADAM_EPS) + ADAM_WD * w), m, v


def _adamw(w, g, m, v, name):
    rows, cols = w.shape
    tr = _tile(rows, max(8, (2**18 // cols) // 8 * 8), 8)

    def body(w_ref, g_ref, m_ref, v_ref, d_ref, nm_ref, nv_ref):
        d_ref[...], nm_ref[...], nv_ref[...] = _adamw_math(w_ref[...], g_ref[...], m_ref[...], v_ref[...])

    spec = pl.BlockSpec((tr, cols), lambda i: (i, 0))
    return pl.pallas_call(
        body, name=name, grid=(rows // tr,), in_specs=[spec] * 4, out_specs=[spec] * 3,
        out_shape=[_sds((rows, cols), F32)] * 3, compiler_params=_params("parallel"),
    )(w, g, m, v)


def _adamw_sum(recv, w, m, v, name):
    n, rows, cols = recv.shape
    tr = _tile(rows, max(16, (2**17 // cols) // 16 * 16), 16)

    def body(r_ref, w_ref, m_ref, v_ref, g_ref, d_ref, nm_ref, nv_ref):
        g = r_ref[0].astype(F32)
        for i in range(1, n):
            g = g + r_ref[i].astype(F32)
        g_ref[...] = g
        d_ref[...], nm_ref[...], nv_ref[...] = _adamw_math(w_ref[...], g, m_ref[...], v_ref[...])

    spec = pl.BlockSpec((tr, cols), lambda i: (i, 0))
    return pl.pallas_call(
        body, name=name, grid=(rows // tr,), in_specs=[pl.BlockSpec((n, tr, cols), lambda i: (0, i, 0))] + [spec] * 3,
        out_specs=[spec] * 4, out_shape=[_sds((rows, cols), F32)] * 4, compiler_params=_params("parallel"),
    )(recv, w, m, v)


COLUMN_CUT = ("w_in", "w_attn_branch", "w_pool_branch", "w_up", "w_ple")
ROW_CUT = ("w_out", "w_down", "w_ple_gate")
REPLICATED = ("norm_mix_pre", "pool_scale", "norm_mix_post", "norm_ffn_pre", "conv_b", "norm_ffn_post", "norm_ple_post")
WEIGHTS = ("norm_mix_pre", "w_in", "w_attn_branch", "w_pool_group", "pool_scale", "w_pool_branch", "w_out", "norm_mix_post",
           "norm_ffn_pre", "w_up", "conv_w", "conv_b", "w_down", "norm_ffn_post", "w_ple", "w_ple_gate", "norm_ple_post")


def _size(shape):
    n = 1
    for s in shape:
        n *= s
    return n


def _pad_rows(flat, row_align):
    n = flat.shape[-1]
    per = PACK_W * row_align
    total = -(-n // per) * per
    return jnp.pad(flat, [(0, total - n)]).reshape(total // PACK_W, PACK_W)


def _natural(shard_major):
    n, r, c = shard_major.shape
    return shard_major.reshape(n * r, c)


def kernel(x, p, norm_mix_pre, w_in, w_attn_branch, w_pool_group, pool_scale, w_pool_branch, w_out, norm_mix_post, norm_ffn_pre, w_up, conv_w, conv_b, w_down, norm_ffn_post, w_ple, w_ple_gate, norm_ple_post, loss_target, m_norm_mix_pre, m_w_in, m_w_attn_branch, m_w_pool_group, m_pool_scale, m_w_pool_branch, m_w_out, m_norm_mix_post, m_norm_ffn_pre, m_w_up, m_conv_w, m_conv_b, m_w_down, m_norm_ffn_post, m_w_ple, m_w_ple_gate, m_norm_ple_post, v_norm_mix_pre, v_w_in, v_w_attn_branch, v_w_pool_group, v_pool_scale, v_w_pool_branch, v_w_out, v_norm_mix_post, v_norm_ffn_pre, v_w_up, v_conv_w, v_conv_b, v_w_down, v_norm_ffn_post, v_w_ple, v_w_ple_gate, v_norm_ple_post):
    given = dict(locals())
    wts = {n: given[n][0] for n in WEIGHTS}
    mom = {n: given["m_" + n][0] for n in WEIGHTS}
    var = {n: given["v_" + n][0] for n in WEIGHTS}
    xs = x[0]
    ps_in = p[0, 0]
    tgt = loss_target[0]
    S, D = xs.shape
    AW = wts["w_attn_branch"].shape[0]
    PW = wts["w_pool_branch"].shape[0]
    G = wts["w_pool_group"].shape[0]
    PGW = PW // G
    H = AW // HEAD_DIM
    F = wts["w_down"].shape[0] * N_DEV
    assert (3 * AW) % PGW == 0 and (3 * AW + PW) % D == 0 and PGW % 128 == 0 and F % 128 == 0
    tr = _tile(S, 256, 16)
    blk = _tile(S, 256, 128)
    chunk = _tile(S, 256, 8)
    me = 4 * lax.axis_index("x") + 2 * lax.axis_index("y") + lax.axis_index("c")

    cw_shape = wts["conv_w"].shape
    conv_b_row = wts["conv_b"].reshape(1, -1)
    g1, g2, g3, g4, g5 = (wts[n].reshape(1, D) for n in
                          ("norm_mix_pre", "norm_mix_post", "norm_ffn_pre", "norm_ffn_post", "norm_ple_post"))
    pscale = wts["pool_scale"].reshape(1, PW)
    big = 4096
    wb = {n: wts[n].astype(BF16) for n in COLUMN_CUT + ROW_CUT}
    wb["w_pool_group"] = wts["w_pool_group"].astype(BF16).reshape(G * PGW // N_DEV, PGW)

    h, w_in = _rms_fwd(xs, g1, tr, _all_gather_comm([wb["w_in"]]))
    proj, w_ab, w_pg, w_pb, w_out = _mm(
        h, w_in, b_sm=True, name="mm_in",
        comm=_all_gather_comm([wb["w_attn_branch"], wb["w_pool_group"], wb["w_pool_branch"], wb["w_out"]]))
    w_pg = jnp.moveaxis(w_pg.reshape(N_DEV, G, PGW // N_DEV, PGW), 0, 1).reshape(G, PGW, PGW)
    w_out = _natural(w_out)
    attn, tot, w_up, conv_w_all = _attn_fwd(proj, H, blk, _all_gather_comm([wb["w_up"], wts["conv_w"]]))
    conv_w_full = jnp.moveaxis(conv_w_all, 0, 1).reshape(cw_shape[0], N_DEV * cw_shape[1])
    y_attn = _mm(attn, w_ab, b_sm=True, name="mm_attn_branch", tm=big, tn=256, tk=big)
    pooled = _pool_fwd(proj, 3 * AW // PGW, G, PGW, chunk)
    pg, ps = _pool_group_fwd(pooled, w_pg, pscale)
    y_pool = _mm(ps, w_pb, b_sm=True, name="mm_pool_branch", tm=big, tn=256, tk=big)
    gate_cb = (3 * AW + PW) // D
    mixed = _gate_mix(proj, y_attn, y_pool, gate_cb, tr)
    mo = _mm(mixed, w_out, name="mm_out")
    x1, h2 = _resid_rms2(xs, mo, g2, g3, tr)
    upre, w_down, w_ple, w_pleg = _mm(h2, w_up, b_sm=True, name="mm_up", tn=2048, tk=512,
                                      comm=_all_gather_comm([wb["w_down"], wb["w_ple"], wb["w_ple_gate"]]))
    w_down, w_pleg = _natural(w_down), _natural(w_pleg)
    act = _conv_fwd(upre, conv_w_full, conv_b_row, chunk)
    yf = _mm(act, w_down, name="mm_down")
    x2, x2b = _resid_rms(x1, yf, g4, tr)
    e = _mm(ps_in, w_ple, b_sm=True, name="mm_ple", tm=big, tn=256, tk=big)
    gl = _mm(x2b, w_pleg, name="mm_ple_gate")
    loss_part, dx3, de, dgl, dg5 = _ple_loss(gl, e, x2, tgt, g5, tr)

    shards = lambda natural: natural.reshape((N_DEV, natural.shape[0] // N_DEV) + natural.shape[1:])
    recv = {}
    dw_ple = _mm(ps_in, de, ta=True, out_sm=True, out_dtype=BF16, name="mm_d_w_ple", tm=256, tn=256, tk=big)
    dw_pleg = shards(_mm(x2b, dgl, ta=True, out_dtype=BF16, name="mm_d_w_ple_gate"))
    dx2g = _mm(dgl, w_pleg, tb=True, name="mm_d_x2")
    dx2, dyf, dg4 = _rms_bwd_a(dx3, dx2g, yf, g4, tr)
    dw_down = shards(_mm(act, dyf, ta=True, out_dtype=BF16, name="mm_d_w_down"))
    dact, recv["w_ple"], recv["w_ple_gate"] = _mm(dyf, w_down, tb=True, name="mm_d_act", tn=1408, tk=512,
                                                  comm=_reduce_scatter_comm([dw_ple, dw_pleg]))
    dup_g, dup_v, dcw_g, dcw_v, dcb_g, dcb_v, recv["w_down"] = _conv_bwd(upre, dact, conv_w_full, conv_b_row, chunk,
                                                                          _reduce_scatter_comm([dw_down]))
    dupre = jnp.concatenate([dup_g, dup_v], axis=1)
    dw_up = _mm(h2, dupre, ta=True, out_sm=True, out_dtype=BF16, name="mm_d_w_up", tn=2048)
    dh2 = _mm(dupre, w_up, tb=True, b_sm=True, name="mm_d_h2", tk=2048)
    dx1, dmo, dg3, dg2 = _rms_bwd_b(dx2, dh2, x1, g3, mo, g2, tr)
    dmixed = _mm(dmo, w_out, tb=True, name="mm_d_mixed")
    dw_out = shards(_mm(mixed, dmo, ta=True, out_dtype=BF16, name="mm_d_w_out"))
    dya, dyp, dga, dgp = _gate_bwd(dmixed, proj, y_attn, y_pool, gate_cb, tr)
    dps = _mm(dyp, w_pb, tb=True, b_sm=True, name="mm_d_ps", tm=2048, tk=256)
    dw_pb = _mm(ps, dyp, ta=True, out_sm=True, out_dtype=BF16, name="mm_d_w_pool_branch", tn=256, tk=big)
    dpg, dscale = _scale_bwd(dps, pg, pscale, tr)
    dpooled = _pool_group_bwd_x(dpg, w_pg)
    dw_pg = _pool_group_bwd_w(pooled, dpg, G)
    dw_pg = jnp.moveaxis(dw_pg.astype(BF16).reshape(G, N_DEV, PGW // N_DEV, PGW), 1, 0).reshape(N_DEV, G * PGW // N_DEV, PGW)
    du = _pool_bwd(dpooled, G, chunk)
    dattn = _mm(dya, w_ab, tb=True, b_sm=True, name="mm_d_attn", tm=2048, tk=256)
    dw_ab = _mm(attn, dya, ta=True, out_sm=True, out_dtype=BF16, name="mm_d_w_attn_branch", tn=256, tk=big)
    dq, dk, dv, recv["w_up"], recv["w_out"], recv["w_pool_branch"], recv["w_pool_group"], recv["w_attn_branch"] = _attn_bwd(
        proj, tot, dattn, H, blk, _reduce_scatter_comm([dw_up, dw_out, dw_pb, dw_pg, dw_ab]))
    dproj = jnp.concatenate([dq, dk, dv, du, dga, dgp], axis=1)
    dw_in = _mm(h, dproj, ta=True, out_sm=True, out_dtype=BF16, name="mm_d_w_in")
    dh, recv["w_in"] = _mm(dproj, w_in, tb=True, b_sm=True, name="mm_d_h", tk=1024, comm=_reduce_scatter_comm([dw_in]))
    grad_x, dg1 = _rms_bwd_c(dx1, dh, xs, g1, tr)

    gshard, delta, new_m, new_v = {}, {}, {}, {}
    for n in COLUMN_CUT + ROW_CUT + ("w_pool_group",):
        shp = wts[n].shape
        two_d = (_size(shp[:-1]), shp[-1])
        g_, d_, m_, v_ = _adamw_sum(recv[n].reshape((N_DEV,) + two_d), wts[n].reshape(two_d), mom[n].reshape(two_d),
                                    var[n].reshape(two_d), "adamw_" + n)
        gshard[n], delta[n], new_m[n], new_v[n] = g_.reshape(shp), d_.reshape(shp), m_.reshape(shp), v_.reshape(shp)

    dconv_w = jnp.concatenate([dcw_g, dcw_v], axis=1)
    dconv_b = jnp.concatenate([dcb_g, dcb_v], axis=1).reshape(-1)
    rep_parts = {"norm_mix_pre": dg1, "pool_scale": dscale, "norm_mix_post": dg2, "norm_ffn_pre": dg3, "conv_b": dconv_b,
                 "norm_ffn_post": dg4, "norm_ple_post": dg5}
    small = jnp.concatenate([rep_parts[n].reshape(-1) for n in REPLICATED] + [dconv_w.reshape(-1)])
    n_small = small.shape[0]
    small_sum = _all_reduce_small(_pad_rows(small, 8)).reshape(-1)[:n_small]
    off = 0
    for n in REPLICATED:
        sz = _size(wts[n].shape)
        gshard[n] = small_sum[off:off + sz].reshape(wts[n].shape)
        off += sz
    dconv_w_sum = small_sum[off:off + 3 * 2 * F].reshape(3, 2 * F)
    gshard["conv_w"] = lax.dynamic_slice_in_dim(dconv_w_sum, me * cw_shape[1], cw_shape[1], axis=1)

    delta["conv_w"], new_m["conv_w"], new_v["conv_w"] = _adamw(wts["conv_w"], gshard["conv_w"], mom["conv_w"], var["conv_w"],
                                                               "adamw_conv_w")
    rep_sizes = [_size(wts[n].shape) for n in REPLICATED]
    n_rep = sum(rep_sizes)
    cat = lambda t: _pad_rows(jnp.concatenate([t[n].reshape(-1) for n in REPLICATED]), 8)
    d_, m_, v_ = _adamw(cat(wts), cat(gshard), cat(mom), cat(var), "adamw_replicated")
    off = 0
    for n, sz in zip(REPLICATED, rep_sizes):
        shp = wts[n].shape
        delta[n], new_m[n], new_v[n] = (t.reshape(-1)[off:off + sz].reshape(shp) for t in (d_, m_, v_))
        off += sz
    assert off == n_rep

    loss = lax.psum(loss_part[0, 0], ("x", "y", "c"))
    lead = lambda t: t[None]
    return (loss, grad_x[None], *[lead(gshard[n]) for n in WEIGHTS], *[lead(delta[n]) for n in WEIGHTS],
            *[lead(new_m[n]) for n in WEIGHTS], *[lead(new_v[n]) for n in WEIGHTS])
```

```python
import functools

import jax
import jax.numpy as jnp
from jax import lax
from jax.experimental import pallas as pl
from jax.experimental.pallas import tpu as pltpu

F32 = jnp.float32
BF16 = jnp.bfloat16
MESH = pl.DeviceIdType.MESH

EPS = 1e-6
HEAD_DIM = 128
POOL_WINDOWS = (2, 4, 8, 16)
POOL_HALO = 16
CONV_HALO = 8
GELU_C0 = 0.7978845608028654
GELU_C1 = 0.044715
ADAM_LR = 0.001
ADAM_B1 = 0.9
ADAM_B2 = 0.999
ADAM_EPS = 1e-08
ADAM_WD = 0.01
ADAM_STEP = 10
N_DEV = 8
PACK_W = 1024
VMEM_LIMIT = 56 * 2**20
ANY = pl.BlockSpec(memory_space=pl.ANY)


def _params(*sem):
    return pltpu.CompilerParams(dimension_semantics=sem, vmem_limit_bytes=VMEM_LIMIT)


def _sds(shape, dtype):
    return jax.ShapeDtypeStruct(shape, dtype)


def _tile(dim, target, align):
    if dim <= target:
        return dim
    t = (target // align) * align
    while t >= align:
        if dim % t == 0:
            return t
        t -= align
    return dim


def _sigmoid(x):
    return 1.0 / (1.0 + jnp.exp(-x))


class _Comm:
    def __init__(self, ins, outs, sems, start, finish):
        self.ins, self.outs, self.sems, self.start, self.finish = list(ins), list(outs), list(sems), start, finish


def _call(body, *, name, grid, in_specs, out_specs, out_shape, args, scratch=(), sem=(), comm=None):
    in_specs, out_specs, out_shape, scratch = list(in_specs), list(out_specs), list(out_shape), list(scratch)
    if comm is None:
        return pl.pallas_call(body, name=name, grid=grid, in_specs=in_specs, out_specs=out_specs, out_shape=out_shape,
                              scratch_shapes=scratch, compiler_params=_params(*sem))(*args)
    n_in, n_out, n_scr, n_ci, n_co = len(in_specs), len(out_specs), len(scratch), len(comm.ins), len(comm.outs)

    def wrapped(*refs):
        ins, refs = refs[:n_in], refs[n_in:]
        c_ins, refs = refs[:n_ci], refs[n_ci:]
        outs, refs = refs[:n_out], refs[n_out:]
        c_outs, refs = refs[:n_co], refs[n_co:]
        scr, c_sems = refs[:n_scr], refs[n_scr:]
        first = last = None
        for axis, size in enumerate(grid):
            at_start, at_end = pl.program_id(axis) == 0, pl.program_id(axis) == size - 1
            first = at_start if first is None else jnp.logical_and(first, at_start)
            last = at_end if last is None else jnp.logical_and(last, at_end)
        if grid:
            pl.when(first)(lambda: comm.start(c_ins, c_outs, c_sems))
            body(*ins, *outs, *scr)
            pl.when(last)(lambda: comm.finish(c_ins, c_outs, c_sems))
        else:
            comm.start(c_ins, c_outs, c_sems)
            body(*ins, *outs, *scr)
            comm.finish(c_ins, c_outs, c_sems)

    return pl.pallas_call(
        wrapped, name=name, grid=grid, in_specs=in_specs + [ANY] * n_ci, out_specs=out_specs + [ANY] * n_co,
        out_shape=out_shape + comm.outs, scratch_shapes=scratch + comm.sems,
        compiler_params=pltpu.CompilerParams(dimension_semantics=("arbitrary",) * len(grid), vmem_limit_bytes=VMEM_LIMIT,
                                             has_side_effects=True),
    )(*args, *comm.ins)


def _mm(a, b, *, ta=False, tb=False, b_sm=False, out_sm=False, out_dtype=F32, name, tm=2048, tn=1024, tk=1024, comm=None):
    M, K = (a.shape[1], a.shape[0]) if ta else a.shape
    if b_sm:
        n_sl, rows, per = b.shape
        N = rows if tb else n_sl * per
        assert K == (n_sl * per if tb else rows)
    else:
        N = b.shape[0] if tb else b.shape[1]
    tm = _tile(M, tm, 128)
    tn = _tile(per if (b_sm and not tb) else N // N_DEV if out_sm else N, tn, 128)
    tk = _tile(per if (b_sm and tb) else K, tk, 128)
    nk = K // tk
    a_spec = pl.BlockSpec((tk, tm), lambda i, j, k: (k, i)) if ta else pl.BlockSpec((tm, tk), lambda i, j, k: (i, k))
    if not b_sm:
        b_spec = pl.BlockSpec((tn, tk), lambda i, j, k: (j, k)) if tb else pl.BlockSpec((tk, tn), lambda i, j, k: (k, j))
    elif tb:
        kp = per // tk
        b_spec = pl.BlockSpec((None, tn, tk), lambda i, j, k: (k // kp, j, k % kp))
    else:
        jp = per // tn
        b_spec = pl.BlockSpec((None, tk, tn), lambda i, j, k: (j // jp, k, j % jp))
    if out_sm:
        jo = (N // N_DEV) // tn
        o_spec = pl.BlockSpec((None, tm, tn), lambda i, j, k: (j // jo, i, j % jo))
        o_shape = _sds((N_DEV, M, N // N_DEV), out_dtype)
    else:
        o_spec = pl.BlockSpec((tm, tn), lambda i, j, k: (i, j))
        o_shape = _sds((M, N), out_dtype)
    dims = (((0 if ta else 1,), (1 if tb else 0,)), ((), ()))

    def product(a_ref, b_ref):
        return lax.dot_general(a_ref[...].astype(BF16), b_ref[...].astype(BF16), dims, preferred_element_type=F32)

    def body(a_ref, b_ref, o_ref, acc_ref):
        k = pl.program_id(2)

        @pl.when(k == 0)
        def _():
            acc_ref[...] = jnp.zeros_like(acc_ref)

        acc_ref[...] += product(a_ref, b_ref)

        @pl.when(k == nk - 1)
        def _():
            o_ref[...] = acc_ref[...].astype(o_ref.dtype)

    def body_one_step(a_ref, b_ref, o_ref):
        o_ref[...] = product(a_ref, b_ref).astype(o_ref.dtype)

    res = _call(body if nk > 1 else body_one_step, name=name, grid=(M // tm, N // tn, nk), in_specs=[a_spec, b_spec],
                out_specs=[o_spec], out_shape=[o_shape], args=(a, b), scratch=[pltpu.VMEM((tm, tn), F32)] if nk > 1 else [],
                sem=("parallel", "parallel", "arbitrary"), comm=comm)
    return res[0] if comm is None else res


def _pool_group_fwd(pooled, w_pg, scale):
    S = pooled.shape[0]
    G, C, C2 = w_pg.shape
    tm = _tile(S, 1024, 16)

    def body(a_ref, w_ref, s_ref, pg_ref, ps_ref):
        pg = jnp.dot(a_ref[...], w_ref[...], preferred_element_type=F32)
        pg_ref[...] = pg
        ps_ref[...] = (pg * s_ref[...]).astype(BF16)

    return pl.pallas_call(
        body, name="pool_group_fwd", grid=(G, S // tm),
        in_specs=[pl.BlockSpec((tm, C), lambda g, i: (i, g)), pl.BlockSpec((None, C, C2), lambda g, i: (g, 0, 0)),
                  pl.BlockSpec((1, C2), lambda g, i: (0, g))],
        out_specs=[pl.BlockSpec((tm, C2), lambda g, i: (i, g)), pl.BlockSpec((tm, C2), lambda g, i: (i, g))],
        out_shape=[jax.ShapeDtypeStruct((S, G * C2), F32), jax.ShapeDtypeStruct((S, G * C2), BF16)],
        compiler_params=_params("parallel", "parallel"),
    )(pooled, w_pg, scale)


def _pool_group_bwd_x(dpg, w_pg):
    S = dpg.shape[0]
    G, C, C2 = w_pg.shape
    tm = _tile(S, 1024, 16)

    def body(d_ref, w_ref, o_ref):
        o_ref[...] = lax.dot_general(d_ref[...], w_ref[...], (((1,), (1,)), ((), ())), preferred_element_type=F32)

    return pl.pallas_call(
        body, name="pool_group_bwd_x", grid=(G, S // tm),
        in_specs=[pl.BlockSpec((tm, C2), lambda g, i: (i, g)), pl.BlockSpec((None, C, C2), lambda g, i: (g, 0, 0))],
        out_specs=pl.BlockSpec((tm, C), lambda g, i: (i, g)), out_shape=jax.ShapeDtypeStruct((S, G * C), F32),
        compiler_params=_params("parallel", "parallel"),
    )(dpg, w_pg)


def _pool_group_bwd_w(pooled, dpg, G):
    S = pooled.shape[0]
    C, C2 = pooled.shape[1] // G, dpg.shape[1] // G
    tk = _tile(S, 1024, 16)

    def body(a_ref, d_ref, o_ref):
        @pl.when(pl.program_id(1) == 0)
        def _():
            o_ref[...] = jnp.zeros_like(o_ref)

        o_ref[...] += lax.dot_general(a_ref[...], d_ref[...], (((0,), (0,)), ((), ())), preferred_element_type=F32)

    return pl.pallas_call(
        body, name="pool_group_bwd_w", grid=(G, S // tk),
        in_specs=[pl.BlockSpec((tk, C), lambda g, k: (k, g)), pl.BlockSpec((tk, C2), lambda g, k: (k, g))],
        out_specs=pl.BlockSpec((None, C, C2), lambda g, k: (g, 0, 0)), out_shape=jax.ShapeDtypeStruct((G, C, C2), F32),
        compiler_params=_params("parallel", "arbitrary"),
    )(pooled, dpg)


def _rms(x, gain):
    r = lax.rsqrt(jnp.mean(x * x, axis=-1, keepdims=True) + EPS)
    return x * r * gain


def _rms_bwd(x, gain, dy):
    r = lax.rsqrt(jnp.mean(x * x, axis=-1, keepdims=True) + EPS)
    xh = x * r
    dgain = jnp.sum(dy * xh, axis=0, keepdims=True)
    dxh = dy * gain
    dx = r * (dxh - xh * jnp.mean(dxh * xh, axis=-1, keepdims=True))
    return dx, dgain


def _row_call(body, name, ins, outs, tr, *, comm=None):
    S = None
    in_specs, args = [], []
    for it in ins:
        arr, kind = it[0], it[1]
        if kind == "row":
            S = arr.shape[0]
            if len(it) == 4:
                width, cb = it[2], it[3]
                in_specs.append(pl.BlockSpec((tr, width), functools.partial(lambda i, cb: (i, cb), cb=cb)))
            else:
                in_specs.append(pl.BlockSpec((tr, arr.shape[1]), lambda i: (i, 0)))
        else:
            assert arr.ndim == 2
            in_specs.append(pl.BlockSpec(arr.shape, lambda i: (0, 0)))
        args.append(arr)
    out_specs, out_shape = [], []
    for sds, kind in outs:
        if kind == "row":
            out_specs.append(pl.BlockSpec((tr, sds.shape[1]), lambda i: (i, 0)))
        else:
            assert len(sds.shape) == 2
            out_specs.append(pl.BlockSpec(sds.shape, lambda i: (0, 0)))
        out_shape.append(sds)
    return _call(body, name=name, grid=(S // tr,), in_specs=in_specs, out_specs=out_specs, out_shape=out_shape, args=args,
                 sem=("arbitrary",), comm=comm)


def _first_step_zero(*refs):
    @pl.when(pl.program_id(0) == 0)
    def _():
        for r in refs:
            r[...] = jnp.zeros_like(r)


def _rms_fwd(x, gain, tr, comm):
    def body(x_ref, g_ref, o_ref):
        o_ref[...] = _rms(x_ref[...], g_ref[...]).astype(BF16)

    S, D = x.shape
    return _row_call(body, "rms_fwd", [(x, "row"), (gain, "full")], [(_sds((S, D), BF16), "row")], tr, comm=comm)


def _gate_mix(proj, ya, yp, gate_cb, tr):
    S, D = ya.shape

    def body(ga_ref, gp_ref, ya_ref, yp_ref, o_ref):
        o_ref[...] = (_sigmoid(ga_ref[...]) * ya_ref[...] + _sigmoid(gp_ref[...]) * yp_ref[...]).astype(BF16)

    return _row_call(body, "gate_mix", [(proj, "row", D, gate_cb), (proj, "row", D, gate_cb + 1), (ya, "row"), (yp, "row")],
                     [(_sds((S, D), BF16), "row")], tr)[0]


def _resid_rms2(x, mo, g2, g3, tr):
    S, D = x.shape

    def body(x_ref, mo_ref, g2_ref, g3_ref, x1_ref, h2_ref):
        x1 = x_ref[...] + _rms(mo_ref[...], g2_ref[...])
        x1_ref[...] = x1
        h2_ref[...] = _rms(x1, g3_ref[...]).astype(BF16)

    return _row_call(body, "resid_rms2", [(x, "row"), (mo, "row"), (g2, "full"), (g3, "full")],
                     [(_sds((S, D), F32), "row"), (_sds((S, D), BF16), "row")], tr)


def _resid_rms(x1, yf, g4, tr):
    S, D = x1.shape

    def body(x_ref, y_ref, g_ref, o_ref, ob_ref):
        x2 = x_ref[...] + _rms(y_ref[...], g_ref[...])
        o_ref[...] = x2
        ob_ref[...] = x2.astype(BF16)

    return _row_call(body, "resid_rms", [(x1, "row"), (yf, "row"), (g4, "full")],
                     [(_sds((S, D), F32), "row"), (_sds((S, D), BF16), "row")], tr)


def _ple_loss(gl, e, x2, tgt, g5, tr):
    S, D = x2.shape

    def body(gl_ref, e_ref, x2_ref, t_ref, g_ref, loss_ref, dx3_ref, de_ref, dgl_ref, dg_ref):
        _first_step_zero(loss_ref, dg_ref)
        s = _sigmoid(gl_ref[...])
        e_ = e_ref[...]
        t = s * e_
        gain = g_ref[...]
        err = x2_ref[...] + _rms(t, gain) - t_ref[...]
        row_loss = jnp.mean(err * err, axis=-1, keepdims=True)
        loss_ref[...] += 0.5 * jnp.sum(row_loss, axis=0, keepdims=True)
        dx3 = err * (1.0 / D)
        dx3_ref[...] = dx3
        dt, dgain = _rms_bwd(t, gain, dx3)
        dg_ref[...] += dgain
        de_ref[...] = (dt * s).astype(BF16)
        dgl_ref[...] = (dt * e_ * s * (1.0 - s)).astype(BF16)

    return _row_call(body, "ple_loss", [(gl, "row"), (e, "row"), (x2, "row"), (tgt, "row"), (g5, "full")],
                     [(_sds((1, 1), F32), "acc"), (_sds((S, D), F32), "row"), (_sds((S, D), BF16), "row"),
                      (_sds((S, D), BF16), "row"), (_sds((1, D), F32), "acc")], tr)


def _rms_bwd_a(dx3, dx2g, yf, g4, tr):
    S, D = yf.shape

    def body(a_ref, b_ref, y_ref, g_ref, dx_ref, dy_ref, dg_ref):
        _first_step_zero(dg_ref)
        dx2 = a_ref[...] + b_ref[...]
        dx_ref[...] = dx2
        dy, dgain = _rms_bwd(y_ref[...], g_ref[...], dx2)
        dy_ref[...] = dy.astype(BF16)
        dg_ref[...] += dgain

    return _row_call(body, "rms_bwd_a", [(dx3, "row"), (dx2g, "row"), (yf, "row"), (g4, "full")],
                     [(_sds((S, D), F32), "row"), (_sds((S, D), BF16), "row"), (_sds((1, D), F32), "acc")], tr)


def _rms_bwd_b(dx2, dh2, x1, g3, mo, g2, tr):
    S, D = x1.shape

    def body(dx2_ref, dh2_ref, x1_ref, g3_ref, mo_ref, g2_ref, dx1_ref, dmo_ref, dg3_ref, dg2_ref):
        _first_step_zero(dg3_ref, dg2_ref)
        d, dgain3 = _rms_bwd(x1_ref[...], g3_ref[...], dh2_ref[...])
        dx1 = dx2_ref[...] + d
        dx1_ref[...] = dx1
        dg3_ref[...] += dgain3
        dmo, dgain2 = _rms_bwd(mo_ref[...], g2_ref[...], dx1)
        dmo_ref[...] = dmo.astype(BF16)
        dg2_ref[...] += dgain2

    return _row_call(body, "rms_bwd_b", [(dx2, "row"), (dh2, "row"), (x1, "row"), (g3, "full"), (mo, "row"), (g2, "full")],
                     [(_sds((S, D), F32), "row"), (_sds((S, D), BF16), "row"), (_sds((1, D), F32), "acc"),
                      (_sds((1, D), F32), "acc")], tr)


def _rms_bwd_c(dx1, dh, x, g1, tr):
    S, D = x.shape

    def body(dx1_ref, dh_ref, x_ref, g_ref, o_ref, dg_ref):
        _first_step_zero(dg_ref)
        d, dgain = _rms_bwd(x_ref[...], g_ref[...], dh_ref[...])
        o_ref[...] = dx1_ref[...] + d
        dg_ref[...] += dgain

    return _row_call(body, "rms_bwd_c", [(dx1, "row"), (dh, "row"), (x, "row"), (g1, "full")],
                     [(_sds((S, D), F32), "row"), (_sds((1, D), F32), "acc")], tr)


def _gate_bwd(dmixed, proj, ya, yp, gate_cb, tr):
    S, D = ya.shape

    def body(dm_ref, ga_ref, gp_ref, ya_ref, yp_ref, dya_ref, dyp_ref, dga_ref, dgp_ref):
        dm = dm_ref[...]
        sa = _sigmoid(ga_ref[...])
        sp = _sigmoid(gp_ref[...])
        dya_ref[...] = (dm * sa).astype(BF16)
        dyp_ref[...] = (dm * sp).astype(BF16)
        dga_ref[...] = (dm * ya_ref[...] * sa * (1.0 - sa)).astype(BF16)
        dgp_ref[...] = (dm * yp_ref[...] * sp * (1.0 - sp)).astype(BF16)

    return _row_call(body, "gate_bwd",
                     [(dmixed, "row"), (proj, "row", D, gate_cb), (proj, "row", D, gate_cb + 1), (ya, "row"), (yp, "row")],
                     [(_sds((S, D), BF16), "row")] * 4, tr)


def _scale_bwd(dps, pg, scale, tr):
    S, W = dps.shape

    def body(d_ref, pg_ref, s_ref, o_ref, ds_ref):
        _first_step_zero(ds_ref)
        d = d_ref[...]
        o_ref[...] = (d * s_ref[...]).astype(BF16)
        ds_ref[...] += jnp.sum(d * pg_ref[...], axis=0, keepdims=True)

    return _row_call(body, "scale_bwd", [(dps, "row"), (pg, "row"), (scale, "full")],
                     [(_sds((S, W), BF16), "row"), (_sds((1, W), F32), "acc")], tr)


CUMSUM_TERMS = 2


def _tri(blk, cmp):
    j = lax.broadcasted_iota(jnp.int32, (blk, blk), 0)
    s = lax.broadcasted_iota(jnp.int32, (blk, blk), 1)
    one = jnp.concatenate([cmp(j, s).astype(BF16), jnp.ones((blk, 128), BF16)], axis=1)
    return jnp.concatenate([one] * CUMSUM_TERMS, axis=0)


def _split(x):
    terms = []
    for _ in range(CUMSUM_TERMS):
        t = x.astype(BF16)
        terms.append(t)
        x = x - t.astype(F32)
    return terms[0] if CUMSUM_TERMS == 1 else jnp.concatenate(terms, axis=1)


def _split_dot(x, u):
    return jnp.dot(_split(x), u, preferred_element_type=F32)


def _causal(blk):
    return lax.broadcasted_iota(jnp.int32, (blk, blk), 1) < lax.broadcasted_iota(jnp.int32, (blk, blk), 0)


def _scores(q, kj, scale, causal):
    z = lax.dot_general(q, kj, (((1,), (1,)), ((), ())), preferred_element_type=F32) * scale
    l1p = jnp.log(1.0 + jnp.exp(-jnp.abs(z)))
    lb = -(jnp.maximum(z, 0.0) + l1p)
    if causal is not None:
        lb = jnp.where(causal, lb, 0.0)
    return z, lb, jnp.minimum(z, 0.0) - l1p


def _attn_fwd(proj, n_heads, blk, comm=None):
    S = proj.shape[0]
    nq = S // blk
    scale = HEAD_DIM ** -0.5
    lanes = blk // 128
    hp = 2 if n_heads % 2 == 0 else 1
    cols = [slice(h * HEAD_DIM, (h + 1) * HEAD_DIM) for h in range(hp)]
    u_incl = _tri(blk, lambda j, s: j >= s)

    def body(q_ref, k_ref, v_ref, u_ref, o_ref, tot_ref, z_buf, hl_buf):
        i = pl.program_id(1)
        qs = [q_ref[:, c].astype(BF16) for c in cols]
        u = u_ref[...]

        def scores(j, h, causal):
            kj = k_ref[pl.ds(pl.multiple_of(j * blk, blk), blk), cols[h]].astype(BF16)
            z, lb, _ = _scores(qs[h], kj, scale, causal)
            return z, _split(lb)

        def weigh(j, h, z, hl, acc, run, causal):
            vj = v_ref[pl.ds(pl.multiple_of(j * blk, blk), blk), cols[h]].astype(BF16)
            ct = jnp.dot(hl, u, preferred_element_type=F32)
            a = jnp.exp(z + ct[:, :blk] + jnp.tile(run, (1, lanes)))
            if causal is not None:
                a = jnp.where(causal, a, 0.0)
            return acc + jnp.dot(a.astype(BF16), vj, preferred_element_type=F32), run + ct[:, blk:]

        def stage_scores(j):
            for h in range(hp):
                z_buf[h], hl_buf[h] = scores(j, h, None)

        zero = jnp.zeros((blk, HEAD_DIM), F32)
        causal = _causal(blk)
        carry = tuple(weigh(i, h, *scores(i, h, causal), zero, zero, causal) for h in range(hp))
        stage_scores(jnp.maximum(i - 1, 0))

        def step(t, carry):
            j = i - 1 - t
            out = tuple(weigh(j, h, z_buf[h], hl_buf[h], *carry[h], None) for h in range(hp))
            stage_scores(jnp.maximum(j - 1, 0))
            return out

        carry = lax.fori_loop(0, i, step, carry)
        for h in range(hp):
            o_ref[:, cols[h]] = carry[h][0].astype(BF16)
            tot_ref[:, cols[h]] = carry[h][1]

    G = n_heads // hp
    W = hp * HEAD_DIM
    return _call(
        body, name="attn_fwd", grid=(G, nq),
        in_specs=[pl.BlockSpec((blk, W), lambda h, i: (i, h)),
                  pl.BlockSpec((S, W), lambda h, i: (0, G + h)),
                  pl.BlockSpec((S, W), lambda h, i: (0, 2 * G + h)),
                  pl.BlockSpec(u_incl.shape, lambda h, i: (0, 0))],
        out_specs=[pl.BlockSpec((blk, W), lambda h, i: (i, h))] * 2,
        out_shape=[_sds((S, n_heads * HEAD_DIM), BF16), _sds((S, n_heads * HEAD_DIM), F32)],
        args=(proj, proj, proj, u_incl), scratch=[pltpu.VMEM((hp, blk, blk), F32), pltpu.VMEM((hp, blk, CUMSUM_TERMS * blk), BF16)],
        sem=("parallel", "arbitrary"), comm=comm)


def _attn_bwd(proj, tot, do, n_heads, blk, comm=None):
    S = proj.shape[0]
    nq = S // blk
    scale = HEAD_DIM ** -0.5
    lanes = blk // 128
    hp = 2 if n_heads % 2 == 0 else 1
    cols = [slice(h * HEAD_DIM, (h + 1) * HEAD_DIM) for h in range(hp)]
    l_strict = _tri(blk, lambda j, s: j < s)
    l_incl = _tri(blk, lambda j, s: j <= s)

    def body(q_ref, k_ref, v_ref, tot_ref, do_ref, ls_ref, li_ref, dq_ref, dk_ref, dv_ref, dk_acc, dv_acc, z_buf, beta_buf,
             da_buf, hl_buf):
        i = pl.program_id(1)

        @pl.when(i == 0)
        def _():
            dk_acc[...] = jnp.zeros_like(dk_acc)
            dv_acc[...] = jnp.zeros_like(dv_acc)

        qs = [q_ref[:, c].astype(BF16) for c in cols]
        dobs = [do_ref[:, c].astype(BF16) for c in cols]
        totals = [jnp.tile(tot_ref[:, c], (1, lanes)) for c in cols]
        ls = ls_ref[...]
        li = li_ref[...]

        def scores(j, h, causal):
            ks = pl.multiple_of(j * blk, blk)
            kj = k_ref[pl.ds(ks, blk), cols[h]].astype(BF16)
            vj = v_ref[pl.ds(ks, blk), cols[h]].astype(BF16)
            z, lb, log_beta = _scores(qs[h], kj, scale, causal)
            da = lax.dot_general(dobs[h], vj, (((1,), (1,)), ((), ())), preferred_element_type=F32)
            return z, jnp.exp(log_beta), da, _split(lb)

        def grads(j, h, z, beta, da, hl, dq, run_lb, run_g, causal):
            ks = pl.multiple_of(j * blk, blk)
            kj = k_ref[pl.ds(ks, blk), cols[h]].astype(BF16)
            pt = jnp.dot(hl, ls, preferred_element_type=F32)
            a = jnp.exp(z + totals[h] - (pt[:, :blk] + jnp.tile(run_lb, (1, lanes))))
            if causal is not None:
                a = jnp.where(causal, a, 0.0)
            g = a * da
            gt = _split_dot(g, li)
            dz = g - beta * (gt[:, :blk] + jnp.tile(run_g, (1, lanes)))
            if causal is not None:
                dz = jnp.where(causal, dz, 0.0)
            dzs = (dz * scale).astype(BF16)
            dk_acc[pl.ds(ks, blk), cols[h]] += lax.dot_general(dzs, qs[h], (((0,), (0,)), ((), ())),
                                                               preferred_element_type=F32)
            dv_acc[pl.ds(ks, blk), cols[h]] += lax.dot_general(a.astype(BF16), dobs[h], (((0,), (0,)), ((), ())),
                                                               preferred_element_type=F32)
            return dq + jnp.dot(dzs, kj, preferred_element_type=F32), run_lb + pt[:, blk:], run_g + gt[:, blk:]

        def stage_scores(j):
            for h in range(hp):
                z_buf[h], beta_buf[h], da_buf[h], hl_buf[h] = scores(j, h, None)

        zero = jnp.zeros((blk, HEAD_DIM), F32)
        stage_scores(0)

        def step(j, carry):
            out = tuple(grads(j, h, z_buf[h], beta_buf[h], da_buf[h], hl_buf[h], *carry[h], None) for h in range(hp))
            stage_scores(j + 1)
            return out

        carry = lax.fori_loop(0, i, step, ((zero, zero, zero),) * hp)
        causal = _causal(blk)
        carry = tuple(grads(i, h, *scores(i, h, causal), *carry[h], causal) for h in range(hp))
        for h in range(hp):
            dq_ref[:, cols[h]] = carry[h][0].astype(BF16)

        @pl.when(i == nq - 1)
        def _():
            dk_ref[...] = dk_acc[...].astype(BF16)
            dv_ref[...] = dv_acc[...].astype(BF16)

    G = n_heads // hp
    W = hp * HEAD_DIM
    AW = n_heads * HEAD_DIM
    return _call(
        body, name="attn_bwd", grid=(G, nq),
        in_specs=[pl.BlockSpec((blk, W), lambda h, i: (i, h)),
                  pl.BlockSpec((S, W), lambda h, i: (0, G + h)),
                  pl.BlockSpec((S, W), lambda h, i: (0, 2 * G + h)),
                  pl.BlockSpec((blk, W), lambda h, i: (i, h)),
                  pl.BlockSpec((blk, W), lambda h, i: (i, h)),
                  pl.BlockSpec(l_strict.shape, lambda h, i: (0, 0)),
                  pl.BlockSpec(l_incl.shape, lambda h, i: (0, 0))],
        out_specs=[pl.BlockSpec((blk, W), lambda h, i: (i, h)),
                   pl.BlockSpec((S, W), lambda h, i: (0, h)),
                   pl.BlockSpec((S, W), lambda h, i: (0, h))],
        out_shape=[_sds((S, AW), BF16)] * 3, args=(proj, proj, proj, tot, do, l_strict, l_incl),
        scratch=[pltpu.VMEM((S, W), F32)] * 2 + [pltpu.VMEM((hp, blk, blk), F32)] * 3 + [pltpu.VMEM((hp, blk, CUMSUM_TERMS * blk), BF16)],
        sem=("parallel", "arbitrary"), comm=comm)


def _pool_count(r0, rows, w):
    t = r0 + lax.broadcasted_iota(jnp.int32, (rows, 1), 0)
    return jnp.minimum(t + 1, w).astype(F32)


def _pool_fwd(proj, col_blk, n_groups, width, chunk):
    S = proj.shape[0]
    H = POOL_HALO

    def body(u_ref, o_ref, pad_ref):
        g = pl.program_id(0)
        pad_ref[0:H, :] = jnp.zeros((H, width), F32)
        pad_ref[H:, :] = u_ref[...]
        for gi, w in enumerate(POOL_WINDOWS[:n_groups]):
            @pl.when(g == gi)
            def _(w=w):
                def one(c, _):
                    r0 = pl.multiple_of(c * chunk, chunk)
                    ext = pad_ref[pl.ds(r0, chunk + H), :]
                    s = ext
                    k = 1
                    while k < w:
                        s = s + pltpu.roll(s, k, 0)
                        k *= 2
                    o_ref[pl.ds(r0, chunk), :] = (s[H:] / _pool_count(r0, chunk, w) - ext[H:]).astype(BF16)
                    return 0

                lax.fori_loop(0, S // chunk, one, 0)

    return pl.pallas_call(
        body, name="pool_fwd", grid=(n_groups,),
        in_specs=[pl.BlockSpec((S, width), lambda g: (0, col_blk + g))],
        out_specs=pl.BlockSpec((S, width), lambda g: (0, g)), out_shape=_sds((S, n_groups * width), BF16),
        scratch_shapes=[pltpu.VMEM((S + H, width), F32)], compiler_params=_params("parallel"),
    )(proj)


def _pool_bwd(dpooled, n_groups, chunk):
    S = dpooled.shape[0]
    width = dpooled.shape[1] // n_groups
    H = POOL_HALO

    def body(d_ref, o_ref, pad_ref):
        g = pl.program_id(0)
        pad_ref[S:, :] = jnp.zeros((H, width), F32)
        for gi, w in enumerate(POOL_WINDOWS[:n_groups]):
            @pl.when(g == gi)
            def _(w=w):
                def fill(c, _):
                    r0 = pl.multiple_of(c * chunk, chunk)
                    pad_ref[pl.ds(r0, chunk), :] = d_ref[pl.ds(r0, chunk), :] / _pool_count(r0, chunk, w)
                    return 0

                lax.fori_loop(0, S // chunk, fill, 0)

                def one(c, _):
                    r0 = pl.multiple_of(c * chunk, chunk)
                    s = pad_ref[pl.ds(r0, chunk + H), :]
                    k = 1
                    while k < w:
                        s = s + pltpu.roll(s, chunk + H - k, 0)
                        k *= 2
                    o_ref[pl.ds(r0, chunk), :] = (s[:chunk] - d_ref[pl.ds(r0, chunk), :]).astype(BF16)
                    return 0

                lax.fori_loop(0, S // chunk, one, 0)

    return pl.pallas_call(
        body, name="pool_bwd", grid=(n_groups,),
        in_specs=[pl.BlockSpec((S, width), lambda g: (0, g))],
        out_specs=pl.BlockSpec((S, width), lambda g: (0, g)), out_shape=_sds((S, n_groups * width), BF16),
        scratch_shapes=[pltpu.VMEM((S + H, width), F32)], compiler_params=_params("parallel"),
    )(dpooled)


def _conv3(x_ext, w, b):
    return b + pltpu.roll(x_ext, 2, 0) * w[0:1, :] + pltpu.roll(x_ext, 1, 0) * w[1:2, :] + x_ext * w[2:3, :]


def _gelu_parts(x):
    th = jnp.tanh(GELU_C0 * (x + GELU_C1 * (x * x * x)))
    return th, 0.5 * (1.0 + th)


def _conv_specs(S, F, cb):
    nb = F // cb
    return [pl.BlockSpec((S, cb), lambda j: (0, j)), pl.BlockSpec((S, cb), lambda j: (0, nb + j)),
            pl.BlockSpec((3, cb), lambda j: (0, j)), pl.BlockSpec((3, cb), lambda j: (0, nb + j)),
            pl.BlockSpec((1, cb), lambda j: (0, j)), pl.BlockSpec((1, cb), lambda j: (0, nb + j))]


def _conv_fwd(upre, cw, cb_, chunk):
    S, F2 = upre.shape
    F = F2 // 2
    cb = 128
    H = CONV_HALO

    def body(g_ref, v_ref, wg_ref, wv_ref, bg_ref, bv_ref, o_ref, pg_ref, pv_ref):
        pg_ref[0:H, :] = jnp.zeros((H, cb), F32)
        pv_ref[0:H, :] = jnp.zeros((H, cb), F32)
        pg_ref[H:, :] = g_ref[...]
        pv_ref[H:, :] = v_ref[...]
        wg, wv, bg, bv = wg_ref[...], wv_ref[...], bg_ref[...], bv_ref[...]

        def one(c, _):
            r0 = pl.multiple_of(c * chunk, chunk)
            up_g = _conv3(pg_ref[pl.ds(r0, chunk + H), :], wg, bg)[H:]
            up_v = _conv3(pv_ref[pl.ds(r0, chunk + H), :], wv, bv)[H:]
            _, cdf = _gelu_parts(up_g)
            o_ref[pl.ds(r0, chunk), :] = (up_g * cdf * up_v).astype(BF16)
            return 0

        lax.fori_loop(0, S // chunk, one, 0)

    return pl.pallas_call(
        body, name="conv_fwd", grid=(F // cb,), in_specs=_conv_specs(S, F, cb),
        out_specs=pl.BlockSpec((S, cb), lambda j: (0, j)), out_shape=_sds((S, F), BF16),
        scratch_shapes=[pltpu.VMEM((S + H, cb), F32)] * 2, compiler_params=_params("parallel"),
    )(upre, upre, cw, cw, cb_, cb_)


def _conv_bwd(upre, dact, cw, cb_, chunk, comm=None):
    S, F2 = upre.shape
    F = F2 // 2
    cb = 128
    H = CONV_HALO
    E = chunk + 2 * H

    def body(g_ref, v_ref, wg_ref, wv_ref, bg_ref, bv_ref, d_ref, dg_ref, dv_ref, dwg_ref, dwv_ref, dbg_ref, dbv_ref,
             pg_ref, pv_ref, pd_ref):
        for p, src in ((pg_ref, g_ref), (pv_ref, v_ref), (pd_ref, d_ref)):
            p[0:H, :] = jnp.zeros((H, cb), F32)
            p[H:S + H, :] = src[...]
            p[S + H:, :] = jnp.zeros((H, cb), F32)
        wg, wv, bg, bv = wg_ref[...], wv_ref[...], bg_ref[...], bv_ref[...]

        def taps_bwd(d, w):
            return d * w[2:3, :] + pltpu.roll(d, E - 1, 0) * w[1:2, :] + pltpu.roll(d, E - 2, 0) * w[0:1, :]

        def wsum(d, x):
            dc = d[H:H + chunk]
            return [jnp.sum(dc * pltpu.roll(x, 2, 0)[H:H + chunk], axis=0, keepdims=True),
                    jnp.sum(dc * pltpu.roll(x, 1, 0)[H:H + chunk], axis=0, keepdims=True),
                    jnp.sum(dc * x[H:H + chunk], axis=0, keepdims=True),
                    jnp.sum(dc, axis=0, keepdims=True)]

        def one(c, acc):
            r0 = pl.multiple_of(c * chunk, chunk)
            xg = pg_ref[pl.ds(r0, E), :]
            xv = pv_ref[pl.ds(r0, E), :]
            d = pd_ref[pl.ds(r0, E), :]
            up_g = _conv3(xg, wg, bg)
            up_v = _conv3(xv, wv, bv)
            th, cdf = _gelu_parts(up_g)
            dgelu = cdf + 0.5 * up_g * (1.0 - th * th) * (GELU_C0 * (1.0 + 3.0 * GELU_C1 * (up_g * up_g)))
            dgate = d * up_v * dgelu
            dval = d * (up_g * cdf)
            dg_ref[pl.ds(r0, chunk), :] = taps_bwd(dgate, wg)[H:H + chunk].astype(BF16)
            dv_ref[pl.ds(r0, chunk), :] = taps_bwd(dval, wv)[H:H + chunk].astype(BF16)
            return tuple(a + b for a, b in zip(acc, wsum(dgate, xg) + wsum(dval, xv)))

        zero = jnp.zeros((1, cb), F32)
        acc = lax.fori_loop(0, S // chunk, one, (zero,) * 8)
        dwg_ref[...] = jnp.concatenate(acc[0:3], axis=0)
        dbg_ref[...] = acc[3]
        dwv_ref[...] = jnp.concatenate(acc[4:7], axis=0)
        dbv_ref[...] = acc[7]

    col = lambda rows: pl.BlockSpec((rows, cb), lambda j: (0, j))
    return _call(
        body, name="conv_bwd", grid=(F // cb,), in_specs=_conv_specs(S, F, cb) + [col(S)],
        out_specs=[col(S), col(S), col(3), col(3), col(1), col(1)],
        out_shape=[_sds((S, F), BF16), _sds((S, F), BF16), _sds((3, F), F32), _sds((3, F), F32), _sds((1, F), F32),
                   _sds((1, F), F32)],
        args=(upre, upre, cw, cw, cb_, cb_, dact), scratch=[pltpu.VMEM((S + 2 * H, cb), F32)] * 3, sem=("parallel",),
        comm=comm)


def _position():
    x, y, c = lax.axis_index("x"), lax.axis_index("y"), lax.axis_index("c")
    return x, y, c, 4 * x + 2 * y + c


def _peer(x, y, c, d):
    px = 1 - x if d & 4 else x
    py = 1 - y if d & 2 else y
    pc = 1 - c if d & 1 else c
    return (px, py, pc), 4 * px + 2 * py + pc


def _all_gather_comm(tensors):
    nt = len(tensors)
    outs = [_sds((N_DEV,) + t.shape, t.dtype) for t in tensors]
    sems = [pltpu.SemaphoreType.DMA((7 * nt,)), pltpu.SemaphoreType.DMA((7 * nt,)), pltpu.SemaphoreType.DMA((nt,))]

    def parts(ins, outs_, sem_refs):
        send, recv, loc = sem_refs
        x, y, c, me = _position()
        chips = [(1 - x, y), (x, 1 - y), (1 - x, 1 - y)]

        def copy(t, k, block, to, src=None):
            slot = outs_[t].at[4 * block[0] + 2 * block[1] + block[2]]
            return pltpu.make_async_remote_copy(src_ref=slot if src is None else src, dst_ref=slot,
                                                send_sem=send.at[7 * t + k], recv_sem=recv.at[7 * t + k], device_id=to,
                                                device_id_type=MESH)

        def mine(t):
            return pltpu.make_async_copy(ins[t], outs_[t].at[me], loc.at[t])

        return (x, y, c), (x, y, 1 - c), chips, copy, mine

    def start(ins, outs_, sem_refs):
        me, sibling, chips, copy, mine = parts(ins, outs_, sem_refs)
        for t in range(nt):
            mine(t).start()
            copy(t, 0, me, sibling, src=ins[t]).start()
            for j, chip in enumerate(chips):
                copy(t, 1 + j, me, (*chip, me[2]), src=ins[t]).start()

    def finish(ins, outs_, sem_refs):
        me, sibling, chips, copy, mine = parts(ins, outs_, sem_refs)
        c = me[2]
        for t in range(nt):
            for j, chip in enumerate(chips):
                copy(t, 1 + j, (*chip, c), me).wait_recv()
                copy(t, 4 + j, (*chip, c), sibling).start()
        for t in range(nt):
            copy(t, 0, sibling, me).wait_recv()
            for j, chip in enumerate(chips):
                copy(t, 4 + j, (*chip, 1 - c), me).wait_recv()
        for t in range(nt):
            copy(t, 0, me, sibling, src=ins[t]).wait_send()
            for j, chip in enumerate(chips):
                copy(t, 1 + j, me, (*chip, c), src=ins[t]).wait_send()
                copy(t, 4 + j, (*chip, c), sibling).wait_send()
            mine(t).wait()

    return _Comm(tensors, outs, sems, start, finish)


def _reduce_scatter_comm(tensors):
    nt = len(tensors)
    outs = [_sds(t.shape, t.dtype) for t in tensors]
    sems = [pltpu.SemaphoreType.DMA((7 * nt,)), pltpu.SemaphoreType.DMA((7 * nt,)), pltpu.SemaphoreType.DMA((nt,))]

    def local(ins, outs_, sem_refs, t, me):
        return pltpu.make_async_copy(ins[t].at[me], outs_[t].at[me], sem_refs[2].at[t])

    def remote(ins, outs_, sem_refs, t, d, inbound):
        x, y, c, me = _position()
        peer, peer_idx = _peer(x, y, c, d)
        k = 7 * t + d - 1
        src, dst, to = (ins[t].at[me], outs_[t].at[peer_idx], (x, y, c)) if inbound else (ins[t].at[peer_idx], outs_[t].at[me], peer)
        return pltpu.make_async_remote_copy(src_ref=src, dst_ref=dst, send_sem=sem_refs[0].at[k], recv_sem=sem_refs[1].at[k],
                                            device_id=to, device_id_type=MESH)

    def start(ins, outs_, sem_refs):
        me = _position()[3]
        for t in range(nt):
            local(ins, outs_, sem_refs, t, me).start()
            for d in range(1, N_DEV):
                remote(ins, outs_, sem_refs, t, d, False).start()

    def finish(ins, outs_, sem_refs):
        me = _position()[3]
        for t in range(nt):
            for d in range(1, N_DEV):
                remote(ins, outs_, sem_refs, t, d, True).wait_recv()
        for t in range(nt):
            for d in range(1, N_DEV):
                remote(ins, outs_, sem_refs, t, d, False).wait_send()
            local(ins, outs_, sem_refs, t, me).wait()

    return _Comm(tensors, outs, sems, start, finish)


HBM_SPEC = pl.BlockSpec(memory_space=pltpu.HBM)
SEM_SPEC = pl.BlockSpec(memory_space=pltpu.SEMAPHORE)
DATAFLOW = pltpu.SideEffectType.DATAFLOW_SIDE_EFFECTING


def _scatter_copy(g_ref, land_ref, send_sems, recv_sems, d):
    x, y, c, me = _position()
    peer, peer_idx = _peer(x, y, c, d)
    return pltpu.make_async_remote_copy(src_ref=g_ref.at[peer_idx], dst_ref=land_ref.at[me], send_sem=send_sems.at[d - 1],
                                        recv_sem=recv_sems.at[d - 1], device_id=peer, device_id_type=MESH)


def _reduce_scatter_start(g, name):
    def body(g_ref, land_ref, send_sems, recv_sems, g_thru, land_thru, token):
        for d in range(1, N_DEV):
            _scatter_copy(g_ref, land_ref, send_sems, recv_sems, d).start()
        token[...] = jnp.zeros_like(token)

    return pl.pallas_call(
        body, name=name,
        out_shape=(pltpu.SemaphoreType.DMA((N_DEV - 1,)), pltpu.SemaphoreType.DMA((N_DEV - 1,)), pltpu.HBM(g.shape, g.dtype),
                   pltpu.HBM(g.shape, g.dtype), _sds((8, 128), F32)),
        in_specs=(HBM_SPEC, HBM_SPEC), out_specs=(SEM_SPEC, SEM_SPEC, HBM_SPEC, HBM_SPEC, pl.BlockSpec(memory_space=pltpu.VMEM)),
        input_output_aliases={0: 2, 1: 3}, compiler_params=pltpu.CompilerParams(has_side_effects=DATAFLOW),
    )(pltpu.with_memory_space_constraint(g, pltpu.HBM), pltpu.with_memory_space_constraint(lax.empty(g.shape, g.dtype), pltpu.HBM))


def _reduce_scatter_wait(send_sems, recv_sems, g_thru, land_thru, after, name):
    def body(g_ref, land_ref, send_sems, recv_sems, after_ref, g_out, land_out):
        for d in range(1, N_DEV):
            copy = _scatter_copy(g_ref, land_ref, send_sems, recv_sems, d)
            copy.wait_send()
            copy.wait_recv()

    return pl.pallas_call(
        body, name=name, out_shape=(pltpu.HBM(g_thru.shape, g_thru.dtype), pltpu.HBM(g_thru.shape, g_thru.dtype)),
        in_specs=(HBM_SPEC, HBM_SPEC, SEM_SPEC, SEM_SPEC, ANY), out_specs=(HBM_SPEC, HBM_SPEC), input_output_aliases={0: 0, 1: 1},
        compiler_params=pltpu.CompilerParams(has_side_effects=DATAFLOW),
    )(g_thru, land_thru, send_sems, recv_sems, after)


def _all_reduce_small(part):
    r, W = part.shape

    def body(p_ref, o_ref, g_ref, send_sems, recv_sems):
        x, y, c, me = _position()
        sends = []
        for d in range(1, N_DEV):
            peer, _ = _peer(x, y, c, d)
            cp = pltpu.make_async_remote_copy(src_ref=p_ref, dst_ref=g_ref.at[me], send_sem=send_sems.at[d - 1],
                                              recv_sem=recv_sems.at[d - 1], device_id=peer, device_id_type=MESH)
            cp.start()
            sends.append(cp)
        g_ref[me] = p_ref[...]
        for d in range(1, N_DEV):
            _, peer_idx = _peer(x, y, c, d)
            pltpu.make_async_remote_copy(src_ref=p_ref, dst_ref=g_ref.at[peer_idx], send_sem=send_sems.at[d - 1],
                                         recv_sem=recv_sems.at[d - 1], device_id=(x, y, c), device_id_type=MESH).wait_recv()
        for cp in sends:
            cp.wait_send()
        acc = g_ref[0]
        for i in range(1, N_DEV):
            acc = acc + g_ref[i]
        o_ref[...] = acc

    vmem = pl.BlockSpec(memory_space=pltpu.VMEM)
    return pl.pallas_call(
        body, name="all_reduce_small", in_specs=[vmem], out_specs=[vmem, vmem],
        out_shape=[_sds((r, W), F32), _sds((N_DEV, r, W), F32)],
        scratch_shapes=[pltpu.SemaphoreType.DMA((7,)), pltpu.SemaphoreType.DMA((7,))],
        compiler_params=pltpu.CompilerParams(has_side_effects=True, vmem_limit_bytes=VMEM_LIMIT),
    )(part)[0]


def _adamw_math(w, g, m, v):
    m = ADAM_B1 * m + (1.0 - ADAM_B1) * g
    v = ADAM_B2 * v + (1.0 - ADAM_B2) * (g * g)
    m_hat = m / (1.0 - ADAM_B1 ** ADAM_STEP)
    v_hat = v / (1.0 - ADAM_B2 ** ADAM_STEP)
    return -ADAM_LR * (m_hat / (jnp.sqrt(v_hat) + ADAM_EPS) + ADAM_WD * w), m, v


def _adamw(w, g, m, v, name):
    rows, cols = w.shape
    tr = _tile(rows, max(8, (2**18 // cols) // 8 * 8), 8)

    def body(w_ref, g_ref, m_ref, v_ref, d_ref, nm_ref, nv_ref):
        d_ref[...], nm_ref[...], nv_ref[...] = _adamw_math(w_ref[...], g_ref[...], m_ref[...], v_ref[...])

    spec = pl.BlockSpec((tr, cols), lambda i: (i, 0))
    return pl.pallas_call(
        body, name=name, grid=(rows // tr,), in_specs=[spec] * 4, out_specs=[spec] * 3,
        out_shape=[_sds((rows, cols), F32)] * 3, compiler_params=_params("parallel"),
    )(w, g, m, v)


def _adamw_sum(recv, w, m, v, name):
    n, rows, cols = recv.shape
    tr = _tile(rows, max(16, (2**17 // cols) // 16 * 16), 16)

    def body(r_ref, w_ref, m_ref, v_ref, g_ref, d_ref, nm_ref, nv_ref):
        g = r_ref[0].astype(F32)
        for i in range(1, n):
            g = g + r_ref[i].astype(F32)
        g_ref[...] = g
        d_ref[...], nm_ref[...], nv_ref[...] = _adamw_math(w_ref[...], g, m_ref[...], v_ref[...])

    spec = pl.BlockSpec((tr, cols), lambda i: (i, 0))
    return pl.pallas_call(
        body, name=name, grid=(rows // tr,), in_specs=[pl.BlockSpec((n, tr, cols), lambda i: (0, i, 0))] + [spec] * 3,
        out_specs=[spec] * 4, out_shape=[_sds((rows, cols), F32)] * 4, compiler_params=_params("parallel"),
    )(recv, w, m, v)


COLUMN_CUT = ("w_in", "w_attn_branch", "w_pool_branch", "w_up", "w_ple")
ROW_CUT = ("w_out", "w_down", "w_ple_gate")
REPLICATED = ("norm_mix_pre", "pool_scale", "norm_mix_post", "norm_ffn_pre", "conv_b", "norm_ffn_post", "norm_ple_post")
WEIGHTS = ("norm_mix_pre", "w_in", "w_attn_branch", "w_pool_group", "pool_scale", "w_pool_branch", "w_out", "norm_mix_post",
           "norm_ffn_pre", "w_up", "conv_w", "conv_b", "w_down", "norm_ffn_post", "w_ple", "w_ple_gate", "norm_ple_post")


def _size(shape):
    n = 1
    for s in shape:
        n *= s
    return n


def _pad_rows(flat, row_align):
    n = flat.shape[-1]
    per = PACK_W * row_align
    total = -(-n // per) * per
    return jnp.pad(flat, [(0, total - n)]).reshape(total // PACK_W, PACK_W)


def _natural(shard_major):
    n, r, c = shard_major.shape
    return shard_major.reshape(n * r, c)


def kernel(x, p, norm_mix_pre, w_in, w_attn_branch, w_pool_group, pool_scale, w_pool_branch, w_out, norm_mix_post, norm_ffn_pre, w_up, conv_w, conv_b, w_down, norm_ffn_post, w_ple, w_ple_gate, norm_ple_post, loss_target, m_norm_mix_pre, m_w_in, m_w_attn_branch, m_w_pool_group, m_pool_scale, m_w_pool_branch, m_w_out, m_norm_mix_post, m_norm_ffn_pre, m_w_up, m_conv_w, m_conv_b, m_w_down, m_norm_ffn_post, m_w_ple, m_w_ple_gate, m_norm_ple_post, v_norm_mix_pre, v_w_in, v_w_attn_branch, v_w_pool_group, v_pool_scale, v_w_pool_branch, v_w_out, v_norm_mix_post, v_norm_ffn_pre, v_w_up, v_conv_w, v_conv_b, v_w_down, v_norm_ffn_post, v_w_ple, v_w_ple_gate, v_norm_ple_post):
    given = dict(locals())
    wts = {n: given[n][0] for n in WEIGHTS}
    mom = {n: given["m_" + n][0] for n in WEIGHTS}
    var = {n: given["v_" + n][0] for n in WEIGHTS}
    xs = x[0]
    ps_in = p[0, 0]
    tgt = loss_target[0]
    S, D = xs.shape
    AW = wts["w_attn_branch"].shape[0]
    PW = wts["w_pool_branch"].shape[0]
    G = wts["w_pool_group"].shape[0]
    PGW = PW // G
    H = AW // HEAD_DIM
    F = wts["w_down"].shape[0] * N_DEV
    assert (3 * AW) % PGW == 0 and (3 * AW + PW) % D == 0 and PGW % 128 == 0 and F % 128 == 0
    tr = _tile(S, 256, 16)
    blk = _tile(S, 256, 128)
    chunk = _tile(S, 256, 8)
    me = 4 * lax.axis_index("x") + 2 * lax.axis_index("y") + lax.axis_index("c")

    cw_shape = wts["conv_w"].shape
    conv_b_row = wts["conv_b"].reshape(1, -1)
    g1, g2, g3, g4, g5 = (wts[n].reshape(1, D) for n in
                          ("norm_mix_pre", "norm_mix_post", "norm_ffn_pre", "norm_ffn_post", "norm_ple_post"))
    pscale = wts["pool_scale"].reshape(1, PW)
    big = 4096
    wb = {n: wts[n].astype(BF16) for n in COLUMN_CUT + ROW_CUT}
    wb["w_pool_group"] = wts["w_pool_group"].astype(BF16).reshape(G * PGW // N_DEV, PGW)

    h, w_in = _rms_fwd(xs, g1, tr, _all_gather_comm([wb["w_in"]]))
    proj, w_ab, w_pg, w_pb, w_out = _mm(
        h, w_in, b_sm=True, name="mm_in", tk=2048,
        comm=_all_gather_comm([wb["w_attn_branch"], wb["w_pool_group"], wb["w_pool_branch"], wb["w_out"]]))
    w_pg = jnp.moveaxis(w_pg.reshape(N_DEV, G, PGW // N_DEV, PGW), 0, 1).reshape(G, PGW, PGW)
    w_out = _natural(w_out)
    attn, tot, w_up, conv_w_all = _attn_fwd(proj, H, blk, _all_gather_comm([wb["w_up"], wts["conv_w"]]))
    conv_w_full = jnp.moveaxis(conv_w_all, 0, 1).reshape(cw_shape[0], N_DEV * cw_shape[1])
    y_attn = _mm(attn, w_ab, b_sm=True, name="mm_attn_branch", tm=big, tn=256, tk=big)
    pooled = _pool_fwd(proj, 3 * AW // PGW, G, PGW, chunk)
    pg, ps = _pool_group_fwd(pooled, w_pg, pscale)
    y_pool = _mm(ps, w_pb, b_sm=True, name="mm_pool_branch", tm=big, tn=256, tk=big)
    gate_cb = (3 * AW + PW) // D
    mixed = _gate_mix(proj, y_attn, y_pool, gate_cb, tr)
    mo = _mm(mixed, w_out, name="mm_out", tk=2048)
    x1, h2 = _resid_rms2(xs, mo, g2, g3, tr)
    upre, w_down, w_ple, w_pleg = _mm(h2, w_up, b_sm=True, name="mm_up", tn=2048, tk=512,
                                      comm=_all_gather_comm([wb["w_down"], wb["w_ple"], wb["w_ple_gate"]]))
    w_down, w_pleg = _natural(w_down), _natural(w_pleg)
    act = _conv_fwd(upre, conv_w_full, conv_b_row, chunk)
    yf = _mm(act, w_down, name="mm_down")
    x2, x2b = _resid_rms(x1, yf, g4, tr)
    e = _mm(ps_in, w_ple, b_sm=True, name="mm_ple", tm=big, tn=256, tk=big)
    gl = _mm(x2b, w_pleg, name="mm_ple_gate", tk=2048)
    loss_part, dx3, de, dgl, dg5 = _ple_loss(gl, e, x2, tgt, g5, tr)

    shards = lambda natural: natural.reshape((N_DEV, natural.shape[0] // N_DEV) + natural.shape[1:])
    recv = {}
    dw_ple = _mm(ps_in, de, ta=True, out_sm=True, out_dtype=BF16, name="mm_d_w_ple", tm=256, tn=256, tk=big)
    dw_pleg = shards(_mm(x2b, dgl, ta=True, out_dtype=BF16, name="mm_d_w_ple_gate"))
    dx2g = _mm(dgl, w_pleg, tb=True, name="mm_d_x2", tk=2048)
    dx2, dyf, dg4 = _rms_bwd_a(dx3, dx2g, yf, g4, tr)
    dw_down = shards(_mm(act, dyf, ta=True, out_dtype=BF16, name="mm_d_w_down"))
    dact, recv["w_ple"], recv["w_ple_gate"] = _mm(dyf, w_down, tb=True, name="mm_d_act", tn=1408, tk=512,
                                                  comm=_reduce_scatter_comm([dw_ple, dw_pleg]))
    dup_g, dup_v, dcw_g, dcw_v, dcb_g, dcb_v, recv["w_down"] = _conv_bwd(upre, dact, conv_w_full, conv_b_row, chunk,
                                                                          _reduce_scatter_comm([dw_down]))
    dupre = jnp.concatenate([dup_g, dup_v], axis=1)
    dw_up = _mm(h2, dupre, ta=True, out_sm=True, out_dtype=BF16, name="mm_d_w_up", tn=2048)
    dh2 = _mm(dupre, w_up, tb=True, b_sm=True, name="mm_d_h2", tk=2048)
    dx1, dmo, dg3, dg2 = _rms_bwd_b(dx2, dh2, x1, g3, mo, g2, tr)
    dmixed = _mm(dmo, w_out, tb=True, name="mm_d_mixed", tk=2048)
    dw_out = shards(_mm(mixed, dmo, ta=True, out_dtype=BF16, name="mm_d_w_out"))
    dya, dyp, dga, dgp = _gate_bwd(dmixed, proj, y_attn, y_pool, gate_cb, tr)
    dps = _mm(dyp, w_pb, tb=True, b_sm=True, name="mm_d_ps", tm=2048, tk=256)
    dw_pb = _mm(ps, dyp, ta=True, out_sm=True, out_dtype=BF16, name="mm_d_w_pool_branch", tn=256, tk=big)
    dpg, dscale = _scale_bwd(dps, pg, pscale, tr)
    dpooled = _pool_group_bwd_x(dpg, w_pg)
    dw_pg = _pool_group_bwd_w(pooled, dpg, G)
    dw_pg = jnp.moveaxis(dw_pg.astype(BF16).reshape(G, N_DEV, PGW // N_DEV, PGW), 1, 0).reshape(N_DEV, G * PGW // N_DEV, PGW)
    du = _pool_bwd(dpooled, G, chunk)
    dattn = _mm(dya, w_ab, tb=True, b_sm=True, name="mm_d_attn", tm=2048, tk=256)
    dw_ab = _mm(attn, dya, ta=True, out_sm=True, out_dtype=BF16, name="mm_d_w_attn_branch", tn=256, tk=big)
    dq, dk, dv, recv["w_up"], recv["w_out"], recv["w_pool_branch"], recv["w_pool_group"], recv["w_attn_branch"] = _attn_bwd(
        proj, tot, dattn, H, blk, _reduce_scatter_comm([dw_up, dw_out, dw_pb, dw_pg, dw_ab]))
    dproj = jnp.concatenate([dq, dk, dv, du, dga, dgp], axis=1)
    dw_in = _mm(h, dproj, ta=True, out_sm=True, out_dtype=BF16, name="mm_d_w_in")
    in_send, in_recv, dw_in, in_land, token = _reduce_scatter_start(dw_in, "rs_w_in_start")
    dproj, _ = lax.optimization_barrier((dproj, token))
    dh = _mm(dproj, w_in, tb=True, b_sm=True, name="mm_d_h", tk=1024)
    grad_x, dg1 = _rms_bwd_c(dx1, dh, xs, g1, tr)

    gshard, delta, new_m, new_v = {}, {}, {}, {}

    def adamw_cut(n):
        shp = wts[n].shape
        two_d = (_size(shp[:-1]), shp[-1])
        g_, d_, m_, v_ = _adamw_sum(recv[n].reshape((N_DEV,) + two_d), wts[n].reshape(two_d), mom[n].reshape(two_d),
                                    var[n].reshape(two_d), "adamw_" + n)
        gshard[n], delta[n], new_m[n], new_v[n] = g_.reshape(shp), d_.reshape(shp), m_.reshape(shp), v_.reshape(shp)

    for n in COLUMN_CUT[1:] + ROW_CUT + ("w_pool_group",):
        adamw_cut(n)

    dconv_w = jnp.concatenate([dcw_g, dcw_v], axis=1)
    dconv_b = jnp.concatenate([dcb_g, dcb_v], axis=1).reshape(-1)
    rep_parts = {"norm_mix_pre": dg1, "pool_scale": dscale, "norm_mix_post": dg2, "norm_ffn_pre": dg3, "conv_b": dconv_b,
                 "norm_ffn_post": dg4, "norm_ple_post": dg5}
    small = jnp.concatenate([rep_parts[n].reshape(-1) for n in REPLICATED] + [dconv_w.reshape(-1)])
    n_small = small.shape[0]
    small_sum = _all_reduce_small(_pad_rows(small, 8)).reshape(-1)[:n_small]
    off = 0
    for n in REPLICATED:
        sz = _size(wts[n].shape)
        gshard[n] = small_sum[off:off + sz].reshape(wts[n].shape)
        off += sz
    dconv_w_sum = small_sum[off:off + 3 * 2 * F].reshape(3, 2 * F)
    gshard["conv_w"] = lax.dynamic_slice_in_dim(dconv_w_sum, me * cw_shape[1], cw_shape[1], axis=1)

    delta["conv_w"], new_m["conv_w"], new_v["conv_w"] = _adamw(wts["conv_w"], gshard["conv_w"], mom["conv_w"], var["conv_w"],
                                                               "adamw_conv_w")
    rep_sizes = [_size(wts[n].shape) for n in REPLICATED]
    n_rep = sum(rep_sizes)
    cat = lambda t: _pad_rows(jnp.concatenate([t[n].reshape(-1) for n in REPLICATED]), 8)
    d_, m_, v_ = _adamw(cat(wts), cat(gshard), cat(mom), cat(var), "adamw_replicated")
    off = 0
    for n, sz in zip(REPLICATED, rep_sizes):
        shp = wts[n].shape
        delta[n], new_m[n], new_v[n] = (t.reshape(-1)[off:off + sz].reshape(shp) for t in (d_, m_, v_))
        off += sz
    assert off == n_rep

    assert COLUMN_CUT[0] == "w_in"
    done = [grad_x] + [delta[n] for n in WEIGHTS if n != "w_in"]
    done = lax.optimization_barrier(tuple(done))
    grad_x = done[0]
    for n, d_ in zip([n for n in WEIGHTS if n != "w_in"], done[1:]):
        delta[n] = d_
    dw_in, in_land = _reduce_scatter_wait(in_send, in_recv, dw_in, in_land, grad_x, "rs_w_in_wait")
    own = lax.dynamic_index_in_dim(dw_in, me, 0, keepdims=True)
    recv["w_in"] = lax.dynamic_update_slice_in_dim(in_land, own, me, 0)
    adamw_cut("w_in")

    loss = lax.psum(loss_part[0, 0], ("x", "y", "c"))
    lead = lambda t: t[None]
    return (loss, grad_x[None], *[lead(gshard[n]) for n in WEIGHTS], *[lead(delta[n]) for n in WEIGHTS],
            *[lead(new_m[n]) for n in WEIGHTS], *[lead(new_v[n]) for n in WEIGHTS])
```

```python
import functools

import jax
import jax.numpy as jnp
from jax import lax
from jax.experimental import pallas as pl
from jax.experimental.pallas import tpu as pltpu

F32 = jnp.float32
BF16 = jnp.bfloat16
MESH = pl.DeviceIdType.MESH

EPS = 1e-6
HEAD_DIM = 128
POOL_WINDOWS = (2, 4, 8, 16)
POOL_HALO = 16
CONV_HALO = 8
GELU_C0 = 0.7978845608028654
GELU_C1 = 0.044715
ADAM_LR = 0.001
ADAM_B1 = 0.9
ADAM_B2 = 0.999
ADAM_EPS = 1e-08
ADAM_WD = 0.01
ADAM_STEP = 10
N_DEV = 8
PACK_W = 1024
VMEM_LIMIT = 56 * 2**20
ANY = pl.BlockSpec(memory_space=pl.ANY)


def _params(*sem):
    return pltpu.CompilerParams(dimension_semantics=sem, vmem_limit_bytes=VMEM_LIMIT)


def _sds(shape, dtype):
    return jax.ShapeDtypeStruct(shape, dtype)


def _tile(dim, target, align):
    if dim <= target:
        return dim
    t = (target // align) * align
    while t >= align:
        if dim % t == 0:
            return t
        t -= align
    return dim


def _sigmoid(x):
    return 1.0 / (1.0 + jnp.exp(-x))


class _Comm:
    def __init__(self, ins, outs, sems, start, finish):
        self.ins, self.outs, self.sems, self.start, self.finish = list(ins), list(outs), list(sems), start, finish


def _call(body, *, name, grid, in_specs, out_specs, out_shape, args, scratch=(), sem=(), comm=None):
    in_specs, out_specs, out_shape, scratch = list(in_specs), list(out_specs), list(out_shape), list(scratch)
    if comm is None:
        return pl.pallas_call(body, name=name, grid=grid, in_specs=in_specs, out_specs=out_specs, out_shape=out_shape,
                              scratch_shapes=scratch, compiler_params=_params(*sem))(*args)
    n_in, n_out, n_scr, n_ci, n_co = len(in_specs), len(out_specs), len(scratch), len(comm.ins), len(comm.outs)

    def wrapped(*refs):
        ins, refs = refs[:n_in], refs[n_in:]
        c_ins, refs = refs[:n_ci], refs[n_ci:]
        outs, refs = refs[:n_out], refs[n_out:]
        c_outs, refs = refs[:n_co], refs[n_co:]
        scr, c_sems = refs[:n_scr], refs[n_scr:]
        first = last = None
        for axis, size in enumerate(grid):
            at_start, at_end = pl.program_id(axis) == 0, pl.program_id(axis) == size - 1
            first = at_start if first is None else jnp.logical_and(first, at_start)
            last = at_end if last is None else jnp.logical_and(last, at_end)
        if grid:
            pl.when(first)(lambda: comm.start(c_ins, c_outs, c_sems))
            body(*ins, *outs, *scr)
            pl.when(last)(lambda: comm.finish(c_ins, c_outs, c_sems))
        else:
            comm.start(c_ins, c_outs, c_sems)
            body(*ins, *outs, *scr)
            comm.finish(c_ins, c_outs, c_sems)

    return pl.pallas_call(
        wrapped, name=name, grid=grid, in_specs=in_specs + [ANY] * n_ci, out_specs=out_specs + [ANY] * n_co,
        out_shape=out_shape + comm.outs, scratch_shapes=scratch + comm.sems,
        compiler_params=pltpu.CompilerParams(dimension_semantics=("arbitrary",) * len(grid), vmem_limit_bytes=VMEM_LIMIT,
                                             has_side_effects=True),
    )(*args, *comm.ins)


def _mm(a, b, *, ta=False, tb=False, b_sm=False, out_sm=False, out_dtype=F32, name, tm=2048, tn=1024, tk=1024, comm=None):
    M, K = (a.shape[1], a.shape[0]) if ta else a.shape
    if b_sm:
        n_sl, rows, per = b.shape
        N = rows if tb else n_sl * per
        assert K == (n_sl * per if tb else rows)
    else:
        N = b.shape[0] if tb else b.shape[1]
    tm = _tile(M, tm, 128)
    tn = _tile(per if (b_sm and not tb) else N // N_DEV if out_sm else N, tn, 128)
    tk = _tile(per if (b_sm and tb) else K, tk, 128)
    nk = K // tk
    a_spec = pl.BlockSpec((tk, tm), lambda i, j, k: (k, i)) if ta else pl.BlockSpec((tm, tk), lambda i, j, k: (i, k))
    if not b_sm:
        b_spec = pl.BlockSpec((tn, tk), lambda i, j, k: (j, k)) if tb else pl.BlockSpec((tk, tn), lambda i, j, k: (k, j))
    elif tb:
        kp = per // tk
        b_spec = pl.BlockSpec((None, tn, tk), lambda i, j, k: (k // kp, j, k % kp))
    else:
        jp = per // tn
        b_spec = pl.BlockSpec((None, tk, tn), lambda i, j, k: (j // jp, k, j % jp))
    if out_sm:
        jo = (N // N_DEV) // tn
        o_spec = pl.BlockSpec((None, tm, tn), lambda i, j, k: (j // jo, i, j % jo))
        o_shape = _sds((N_DEV, M, N // N_DEV), out_dtype)
    else:
        o_spec = pl.BlockSpec((tm, tn), lambda i, j, k: (i, j))
        o_shape = _sds((M, N), out_dtype)
    dims = (((0 if ta else 1,), (1 if tb else 0,)), ((), ()))

    def product(a_ref, b_ref):
        return lax.dot_general(a_ref[...].astype(BF16), b_ref[...].astype(BF16), dims, preferred_element_type=F32)

    def body(a_ref, b_ref, o_ref, acc_ref):
        k = pl.program_id(2)

        @pl.when(k == 0)
        def _():
            acc_ref[...] = jnp.zeros_like(acc_ref)

        acc_ref[...] += product(a_ref, b_ref)

        @pl.when(k == nk - 1)
        def _():
            o_ref[...] = acc_ref[...].astype(o_ref.dtype)

    def body_one_step(a_ref, b_ref, o_ref):
        o_ref[...] = product(a_ref, b_ref).astype(o_ref.dtype)

    res = _call(body if nk > 1 else body_one_step, name=name, grid=(M // tm, N // tn, nk), in_specs=[a_spec, b_spec],
                out_specs=[o_spec], out_shape=[o_shape], args=(a, b), scratch=[pltpu.VMEM((tm, tn), F32)] if nk > 1 else [],
                sem=("parallel", "parallel", "arbitrary"), comm=comm)
    return res[0] if comm is None else res


def _pool_group_fwd(pooled, w_pg, scale):
    S = pooled.shape[0]
    G, C, C2 = w_pg.shape
    tm = _tile(S, 1024, 16)

    def body(a_ref, w_ref, s_ref, pg_ref, ps_ref):
        pg = jnp.dot(a_ref[...], w_ref[...], preferred_element_type=F32)
        pg_ref[...] = pg
        ps_ref[...] = (pg * s_ref[...]).astype(BF16)

    return pl.pallas_call(
        body, name="pool_group_fwd", grid=(G, S // tm),
        in_specs=[pl.BlockSpec((tm, C), lambda g, i: (i, g)), pl.BlockSpec((None, C, C2), lambda g, i: (g, 0, 0)),
                  pl.BlockSpec((1, C2), lambda g, i: (0, g))],
        out_specs=[pl.BlockSpec((tm, C2), lambda g, i: (i, g)), pl.BlockSpec((tm, C2), lambda g, i: (i, g))],
        out_shape=[jax.ShapeDtypeStruct((S, G * C2), F32), jax.ShapeDtypeStruct((S, G * C2), BF16)],
        compiler_params=_params("parallel", "parallel"),
    )(pooled, w_pg, scale)


def _pool_group_bwd_x(dpg, w_pg):
    S = dpg.shape[0]
    G, C, C2 = w_pg.shape
    tm = _tile(S, 1024, 16)

    def body(d_ref, w_ref, o_ref):
        o_ref[...] = lax.dot_general(d_ref[...], w_ref[...], (((1,), (1,)), ((), ())), preferred_element_type=F32)

    return pl.pallas_call(
        body, name="pool_group_bwd_x", grid=(G, S // tm),
        in_specs=[pl.BlockSpec((tm, C2), lambda g, i: (i, g)), pl.BlockSpec((None, C, C2), lambda g, i: (g, 0, 0))],
        out_specs=pl.BlockSpec((tm, C), lambda g, i: (i, g)), out_shape=jax.ShapeDtypeStruct((S, G * C), F32),
        compiler_params=_params("parallel", "parallel"),
    )(dpg, w_pg)


def _pool_group_bwd_w(pooled, dpg, G):
    S = pooled.shape[0]
    C, C2 = pooled.shape[1] // G, dpg.shape[1] // G
    tk = _tile(S, 1024, 16)

    def body(a_ref, d_ref, o_ref):
        @pl.when(pl.program_id(1) == 0)
        def _():
            o_ref[...] = jnp.zeros_like(o_ref)

        o_ref[...] += lax.dot_general(a_ref[...], d_ref[...], (((0,), (0,)), ((), ())), preferred_element_type=F32)

    return pl.pallas_call(
        body, name="pool_group_bwd_w", grid=(G, S // tk),
        in_specs=[pl.BlockSpec((tk, C), lambda g, k: (k, g)), pl.BlockSpec((tk, C2), lambda g, k: (k, g))],
        out_specs=pl.BlockSpec((None, C, C2), lambda g, k: (g, 0, 0)), out_shape=jax.ShapeDtypeStruct((G, C, C2), F32),
        compiler_params=_params("parallel", "arbitrary"),
    )(pooled, dpg)


def _rms(x, gain):
    r = lax.rsqrt(jnp.mean(x * x, axis=-1, keepdims=True) + EPS)
    return x * r * gain


def _rms_bwd(x, gain, dy):
    r = lax.rsqrt(jnp.mean(x * x, axis=-1, keepdims=True) + EPS)
    xh = x * r
    dgain = jnp.sum(dy * xh, axis=0, keepdims=True)
    dxh = dy * gain
    dx = r * (dxh - xh * jnp.mean(dxh * xh, axis=-1, keepdims=True))
    return dx, dgain


def _row_call(body, name, ins, outs, tr, *, comm=None):
    S = None
    in_specs, args = [], []
    for it in ins:
        arr, kind = it[0], it[1]
        if kind == "row":
            S = arr.shape[0]
            if len(it) == 4:
                width, cb = it[2], it[3]
                in_specs.append(pl.BlockSpec((tr, width), functools.partial(lambda i, cb: (i, cb), cb=cb)))
            else:
                in_specs.append(pl.BlockSpec((tr, arr.shape[1]), lambda i: (i, 0)))
        else:
            assert arr.ndim == 2
            in_specs.append(pl.BlockSpec(arr.shape, lambda i: (0, 0)))
        args.append(arr)
    out_specs, out_shape = [], []
    for sds, kind in outs:
        if kind == "row":
            out_specs.append(pl.BlockSpec((tr, sds.shape[1]), lambda i: (i, 0)))
        else:
            assert len(sds.shape) == 2
            out_specs.append(pl.BlockSpec(sds.shape, lambda i: (0, 0)))
        out_shape.append(sds)
    return _call(body, name=name, grid=(S // tr,), in_specs=in_specs, out_specs=out_specs, out_shape=out_shape, args=args,
                 sem=("arbitrary",), comm=comm)


def _first_step_zero(*refs):
    @pl.when(pl.program_id(0) == 0)
    def _():
        for r in refs:
            r[...] = jnp.zeros_like(r)


def _rms_fwd(x, gain, tr, comm):
    def body(x_ref, g_ref, o_ref):
        o_ref[...] = _rms(x_ref[...], g_ref[...]).astype(BF16)

    S, D = x.shape
    return _row_call(body, "rms_fwd", [(x, "row"), (gain, "full")], [(_sds((S, D), BF16), "row")], tr, comm=comm)


def _gate_mix(proj, ya, yp, gate_cb, tr):
    S, D = ya.shape

    def body(ga_ref, gp_ref, ya_ref, yp_ref, o_ref):
        o_ref[...] = (_sigmoid(ga_ref[...]) * ya_ref[...] + _sigmoid(gp_ref[...]) * yp_ref[...]).astype(BF16)

    return _row_call(body, "gate_mix", [(proj, "row", D, gate_cb), (proj, "row", D, gate_cb + 1), (ya, "row"), (yp, "row")],
                     [(_sds((S, D), BF16), "row")], tr)[0]


def _resid_rms2(x, mo, g2, g3, tr):
    S, D = x.shape

    def body(x_ref, mo_ref, g2_ref, g3_ref, x1_ref, h2_ref):
        x1 = x_ref[...] + _rms(mo_ref[...], g2_ref[...])
        x1_ref[...] = x1
        h2_ref[...] = _rms(x1, g3_ref[...]).astype(BF16)

    return _row_call(body, "resid_rms2", [(x, "row"), (mo, "row"), (g2, "full"), (g3, "full")],
                     [(_sds((S, D), F32), "row"), (_sds((S, D), BF16), "row")], tr)


def _resid_rms(x1, yf, g4, tr):
    S, D = x1.shape

    def body(x_ref, y_ref, g_ref, o_ref, ob_ref):
        x2 = x_ref[...] + _rms(y_ref[...], g_ref[...])
        o_ref[...] = x2
        ob_ref[...] = x2.astype(BF16)

    return _row_call(body, "resid_rms", [(x1, "row"), (yf, "row"), (g4, "full")],
                     [(_sds((S, D), F32), "row"), (_sds((S, D), BF16), "row")], tr)


def _ple_loss(gl, e, x2, tgt, g5, tr):
    S, D = x2.shape

    def body(gl_ref, e_ref, x2_ref, t_ref, g_ref, loss_ref, dx3_ref, de_ref, dgl_ref, dg_ref):
        _first_step_zero(loss_ref, dg_ref)
        s = _sigmoid(gl_ref[...])
        e_ = e_ref[...]
        t = s * e_
        gain = g_ref[...]
        err = x2_ref[...] + _rms(t, gain) - t_ref[...]
        row_loss = jnp.mean(err * err, axis=-1, keepdims=True)
        loss_ref[...] += 0.5 * jnp.sum(row_loss, axis=0, keepdims=True)
        dx3 = err * (1.0 / D)
        dx3_ref[...] = dx3
        dt, dgain = _rms_bwd(t, gain, dx3)
        dg_ref[...] += dgain
        de_ref[...] = (dt * s).astype(BF16)
        dgl_ref[...] = (dt * e_ * s * (1.0 - s)).astype(BF16)

    return _row_call(body, "ple_loss", [(gl, "row"), (e, "row"), (x2, "row"), (tgt, "row"), (g5, "full")],
                     [(_sds((1, 1), F32), "acc"), (_sds((S, D), F32), "row"), (_sds((S, D), BF16), "row"),
                      (_sds((S, D), BF16), "row"), (_sds((1, D), F32), "acc")], tr)


def _rms_bwd_a(dx3, dx2g, yf, g4, tr):
    S, D = yf.shape

    def body(a_ref, b_ref, y_ref, g_ref, dx_ref, dy_ref, dg_ref):
        _first_step_zero(dg_ref)
        dx2 = a_ref[...] + b_ref[...]
        dx_ref[...] = dx2
        dy, dgain = _rms_bwd(y_ref[...], g_ref[...], dx2)
        dy_ref[...] = dy.astype(BF16)
        dg_ref[...] += dgain

    return _row_call(body, "rms_bwd_a", [(dx3, "row"), (dx2g, "row"), (yf, "row"), (g4, "full")],
                     [(_sds((S, D), F32), "row"), (_sds((S, D), BF16), "row"), (_sds((1, D), F32), "acc")], tr)


def _rms_bwd_b(dx2, dh2, x1, g3, mo, g2, tr):
    S, D = x1.shape

    def body(dx2_ref, dh2_ref, x1_ref, g3_ref, mo_ref, g2_ref, dx1_ref, dmo_ref, dg3_ref, dg2_ref):
        _first_step_zero(dg3_ref, dg2_ref)
        d, dgain3 = _rms_bwd(x1_ref[...], g3_ref[...], dh2_ref[...])
        dx1 = dx2_ref[...] + d
        dx1_ref[...] = dx1
        dg3_ref[...] += dgain3
        dmo, dgain2 = _rms_bwd(mo_ref[...], g2_ref[...], dx1)
        dmo_ref[...] = dmo.astype(BF16)
        dg2_ref[...] += dgain2

    return _row_call(body, "rms_bwd_b", [(dx2, "row"), (dh2, "row"), (x1, "row"), (g3, "full"), (mo, "row"), (g2, "full")],
                     [(_sds((S, D), F32), "row"), (_sds((S, D), BF16), "row"), (_sds((1, D), F32), "acc"),
                      (_sds((1, D), F32), "acc")], tr)


def _rms_bwd_c(dx1, dh, x, g1, tr):
    S, D = x.shape

    def body(dx1_ref, dh_ref, x_ref, g_ref, o_ref, dg_ref):
        _first_step_zero(dg_ref)
        d, dgain = _rms_bwd(x_ref[...], g_ref[...], dh_ref[...])
        o_ref[...] = dx1_ref[...] + d
        dg_ref[...] += dgain

    return _row_call(body, "rms_bwd_c", [(dx1, "row"), (dh, "row"), (x, "row"), (g1, "full")],
                     [(_sds((S, D), F32), "row"), (_sds((1, D), F32), "acc")], tr)


def _gate_bwd(dmixed, proj, ya, yp, gate_cb, tr):
    S, D = ya.shape

    def body(dm_ref, ga_ref, gp_ref, ya_ref, yp_ref, dya_ref, dyp_ref, dga_ref, dgp_ref):
        dm = dm_ref[...]
        sa = _sigmoid(ga_ref[...])
        sp = _sigmoid(gp_ref[...])
        dya_ref[...] = (dm * sa).astype(BF16)
        dyp_ref[...] = (dm * sp).astype(BF16)
        dga_ref[...] = (dm * ya_ref[...] * sa * (1.0 - sa)).astype(BF16)
        dgp_ref[...] = (dm * yp_ref[...] * sp * (1.0 - sp)).astype(BF16)

    return _row_call(body, "gate_bwd",
                     [(dmixed, "row"), (proj, "row", D, gate_cb), (proj, "row", D, gate_cb + 1), (ya, "row"), (yp, "row")],
                     [(_sds((S, D), BF16), "row")] * 4, tr)


def _scale_bwd(dps, pg, scale, tr):
    S, W = dps.shape

    def body(d_ref, pg_ref, s_ref, o_ref, ds_ref):
        _first_step_zero(ds_ref)
        d = d_ref[...]
        o_ref[...] = (d * s_ref[...]).astype(BF16)
        ds_ref[...] += jnp.sum(d * pg_ref[...], axis=0, keepdims=True)

    return _row_call(body, "scale_bwd", [(dps, "row"), (pg, "row"), (scale, "full")],
                     [(_sds((S, W), BF16), "row"), (_sds((1, W), F32), "acc")], tr)


CUMSUM_TERMS = 2


def _tri(blk, cmp):
    j = lax.broadcasted_iota(jnp.int32, (blk, blk), 0)
    s = lax.broadcasted_iota(jnp.int32, (blk, blk), 1)
    one = jnp.concatenate([cmp(j, s).astype(BF16), jnp.ones((blk, 128), BF16)], axis=1)
    return jnp.concatenate([one] * CUMSUM_TERMS, axis=0)


def _split(x):
    terms = []
    for _ in range(CUMSUM_TERMS):
        t = x.astype(BF16)
        terms.append(t)
        x = x - t.astype(F32)
    return terms[0] if CUMSUM_TERMS == 1 else jnp.concatenate(terms, axis=1)


def _split_dot(x, u):
    return jnp.dot(_split(x), u, preferred_element_type=F32)


def _causal(blk):
    return lax.broadcasted_iota(jnp.int32, (blk, blk), 1) < lax.broadcasted_iota(jnp.int32, (blk, blk), 0)


def _scores(q, kj, scale, causal):
    z = lax.dot_general(q, kj, (((1,), (1,)), ((), ())), preferred_element_type=F32) * scale
    l1p = jnp.log(1.0 + jnp.exp(-jnp.abs(z)))
    lb = -(jnp.maximum(z, 0.0) + l1p)
    if causal is not None:
        lb = jnp.where(causal, lb, 0.0)
    return z, lb, jnp.minimum(z, 0.0) - l1p


def _attn_fwd(proj, n_heads, blk, comm=None):
    S = proj.shape[0]
    nq = S // blk
    scale = HEAD_DIM ** -0.5
    lanes = blk // 128
    hp = 4 if n_heads % 4 == 0 else 2 if n_heads % 2 == 0 else 1
    cols = [slice(h * HEAD_DIM, (h + 1) * HEAD_DIM) for h in range(hp)]
    u_incl = _tri(blk, lambda j, s: j >= s)

    def body(q_ref, k_ref, v_ref, u_ref, o_ref, tot_ref, z_buf, hl_buf):
        i = pl.program_id(1)
        qs = [q_ref[:, c].astype(BF16) for c in cols]
        u = u_ref[...]

        def scores(j, h, causal):
            kj = k_ref[pl.ds(pl.multiple_of(j * blk, blk), blk), cols[h]].astype(BF16)
            z, lb, _ = _scores(qs[h], kj, scale, causal)
            return z, _split(lb)

        def weigh(j, h, z, hl, acc, run, causal):
            vj = v_ref[pl.ds(pl.multiple_of(j * blk, blk), blk), cols[h]].astype(BF16)
            ct = jnp.dot(hl, u, preferred_element_type=F32)
            a = jnp.exp(z + ct[:, :blk] + jnp.tile(run, (1, lanes)))
            if causal is not None:
                a = jnp.where(causal, a, 0.0)
            return acc + jnp.dot(a.astype(BF16), vj, preferred_element_type=F32), run + ct[:, blk:]

        def stage_scores(j):
            for h in range(hp):
                z_buf[h], hl_buf[h] = scores(j, h, None)

        zero = jnp.zeros((blk, HEAD_DIM), F32)
        causal = _causal(blk)
        carry = tuple(weigh(i, h, *scores(i, h, causal), zero, zero, causal) for h in range(hp))
        stage_scores(jnp.maximum(i - 1, 0))

        def step(t, carry):
            j = i - 1 - t
            out = tuple(weigh(j, h, z_buf[h], hl_buf[h], *carry[h], None) for h in range(hp))
            stage_scores(jnp.maximum(j - 1, 0))
            return out

        carry = lax.fori_loop(0, i, step, carry)
        for h in range(hp):
            o_ref[:, cols[h]] = carry[h][0].astype(BF16)
            tot_ref[:, cols[h]] = carry[h][1]

    G = n_heads // hp
    W = hp * HEAD_DIM
    return _call(
        body, name="attn_fwd", grid=(G, nq),
        in_specs=[pl.BlockSpec((blk, W), lambda h, i: (i, h)),
                  pl.BlockSpec((S, W), lambda h, i: (0, G + h)),
                  pl.BlockSpec((S, W), lambda h, i: (0, 2 * G + h)),
                  pl.BlockSpec(u_incl.shape, lambda h, i: (0, 0))],
        out_specs=[pl.BlockSpec((blk, W), lambda h, i: (i, h))] * 2,
        out_shape=[_sds((S, n_heads * HEAD_DIM), BF16), _sds((S, n_heads * HEAD_DIM), F32)],
        args=(proj, proj, proj, u_incl), scratch=[pltpu.VMEM((hp, blk, blk), F32), pltpu.VMEM((hp, blk, CUMSUM_TERMS * blk), BF16)],
        sem=("parallel", "arbitrary"), comm=comm)


def _attn_bwd(proj, tot, do, n_heads, blk, comm=None):
    S = proj.shape[0]
    nq = S // blk
    scale = HEAD_DIM ** -0.5
    lanes = blk // 128
    hp = 2 if n_heads % 2 == 0 else 1
    cols = [slice(h * HEAD_DIM, (h + 1) * HEAD_DIM) for h in range(hp)]
    l_strict = _tri(blk, lambda j, s: j < s)
    l_incl = _tri(blk, lambda j, s: j <= s)

    def body(q_ref, k_ref, v_ref, tot_ref, do_ref, ls_ref, li_ref, dq_ref, dk_ref, dv_ref, dk_acc, dv_acc, z_buf, beta_buf,
             da_buf, hl_buf):
        i = pl.program_id(1)

        @pl.when(i == 0)
        def _():
            dk_acc[...] = jnp.zeros_like(dk_acc)
            dv_acc[...] = jnp.zeros_like(dv_acc)

        qs = [q_ref[:, c].astype(BF16) for c in cols]
        dobs = [do_ref[:, c].astype(BF16) for c in cols]
        totals = [jnp.tile(tot_ref[:, c], (1, lanes)) for c in cols]
        ls = ls_ref[...]
        li = li_ref[...]

        def scores(j, h, causal):
            ks = pl.multiple_of(j * blk, blk)
            kj = k_ref[pl.ds(ks, blk), cols[h]].astype(BF16)
            vj = v_ref[pl.ds(ks, blk), cols[h]].astype(BF16)
            z, lb, log_beta = _scores(qs[h], kj, scale, causal)
            da = lax.dot_general(dobs[h], vj, (((1,), (1,)), ((), ())), preferred_element_type=F32)
            return z, jnp.exp(log_beta), da, _split(lb)

        def grads(j, h, z, beta, da, hl, dq, run_lb, run_g, causal):
            ks = pl.multiple_of(j * blk, blk)
            kj = k_ref[pl.ds(ks, blk), cols[h]].astype(BF16)
            pt = jnp.dot(hl, ls, preferred_element_type=F32)
            a = jnp.exp(z + totals[h] - (pt[:, :blk] + jnp.tile(run_lb, (1, lanes))))
            if causal is not None:
                a = jnp.where(causal, a, 0.0)
            g = a * da
            gt = _split_dot(g, li)
            dz = g - beta * (gt[:, :blk] + jnp.tile(run_g, (1, lanes)))
            if causal is not None:
                dz = jnp.where(causal, dz, 0.0)
            dzs = (dz * scale).astype(BF16)
            dk_acc[pl.ds(ks, blk), cols[h]] += lax.dot_general(dzs, qs[h], (((0,), (0,)), ((), ())),
                                                               preferred_element_type=F32)
            dv_acc[pl.ds(ks, blk), cols[h]] += lax.dot_general(a.astype(BF16), dobs[h], (((0,), (0,)), ((), ())),
                                                               preferred_element_type=F32)
            return dq + jnp.dot(dzs, kj, preferred_element_type=F32), run_lb + pt[:, blk:], run_g + gt[:, blk:]

        def stage_scores(j):
            for h in range(hp):
                z_buf[h], beta_buf[h], da_buf[h], hl_buf[h] = scores(j, h, None)

        zero = jnp.zeros((blk, HEAD_DIM), F32)
        stage_scores(0)

        def step(j, carry):
            out = tuple(grads(j, h, z_buf[h], beta_buf[h], da_buf[h], hl_buf[h], *carry[h], None) for h in range(hp))
            stage_scores(j + 1)
            return out

        carry = lax.fori_loop(0, i, step, ((zero, zero, zero),) * hp)
        causal = _causal(blk)
        carry = tuple(grads(i, h, *scores(i, h, causal), *carry[h], causal) for h in range(hp))
        for h in range(hp):
            dq_ref[:, cols[h]] = carry[h][0].astype(BF16)

        @pl.when(i == nq - 1)
        def _():
            dk_ref[...] = dk_acc[...].astype(BF16)
            dv_ref[...] = dv_acc[...].astype(BF16)

    G = n_heads // hp
    W = hp * HEAD_DIM
    AW = n_heads * HEAD_DIM
    return _call(
        body, name="attn_bwd", grid=(G, nq),
        in_specs=[pl.BlockSpec((blk, W), lambda h, i: (i, h)),
                  pl.BlockSpec((S, W), lambda h, i: (0, G + h)),
                  pl.BlockSpec((S, W), lambda h, i: (0, 2 * G + h)),
                  pl.BlockSpec((blk, W), lambda h, i: (i, h)),
                  pl.BlockSpec((blk, W), lambda h, i: (i, h)),
                  pl.BlockSpec(l_strict.shape, lambda h, i: (0, 0)),
                  pl.BlockSpec(l_incl.shape, lambda h, i: (0, 0))],
        out_specs=[pl.BlockSpec((blk, W), lambda h, i: (i, h)),
                   pl.BlockSpec((S, W), lambda h, i: (0, h)),
                   pl.BlockSpec((S, W), lambda h, i: (0, h))],
        out_shape=[_sds((S, AW), BF16)] * 3, args=(proj, proj, proj, tot, do, l_strict, l_incl),
        scratch=[pltpu.VMEM((S, W), F32)] * 2 + [pltpu.VMEM((hp, blk, blk), F32)] * 3 + [pltpu.VMEM((hp, blk, CUMSUM_TERMS * blk), BF16)],
        sem=("parallel", "arbitrary"), comm=comm)


def _pool_count(r0, rows, w):
    t = r0 + lax.broadcasted_iota(jnp.int32, (rows, 1), 0)
    return jnp.minimum(t + 1, w).astype(F32)


def _pool_fwd(proj, col_blk, n_groups, width, chunk):
    S = proj.shape[0]
    H = POOL_HALO

    def body(u_ref, o_ref, pad_ref):
        g = pl.program_id(0)
        pad_ref[0:H, :] = jnp.zeros((H, width), F32)
        pad_ref[H:, :] = u_ref[...]
        for gi, w in enumerate(POOL_WINDOWS[:n_groups]):
            @pl.when(g == gi)
            def _(w=w):
                def one(c, _):
                    r0 = pl.multiple_of(c * chunk, chunk)
                    ext = pad_ref[pl.ds(r0, chunk + H), :]
                    s = ext
                    k = 1
                    while k < w:
                        s = s + pltpu.roll(s, k, 0)
                        k *= 2
                    o_ref[pl.ds(r0, chunk), :] = (s[H:] / _pool_count(r0, chunk, w) - ext[H:]).astype(BF16)
                    return 0

                lax.fori_loop(0, S // chunk, one, 0)

    return pl.pallas_call(
        body, name="pool_fwd", grid=(n_groups,),
        in_specs=[pl.BlockSpec((S, width), lambda g: (0, col_blk + g))],
        out_specs=pl.BlockSpec((S, width), lambda g: (0, g)), out_shape=_sds((S, n_groups * width), BF16),
        scratch_shapes=[pltpu.VMEM((S + H, width), F32)], compiler_params=_params("parallel"),
    )(proj)


def _pool_bwd(dpooled, n_groups, chunk):
    S = dpooled.shape[0]
    width = dpooled.shape[1] // n_groups
    H = POOL_HALO

    def body(d_ref, o_ref, pad_ref):
        g = pl.program_id(0)
        pad_ref[S:, :] = jnp.zeros((H, width), F32)
        for gi, w in enumerate(POOL_WINDOWS[:n_groups]):
            @pl.when(g == gi)
            def _(w=w):
                def fill(c, _):
                    r0 = pl.multiple_of(c * chunk, chunk)
                    pad_ref[pl.ds(r0, chunk), :] = d_ref[pl.ds(r0, chunk), :] / _pool_count(r0, chunk, w)
                    return 0

                lax.fori_loop(0, S // chunk, fill, 0)

                def one(c, _):
                    r0 = pl.multiple_of(c * chunk, chunk)
                    s = pad_ref[pl.ds(r0, chunk + H), :]
                    k = 1
                    while k < w:
                        s = s + pltpu.roll(s, chunk + H - k, 0)
                        k *= 2
                    o_ref[pl.ds(r0, chunk), :] = (s[:chunk] - d_ref[pl.ds(r0, chunk), :]).astype(BF16)
                    return 0

                lax.fori_loop(0, S // chunk, one, 0)

    return pl.pallas_call(
        body, name="pool_bwd", grid=(n_groups,),
        in_specs=[pl.BlockSpec((S, width), lambda g: (0, g))],
        out_specs=pl.BlockSpec((S, width), lambda g: (0, g)), out_shape=_sds((S, n_groups * width), BF16),
        scratch_shapes=[pltpu.VMEM((S + H, width), F32)], compiler_params=_params("parallel"),
    )(dpooled)


def _conv3(x_ext, w, b, shifted=None):
    x2, x1 = shifted if shifted is not None else (pltpu.roll(x_ext, 2, 0), pltpu.roll(x_ext, 1, 0))
    return b + x2 * w[0:1, :] + x1 * w[1:2, :] + x_ext * w[2:3, :]


def _gelu_parts(x):
    th = jnp.tanh(GELU_C0 * (x + GELU_C1 * (x * x * x)))
    return th, 0.5 * (1.0 + th)


def _conv_specs(S, F, cb):
    nb = F // cb
    return [pl.BlockSpec((S, cb), lambda j: (0, j)), pl.BlockSpec((S, cb), lambda j: (0, nb + j)),
            pl.BlockSpec((3, cb), lambda j: (0, j)), pl.BlockSpec((3, cb), lambda j: (0, nb + j)),
            pl.BlockSpec((1, cb), lambda j: (0, j)), pl.BlockSpec((1, cb), lambda j: (0, nb + j))]


def _conv_fwd(upre, cw, cb_, chunk):
    S, F2 = upre.shape
    F = F2 // 2
    cb = 128
    H = CONV_HALO

    def body(g_ref, v_ref, wg_ref, wv_ref, bg_ref, bv_ref, o_ref, pg_ref, pv_ref):
        pg_ref[0:H, :] = jnp.zeros((H, cb), F32)
        pv_ref[0:H, :] = jnp.zeros((H, cb), F32)
        pg_ref[H:, :] = g_ref[...]
        pv_ref[H:, :] = v_ref[...]
        wg, wv, bg, bv = wg_ref[...], wv_ref[...], bg_ref[...], bv_ref[...]

        def one(c, _):
            r0 = pl.multiple_of(c * chunk, chunk)
            up_g = _conv3(pg_ref[pl.ds(r0, chunk + H), :], wg, bg)[H:]
            up_v = _conv3(pv_ref[pl.ds(r0, chunk + H), :], wv, bv)[H:]
            _, cdf = _gelu_parts(up_g)
            o_ref[pl.ds(r0, chunk), :] = (up_g * cdf * up_v).astype(BF16)
            return 0

        lax.fori_loop(0, S // chunk, one, 0)

    return pl.pallas_call(
        body, name="conv_fwd", grid=(F // cb,), in_specs=_conv_specs(S, F, cb),
        out_specs=pl.BlockSpec((S, cb), lambda j: (0, j)), out_shape=_sds((S, F), BF16),
        scratch_shapes=[pltpu.VMEM((S + H, cb), F32)] * 2, compiler_params=_params("parallel"),
    )(upre, upre, cw, cw, cb_, cb_)


def _conv_bwd(upre, dact, cw, cb_, chunk, comm=None):
    S, F2 = upre.shape
    F = F2 // 2
    cb = 128
    H = CONV_HALO
    E = chunk + 2 * H

    def body(g_ref, v_ref, wg_ref, wv_ref, bg_ref, bv_ref, d_ref, dg_ref, dv_ref, dwg_ref, dwv_ref, dbg_ref, dbv_ref,
             pg_ref, pv_ref, pd_ref):
        for p, src in ((pg_ref, g_ref), (pv_ref, v_ref), (pd_ref, d_ref)):
            p[0:H, :] = jnp.zeros((H, cb), F32)
            p[H:S + H, :] = src[...]
            p[S + H:, :] = jnp.zeros((H, cb), F32)
        wg, wv, bg, bv = wg_ref[...], wv_ref[...], bg_ref[...], bv_ref[...]

        def taps_bwd(d, w):
            return d * w[2:3, :] + pltpu.roll(d, E - 1, 0) * w[1:2, :] + pltpu.roll(d, E - 2, 0) * w[0:1, :]

        def wsum(d, x, x2, x1):
            dc = d[H:H + chunk]
            return [jnp.sum(dc * x2[H:H + chunk], axis=0, keepdims=True),
                    jnp.sum(dc * x1[H:H + chunk], axis=0, keepdims=True),
                    jnp.sum(dc * x[H:H + chunk], axis=0, keepdims=True),
                    jnp.sum(dc, axis=0, keepdims=True)]

        def one(c, acc):
            r0 = pl.multiple_of(c * chunk, chunk)
            xg = pg_ref[pl.ds(r0, E), :]
            xv = pv_ref[pl.ds(r0, E), :]
            d = pd_ref[pl.ds(r0, E), :]
            sg = (pltpu.roll(xg, 2, 0), pltpu.roll(xg, 1, 0))
            sv = (pltpu.roll(xv, 2, 0), pltpu.roll(xv, 1, 0))
            up_g = _conv3(xg, wg, bg, sg)
            up_v = _conv3(xv, wv, bv, sv)
            th, cdf = _gelu_parts(up_g)
            dgelu = cdf + 0.5 * up_g * (1.0 - th * th) * (GELU_C0 * (1.0 + 3.0 * GELU_C1 * (up_g * up_g)))
            dgate = d * up_v * dgelu
            dval = d * (up_g * cdf)
            dg_ref[pl.ds(r0, chunk), :] = taps_bwd(dgate, wg)[H:H + chunk].astype(BF16)
            dv_ref[pl.ds(r0, chunk), :] = taps_bwd(dval, wv)[H:H + chunk].astype(BF16)
            return tuple(a + b for a, b in zip(acc, wsum(dgate, xg, *sg) + wsum(dval, xv, *sv)))

        zero = jnp.zeros((1, cb), F32)
        acc = lax.fori_loop(0, S // chunk, one, (zero,) * 8)
        dwg_ref[...] = jnp.concatenate(acc[0:3], axis=0)
        dbg_ref[...] = acc[3]
        dwv_ref[...] = jnp.concatenate(acc[4:7], axis=0)
        dbv_ref[...] = acc[7]

    col = lambda rows: pl.BlockSpec((rows, cb), lambda j: (0, j))
    return _call(
        body, name="conv_bwd", grid=(F // cb,), in_specs=_conv_specs(S, F, cb) + [col(S)],
        out_specs=[col(S), col(S), col(3), col(3), col(1), col(1)],
        out_shape=[_sds((S, F), BF16), _sds((S, F), BF16), _sds((3, F), F32), _sds((3, F), F32), _sds((1, F), F32),
                   _sds((1, F), F32)],
        args=(upre, upre, cw, cw, cb_, cb_, dact), scratch=[pltpu.VMEM((S + 2 * H, cb), F32)] * 3, sem=("parallel",),
        comm=comm)


def _position():
    x, y, c = lax.axis_index("x"), lax.axis_index("y"), lax.axis_index("c")
    return x, y, c, 4 * x + 2 * y + c


def _peer(x, y, c, d):
    px = 1 - x if d & 4 else x
    py = 1 - y if d & 2 else y
    pc = 1 - c if d & 1 else c
    return (px, py, pc), 4 * px + 2 * py + pc


def _all_gather_comm(tensors):
    nt = len(tensors)
    outs = [_sds((N_DEV,) + t.shape, t.dtype) for t in tensors]
    sems = [pltpu.SemaphoreType.DMA((7 * nt,)), pltpu.SemaphoreType.DMA((7 * nt,)), pltpu.SemaphoreType.DMA((nt,))]

    def parts(ins, outs_, sem_refs):
        send, recv, loc = sem_refs
        x, y, c, me = _position()
        chips = [(1 - x, y), (x, 1 - y), (1 - x, 1 - y)]

        def copy(t, k, block, to, src=None):
            slot = outs_[t].at[4 * block[0] + 2 * block[1] + block[2]]
            return pltpu.make_async_remote_copy(src_ref=slot if src is None else src, dst_ref=slot,
                                                send_sem=send.at[7 * t + k], recv_sem=recv.at[7 * t + k], device_id=to,
                                                device_id_type=MESH)

        def mine(t):
            return pltpu.make_async_copy(ins[t], outs_[t].at[me], loc.at[t])

        return (x, y, c), (x, y, 1 - c), chips, copy, mine

    def start(ins, outs_, sem_refs):
        me, sibling, chips, copy, mine = parts(ins, outs_, sem_refs)
        for t in range(nt):
            mine(t).start()
            copy(t, 0, me, sibling, src=ins[t]).start()
            for j, chip in enumerate(chips):
                copy(t, 1 + j, me, (*chip, me[2]), src=ins[t]).start()

    def finish(ins, outs_, sem_refs):
        me, sibling, chips, copy, mine = parts(ins, outs_, sem_refs)
        c = me[2]
        for t in range(nt):
            for j, chip in enumerate(chips):
                copy(t, 1 + j, (*chip, c), me).wait_recv()
                copy(t, 4 + j, (*chip, c), sibling).start()
        for t in range(nt):
            copy(t, 0, sibling, me).wait_recv()
            for j, chip in enumerate(chips):
                copy(t, 4 + j, (*chip, 1 - c), me).wait_recv()
        for t in range(nt):
            copy(t, 0, me, sibling, src=ins[t]).wait_send()
            for j, chip in enumerate(chips):
                copy(t, 1 + j, me, (*chip, c), src=ins[t]).wait_send()
                copy(t, 4 + j, (*chip, c), sibling).wait_send()
            mine(t).wait()

    return _Comm(tensors, outs, sems, start, finish)


def _reduce_scatter_comm(tensors):
    nt = len(tensors)
    outs = [_sds(t.shape, t.dtype) for t in tensors]
    sems = [pltpu.SemaphoreType.DMA((7 * nt,)), pltpu.SemaphoreType.DMA((7 * nt,)), pltpu.SemaphoreType.DMA((nt,))]

    def local(ins, outs_, sem_refs, t, me):
        return pltpu.make_async_copy(ins[t].at[me], outs_[t].at[me], sem_refs[2].at[t])

    def remote(ins, outs_, sem_refs, t, d, inbound):
        x, y, c, me = _position()
        peer, peer_idx = _peer(x, y, c, d)
        k = 7 * t + d - 1
        src, dst, to = (ins[t].at[me], outs_[t].at[peer_idx], (x, y, c)) if inbound else (ins[t].at[peer_idx], outs_[t].at[me], peer)
        return pltpu.make_async_remote_copy(src_ref=src, dst_ref=dst, send_sem=sem_refs[0].at[k], recv_sem=sem_refs[1].at[k],
                                            device_id=to, device_id_type=MESH)

    def start(ins, outs_, sem_refs):
        me = _position()[3]
        for t in range(nt):
            local(ins, outs_, sem_refs, t, me).start()
            for d in range(1, N_DEV):
                remote(ins, outs_, sem_refs, t, d, False).start()

    def finish(ins, outs_, sem_refs):
        me = _position()[3]
        for t in range(nt):
            for d in range(1, N_DEV):
                remote(ins, outs_, sem_refs, t, d, True).wait_recv()
        for t in range(nt):
            for d in range(1, N_DEV):
                remote(ins, outs_, sem_refs, t, d, False).wait_send()
            local(ins, outs_, sem_refs, t, me).wait()

    return _Comm(tensors, outs, sems, start, finish)


HBM_SPEC = pl.BlockSpec(memory_space=pltpu.HBM)
SEM_SPEC = pl.BlockSpec(memory_space=pltpu.SEMAPHORE)
DATAFLOW = pltpu.SideEffectType.DATAFLOW_SIDE_EFFECTING


def _scatter_copy(g_ref, land_ref, send_sems, recv_sems, d):
    x, y, c, me = _position()
    peer, peer_idx = _peer(x, y, c, d)
    return pltpu.make_async_remote_copy(src_ref=g_ref.at[peer_idx], dst_ref=land_ref.at[me], send_sem=send_sems.at[d - 1],
                                        recv_sem=recv_sems.at[d - 1], device_id=peer, device_id_type=MESH)


def _reduce_scatter_start(g, name):
    def body(g_ref, land_ref, send_sems, recv_sems, g_thru, land_thru, token):
        for d in range(1, N_DEV):
            _scatter_copy(g_ref, land_ref, send_sems, recv_sems, d).start()
        token[...] = jnp.zeros_like(token)

    return pl.pallas_call(
        body, name=name,
        out_shape=(pltpu.SemaphoreType.DMA((N_DEV - 1,)), pltpu.SemaphoreType.DMA((N_DEV - 1,)), pltpu.HBM(g.shape, g.dtype),
                   pltpu.HBM(g.shape, g.dtype), _sds((8, 128), F32)),
        in_specs=(HBM_SPEC, HBM_SPEC), out_specs=(SEM_SPEC, SEM_SPEC, HBM_SPEC, HBM_SPEC, pl.BlockSpec(memory_space=pltpu.VMEM)),
        input_output_aliases={0: 2, 1: 3}, compiler_params=pltpu.CompilerParams(has_side_effects=DATAFLOW),
    )(pltpu.with_memory_space_constraint(g, pltpu.HBM), pltpu.with_memory_space_constraint(lax.empty(g.shape, g.dtype), pltpu.HBM))


def _reduce_scatter_wait(send_sems, recv_sems, g_thru, land_thru, after, name):
    def body(g_ref, land_ref, send_sems, recv_sems, after_ref, g_out, land_out):
        for d in range(1, N_DEV):
            copy = _scatter_copy(g_ref, land_ref, send_sems, recv_sems, d)
            copy.wait_send()
            copy.wait_recv()

    return pl.pallas_call(
        body, name=name, out_shape=(pltpu.HBM(g_thru.shape, g_thru.dtype), pltpu.HBM(g_thru.shape, g_thru.dtype)),
        in_specs=(HBM_SPEC, HBM_SPEC, SEM_SPEC, SEM_SPEC, ANY), out_specs=(HBM_SPEC, HBM_SPEC), input_output_aliases={0: 0, 1: 1},
        compiler_params=pltpu.CompilerParams(has_side_effects=DATAFLOW),
    )(g_thru, land_thru, send_sems, recv_sems, after)


def _all_reduce_small(part):
    r, W = part.shape

    def body(p_ref, o_ref, g_ref, send_sems, recv_sems):
        x, y, c, me = _position()
        sends = []
        for d in range(1, N_DEV):
            peer, _ = _peer(x, y, c, d)
            cp = pltpu.make_async_remote_copy(src_ref=p_ref, dst_ref=g_ref.at[me], send_sem=send_sems.at[d - 1],
                                              recv_sem=recv_sems.at[d - 1], device_id=peer, device_id_type=MESH)
            cp.start()
            sends.append(cp)
        g_ref[me] = p_ref[...]
        for d in range(1, N_DEV):
            _, peer_idx = _peer(x, y, c, d)
            pltpu.make_async_remote_copy(src_ref=p_ref, dst_ref=g_ref.at[peer_idx], send_sem=send_sems.at[d - 1],
                                         recv_sem=recv_sems.at[d - 1], device_id=(x, y, c), device_id_type=MESH).wait_recv()
        for cp in sends:
            cp.wait_send()
        acc = g_ref[0]
        for i in range(1, N_DEV):
            acc = acc + g_ref[i]
        o_ref[...] = acc

    vmem = pl.BlockSpec(memory_space=pltpu.VMEM)
    return pl.pallas_call(
        body, name="all_reduce_small", in_specs=[vmem], out_specs=[vmem, vmem],
        out_shape=[_sds((r, W), F32), _sds((N_DEV, r, W), F32)],
        scratch_shapes=[pltpu.SemaphoreType.DMA((7,)), pltpu.SemaphoreType.DMA((7,))],
        compiler_params=pltpu.CompilerParams(has_side_effects=True, vmem_limit_bytes=VMEM_LIMIT),
    )(part)[0]


def _adamw_math(w, g, m, v):
    m = ADAM_B1 * m + (1.0 - ADAM_B1) * g
    v = ADAM_B2 * v + (1.0 - ADAM_B2) * (g * g)
    m_hat = m / (1.0 - ADAM_B1 ** ADAM_STEP)
    v_hat = v / (1.0 - ADAM_B2 ** ADAM_STEP)
    return -ADAM_LR * (m_hat / (jnp.sqrt(v_hat) + ADAM_EPS) + ADAM_WD * w), m, v


def _adamw(w, g, m, v, name):
    rows, cols = w.shape
    tr = _tile(rows, max(8, (2**18 // cols) // 8 * 8), 8)

    def body(w_ref, g_ref, m_ref, v_ref, d_ref, nm_ref, nv_ref):
        d_ref[...], nm_ref[...], nv_ref[...] = _adamw_math(w_ref[...], g_ref[...], m_ref[...], v_ref[...])

    spec = pl.BlockSpec((tr, cols), lambda i: (i, 0))
    return pl.pallas_call(
        body, name=name, grid=(rows // tr,), in_specs=[spec] * 4, out_specs=[spec] * 3,
        out_shape=[_sds((rows, cols), F32)] * 3, compiler_params=_params("parallel"),
    )(w, g, m, v)


def _adamw_sum(recv, w, m, v, name):
    n, rows, cols = recv.shape
    tr = _tile(rows, max(16, (2**17 // cols) // 16 * 16), 16)

    def body(r_ref, w_ref, m_ref, v_ref, g_ref, d_ref, nm_ref, nv_ref):
        g = r_ref[0].astype(F32)
        for i in range(1, n):
            g = g + r_ref[i].astype(F32)
        g_ref[...] = g
        d_ref[...], nm_ref[...], nv_ref[...] = _adamw_math(w_ref[...], g, m_ref[...], v_ref[...])

    spec = pl.BlockSpec((tr, cols), lambda i: (i, 0))
    return pl.pallas_call(
        body, name=name, grid=(rows // tr,), in_specs=[pl.BlockSpec((n, tr, cols), lambda i: (0, i, 0))] + [spec] * 3,
        out_specs=[spec] * 4, out_shape=[_sds((rows, cols), F32)] * 4, compiler_params=_params("parallel"),
    )(recv, w, m, v)


COLUMN_CUT = ("w_in", "w_attn_branch", "w_pool_branch", "w_up", "w_ple")
ROW_CUT = ("w_out", "w_down", "w_ple_gate")
REPLICATED = ("norm_mix_pre", "pool_scale", "norm_mix_post", "norm_ffn_pre", "conv_b", "norm_ffn_post", "norm_ple_post")
WEIGHTS = ("norm_mix_pre", "w_in", "w_attn_branch", "w_pool_group", "pool_scale", "w_pool_branch", "w_out", "norm_mix_post",
           "norm_ffn_pre", "w_up", "conv_w", "conv_b", "w_down", "norm_ffn_post", "w_ple", "w_ple_gate", "norm_ple_post")


def _size(shape):
    n = 1
    for s in shape:
        n *= s
    return n


def _pad_rows(flat, row_align):
    n = flat.shape[-1]
    per = PACK_W * row_align
    total = -(-n // per) * per
    return jnp.pad(flat, [(0, total - n)]).reshape(total // PACK_W, PACK_W)


def _natural(shard_major):
    n, r, c = shard_major.shape
    return shard_major.reshape(n * r, c)


def kernel(x, p, norm_mix_pre, w_in, w_attn_branch, w_pool_group, pool_scale, w_pool_branch, w_out, norm_mix_post, norm_ffn_pre, w_up, conv_w, conv_b, w_down, norm_ffn_post, w_ple, w_ple_gate, norm_ple_post, loss_target, m_norm_mix_pre, m_w_in, m_w_attn_branch, m_w_pool_group, m_pool_scale, m_w_pool_branch, m_w_out, m_norm_mix_post, m_norm_ffn_pre, m_w_up, m_conv_w, m_conv_b, m_w_down, m_norm_ffn_post, m_w_ple, m_w_ple_gate, m_norm_ple_post, v_norm_mix_pre, v_w_in, v_w_attn_branch, v_w_pool_group, v_pool_scale, v_w_pool_branch, v_w_out, v_norm_mix_post, v_norm_ffn_pre, v_w_up, v_conv_w, v_conv_b, v_w_down, v_norm_ffn_post, v_w_ple, v_w_ple_gate, v_norm_ple_post):
    given = dict(locals())
    wts = {n: given[n][0] for n in WEIGHTS}
    mom = {n: given["m_" + n][0] for n in WEIGHTS}
    var = {n: given["v_" + n][0] for n in WEIGHTS}
    xs = x[0]
    ps_in = p[0, 0]
    tgt = loss_target[0]
    S, D = xs.shape
    AW = wts["w_attn_branch"].shape[0]
    PW = wts["w_pool_branch"].shape[0]
    G = wts["w_pool_group"].shape[0]
    PGW = PW // G
    H = AW // HEAD_DIM
    F = wts["w_down"].shape[0] * N_DEV
    assert (3 * AW) % PGW == 0 and (3 * AW + PW) % D == 0 and PGW % 128 == 0 and F % 128 == 0
    tr = _tile(S, 256, 16)
    blk = _tile(S, 256, 128)
    chunk = _tile(S, 256, 8)
    me = 4 * lax.axis_index("x") + 2 * lax.axis_index("y") + lax.axis_index("c")

    cw_shape = wts["conv_w"].shape
    conv_b_row = wts["conv_b"].reshape(1, -1)
    g1, g2, g3, g4, g5 = (wts[n].reshape(1, D) for n in
                          ("norm_mix_pre", "norm_mix_post", "norm_ffn_pre", "norm_ffn_post", "norm_ple_post"))
    pscale = wts["pool_scale"].reshape(1, PW)
    big = 4096
    wb = {n: wts[n].astype(BF16) for n in COLUMN_CUT + ROW_CUT}
    wb["w_pool_group"] = wts["w_pool_group"].astype(BF16).reshape(G * PGW // N_DEV, PGW)

    h, w_in = _rms_fwd(xs, g1, tr, _all_gather_comm([wb["w_in"]]))
    proj, w_ab, w_pg, w_pb, w_out = _mm(
        h, w_in, b_sm=True, name="mm_in", tk=2048,
        comm=_all_gather_comm([wb["w_attn_branch"], wb["w_pool_group"], wb["w_pool_branch"], wb["w_out"]]))
    w_pg = jnp.moveaxis(w_pg.reshape(N_DEV, G, PGW // N_DEV, PGW), 0, 1).reshape(G, PGW, PGW)
    w_out = _natural(w_out)
    attn, tot, w_up, conv_w_all = _attn_fwd(proj, H, blk, _all_gather_comm([wb["w_up"], wts["conv_w"]]))
    conv_w_full = jnp.moveaxis(conv_w_all, 0, 1).reshape(cw_shape[0], N_DEV * cw_shape[1])
    y_attn = _mm(attn, w_ab, b_sm=True, name="mm_attn_branch", tm=big, tn=256, tk=big)
    pooled = _pool_fwd(proj, 3 * AW // PGW, G, PGW, chunk)
    pg, ps = _pool_group_fwd(pooled, w_pg, pscale)
    y_pool = _mm(ps, w_pb, b_sm=True, name="mm_pool_branch", tm=big, tn=256, tk=big)
    gate_cb = (3 * AW + PW) // D
    mixed = _gate_mix(proj, y_attn, y_pool, gate_cb, tr)
    mo = _mm(mixed, w_out, name="mm_out", tk=2048)
    x1, h2 = _resid_rms2(xs, mo, g2, g3, tr)
    upre, w_down, w_ple, w_pleg = _mm(h2, w_up, b_sm=True, name="mm_up", tn=2048, tk=1024,
                                      comm=_all_gather_comm([wb["w_down"], wb["w_ple"], wb["w_ple_gate"]]))
    w_down, w_pleg = _natural(w_down), _natural(w_pleg)
    act = _conv_fwd(upre, conv_w_full, conv_b_row, chunk)
    yf = _mm(act, w_down, name="mm_down", tk=1408)
    x2, x2b = _resid_rms(x1, yf, g4, tr)
    e = _mm(ps_in, w_ple, b_sm=True, name="mm_ple", tm=big, tn=256, tk=big)
    gl = _mm(x2b, w_pleg, name="mm_ple_gate", tk=2048)
    loss_part, dx3, de, dgl, dg5 = _ple_loss(gl, e, x2, tgt, g5, tr)

    shards = lambda natural: natural.reshape((N_DEV, natural.shape[0] // N_DEV) + natural.shape[1:])
    recv = {}
    dw_ple = _mm(ps_in, de, ta=True, out_sm=True, out_dtype=BF16, name="mm_d_w_ple", tm=256, tn=256, tk=big)
    dw_pleg = shards(_mm(x2b, dgl, ta=True, out_dtype=BF16, name="mm_d_w_ple_gate"))
    dx2g = _mm(dgl, w_pleg, tb=True, name="mm_d_x2", tk=2048)
    dx2, dyf, dg4 = _rms_bwd_a(dx3, dx2g, yf, g4, tr)
    dw_down = shards(_mm(act, dyf, ta=True, out_dtype=BF16, name="mm_d_w_down"))
    dact, recv["w_ple"], recv["w_ple_gate"] = _mm(dyf, w_down, tb=True, name="mm_d_act", tn=1408, tk=1024,
                                                  comm=_reduce_scatter_comm([dw_ple, dw_pleg]))
    dup_g, dup_v, dcw_g, dcw_v, dcb_g, dcb_v, recv["w_down"] = _conv_bwd(upre, dact, conv_w_full, conv_b_row, chunk,
                                                                          _reduce_scatter_comm([dw_down]))
    dupre = jnp.concatenate([dup_g, dup_v], axis=1)
    dw_up = _mm(h2, dupre, ta=True, out_sm=True, out_dtype=BF16, name="mm_d_w_up", tn=2048)
    dh2 = _mm(dupre, w_up, tb=True, b_sm=True, name="mm_d_h2", tk=2048)
    dx1, dmo, dg3, dg2 = _rms_bwd_b(dx2, dh2, x1, g3, mo, g2, tr)
    dmixed = _mm(dmo, w_out, tb=True, name="mm_d_mixed", tk=2048)
    dw_out = shards(_mm(mixed, dmo, ta=True, out_dtype=BF16, name="mm_d_w_out"))
    dya, dyp, dga, dgp = _gate_bwd(dmixed, proj, y_attn, y_pool, gate_cb, tr)
    dps = _mm(dyp, w_pb, tb=True, b_sm=True, name="mm_d_ps", tm=2048, tk=256)
    dw_pb = _mm(ps, dyp, ta=True, out_sm=True, out_dtype=BF16, name="mm_d_w_pool_branch", tn=256, tk=big)
    dpg, dscale = _scale_bwd(dps, pg, pscale, tr)
    dpooled = _pool_group_bwd_x(dpg, w_pg)
    dw_pg = _pool_group_bwd_w(pooled, dpg, G)
    dw_pg = jnp.moveaxis(dw_pg.astype(BF16).reshape(G, N_DEV, PGW // N_DEV, PGW), 1, 0).reshape(N_DEV, G * PGW // N_DEV, PGW)
    du = _pool_bwd(dpooled, G, chunk)
    dattn = _mm(dya, w_ab, tb=True, b_sm=True, name="mm_d_attn", tm=2048, tk=256)
    dw_ab = _mm(attn, dya, ta=True, out_sm=True, out_dtype=BF16, name="mm_d_w_attn_branch", tn=256, tk=big)
    dq, dk, dv, recv["w_up"], recv["w_out"], recv["w_pool_branch"], recv["w_pool_group"], recv["w_attn_branch"] = _attn_bwd(
        proj, tot, dattn, H, blk, _reduce_scatter_comm([dw_up, dw_out, dw_pb, dw_pg, dw_ab]))
    dproj = jnp.concatenate([dq, dk, dv, du, dga, dgp], axis=1)
    dw_in = _mm(h, dproj, ta=True, out_sm=True, out_dtype=BF16, name="mm_d_w_in")
    in_send, in_recv, dw_in, in_land, token = _reduce_scatter_start(dw_in, "rs_w_in_start")
    dproj, _ = lax.optimization_barrier((dproj, token))
    dh = _mm(dproj, w_in, tb=True, b_sm=True, name="mm_d_h", tk=1024)
    grad_x, dg1 = _rms_bwd_c(dx1, dh, xs, g1, tr)

    gshard, delta, new_m, new_v = {}, {}, {}, {}

    def adamw_cut(n):
        shp = wts[n].shape
        two_d = (_size(shp[:-1]), shp[-1])
        g_, d_, m_, v_ = _adamw_sum(recv[n].reshape((N_DEV,) + two_d), wts[n].reshape(two_d), mom[n].reshape(two_d),
                                    var[n].reshape(two_d), "adamw_" + n)
        gshard[n], delta[n], new_m[n], new_v[n] = g_.reshape(shp), d_.reshape(shp), m_.reshape(shp), v_.reshape(shp)

    for n in COLUMN_CUT[1:] + ROW_CUT + ("w_pool_group",):
        adamw_cut(n)

    assert COLUMN_CUT[0] == "w_in"
    done = lax.optimization_barrier((grad_x,) + tuple(delta[n] for n in delta))
    grad_x = done[0]
    for n, d_ in zip(list(delta), done[1:]):
        delta[n] = d_
    dw_in, in_land = _reduce_scatter_wait(in_send, in_recv, dw_in, in_land, grad_x, "rs_w_in_wait")

    dconv_w = jnp.concatenate([dcw_g, dcw_v], axis=1)
    dconv_b = jnp.concatenate([dcb_g, dcb_v], axis=1).reshape(-1)
    rep_parts = {"norm_mix_pre": dg1, "pool_scale": dscale, "norm_mix_post": dg2, "norm_ffn_pre": dg3, "conv_b": dconv_b,
                 "norm_ffn_post": dg4, "norm_ple_post": dg5}
    small = jnp.concatenate([rep_parts[n].reshape(-1) for n in REPLICATED] + [dconv_w.reshape(-1)])
    n_small = small.shape[0]
    small, in_land = lax.optimization_barrier((small, in_land))
    small_sum = _all_reduce_small(_pad_rows(small, 8)).reshape(-1)[:n_small]
    off = 0
    for n in REPLICATED:
        sz = _size(wts[n].shape)
        gshard[n] = small_sum[off:off + sz].reshape(wts[n].shape)
        off += sz
    dconv_w_sum = small_sum[off:off + 3 * 2 * F].reshape(3, 2 * F)
    gshard["conv_w"] = lax.dynamic_slice_in_dim(dconv_w_sum, me * cw_shape[1], cw_shape[1], axis=1)

    delta["conv_w"], new_m["conv_w"], new_v["conv_w"] = _adamw(wts["conv_w"], gshard["conv_w"], mom["conv_w"], var["conv_w"],
                                                               "adamw_conv_w")
    rep_sizes = [_size(wts[n].shape) for n in REPLICATED]
    n_rep = sum(rep_sizes)
    cat = lambda t: _pad_rows(jnp.concatenate([t[n].reshape(-1) for n in REPLICATED]), 8)
    d_, m_, v_ = _adamw(cat(wts), cat(gshard), cat(mom), cat(var), "adamw_replicated")
    off = 0
    for n, sz in zip(REPLICATED, rep_sizes):
        shp = wts[n].shape
        delta[n], new_m[n], new_v[n] = (t.reshape(-1)[off:off + sz].reshape(shp) for t in (d_, m_, v_))
        off += sz
    assert off == n_rep

    own = lax.dynamic_index_in_dim(dw_in, me, 0, keepdims=True)
    recv["w_in"] = lax.dynamic_update_slice_in_dim(in_land, own, me, 0)
    adamw_cut("w_in")

    loss = lax.psum(loss_part[0, 0], ("x", "y", "c"))
    lead = lambda t: t[None]
    return (loss, grad_x[None], *[lead(gshard[n]) for n in WEIGHTS], *[lead(delta[n]) for n in WEIGHTS],
            *[lead(new_m[n]) for n in WEIGHTS], *[lead(new_v[n]) for n in WEIGHTS])
```

```python
import functools

import jax
import jax.numpy as jnp
from jax import lax
from jax.experimental import pallas as pl
from jax.experimental.pallas import tpu as pltpu

F32 = jnp.float32
BF16 = jnp.bfloat16
MESH = pl.DeviceIdType.MESH

EPS = 1e-6
HEAD_DIM = 128
POOL_WINDOWS = (2, 4, 8, 16)
POOL_HALO = 16
CONV_HALO = 8
GELU_C0 = 0.7978845608028654
GELU_C1 = 0.044715
ADAM_LR = 0.001
ADAM_B1 = 0.9
ADAM_B2 = 0.999
ADAM_EPS = 1e-08
ADAM_WD = 0.01
ADAM_STEP = 10
N_DEV = 8
PACK_W = 1024
VMEM_LIMIT = 56 * 2**20
ANY = pl.BlockSpec(memory_space=pl.ANY)


def _params(*sem):
    return pltpu.CompilerParams(dimension_semantics=sem, vmem_limit_bytes=VMEM_LIMIT)


def _sds(shape, dtype):
    return jax.ShapeDtypeStruct(shape, dtype)


def _tile(dim, target, align):
    if dim <= target:
        return dim
    t = (target // align) * align
    while t >= align:
        if dim % t == 0:
            return t
        t -= align
    return dim


def _sigmoid(x):
    return 1.0 / (1.0 + jnp.exp(-x))


class _Comm:
    def __init__(self, ins, outs, sems, start, finish):
        self.ins, self.outs, self.sems, self.start, self.finish = list(ins), list(outs), list(sems), start, finish


def _call(body, *, name, grid, in_specs, out_specs, out_shape, args, scratch=(), sem=(), comm=None):
    in_specs, out_specs, out_shape, scratch = list(in_specs), list(out_specs), list(out_shape), list(scratch)
    if comm is None:
        return pl.pallas_call(body, name=name, grid=grid, in_specs=in_specs, out_specs=out_specs, out_shape=out_shape,
                              scratch_shapes=scratch, compiler_params=_params(*sem))(*args)
    n_in, n_out, n_scr, n_ci, n_co = len(in_specs), len(out_specs), len(scratch), len(comm.ins), len(comm.outs)

    def wrapped(*refs):
        ins, refs = refs[:n_in], refs[n_in:]
        c_ins, refs = refs[:n_ci], refs[n_ci:]
        outs, refs = refs[:n_out], refs[n_out:]
        c_outs, refs = refs[:n_co], refs[n_co:]
        scr, c_sems = refs[:n_scr], refs[n_scr:]
        first = last = None
        for axis, size in enumerate(grid):
            at_start, at_end = pl.program_id(axis) == 0, pl.program_id(axis) == size - 1
            first = at_start if first is None else jnp.logical_and(first, at_start)
            last = at_end if last is None else jnp.logical_and(last, at_end)
        if grid:
            pl.when(first)(lambda: comm.start(c_ins, c_outs, c_sems))
            body(*ins, *outs, *scr)
            pl.when(last)(lambda: comm.finish(c_ins, c_outs, c_sems))
        else:
            comm.start(c_ins, c_outs, c_sems)
            body(*ins, *outs, *scr)
            comm.finish(c_ins, c_outs, c_sems)

    return pl.pallas_call(
        wrapped, name=name, grid=grid, in_specs=in_specs + [ANY] * n_ci, out_specs=out_specs + [ANY] * n_co,
        out_shape=out_shape + comm.outs, scratch_shapes=scratch + comm.sems,
        compiler_params=pltpu.CompilerParams(dimension_semantics=("arbitrary",) * len(grid), vmem_limit_bytes=VMEM_LIMIT,
                                             has_side_effects=True),
    )(*args, *comm.ins)


def _mm(a, b, *, ta=False, tb=False, b_sm=False, out_sm=False, out_dtype=F32, name, tm=2048, tn=1024, tk=1024, comm=None):
    M, K = (a.shape[1], a.shape[0]) if ta else a.shape
    if b_sm:
        n_sl, rows, per = b.shape
        N = rows if tb else n_sl * per
        assert K == (n_sl * per if tb else rows)
    else:
        N = b.shape[0] if tb else b.shape[1]
    tm = _tile(M, tm, 128)
    tn = _tile(per if (b_sm and not tb) else N // N_DEV if out_sm else N, tn, 128)
    tk = _tile(per if (b_sm and tb) else K, tk, 128)
    nk = K // tk
    a_spec = pl.BlockSpec((tk, tm), lambda i, j, k: (k, i)) if ta else pl.BlockSpec((tm, tk), lambda i, j, k: (i, k))
    if not b_sm:
        b_spec = pl.BlockSpec((tn, tk), lambda i, j, k: (j, k)) if tb else pl.BlockSpec((tk, tn), lambda i, j, k: (k, j))
    elif tb:
        kp = per // tk
        b_spec = pl.BlockSpec((None, tn, tk), lambda i, j, k: (k // kp, j, k % kp))
    else:
        jp = per // tn
        b_spec = pl.BlockSpec((None, tk, tn), lambda i, j, k: (j // jp, k, j % jp))
    if out_sm:
        jo = (N // N_DEV) // tn
        o_spec = pl.BlockSpec((None, tm, tn), lambda i, j, k: (j // jo, i, j % jo))
        o_shape = _sds((N_DEV, M, N // N_DEV), out_dtype)
    else:
        o_spec = pl.BlockSpec((tm, tn), lambda i, j, k: (i, j))
        o_shape = _sds((M, N), out_dtype)
    dims = (((0 if ta else 1,), (1 if tb else 0,)), ((), ()))

    def product(a_ref, b_ref):
        return lax.dot_general(a_ref[...].astype(BF16), b_ref[...].astype(BF16), dims, preferred_element_type=F32)

    def body(a_ref, b_ref, o_ref, acc_ref):
        k = pl.program_id(2)

        @pl.when(k == 0)
        def _():
            acc_ref[...] = jnp.zeros_like(acc_ref)

        acc_ref[...] += product(a_ref, b_ref)

        @pl.when(k == nk - 1)
        def _():
            o_ref[...] = acc_ref[...].astype(o_ref.dtype)

    def body_one_step(a_ref, b_ref, o_ref):
        o_ref[...] = product(a_ref, b_ref).astype(o_ref.dtype)

    res = _call(body if nk > 1 else body_one_step, name=name, grid=(M // tm, N // tn, nk), in_specs=[a_spec, b_spec],
                out_specs=[o_spec], out_shape=[o_shape], args=(a, b), scratch=[pltpu.VMEM((tm, tn), F32)] if nk > 1 else [],
                sem=("parallel", "parallel", "arbitrary"), comm=comm)
    return res[0] if comm is None else res


def _pool_group_fwd(pooled, w_pg, scale):
    S = pooled.shape[0]
    G, C, C2 = w_pg.shape
    tm = _tile(S, 1024, 16)

    def body(a_ref, w_ref, s_ref, pg_ref, ps_ref):
        pg = jnp.dot(a_ref[...], w_ref[...], preferred_element_type=F32)
        pg_ref[...] = pg
        ps_ref[...] = (pg * s_ref[...]).astype(BF16)

    return pl.pallas_call(
        body, name="pool_group_fwd", grid=(G, S // tm),
        in_specs=[pl.BlockSpec((tm, C), lambda g, i: (i, g)), pl.BlockSpec((None, C, C2), lambda g, i: (g, 0, 0)),
                  pl.BlockSpec((1, C2), lambda g, i: (0, g))],
        out_specs=[pl.BlockSpec((tm, C2), lambda g, i: (i, g)), pl.BlockSpec((tm, C2), lambda g, i: (i, g))],
        out_shape=[jax.ShapeDtypeStruct((S, G * C2), F32), jax.ShapeDtypeStruct((S, G * C2), BF16)],
        compiler_params=_params("parallel", "parallel"),
    )(pooled, w_pg, scale)


def _pool_group_bwd_x(dpg, w_pg):
    S = dpg.shape[0]
    G, C, C2 = w_pg.shape
    tm = _tile(S, 1024, 16)

    def body(d_ref, w_ref, o_ref):
        o_ref[...] = lax.dot_general(d_ref[...], w_ref[...], (((1,), (1,)), ((), ())), preferred_element_type=F32)

    return pl.pallas_call(
        body, name="pool_group_bwd_x", grid=(G, S // tm),
        in_specs=[pl.BlockSpec((tm, C2), lambda g, i: (i, g)), pl.BlockSpec((None, C, C2), lambda g, i: (g, 0, 0))],
        out_specs=pl.BlockSpec((tm, C), lambda g, i: (i, g)), out_shape=jax.ShapeDtypeStruct((S, G * C), F32),
        compiler_params=_params("parallel", "parallel"),
    )(dpg, w_pg)


def _pool_group_bwd_w(pooled, dpg, G):
    S = pooled.shape[0]
    C, C2 = pooled.shape[1] // G, dpg.shape[1] // G
    tk = _tile(S, 1024, 16)

    def body(a_ref, d_ref, o_ref):
        @pl.when(pl.program_id(1) == 0)
        def _():
            o_ref[...] = jnp.zeros_like(o_ref)

        o_ref[...] += lax.dot_general(a_ref[...], d_ref[...], (((0,), (0,)), ((), ())), preferred_element_type=F32)

    return pl.pallas_call(
        body, name="pool_group_bwd_w", grid=(G, S // tk),
        in_specs=[pl.BlockSpec((tk, C), lambda g, k: (k, g)), pl.BlockSpec((tk, C2), lambda g, k: (k, g))],
        out_specs=pl.BlockSpec((None, C, C2), lambda g, k: (g, 0, 0)), out_shape=jax.ShapeDtypeStruct((G, C, C2), F32),
        compiler_params=_params("parallel", "arbitrary"),
    )(pooled, dpg)


def _rms(x, gain):
    r = lax.rsqrt(jnp.mean(x * x, axis=-1, keepdims=True) + EPS)
    return x * r * gain


def _rms_bwd(x, gain, dy):
    r = lax.rsqrt(jnp.mean(x * x, axis=-1, keepdims=True) + EPS)
    xh = x * r
    dgain = jnp.sum(dy * xh, axis=0, keepdims=True)
    dxh = dy * gain
    dx = r * (dxh - xh * jnp.mean(dxh * xh, axis=-1, keepdims=True))
    return dx, dgain


def _row_call(body, name, ins, outs, tr, *, comm=None):
    S = None
    in_specs, args = [], []
    for it in ins:
        arr, kind = it[0], it[1]
        if kind == "row":
            S = arr.shape[0]
            if len(it) == 4:
                width, cb = it[2], it[3]
                in_specs.append(pl.BlockSpec((tr, width), functools.partial(lambda i, cb: (i, cb), cb=cb)))
            else:
                in_specs.append(pl.BlockSpec((tr, arr.shape[1]), lambda i: (i, 0)))
        else:
            assert arr.ndim == 2
            in_specs.append(pl.BlockSpec(arr.shape, lambda i: (0, 0)))
        args.append(arr)
    out_specs, out_shape = [], []
    for sds, kind in outs:
        if kind == "row":
            out_specs.append(pl.BlockSpec((tr, sds.shape[1]), lambda i: (i, 0)))
        else:
            assert len(sds.shape) == 2
            out_specs.append(pl.BlockSpec(sds.shape, lambda i: (0, 0)))
        out_shape.append(sds)
    return _call(body, name=name, grid=(S // tr,), in_specs=in_specs, out_specs=out_specs, out_shape=out_shape, args=args,
                 sem=("arbitrary",), comm=comm)


def _first_step_zero(*refs):
    @pl.when(pl.program_id(0) == 0)
    def _():
        for r in refs:
            r[...] = jnp.zeros_like(r)


def _rms_fwd(x, gain, tr, comm):
    def body(x_ref, g_ref, o_ref):
        o_ref[...] = _rms(x_ref[...], g_ref[...]).astype(BF16)

    S, D = x.shape
    return _row_call(body, "rms_fwd", [(x, "row"), (gain, "full")], [(_sds((S, D), BF16), "row")], tr, comm=comm)


def _gate_mix(proj, ya, yp, gate_cb, tr):
    S, D = ya.shape

    def body(ga_ref, gp_ref, ya_ref, yp_ref, o_ref):
        o_ref[...] = (_sigmoid(ga_ref[...]) * ya_ref[...] + _sigmoid(gp_ref[...]) * yp_ref[...]).astype(BF16)

    return _row_call(body, "gate_mix", [(proj, "row", D, gate_cb), (proj, "row", D, gate_cb + 1), (ya, "row"), (yp, "row")],
                     [(_sds((S, D), BF16), "row")], tr)[0]


def _resid_rms2(x, mo, g2, g3, tr):
    S, D = x.shape

    def body(x_ref, mo_ref, g2_ref, g3_ref, x1_ref, h2_ref):
        x1 = x_ref[...] + _rms(mo_ref[...], g2_ref[...])
        x1_ref[...] = x1
        h2_ref[...] = _rms(x1, g3_ref[...]).astype(BF16)

    return _row_call(body, "resid_rms2", [(x, "row"), (mo, "row"), (g2, "full"), (g3, "full")],
                     [(_sds((S, D), F32), "row"), (_sds((S, D), BF16), "row")], tr)


def _resid_rms(x1, yf, g4, tr):
    S, D = x1.shape

    def body(x_ref, y_ref, g_ref, o_ref, ob_ref):
        x2 = x_ref[...] + _rms(y_ref[...], g_ref[...])
        o_ref[...] = x2
        ob_ref[...] = x2.astype(BF16)

    return _row_call(body, "resid_rms", [(x1, "row"), (yf, "row"), (g4, "full")],
                     [(_sds((S, D), F32), "row"), (_sds((S, D), BF16), "row")], tr)


def _ple_loss(gl, e, x2, tgt, g5, tr):
    S, D = x2.shape

    def body(gl_ref, e_ref, x2_ref, t_ref, g_ref, loss_ref, dx3_ref, de_ref, dgl_ref, dg_ref):
        _first_step_zero(loss_ref, dg_ref)
        s = _sigmoid(gl_ref[...])
        e_ = e_ref[...]
        t = s * e_
        gain = g_ref[...]
        err = x2_ref[...] + _rms(t, gain) - t_ref[...]
        row_loss = jnp.mean(err * err, axis=-1, keepdims=True)
        loss_ref[...] += 0.5 * jnp.sum(row_loss, axis=0, keepdims=True)
        dx3 = err * (1.0 / D)
        dx3_ref[...] = dx3
        dt, dgain = _rms_bwd(t, gain, dx3)
        dg_ref[...] += dgain
        de_ref[...] = (dt * s).astype(BF16)
        dgl_ref[...] = (dt * e_ * s * (1.0 - s)).astype(BF16)

    return _row_call(body, "ple_loss", [(gl, "row"), (e, "row"), (x2, "row"), (tgt, "row"), (g5, "full")],
                     [(_sds((1, 1), F32), "acc"), (_sds((S, D), F32), "row"), (_sds((S, D), BF16), "row"),
                      (_sds((S, D), BF16), "row"), (_sds((1, D), F32), "acc")], tr)


def _rms_bwd_a(dx3, dx2g, yf, g4, tr):
    S, D = yf.shape

    def body(a_ref, b_ref, y_ref, g_ref, dx_ref, dy_ref, dg_ref):
        _first_step_zero(dg_ref)
        dx2 = a_ref[...] + b_ref[...]
        dx_ref[...] = dx2
        dy, dgain = _rms_bwd(y_ref[...], g_ref[...], dx2)
        dy_ref[...] = dy.astype(BF16)
        dg_ref[...] += dgain

    return _row_call(body, "rms_bwd_a", [(dx3, "row"), (dx2g, "row"), (yf, "row"), (g4, "full")],
                     [(_sds((S, D), F32), "row"), (_sds((S, D), BF16), "row"), (_sds((1, D), F32), "acc")], tr)


def _rms_bwd_b(dx2, dh2, x1, g3, mo, g2, tr):
    S, D = x1.shape

    def body(dx2_ref, dh2_ref, x1_ref, g3_ref, mo_ref, g2_ref, dx1_ref, dmo_ref, dg3_ref, dg2_ref):
        _first_step_zero(dg3_ref, dg2_ref)
        d, dgain3 = _rms_bwd(x1_ref[...], g3_ref[...], dh2_ref[...])
        dx1 = dx2_ref[...] + d
        dx1_ref[...] = dx1
        dg3_ref[...] += dgain3
        dmo, dgain2 = _rms_bwd(mo_ref[...], g2_ref[...], dx1)
        dmo_ref[...] = dmo.astype(BF16)
        dg2_ref[...] += dgain2

    return _row_call(body, "rms_bwd_b", [(dx2, "row"), (dh2, "row"), (x1, "row"), (g3, "full"), (mo, "row"), (g2, "full")],
                     [(_sds((S, D), F32), "row"), (_sds((S, D), BF16), "row"), (_sds((1, D), F32), "acc"),
                      (_sds((1, D), F32), "acc")], tr)


def _rms_bwd_c(dx1, dh, x, g1, tr):
    S, D = x.shape

    def body(dx1_ref, dh_ref, x_ref, g_ref, o_ref, dg_ref):
        _first_step_zero(dg_ref)
        d, dgain = _rms_bwd(x_ref[...], g_ref[...], dh_ref[...])
        o_ref[...] = dx1_ref[...] + d
        dg_ref[...] += dgain

    return _row_call(body, "rms_bwd_c", [(dx1, "row"), (dh, "row"), (x, "row"), (g1, "full")],
                     [(_sds((S, D), F32), "row"), (_sds((1, D), F32), "acc")], tr)


def _gate_bwd(dmixed, proj, ya, yp, gate_cb, tr):
    S, D = ya.shape

    def body(dm_ref, ga_ref, gp_ref, ya_ref, yp_ref, dya_ref, dyp_ref, dga_ref, dgp_ref):
        dm = dm_ref[...]
        sa = _sigmoid(ga_ref[...])
        sp = _sigmoid(gp_ref[...])
        dya_ref[...] = (dm * sa).astype(BF16)
        dyp_ref[...] = (dm * sp).astype(BF16)
        dga_ref[...] = (dm * ya_ref[...] * sa * (1.0 - sa)).astype(BF16)
        dgp_ref[...] = (dm * yp_ref[...] * sp * (1.0 - sp)).astype(BF16)

    return _row_call(body, "gate_bwd",
                     [(dmixed, "row"), (proj, "row", D, gate_cb), (proj, "row", D, gate_cb + 1), (ya, "row"), (yp, "row")],
                     [(_sds((S, D), BF16), "row")] * 4, tr)


def _scale_bwd(dps, pg, scale, tr):
    S, W = dps.shape

    def body(d_ref, pg_ref, s_ref, o_ref, ds_ref):
        _first_step_zero(ds_ref)
        d = d_ref[...]
        o_ref[...] = (d * s_ref[...]).astype(BF16)
        ds_ref[...] += jnp.sum(d * pg_ref[...], axis=0, keepdims=True)

    return _row_call(body, "scale_bwd", [(dps, "row"), (pg, "row"), (scale, "full")],
                     [(_sds((S, W), BF16), "row"), (_sds((1, W), F32), "acc")], tr)


CUMSUM_TERMS = 2


def _tri(blk, cmp):
    j = lax.broadcasted_iota(jnp.int32, (blk, blk), 0)
    s = lax.broadcasted_iota(jnp.int32, (blk, blk), 1)
    one = jnp.concatenate([cmp(j, s).astype(BF16), jnp.ones((blk, 128), BF16)], axis=1)
    return jnp.concatenate([one] * CUMSUM_TERMS, axis=0)


def _split(x):
    terms = []
    for _ in range(CUMSUM_TERMS):
        t = x.astype(BF16)
        terms.append(t)
        x = x - t.astype(F32)
    return terms[0] if CUMSUM_TERMS == 1 else jnp.concatenate(terms, axis=1)


def _split_dot(x, u):
    return jnp.dot(_split(x), u, preferred_element_type=F32)


def _causal(blk):
    return lax.broadcasted_iota(jnp.int32, (blk, blk), 1) < lax.broadcasted_iota(jnp.int32, (blk, blk), 0)


def _scores(q, kj, scale, causal):
    z = lax.dot_general(q, kj, (((1,), (1,)), ((), ())), preferred_element_type=F32) * scale
    l1p = jnp.log(1.0 + jnp.exp(-jnp.abs(z)))
    lb = -(jnp.maximum(z, 0.0) + l1p)
    if causal is not None:
        lb = jnp.where(causal, lb, 0.0)
    return z, lb, jnp.minimum(z, 0.0) - l1p


def _attn_fwd(proj, n_heads, blk, comm=None):
    S = proj.shape[0]
    nq = S // blk
    scale = HEAD_DIM ** -0.5
    lanes = blk // 128
    hp = 2 if n_heads % 2 == 0 else 1
    cols = [slice(h * HEAD_DIM, (h + 1) * HEAD_DIM) for h in range(hp)]
    u_incl = _tri(blk, lambda j, s: j >= s)

    def body(q_ref, k_ref, v_ref, u_ref, o_ref, a_ref, b_ref, z_buf, hl_buf):
        i = pl.program_id(1)
        qs = [q_ref[:, c].astype(BF16) for c in cols]
        u = u_ref[...]

        def scores(j, h, causal):
            kj = k_ref[pl.ds(pl.multiple_of(j * blk, blk), blk), cols[h]].astype(BF16)
            z, lb, log_beta = _scores(qs[h], kj, scale, causal)
            b_ref[h, j] = jnp.exp(log_beta).astype(BF16)
            return z, _split(lb)

        def weigh(j, h, z, hl, acc, run, causal):
            vj = v_ref[pl.ds(pl.multiple_of(j * blk, blk), blk), cols[h]].astype(BF16)
            ct = jnp.dot(hl, u, preferred_element_type=F32)
            a = jnp.exp(z + ct[:, :blk] + jnp.tile(run, (1, lanes)))
            if causal is not None:
                a = jnp.where(causal, a, 0.0)
            a = a.astype(BF16)
            a_ref[h, j] = a
            return acc + jnp.dot(a, vj, preferred_element_type=F32), run + ct[:, blk:]

        def stage_scores(j):
            for h in range(hp):
                z_buf[h], hl_buf[h] = scores(j, h, None)

        zero = jnp.zeros((blk, HEAD_DIM), F32)
        causal = _causal(blk)
        carry = tuple(weigh(i, h, *scores(i, h, causal), zero, zero, causal) for h in range(hp))
        stage_scores(jnp.maximum(i - 1, 0))

        def step(t, carry):
            j = i - 1 - t
            out = tuple(weigh(j, h, z_buf[h], hl_buf[h], *carry[h], None) for h in range(hp))
            stage_scores(jnp.maximum(j - 1, 0))
            return out

        carry = lax.fori_loop(0, i, step, carry)
        for h in range(hp):
            o_ref[:, cols[h]] = carry[h][0].astype(BF16)

    G = n_heads // hp
    W = hp * HEAD_DIM
    saved = _sds((n_heads, nq, nq, blk, blk), BF16)
    saved_spec = pl.BlockSpec((hp, None, nq, blk, blk), lambda h, i: (h, i, 0, 0, 0))
    return _call(
        body, name="attn_fwd", grid=(G, nq),
        in_specs=[pl.BlockSpec((blk, W), lambda h, i: (i, h)),
                  pl.BlockSpec((S, W), lambda h, i: (0, G + h)),
                  pl.BlockSpec((S, W), lambda h, i: (0, 2 * G + h)),
                  pl.BlockSpec(u_incl.shape, lambda h, i: (0, 0))],
        out_specs=[pl.BlockSpec((blk, W), lambda h, i: (i, h)), saved_spec, saved_spec],
        out_shape=[_sds((S, n_heads * HEAD_DIM), BF16), saved, saved],
        args=(proj, proj, proj, u_incl), scratch=[pltpu.VMEM((hp, blk, blk), F32), pltpu.VMEM((hp, blk, CUMSUM_TERMS * blk), BF16)],
        sem=("parallel", "arbitrary"), comm=comm)


def _attn_bwd(proj, a_saved, b_saved, do, n_heads, blk, comm=None):
    S = proj.shape[0]
    nq = S // blk
    scale = HEAD_DIM ** -0.5
    lanes = blk // 128
    hp = 2 if n_heads % 2 == 0 else 1
    cols = [slice(h * HEAD_DIM, (h + 1) * HEAD_DIM) for h in range(hp)]
    l_incl = _tri(blk, lambda j, s: j <= s)

    def body(q_ref, k_ref, v_ref, a_ref, b_ref, do_ref, li_ref, dq_ref, dk_ref, dv_ref, dk_acc, dv_acc, g_buf, gl_buf):
        i = pl.program_id(1)

        @pl.when(i == 0)
        def _():
            dk_acc[...] = jnp.zeros_like(dk_acc)
            dv_acc[...] = jnp.zeros_like(dv_acc)

        qs = [q_ref[:, c].astype(BF16) for c in cols]
        dobs = [do_ref[:, c].astype(BF16) for c in cols]
        li = li_ref[...]

        def stage_products(j):
            ks = pl.multiple_of(j * blk, blk)
            for h in range(hp):
                vj = v_ref[pl.ds(ks, blk), cols[h]].astype(BF16)
                da = lax.dot_general(dobs[h], vj, (((1,), (1,)), ((), ())), preferred_element_type=F32)
                g = a_ref[h, j].astype(F32) * da
                g_buf[h] = g
                gl_buf[h] = _split(g)

        def grads(j, h, dq, run_g, causal):
            ks = pl.multiple_of(j * blk, blk)
            kj = k_ref[pl.ds(ks, blk), cols[h]].astype(BF16)
            gt = jnp.dot(gl_buf[h], li, preferred_element_type=F32)
            dz = g_buf[h] - b_ref[h, j].astype(F32) * (gt[:, :blk] + jnp.tile(run_g, (1, lanes)))
            if causal is not None:
                dz = jnp.where(causal, dz, 0.0)
            dzs = (dz * scale).astype(BF16)
            dk_acc[pl.ds(ks, blk), cols[h]] += lax.dot_general(dzs, qs[h], (((0,), (0,)), ((), ())),
                                                               preferred_element_type=F32)
            dv_acc[pl.ds(ks, blk), cols[h]] += lax.dot_general(a_ref[h, j], dobs[h], (((0,), (0,)), ((), ())),
                                                               preferred_element_type=F32)
            return dq + jnp.dot(dzs, kj, preferred_element_type=F32), run_g + gt[:, blk:]

        zero = jnp.zeros((blk, HEAD_DIM), F32)
        stage_products(0)

        def step(j, carry):
            out = tuple(grads(j, h, *carry[h], None) for h in range(hp))
            stage_products(j + 1)
            return out

        carry = lax.fori_loop(0, i, step, ((zero, zero),) * hp)
        causal = _causal(blk)
        carry = tuple(grads(i, h, *carry[h], causal) for h in range(hp))
        for h in range(hp):
            dq_ref[:, cols[h]] = carry[h][0].astype(BF16)

        @pl.when(i == nq - 1)
        def _():
            dk_ref[...] = dk_acc[...].astype(BF16)
            dv_ref[...] = dv_acc[...].astype(BF16)

    G = n_heads // hp
    W = hp * HEAD_DIM
    AW = n_heads * HEAD_DIM
    saved_spec = pl.BlockSpec((hp, None, nq, blk, blk), lambda h, i: (h, i, 0, 0, 0))
    return _call(
        body, name="attn_bwd", grid=(G, nq),
        in_specs=[pl.BlockSpec((blk, W), lambda h, i: (i, h)),
                  pl.BlockSpec((S, W), lambda h, i: (0, G + h)),
                  pl.BlockSpec((S, W), lambda h, i: (0, 2 * G + h)),
                  saved_spec, saved_spec,
                  pl.BlockSpec((blk, W), lambda h, i: (i, h)),
                  pl.BlockSpec(l_incl.shape, lambda h, i: (0, 0))],
        out_specs=[pl.BlockSpec((blk, W), lambda h, i: (i, h)),
                   pl.BlockSpec((S, W), lambda h, i: (0, h)),
                   pl.BlockSpec((S, W), lambda h, i: (0, h))],
        out_shape=[_sds((S, AW), BF16)] * 3, args=(proj, proj, proj, a_saved, b_saved, do, l_incl),
        scratch=[pltpu.VMEM((S, W), F32)] * 2 + [pltpu.VMEM((hp, blk, blk), F32),
                                                 pltpu.VMEM((hp, blk, CUMSUM_TERMS * blk), BF16)],
        sem=("parallel", "arbitrary"), comm=comm)


def _pool_count(r0, rows, w):
    t = r0 + lax.broadcasted_iota(jnp.int32, (rows, 1), 0)
    return jnp.minimum(t + 1, w).astype(F32)


def _pool_fwd(proj, col_blk, n_groups, width, chunk):
    S = proj.shape[0]
    H = POOL_HALO

    def body(u_ref, o_ref, pad_ref):
        g = pl.program_id(0)
        pad_ref[0:H, :] = jnp.zeros((H, width), F32)
        pad_ref[H:, :] = u_ref[...]
        for gi, w in enumerate(POOL_WINDOWS[:n_groups]):
            @pl.when(g == gi)
            def _(w=w):
                def one(c, _):
                    r0 = pl.multiple_of(c * chunk, chunk)
                    ext = pad_ref[pl.ds(r0, chunk + H), :]
                    s = ext
                    k = 1
                    while k < w:
                        s = s + pltpu.roll(s, k, 0)
                        k *= 2
                    o_ref[pl.ds(r0, chunk), :] = (s[H:] / _pool_count(r0, chunk, w) - ext[H:]).astype(BF16)
                    return 0

                lax.fori_loop(0, S // chunk, one, 0)

    return pl.pallas_call(
        body, name="pool_fwd", grid=(n_groups,),
        in_specs=[pl.BlockSpec((S, width), lambda g: (0, col_blk + g))],
        out_specs=pl.BlockSpec((S, width), lambda g: (0, g)), out_shape=_sds((S, n_groups * width), BF16),
        scratch_shapes=[pltpu.VMEM((S + H, width), F32)], compiler_params=_params("parallel"),
    )(proj)


def _pool_bwd(dpooled, n_groups, chunk):
    S = dpooled.shape[0]
    width = dpooled.shape[1] // n_groups
    H = POOL_HALO

    def body(d_ref, o_ref, pad_ref):
        g = pl.program_id(0)
        pad_ref[S:, :] = jnp.zeros((H, width), F32)
        for gi, w in enumerate(POOL_WINDOWS[:n_groups]):
            @pl.when(g == gi)
            def _(w=w):
                def fill(c, _):
                    r0 = pl.multiple_of(c * chunk, chunk)
                    pad_ref[pl.ds(r0, chunk), :] = d_ref[pl.ds(r0, chunk), :] / _pool_count(r0, chunk, w)
                    return 0

                lax.fori_loop(0, S // chunk, fill, 0)

                def one(c, _):
                    r0 = pl.multiple_of(c * chunk, chunk)
                    s = pad_ref[pl.ds(r0, chunk + H), :]
                    k = 1
                    while k < w:
                        s = s + pltpu.roll(s, chunk + H - k, 0)
                        k *= 2
                    o_ref[pl.ds(r0, chunk), :] = (s[:chunk] - d_ref[pl.ds(r0, chunk), :]).astype(BF16)
                    return 0

                lax.fori_loop(0, S // chunk, one, 0)

    return pl.pallas_call(
        body, name="pool_bwd", grid=(n_groups,),
        in_specs=[pl.BlockSpec((S, width), lambda g: (0, g))],
        out_specs=pl.BlockSpec((S, width), lambda g: (0, g)), out_shape=_sds((S, n_groups * width), BF16),
        scratch_shapes=[pltpu.VMEM((S + H, width), F32)], compiler_params=_params("parallel"),
    )(dpooled)


def _conv3(x_ext, w, b, shifted=None):
    x2, x1 = shifted if shifted is not None else (pltpu.roll(x_ext, 2, 0), pltpu.roll(x_ext, 1, 0))
    return b + x2 * w[0:1, :] + x1 * w[1:2, :] + x_ext * w[2:3, :]


def _gelu_parts(x):
    th = jnp.tanh(GELU_C0 * (x + GELU_C1 * (x * x * x)))
    return th, 0.5 * (1.0 + th)


def _conv_specs(S, F, cb):
    nb = F // cb
    return [pl.BlockSpec((S, cb), lambda j: (0, j)), pl.BlockSpec((S, cb), lambda j: (0, nb + j)),
            pl.BlockSpec((3, cb), lambda j: (0, j)), pl.BlockSpec((3, cb), lambda j: (0, nb + j)),
            pl.BlockSpec((1, cb), lambda j: (0, j)), pl.BlockSpec((1, cb), lambda j: (0, nb + j))]


def _conv_fwd(upre, cw, cb_, chunk):
    S, F2 = upre.shape
    F = F2 // 2
    cb = 128
    H = CONV_HALO

    def body(g_ref, v_ref, wg_ref, wv_ref, bg_ref, bv_ref, o_ref, pg_ref, pv_ref):
        pg_ref[0:H, :] = jnp.zeros((H, cb), F32)
        pv_ref[0:H, :] = jnp.zeros((H, cb), F32)
        pg_ref[H:, :] = g_ref[...]
        pv_ref[H:, :] = v_ref[...]
        wg, wv, bg, bv = wg_ref[...], wv_ref[...], bg_ref[...], bv_ref[...]

        def one(c, _):
            r0 = pl.multiple_of(c * chunk, chunk)
            up_g = _conv3(pg_ref[pl.ds(r0, chunk + H), :], wg, bg)[H:]
            up_v = _conv3(pv_ref[pl.ds(r0, chunk + H), :], wv, bv)[H:]
            _, cdf = _gelu_parts(up_g)
            o_ref[pl.ds(r0, chunk), :] = (up_g * cdf * up_v).astype(BF16)
            return 0

        lax.fori_loop(0, S // chunk, one, 0)

    return pl.pallas_call(
        body, name="conv_fwd", grid=(F // cb,), in_specs=_conv_specs(S, F, cb),
        out_specs=pl.BlockSpec((S, cb), lambda j: (0, j)), out_shape=_sds((S, F), BF16),
        scratch_shapes=[pltpu.VMEM((S + H, cb), F32)] * 2, compiler_params=_params("parallel"),
    )(upre, upre, cw, cw, cb_, cb_)


def _conv_bwd(upre, dact, cw, cb_, chunk, comm=None):
    S, F2 = upre.shape
    F = F2 // 2
    cb = 128
    H = CONV_HALO
    E = chunk + 2 * H

    def body(g_ref, v_ref, wg_ref, wv_ref, bg_ref, bv_ref, d_ref, dg_ref, dv_ref, dwg_ref, dwv_ref, dbg_ref, dbv_ref,
             pg_ref, pv_ref, pd_ref):
        for p, src in ((pg_ref, g_ref), (pv_ref, v_ref), (pd_ref, d_ref)):
            p[0:H, :] = jnp.zeros((H, cb), F32)
            p[H:S + H, :] = src[...]
            p[S + H:, :] = jnp.zeros((H, cb), F32)
        wg, wv, bg, bv = wg_ref[...], wv_ref[...], bg_ref[...], bv_ref[...]

        def taps_bwd(d, w):
            return d * w[2:3, :] + pltpu.roll(d, E - 1, 0) * w[1:2, :] + pltpu.roll(d, E - 2, 0) * w[0:1, :]

        def wsum(d, x, x2, x1):
            dc = d[H:H + chunk]
            return [jnp.sum(dc * x2[H:H + chunk], axis=0, keepdims=True),
                    jnp.sum(dc * x1[H:H + chunk], axis=0, keepdims=True),
                    jnp.sum(dc * x[H:H + chunk], axis=0, keepdims=True),
                    jnp.sum(dc, axis=0, keepdims=True)]

        def one(c, acc):
            r0 = pl.multiple_of(c * chunk, chunk)
            xg = pg_ref[pl.ds(r0, E), :]
            xv = pv_ref[pl.ds(r0, E), :]
            d = pd_ref[pl.ds(r0, E), :]
            sg = (pltpu.roll(xg, 2, 0), pltpu.roll(xg, 1, 0))
            sv = (pltpu.roll(xv, 2, 0), pltpu.roll(xv, 1, 0))
            up_g = _conv3(xg, wg, bg, sg)
            up_v = _conv3(xv, wv, bv, sv)
            th, cdf = _gelu_parts(up_g)
            dgelu = cdf + 0.5 * up_g * (1.0 - th * th) * (GELU_C0 * (1.0 + 3.0 * GELU_C1 * (up_g * up_g)))
            dgate = d * up_v * dgelu
            dval = d * (up_g * cdf)
            dg_ref[pl.ds(r0, chunk), :] = taps_bwd(dgate, wg)[H:H + chunk].astype(BF16)
            dv_ref[pl.ds(r0, chunk), :] = taps_bwd(dval, wv)[H:H + chunk].astype(BF16)
            return tuple(a + b for a, b in zip(acc, wsum(dgate, xg, *sg) + wsum(dval, xv, *sv)))

        zero = jnp.zeros((1, cb), F32)
        acc = lax.fori_loop(0, S // chunk, one, (zero,) * 8)
        dwg_ref[...] = jnp.concatenate(acc[0:3], axis=0)
        dbg_ref[...] = acc[3]
        dwv_ref[...] = jnp.concatenate(acc[4:7], axis=0)
        dbv_ref[...] = acc[7]

    col = lambda rows: pl.BlockSpec((rows, cb), lambda j: (0, j))
    return _call(
        body, name="conv_bwd", grid=(F // cb,), in_specs=_conv_specs(S, F, cb) + [col(S)],
        out_specs=[col(S), col(S), col(3), col(3), col(1), col(1)],
        out_shape=[_sds((S, F), BF16), _sds((S, F), BF16), _sds((3, F), F32), _sds((3, F), F32), _sds((1, F), F32),
                   _sds((1, F), F32)],
        args=(upre, upre, cw, cw, cb_, cb_, dact), scratch=[pltpu.VMEM((S + 2 * H, cb), F32)] * 3, sem=("parallel",),
        comm=comm)


def _position():
    x, y, c = lax.axis_index("x"), lax.axis_index("y"), lax.axis_index("c")
    return x, y, c, 4 * x + 2 * y + c


def _peer(x, y, c, d):
    px = 1 - x if d & 4 else x
    py = 1 - y if d & 2 else y
    pc = 1 - c if d & 1 else c
    return (px, py, pc), 4 * px + 2 * py + pc


def _all_gather_comm(tensors):
    nt = len(tensors)
    outs = [_sds((N_DEV,) + t.shape, t.dtype) for t in tensors]
    sems = [pltpu.SemaphoreType.DMA((7 * nt,)), pltpu.SemaphoreType.DMA((7 * nt,)), pltpu.SemaphoreType.DMA((nt,))]

    def parts(ins, outs_, sem_refs):
        send, recv, loc = sem_refs
        x, y, c, me = _position()
        chips = [(1 - x, y), (x, 1 - y), (1 - x, 1 - y)]

        def copy(t, k, block, to, src=None):
            slot = outs_[t].at[4 * block[0] + 2 * block[1] + block[2]]
            return pltpu.make_async_remote_copy(src_ref=slot if src is None else src, dst_ref=slot,
                                                send_sem=send.at[7 * t + k], recv_sem=recv.at[7 * t + k], device_id=to,
                                                device_id_type=MESH)

        def mine(t):
            return pltpu.make_async_copy(ins[t], outs_[t].at[me], loc.at[t])

        return (x, y, c), (x, y, 1 - c), chips, copy, mine

    def start(ins, outs_, sem_refs):
        me, sibling, chips, copy, mine = parts(ins, outs_, sem_refs)
        for t in range(nt):
            mine(t).start()
            copy(t, 0, me, sibling, src=ins[t]).start()
            for j, chip in enumerate(chips):
                copy(t, 1 + j, me, (*chip, me[2]), src=ins[t]).start()

    def finish(ins, outs_, sem_refs):
        me, sibling, chips, copy, mine = parts(ins, outs_, sem_refs)
        c = me[2]
        for t in range(nt):
            for j, chip in enumerate(chips):
                copy(t, 1 + j, (*chip, c), me).wait_recv()
                copy(t, 4 + j, (*chip, c), sibling).start()
        for t in range(nt):
            copy(t, 0, sibling, me).wait_recv()
            for j, chip in enumerate(chips):
                copy(t, 4 + j, (*chip, 1 - c), me).wait_recv()
        for t in range(nt):
            copy(t, 0, me, sibling, src=ins[t]).wait_send()
            for j, chip in enumerate(chips):
                copy(t, 1 + j, me, (*chip, c), src=ins[t]).wait_send()
                copy(t, 4 + j, (*chip, c), sibling).wait_send()
            mine(t).wait()

    return _Comm(tensors, outs, sems, start, finish)


def _reduce_scatter_comm(tensors):
    nt = len(tensors)
    outs = [_sds(t.shape, t.dtype) for t in tensors]
    sems = [pltpu.SemaphoreType.DMA((7 * nt,)), pltpu.SemaphoreType.DMA((7 * nt,)), pltpu.SemaphoreType.DMA((nt,))]

    def local(ins, outs_, sem_refs, t, me):
        return pltpu.make_async_copy(ins[t].at[me], outs_[t].at[me], sem_refs[2].at[t])

    def remote(ins, outs_, sem_refs, t, d, inbound):
        x, y, c, me = _position()
        peer, peer_idx = _peer(x, y, c, d)
        k = 7 * t + d - 1
        src, dst, to = (ins[t].at[me], outs_[t].at[peer_idx], (x, y, c)) if inbound else (ins[t].at[peer_idx], outs_[t].at[me], peer)
        return pltpu.make_async_remote_copy(src_ref=src, dst_ref=dst, send_sem=sem_refs[0].at[k], recv_sem=sem_refs[1].at[k],
                                            device_id=to, device_id_type=MESH)

    def start(ins, outs_, sem_refs):
        me = _position()[3]
        for t in range(nt):
            local(ins, outs_, sem_refs, t, me).start()
            for d in range(1, N_DEV):
                remote(ins, outs_, sem_refs, t, d, False).start()

    def finish(ins, outs_, sem_refs):
        me = _position()[3]
        for t in range(nt):
            for d in range(1, N_DEV):
                remote(ins, outs_, sem_refs, t, d, True).wait_recv()
        for t in range(nt):
            for d in range(1, N_DEV):
                remote(ins, outs_, sem_refs, t, d, False).wait_send()
            local(ins, outs_, sem_refs, t, me).wait()

    return _Comm(tensors, outs, sems, start, finish)


HBM_SPEC = pl.BlockSpec(memory_space=pltpu.HBM)
SEM_SPEC = pl.BlockSpec(memory_space=pltpu.SEMAPHORE)
DATAFLOW = pltpu.SideEffectType.DATAFLOW_SIDE_EFFECTING


def _scatter_copy(g_ref, land_ref, send_sems, recv_sems, d):
    x, y, c, me = _position()
    peer, peer_idx = _peer(x, y, c, d)
    return pltpu.make_async_remote_copy(src_ref=g_ref.at[peer_idx], dst_ref=land_ref.at[me], send_sem=send_sems.at[d - 1],
                                        recv_sem=recv_sems.at[d - 1], device_id=peer, device_id_type=MESH)


def _reduce_scatter_start(g, name):
    def body(g_ref, land_ref, send_sems, recv_sems, g_thru, land_thru, token):
        for d in range(1, N_DEV):
            _scatter_copy(g_ref, land_ref, send_sems, recv_sems, d).start()
        token[...] = jnp.zeros_like(token)

    return pl.pallas_call(
        body, name=name,
        out_shape=(pltpu.SemaphoreType.DMA((N_DEV - 1,)), pltpu.SemaphoreType.DMA((N_DEV - 1,)), pltpu.HBM(g.shape, g.dtype),
                   pltpu.HBM(g.shape, g.dtype), _sds((8, 128), F32)),
        in_specs=(HBM_SPEC, HBM_SPEC), out_specs=(SEM_SPEC, SEM_SPEC, HBM_SPEC, HBM_SPEC, pl.BlockSpec(memory_space=pltpu.VMEM)),
        input_output_aliases={0: 2, 1: 3}, compiler_params=pltpu.CompilerParams(has_side_effects=DATAFLOW),
    )(pltpu.with_memory_space_constraint(g, pltpu.HBM), pltpu.with_memory_space_constraint(lax.empty(g.shape, g.dtype), pltpu.HBM))


def _reduce_scatter_wait(send_sems, recv_sems, g_thru, land_thru, after, name):
    def body(g_ref, land_ref, send_sems, recv_sems, after_ref, g_out, land_out):
        for d in range(1, N_DEV):
            copy = _scatter_copy(g_ref, land_ref, send_sems, recv_sems, d)
            copy.wait_send()
            copy.wait_recv()

    return pl.pallas_call(
        body, name=name, out_shape=(pltpu.HBM(g_thru.shape, g_thru.dtype), pltpu.HBM(g_thru.shape, g_thru.dtype)),
        in_specs=(HBM_SPEC, HBM_SPEC, SEM_SPEC, SEM_SPEC, ANY), out_specs=(HBM_SPEC, HBM_SPEC), input_output_aliases={0: 0, 1: 1},
        compiler_params=pltpu.CompilerParams(has_side_effects=DATAFLOW),
    )(g_thru, land_thru, send_sems, recv_sems, after)


def _all_reduce_small(part):
    r, W = part.shape

    def body(p_ref, o_ref, g_ref, send_sems, recv_sems):
        x, y, c, me = _position()
        sends = []
        for d in range(1, N_DEV):
            peer, _ = _peer(x, y, c, d)
            cp = pltpu.make_async_remote_copy(src_ref=p_ref, dst_ref=g_ref.at[me], send_sem=send_sems.at[d - 1],
                                              recv_sem=recv_sems.at[d - 1], device_id=peer, device_id_type=MESH)
            cp.start()
            sends.append(cp)
        g_ref[me] = p_ref[...]
        for d in range(1, N_DEV):
            _, peer_idx = _peer(x, y, c, d)
            pltpu.make_async_remote_copy(src_ref=p_ref, dst_ref=g_ref.at[peer_idx], send_sem=send_sems.at[d - 1],
                                         recv_sem=recv_sems.at[d - 1], device_id=(x, y, c), device_id_type=MESH).wait_recv()
        for cp in sends:
            cp.wait_send()
        acc = g_ref[0]
        for i in range(1, N_DEV):
            acc = acc + g_ref[i]
        o_ref[...] = acc

    vmem = pl.BlockSpec(memory_space=pltpu.VMEM)
    return pl.pallas_call(
        body, name="all_reduce_small", in_specs=[vmem], out_specs=[vmem, vmem],
        out_shape=[_sds((r, W), F32), _sds((N_DEV, r, W), F32)],
        scratch_shapes=[pltpu.SemaphoreType.DMA((7,)), pltpu.SemaphoreType.DMA((7,))],
        compiler_params=pltpu.CompilerParams(has_side_effects=True, vmem_limit_bytes=VMEM_LIMIT),
    )(part)[0]


def _adamw_math(w, g, m, v):
    m = ADAM_B1 * m + (1.0 - ADAM_B1) * g
    v = ADAM_B2 * v + (1.0 - ADAM_B2) * (g * g)
    m_hat = m / (1.0 - ADAM_B1 ** ADAM_STEP)
    v_hat = v / (1.0 - ADAM_B2 ** ADAM_STEP)
    return -ADAM_LR * (m_hat / (jnp.sqrt(v_hat) + ADAM_EPS) + ADAM_WD * w), m, v


def _adamw(w, g, m, v, name):
    rows, cols = w.shape
    tr = _tile(rows, max(8, (2**18 // cols) // 8 * 8), 8)

    def body(w_ref, g_ref, m_ref, v_ref, d_ref, nm_ref, nv_ref):
        d_ref[...], nm_ref[...], nv_ref[...] = _adamw_math(w_ref[...], g_ref[...], m_ref[...], v_ref[...])

    spec = pl.BlockSpec((tr, cols), lambda i: (i, 0))
    return pl.pallas_call(
        body, name=name, grid=(rows // tr,), in_specs=[spec] * 4, out_specs=[spec] * 3,
        out_shape=[_sds((rows, cols), F32)] * 3, compiler_params=_params("parallel"),
    )(w, g, m, v)


def _adamw_sum(recv, w, m, v, name):
    n, rows, cols = recv.shape
    tr = _tile(rows, max(16, (2**17 // cols) // 16 * 16), 16)

    def body(r_ref, w_ref, m_ref, v_ref, g_ref, d_ref, nm_ref, nv_ref):
        g = r_ref[0].astype(F32)
        for i in range(1, n):
            g = g + r_ref[i].astype(F32)
        g_ref[...] = g
        d_ref[...], nm_ref[...], nv_ref[...] = _adamw_math(w_ref[...], g, m_ref[...], v_ref[...])

    spec = pl.BlockSpec((tr, cols), lambda i: (i, 0))
    return pl.pallas_call(
        body, name=name, grid=(rows // tr,), in_specs=[pl.BlockSpec((n, tr, cols), lambda i: (0, i, 0))] + [spec] * 3,
        out_specs=[spec] * 4, out_shape=[_sds((rows, cols), F32)] * 4, compiler_params=_params("parallel"),
    )(recv, w, m, v)


COLUMN_CUT = ("w_in", "w_attn_branch", "w_pool_branch", "w_up", "w_ple")
ROW_CUT = ("w_out", "w_down", "w_ple_gate")
REPLICATED = ("norm_mix_pre", "pool_scale", "norm_mix_post", "norm_ffn_pre", "conv_b", "norm_ffn_post", "norm_ple_post")
WEIGHTS = ("norm_mix_pre", "w_in", "w_attn_branch", "w_pool_group", "pool_scale", "w_pool_branch", "w_out", "norm_mix_post",
           "norm_ffn_pre", "w_up", "conv_w", "conv_b", "w_down", "norm_ffn_post", "w_ple", "w_ple_gate", "norm_ple_post")


def _size(shape):
    n = 1
    for s in shape:
        n *= s
    return n


def _pad_rows(flat, row_align):
    n = flat.shape[-1]
    per = PACK_W * row_align
    total = -(-n // per) * per
    return jnp.pad(flat, [(0, total - n)]).reshape(total // PACK_W, PACK_W)


def _natural(shard_major):
    n, r, c = shard_major.shape
    return shard_major.reshape(n * r, c)


def kernel(x, p, norm_mix_pre, w_in, w_attn_branch, w_pool_group, pool_scale, w_pool_branch, w_out, norm_mix_post, norm_ffn_pre, w_up, conv_w, conv_b, w_down, norm_ffn_post, w_ple, w_ple_gate, norm_ple_post, loss_target, m_norm_mix_pre, m_w_in, m_w_attn_branch, m_w_pool_group, m_pool_scale, m_w_pool_branch, m_w_out, m_norm_mix_post, m_norm_ffn_pre, m_w_up, m_conv_w, m_conv_b, m_w_down, m_norm_ffn_post, m_w_ple, m_w_ple_gate, m_norm_ple_post, v_norm_mix_pre, v_w_in, v_w_attn_branch, v_w_pool_group, v_pool_scale, v_w_pool_branch, v_w_out, v_norm_mix_post, v_norm_ffn_pre, v_w_up, v_conv_w, v_conv_b, v_w_down, v_norm_ffn_post, v_w_ple, v_w_ple_gate, v_norm_ple_post):
    given = dict(locals())
    wts = {n: given[n][0] for n in WEIGHTS}
    mom = {n: given["m_" + n][0] for n in WEIGHTS}
    var = {n: given["v_" + n][0] for n in WEIGHTS}
    xs = x[0]
    ps_in = p[0, 0]
    tgt = loss_target[0]
    S, D = xs.shape
    AW = wts["w_attn_branch"].shape[0]
    PW = wts["w_pool_branch"].shape[0]
    G = wts["w_pool_group"].shape[0]
    PGW = PW // G
    H = AW // HEAD_DIM
    F = wts["w_down"].shape[0] * N_DEV
    assert (3 * AW) % PGW == 0 and (3 * AW + PW) % D == 0 and PGW % 128 == 0 and F % 128 == 0
    tr = _tile(S, 256, 16)
    blk = _tile(S, 256, 128)
    chunk = _tile(S, 256, 8)
    me = 4 * lax.axis_index("x") + 2 * lax.axis_index("y") + lax.axis_index("c")

    cw_shape = wts["conv_w"].shape
    conv_b_row = wts["conv_b"].reshape(1, -1)
    g1, g2, g3, g4, g5 = (wts[n].reshape(1, D) for n in
                          ("norm_mix_pre", "norm_mix_post", "norm_ffn_pre", "norm_ffn_post", "norm_ple_post"))
    pscale = wts["pool_scale"].reshape(1, PW)
    big = 4096
    wb = {n: wts[n].astype(BF16) for n in COLUMN_CUT + ROW_CUT}
    wb["w_pool_group"] = wts["w_pool_group"].astype(BF16).reshape(G * PGW // N_DEV, PGW)

    h, w_in = _rms_fwd(xs, g1, tr, _all_gather_comm([wb["w_in"]]))
    proj, w_ab, w_pg, w_pb, w_out = _mm(
        h, w_in, b_sm=True, name="mm_in", tk=2048,
        comm=_all_gather_comm([wb["w_attn_branch"], wb["w_pool_group"], wb["w_pool_branch"], wb["w_out"]]))
    w_pg = jnp.moveaxis(w_pg.reshape(N_DEV, G, PGW // N_DEV, PGW), 0, 1).reshape(G, PGW, PGW)
    w_out = _natural(w_out)
    attn, a_saved, b_saved, w_up, conv_w_all = _attn_fwd(proj, H, blk, _all_gather_comm([wb["w_up"], wts["conv_w"]]))
    conv_w_full = jnp.moveaxis(conv_w_all, 0, 1).reshape(cw_shape[0], N_DEV * cw_shape[1])
    y_attn = _mm(attn, w_ab, b_sm=True, name="mm_attn_branch", tm=big, tn=256, tk=big)
    pooled = _pool_fwd(proj, 3 * AW // PGW, G, PGW, chunk)
    pg, ps = _pool_group_fwd(pooled, w_pg, pscale)
    y_pool = _mm(ps, w_pb, b_sm=True, name="mm_pool_branch", tm=big, tn=256, tk=big)
    gate_cb = (3 * AW + PW) // D
    mixed = _gate_mix(proj, y_attn, y_pool, gate_cb, tr)
    mo = _mm(mixed, w_out, name="mm_out", tk=2048)
    x1, h2 = _resid_rms2(xs, mo, g2, g3, tr)
    upre, w_down, w_ple, w_pleg = _mm(h2, w_up, b_sm=True, name="mm_up", tn=2048, tk=1024,
                                      comm=_all_gather_comm([wb["w_down"], wb["w_ple"], wb["w_ple_gate"]]))
    w_down, w_pleg = _natural(w_down), _natural(w_pleg)
    act = _conv_fwd(upre, conv_w_full, conv_b_row, chunk)
    yf = _mm(act, w_down, name="mm_down", tk=1408)
    x2, x2b = _resid_rms(x1, yf, g4, tr)
    e = _mm(ps_in, w_ple, b_sm=True, name="mm_ple", tm=big, tn=256, tk=big)
    gl = _mm(x2b, w_pleg, name="mm_ple_gate", tk=2048)
    loss_part, dx3, de, dgl, dg5 = _ple_loss(gl, e, x2, tgt, g5, tr)

    shards = lambda natural: natural.reshape((N_DEV, natural.shape[0] // N_DEV) + natural.shape[1:])
    recv = {}
    dw_ple = _mm(ps_in, de, ta=True, out_sm=True, out_dtype=BF16, name="mm_d_w_ple", tm=256, tn=256, tk=big)
    dw_pleg = shards(_mm(x2b, dgl, ta=True, out_dtype=BF16, name="mm_d_w_ple_gate"))
    dx2g = _mm(dgl, w_pleg, tb=True, name="mm_d_x2", tk=2048)
    dx2, dyf, dg4 = _rms_bwd_a(dx3, dx2g, yf, g4, tr)
    dw_down = shards(_mm(act, dyf, ta=True, out_dtype=BF16, name="mm_d_w_down"))
    dact, recv["w_ple"], recv["w_ple_gate"] = _mm(dyf, w_down, tb=True, name="mm_d_act", tn=1408, tk=1024,
                                                  comm=_reduce_scatter_comm([dw_ple, dw_pleg]))
    dup_g, dup_v, dcw_g, dcw_v, dcb_g, dcb_v, recv["w_down"] = _conv_bwd(upre, dact, conv_w_full, conv_b_row, chunk,
                                                                          _reduce_scatter_comm([dw_down]))
    dupre = jnp.concatenate([dup_g, dup_v], axis=1)
    dw_up = _mm(h2, dupre, ta=True, out_sm=True, out_dtype=BF16, name="mm_d_w_up", tn=2048)
    dh2 = _mm(dupre, w_up, tb=True, b_sm=True, name="mm_d_h2", tk=2048)
    dx1, dmo, dg3, dg2 = _rms_bwd_b(dx2, dh2, x1, g3, mo, g2, tr)
    dmixed = _mm(dmo, w_out, tb=True, name="mm_d_mixed", tk=2048)
    dw_out = shards(_mm(mixed, dmo, ta=True, out_dtype=BF16, name="mm_d_w_out"))
    dya, dyp, dga, dgp = _gate_bwd(dmixed, proj, y_attn, y_pool, gate_cb, tr)
    dps = _mm(dyp, w_pb, tb=True, b_sm=True, name="mm_d_ps", tm=2048, tk=256)
    dw_pb = _mm(ps, dyp, ta=True, out_sm=True, out_dtype=BF16, name="mm_d_w_pool_branch", tn=256, tk=big)
    dpg, dscale = _scale_bwd(dps, pg, pscale, tr)
    dpooled = _pool_group_bwd_x(dpg, w_pg)
    dw_pg = _pool_group_bwd_w(pooled, dpg, G)
    dw_pg = jnp.moveaxis(dw_pg.astype(BF16).reshape(G, N_DEV, PGW // N_DEV, PGW), 1, 0).reshape(N_DEV, G * PGW // N_DEV, PGW)
    du = _pool_bwd(dpooled, G, chunk)
    dattn = _mm(dya, w_ab, tb=True, b_sm=True, name="mm_d_attn", tm=2048, tk=256)
    dw_ab = _mm(attn, dya, ta=True, out_sm=True, out_dtype=BF16, name="mm_d_w_attn_branch", tn=256, tk=big)
    dq, dk, dv, recv["w_up"], recv["w_out"], recv["w_pool_branch"], recv["w_pool_group"], recv["w_attn_branch"] = _attn_bwd(
        proj, a_saved, b_saved, dattn, H, blk, _reduce_scatter_comm([dw_up, dw_out, dw_pb, dw_pg, dw_ab]))
    dproj = jnp.concatenate([dq, dk, dv, du, dga, dgp], axis=1)
    dw_in = _mm(h, dproj, ta=True, out_sm=True, out_dtype=BF16, name="mm_d_w_in")
    in_send, in_recv, dw_in, in_land, token = _reduce_scatter_start(dw_in, "rs_w_in_start")
    dproj, _ = lax.optimization_barrier((dproj, token))
    dh = _mm(dproj, w_in, tb=True, b_sm=True, name="mm_d_h", tk=1024)
    grad_x, dg1 = _rms_bwd_c(dx1, dh, xs, g1, tr)

    gshard, delta, new_m, new_v = {}, {}, {}, {}

    def adamw_cut(n):
        shp = wts[n].shape
        two_d = (_size(shp[:-1]), shp[-1])
        g_, d_, m_, v_ = _adamw_sum(recv[n].reshape((N_DEV,) + two_d), wts[n].reshape(two_d), mom[n].reshape(two_d),
                                    var[n].reshape(two_d), "adamw_" + n)
        gshard[n], delta[n], new_m[n], new_v[n] = g_.reshape(shp), d_.reshape(shp), m_.reshape(shp), v_.reshape(shp)

    for n in COLUMN_CUT[1:] + ROW_CUT + ("w_pool_group",):
        adamw_cut(n)

    assert COLUMN_CUT[0] == "w_in"
    done = lax.optimization_barrier((grad_x,) + tuple(delta[n] for n in delta))
    grad_x = done[0]
    for n, d_ in zip(list(delta), done[1:]):
        delta[n] = d_
    dw_in, in_land = _reduce_scatter_wait(in_send, in_recv, dw_in, in_land, grad_x, "rs_w_in_wait")

    dconv_w = jnp.concatenate([dcw_g, dcw_v], axis=1)
    dconv_b = jnp.concatenate([dcb_g, dcb_v], axis=1).reshape(-1)
    rep_parts = {"norm_mix_pre": dg1, "pool_scale": dscale, "norm_mix_post": dg2, "norm_ffn_pre": dg3, "conv_b": dconv_b,
                 "norm_ffn_post": dg4, "norm_ple_post": dg5}
    small = jnp.concatenate([rep_parts[n].reshape(-1) for n in REPLICATED] + [dconv_w.reshape(-1)])
    n_small = small.shape[0]
    small, in_land = lax.optimization_barrier((small, in_land))
    small_sum = _all_reduce_small(_pad_rows(small, 8)).reshape(-1)[:n_small]
    off = 0
    for n in REPLICATED:
        sz = _size(wts[n].shape)
        gshard[n] = small_sum[off:off + sz].reshape(wts[n].shape)
        off += sz
    dconv_w_sum = small_sum[off:off + 3 * 2 * F].reshape(3, 2 * F)
    gshard["conv_w"] = lax.dynamic_slice_in_dim(dconv_w_sum, me * cw_shape[1], cw_shape[1], axis=1)

    delta["conv_w"], new_m["conv_w"], new_v["conv_w"] = _adamw(wts["conv_w"], gshard["conv_w"], mom["conv_w"], var["conv_w"],
                                                               "adamw_conv_w")
    rep_sizes = [_size(wts[n].shape) for n in REPLICATED]
    n_rep = sum(rep_sizes)
    cat = lambda t: _pad_rows(jnp.concatenate([t[n].reshape(-1) for n in REPLICATED]), 8)
    d_, m_, v_ = _adamw(cat(wts), cat(gshard), cat(mom), cat(var), "adamw_replicated")
    off = 0
    for n, sz in zip(REPLICATED, rep_sizes):
        shp = wts[n].shape
        delta[n], new_m[n], new_v[n] = (t.reshape(-1)[off:off + sz].reshape(shp) for t in (d_, m_, v_))
        off += sz
    assert off == n_rep

    own = lax.dynamic_index_in_dim(dw_in, me, 0, keepdims=True)
    recv["w_in"] = lax.dynamic_update_slice_in_dim(in_land, own, me, 0)
    adamw_cut("w_in")

    loss = lax.psum(loss_part[0, 0], ("x", "y", "c"))
    lead = lambda t: t[None]
    return (loss, grad_x[None], *[lead(gshard[n]) for n in WEIGHTS], *[lead(delta[n]) for n in WEIGHTS],
            *[lead(new_m[n]) for n in WEIGHTS], *[lead(new_v[n]) for n in WEIGHTS])
```

```python
import functools

import jax
import jax.numpy as jnp
from jax import lax
from jax.experimental import pallas as pl
from jax.experimental.pallas import tpu as pltpu

F32 = jnp.float32
BF16 = jnp.bfloat16
MESH = pl.DeviceIdType.MESH

EPS = 1e-6
HEAD_DIM = 128
POOL_WINDOWS = (2, 4, 8, 16)
POOL_HALO = 16
CONV_HALO = 8
GELU_C0 = 0.7978845608028654
GELU_C1 = 0.044715
ADAM_LR = 0.001
ADAM_B1 = 0.9
ADAM_B2 = 0.999
ADAM_EPS = 1e-08
ADAM_WD = 0.01
ADAM_STEP = 10
N_DEV = 8
PACK_W = 1024
VMEM_LIMIT = 56 * 2**20
ANY = pl.BlockSpec(memory_space=pl.ANY)


def _params(*sem):
    return pltpu.CompilerParams(dimension_semantics=sem, vmem_limit_bytes=VMEM_LIMIT)


def _sds(shape, dtype):
    return jax.ShapeDtypeStruct(shape, dtype)


def _tile(dim, target, align):
    if dim <= target:
        return dim
    t = (target // align) * align
    while t >= align:
        if dim % t == 0:
            return t
        t -= align
    return dim


def _sigmoid(x):
    return 1.0 / (1.0 + jnp.exp(-x))


class _Comm:
    def __init__(self, ins, outs, sems, start, finish):
        self.ins, self.outs, self.sems, self.start, self.finish = list(ins), list(outs), list(sems), start, finish


def _call(body, *, name, grid, in_specs, out_specs, out_shape, args, scratch=(), sem=(), comm=None):
    in_specs, out_specs, out_shape, scratch = list(in_specs), list(out_specs), list(out_shape), list(scratch)
    if comm is None:
        return pl.pallas_call(body, name=name, grid=grid, in_specs=in_specs, out_specs=out_specs, out_shape=out_shape,
                              scratch_shapes=scratch, compiler_params=_params(*sem))(*args)
    n_in, n_out, n_scr, n_ci, n_co = len(in_specs), len(out_specs), len(scratch), len(comm.ins), len(comm.outs)

    def wrapped(*refs):
        ins, refs = refs[:n_in], refs[n_in:]
        c_ins, refs = refs[:n_ci], refs[n_ci:]
        outs, refs = refs[:n_out], refs[n_out:]
        c_outs, refs = refs[:n_co], refs[n_co:]
        scr, c_sems = refs[:n_scr], refs[n_scr:]
        first = last = None
        for axis, size in enumerate(grid):
            at_start, at_end = pl.program_id(axis) == 0, pl.program_id(axis) == size - 1
            first = at_start if first is None else jnp.logical_and(first, at_start)
            last = at_end if last is None else jnp.logical_and(last, at_end)
        if grid:
            pl.when(first)(lambda: comm.start(c_ins, c_outs, c_sems))
            body(*ins, *outs, *scr)
            pl.when(last)(lambda: comm.finish(c_ins, c_outs, c_sems))
        else:
            comm.start(c_ins, c_outs, c_sems)
            body(*ins, *outs, *scr)
            comm.finish(c_ins, c_outs, c_sems)

    return pl.pallas_call(
        wrapped, name=name, grid=grid, in_specs=in_specs + [ANY] * n_ci, out_specs=out_specs + [ANY] * n_co,
        out_shape=out_shape + comm.outs, scratch_shapes=scratch + comm.sems,
        compiler_params=pltpu.CompilerParams(dimension_semantics=("arbitrary",) * len(grid), vmem_limit_bytes=VMEM_LIMIT,
                                             has_side_effects=True),
    )(*args, *comm.ins)


def _mm(a, b, *, ta=False, tb=False, b_sm=False, out_sm=False, out_dtype=F32, name, tm=2048, tn=1024, tk=1024, comm=None):
    M, K = (a.shape[1], a.shape[0]) if ta else a.shape
    if b_sm:
        n_sl, rows, per = b.shape
        N = rows if tb else n_sl * per
        assert K == (n_sl * per if tb else rows)
    else:
        N = b.shape[0] if tb else b.shape[1]
    tm = _tile(M, tm, 128)
    tn = _tile(per if (b_sm and not tb) else N // N_DEV if out_sm else N, tn, 128)
    tk = _tile(per if (b_sm and tb) else K, tk, 128)
    nk = K // tk
    a_spec = pl.BlockSpec((tk, tm), lambda i, j, k: (k, i)) if ta else pl.BlockSpec((tm, tk), lambda i, j, k: (i, k))
    if not b_sm:
        b_spec = pl.BlockSpec((tn, tk), lambda i, j, k: (j, k)) if tb else pl.BlockSpec((tk, tn), lambda i, j, k: (k, j))
    elif tb:
        kp = per // tk
        b_spec = pl.BlockSpec((None, tn, tk), lambda i, j, k: (k // kp, j, k % kp))
    else:
        jp = per // tn
        b_spec = pl.BlockSpec((None, tk, tn), lambda i, j, k: (j // jp, k, j % jp))
    if out_sm:
        jo = (N // N_DEV) // tn
        o_spec = pl.BlockSpec((None, tm, tn), lambda i, j, k: (j // jo, i, j % jo))
        o_shape = _sds((N_DEV, M, N // N_DEV), out_dtype)
    else:
        o_spec = pl.BlockSpec((tm, tn), lambda i, j, k: (i, j))
        o_shape = _sds((M, N), out_dtype)
    dims = (((0 if ta else 1,), (1 if tb else 0,)), ((), ()))

    def product(a_ref, b_ref):
        return lax.dot_general(a_ref[...].astype(BF16), b_ref[...].astype(BF16), dims, preferred_element_type=F32)

    def body(a_ref, b_ref, o_ref, acc_ref):
        k = pl.program_id(2)

        @pl.when(k == 0)
        def _():
            acc_ref[...] = jnp.zeros_like(acc_ref)

        acc_ref[...] += product(a_ref, b_ref)

        @pl.when(k == nk - 1)
        def _():
            o_ref[...] = acc_ref[...].astype(o_ref.dtype)

    def body_one_step(a_ref, b_ref, o_ref):
        o_ref[...] = product(a_ref, b_ref).astype(o_ref.dtype)

    res = _call(body if nk > 1 else body_one_step, name=name, grid=(M // tm, N // tn, nk), in_specs=[a_spec, b_spec],
                out_specs=[o_spec], out_shape=[o_shape], args=(a, b), scratch=[pltpu.VMEM((tm, tn), F32)] if nk > 1 else [],
                sem=("parallel", "parallel", "arbitrary"), comm=comm)
    return res[0] if comm is None else res


def _pool_group_fwd(pooled, w_pg, scale):
    S = pooled.shape[0]
    G, C, C2 = w_pg.shape
    tm = _tile(S, 1024, 16)

    def body(a_ref, w_ref, s_ref, pg_ref, ps_ref):
        pg = jnp.dot(a_ref[...], w_ref[...], preferred_element_type=F32)
        pg_ref[...] = pg
        ps_ref[...] = (pg * s_ref[...]).astype(BF16)

    return pl.pallas_call(
        body, name="pool_group_fwd", grid=(G, S // tm),
        in_specs=[pl.BlockSpec((tm, C), lambda g, i: (i, g)), pl.BlockSpec((None, C, C2), lambda g, i: (g, 0, 0)),
                  pl.BlockSpec((1, C2), lambda g, i: (0, g))],
        out_specs=[pl.BlockSpec((tm, C2), lambda g, i: (i, g)), pl.BlockSpec((tm, C2), lambda g, i: (i, g))],
        out_shape=[jax.ShapeDtypeStruct((S, G * C2), F32), jax.ShapeDtypeStruct((S, G * C2), BF16)],
        compiler_params=_params("parallel", "parallel"),
    )(pooled, w_pg, scale)


def _pool_group_bwd_x(dpg, w_pg):
    S = dpg.shape[0]
    G, C, C2 = w_pg.shape
    tm = _tile(S, 1024, 16)

    def body(d_ref, w_ref, o_ref):
        o_ref[...] = lax.dot_general(d_ref[...], w_ref[...], (((1,), (1,)), ((), ())), preferred_element_type=F32)

    return pl.pallas_call(
        body, name="pool_group_bwd_x", grid=(G, S // tm),
        in_specs=[pl.BlockSpec((tm, C2), lambda g, i: (i, g)), pl.BlockSpec((None, C, C2), lambda g, i: (g, 0, 0))],
        out_specs=pl.BlockSpec((tm, C), lambda g, i: (i, g)), out_shape=jax.ShapeDtypeStruct((S, G * C), F32),
        compiler_params=_params("parallel", "parallel"),
    )(dpg, w_pg)


def _pool_group_bwd_w(pooled, dpg, G):
    S = pooled.shape[0]
    C, C2 = pooled.shape[1] // G, dpg.shape[1] // G
    tk = _tile(S, 1024, 16)

    def body(a_ref, d_ref, o_ref):
        @pl.when(pl.program_id(1) == 0)
        def _():
            o_ref[...] = jnp.zeros_like(o_ref)

        o_ref[...] += lax.dot_general(a_ref[...], d_ref[...], (((0,), (0,)), ((), ())), preferred_element_type=F32)

    return pl.pallas_call(
        body, name="pool_group_bwd_w", grid=(G, S // tk),
        in_specs=[pl.BlockSpec((tk, C), lambda g, k: (k, g)), pl.BlockSpec((tk, C2), lambda g, k: (k, g))],
        out_specs=pl.BlockSpec((None, C, C2), lambda g, k: (g, 0, 0)), out_shape=jax.ShapeDtypeStruct((G, C, C2), F32),
        compiler_params=_params("parallel", "arbitrary"),
    )(pooled, dpg)


def _rms(x, gain):
    r = lax.rsqrt(jnp.mean(x * x, axis=-1, keepdims=True) + EPS)
    return x * r * gain


def _rms_bwd(x, gain, dy):
    r = lax.rsqrt(jnp.mean(x * x, axis=-1, keepdims=True) + EPS)
    xh = x * r
    dgain = jnp.sum(dy * xh, axis=0, keepdims=True)
    dxh = dy * gain
    dx = r * (dxh - xh * jnp.mean(dxh * xh, axis=-1, keepdims=True))
    return dx, dgain


def _row_call(body, name, ins, outs, tr, *, comm=None):
    S = None
    in_specs, args = [], []
    for it in ins:
        arr, kind = it[0], it[1]
        if kind == "row":
            S = arr.shape[0]
            if len(it) == 4:
                width, cb = it[2], it[3]
                in_specs.append(pl.BlockSpec((tr, width), functools.partial(lambda i, cb: (i, cb), cb=cb)))
            else:
                in_specs.append(pl.BlockSpec((tr, arr.shape[1]), lambda i: (i, 0)))
        else:
            assert arr.ndim == 2
            in_specs.append(pl.BlockSpec(arr.shape, lambda i: (0, 0)))
        args.append(arr)
    out_specs, out_shape = [], []
    for sds, kind in outs:
        if kind == "row":
            out_specs.append(pl.BlockSpec((tr, sds.shape[1]), lambda i: (i, 0)))
        else:
            assert len(sds.shape) == 2
            out_specs.append(pl.BlockSpec(sds.shape, lambda i: (0, 0)))
        out_shape.append(sds)
    return _call(body, name=name, grid=(S // tr,), in_specs=in_specs, out_specs=out_specs, out_shape=out_shape, args=args,
                 sem=("arbitrary",), comm=comm)


def _first_step_zero(*refs):
    @pl.when(pl.program_id(0) == 0)
    def _():
        for r in refs:
            r[...] = jnp.zeros_like(r)


def _rms_fwd(x, gain, tr, comm):
    def body(x_ref, g_ref, o_ref):
        o_ref[...] = _rms(x_ref[...], g_ref[...]).astype(BF16)

    S, D = x.shape
    return _row_call(body, "rms_fwd", [(x, "row"), (gain, "full")], [(_sds((S, D), BF16), "row")], tr, comm=comm)


def _gate_mix(proj, ya, yp, gate_cb, tr):
    S, D = ya.shape

    def body(ga_ref, gp_ref, ya_ref, yp_ref, o_ref):
        o_ref[...] = (_sigmoid(ga_ref[...]) * ya_ref[...] + _sigmoid(gp_ref[...]) * yp_ref[...]).astype(BF16)

    return _row_call(body, "gate_mix", [(proj, "row", D, gate_cb), (proj, "row", D, gate_cb + 1), (ya, "row"), (yp, "row")],
                     [(_sds((S, D), BF16), "row")], tr)[0]


def _resid_rms2(x, mo, g2, g3, tr):
    S, D = x.shape

    def body(x_ref, mo_ref, g2_ref, g3_ref, x1_ref, h2_ref):
        x1 = x_ref[...] + _rms(mo_ref[...], g2_ref[...])
        x1_ref[...] = x1
        h2_ref[...] = _rms(x1, g3_ref[...]).astype(BF16)

    return _row_call(body, "resid_rms2", [(x, "row"), (mo, "row"), (g2, "full"), (g3, "full")],
                     [(_sds((S, D), F32), "row"), (_sds((S, D), BF16), "row")], tr)


def _resid_rms(x1, yf, g4, tr):
    S, D = x1.shape

    def body(x_ref, y_ref, g_ref, o_ref, ob_ref):
        x2 = x_ref[...] + _rms(y_ref[...], g_ref[...])
        o_ref[...] = x2
        ob_ref[...] = x2.astype(BF16)

    return _row_call(body, "resid_rms", [(x1, "row"), (yf, "row"), (g4, "full")],
                     [(_sds((S, D), F32), "row"), (_sds((S, D), BF16), "row")], tr)


def _ple_loss(gl, e, x2, tgt, g5, tr):
    S, D = x2.shape

    def body(gl_ref, e_ref, x2_ref, t_ref, g_ref, loss_ref, dx3_ref, de_ref, dgl_ref, dg_ref):
        _first_step_zero(loss_ref, dg_ref)
        s = _sigmoid(gl_ref[...])
        e_ = e_ref[...]
        t = s * e_
        gain = g_ref[...]
        err = x2_ref[...] + _rms(t, gain) - t_ref[...]
        row_loss = jnp.mean(err * err, axis=-1, keepdims=True)
        loss_ref[...] += 0.5 * jnp.sum(row_loss, axis=0, keepdims=True)
        dx3 = err * (1.0 / D)
        dx3_ref[...] = dx3
        dt, dgain = _rms_bwd(t, gain, dx3)
        dg_ref[...] += dgain
        de_ref[...] = (dt * s).astype(BF16)
        dgl_ref[...] = (dt * e_ * s * (1.0 - s)).astype(BF16)

    return _row_call(body, "ple_loss", [(gl, "row"), (e, "row"), (x2, "row"), (tgt, "row"), (g5, "full")],
                     [(_sds((1, 1), F32), "acc"), (_sds((S, D), F32), "row"), (_sds((S, D), BF16), "row"),
                      (_sds((S, D), BF16), "row"), (_sds((1, D), F32), "acc")], tr)


def _rms_bwd_a(dx3, dx2g, yf, g4, tr):
    S, D = yf.shape

    def body(a_ref, b_ref, y_ref, g_ref, dx_ref, dy_ref, dg_ref):
        _first_step_zero(dg_ref)
        dx2 = a_ref[...] + b_ref[...]
        dx_ref[...] = dx2
        dy, dgain = _rms_bwd(y_ref[...], g_ref[...], dx2)
        dy_ref[...] = dy.astype(BF16)
        dg_ref[...] += dgain

    return _row_call(body, "rms_bwd_a", [(dx3, "row"), (dx2g, "row"), (yf, "row"), (g4, "full")],
                     [(_sds((S, D), F32), "row"), (_sds((S, D), BF16), "row"), (_sds((1, D), F32), "acc")], tr)


def _rms_bwd_b(dx2, dh2, x1, g3, mo, g2, tr):
    S, D = x1.shape

    def body(dx2_ref, dh2_ref, x1_ref, g3_ref, mo_ref, g2_ref, dx1_ref, dmo_ref, dg3_ref, dg2_ref):
        _first_step_zero(dg3_ref, dg2_ref)
        d, dgain3 = _rms_bwd(x1_ref[...], g3_ref[...], dh2_ref[...])
        dx1 = dx2_ref[...] + d
        dx1_ref[...] = dx1
        dg3_ref[...] += dgain3
        dmo, dgain2 = _rms_bwd(mo_ref[...], g2_ref[...], dx1)
        dmo_ref[...] = dmo.astype(BF16)
        dg2_ref[...] += dgain2

    return _row_call(body, "rms_bwd_b", [(dx2, "row"), (dh2, "row"), (x1, "row"), (g3, "full"), (mo, "row"), (g2, "full")],
                     [(_sds((S, D), F32), "row"), (_sds((S, D), BF16), "row"), (_sds((1, D), F32), "acc"),
                      (_sds((1, D), F32), "acc")], tr)


def _rms_bwd_c(dx1, dh, x, g1, tr):
    S, D = x.shape

    def body(dx1_ref, dh_ref, x_ref, g_ref, o_ref, dg_ref):
        _first_step_zero(dg_ref)
        d, dgain = _rms_bwd(x_ref[...], g_ref[...], dh_ref[...])
        o_ref[...] = dx1_ref[...] + d
        dg_ref[...] += dgain

    return _row_call(body, "rms_bwd_c", [(dx1, "row"), (dh, "row"), (x, "row"), (g1, "full")],
                     [(_sds((S, D), F32), "row"), (_sds((1, D), F32), "acc")], tr)


def _gate_bwd(dmixed, proj, ya, yp, gate_cb, tr):
    S, D = ya.shape

    def body(dm_ref, ga_ref, gp_ref, ya_ref, yp_ref, dya_ref, dyp_ref, dga_ref, dgp_ref):
        dm = dm_ref[...]
        sa = _sigmoid(ga_ref[...])
        sp = _sigmoid(gp_ref[...])
        dya_ref[...] = (dm * sa).astype(BF16)
        dyp_ref[...] = (dm * sp).astype(BF16)
        dga_ref[...] = (dm * ya_ref[...] * sa * (1.0 - sa)).astype(BF16)
        dgp_ref[...] = (dm * yp_ref[...] * sp * (1.0 - sp)).astype(BF16)

    return _row_call(body, "gate_bwd",
                     [(dmixed, "row"), (proj, "row", D, gate_cb), (proj, "row", D, gate_cb + 1), (ya, "row"), (yp, "row")],
                     [(_sds((S, D), BF16), "row")] * 4, tr)


def _scale_bwd(dps, pg, scale, tr):
    S, W = dps.shape

    def body(d_ref, pg_ref, s_ref, o_ref, ds_ref):
        _first_step_zero(ds_ref)
        d = d_ref[...]
        o_ref[...] = (d * s_ref[...]).astype(BF16)
        ds_ref[...] += jnp.sum(d * pg_ref[...], axis=0, keepdims=True)

    return _row_call(body, "scale_bwd", [(dps, "row"), (pg, "row"), (scale, "full")],
                     [(_sds((S, W), BF16), "row"), (_sds((1, W), F32), "acc")], tr)


CUMSUM_TERMS = 2


def _tri(blk, cmp):
    j = lax.broadcasted_iota(jnp.int32, (blk, blk), 0)
    s = lax.broadcasted_iota(jnp.int32, (blk, blk), 1)
    one = jnp.concatenate([cmp(j, s).astype(BF16), jnp.ones((blk, 128), BF16)], axis=1)
    return jnp.concatenate([one] * CUMSUM_TERMS, axis=0)


def _split(x):
    terms = []
    for _ in range(CUMSUM_TERMS):
        t = x.astype(BF16)
        terms.append(t)
        x = x - t.astype(F32)
    return terms[0] if CUMSUM_TERMS == 1 else jnp.concatenate(terms, axis=1)


def _split_dot(x, u):
    return jnp.dot(_split(x), u, preferred_element_type=F32)


def _causal(blk):
    return lax.broadcasted_iota(jnp.int32, (blk, blk), 1) < lax.broadcasted_iota(jnp.int32, (blk, blk), 0)


def _scores(q, kj, scale, causal):
    z = lax.dot_general(q, kj, (((1,), (1,)), ((), ())), preferred_element_type=F32) * scale
    l1p = jnp.log(1.0 + jnp.exp(-jnp.abs(z)))
    lb = -(jnp.maximum(z, 0.0) + l1p)
    if causal is not None:
        lb = jnp.where(causal, lb, 0.0)
    return z, lb, jnp.minimum(z, 0.0) - l1p


def _attn_fwd(proj, n_heads, blk, comm=None):
    S = proj.shape[0]
    nq = S // blk
    scale = HEAD_DIM ** -0.5
    lanes = blk // 128
    hp = 2 if n_heads % 2 == 0 else 1
    cols = [slice(h * HEAD_DIM, (h + 1) * HEAD_DIM) for h in range(hp)]
    u_incl = _tri(blk, lambda j, s: j >= s)

    def body(q_ref, k_ref, v_ref, u_ref, o_ref, a_ref, b_ref, z_buf, hl_buf):
        i = pl.program_id(1)
        qs = [q_ref[:, c].astype(BF16) for c in cols]
        u = u_ref[...]

        def scores(j, h, causal):
            kj = k_ref[pl.ds(pl.multiple_of(j * blk, blk), blk), cols[h]].astype(BF16)
            z, lb, log_beta = _scores(qs[h], kj, scale, causal)
            b_ref[h, j] = jnp.exp(log_beta).astype(BF16)
            return z, _split(lb)

        def weigh(j, h, z, hl, acc, run, causal):
            vj = v_ref[pl.ds(pl.multiple_of(j * blk, blk), blk), cols[h]].astype(BF16)
            ct = jnp.dot(hl, u, preferred_element_type=F32)
            a = jnp.exp(z + ct[:, :blk] + jnp.tile(run, (1, lanes)))
            if causal is not None:
                a = jnp.where(causal, a, 0.0)
            a = a.astype(BF16)
            a_ref[h, j] = a
            return acc + jnp.dot(a, vj, preferred_element_type=F32), run + ct[:, blk:]

        def stage_scores(j):
            for h in range(hp):
                z_buf[h], hl_buf[h] = scores(j, h, None)

        zero = jnp.zeros((blk, HEAD_DIM), F32)
        causal = _causal(blk)
        carry = tuple(weigh(i, h, *scores(i, h, causal), zero, zero, causal) for h in range(hp))
        stage_scores(jnp.maximum(i - 1, 0))

        def step(t, carry):
            j = i - 1 - t
            out = tuple(weigh(j, h, z_buf[h], hl_buf[h], *carry[h], None) for h in range(hp))
            stage_scores(j - 1)
            return out

        carry = lax.fori_loop(0, jnp.maximum(i - 1, 0), step, carry)
        carry = lax.fori_loop(0, jnp.minimum(i, 1),
                              lambda t, c: tuple(weigh(t, h, z_buf[h], hl_buf[h], *c[h], None) for h in range(hp)), carry)
        for h in range(hp):
            o_ref[:, cols[h]] = carry[h][0].astype(BF16)

    G = n_heads // hp
    W = hp * HEAD_DIM
    saved = _sds((n_heads, nq, nq, blk, blk), BF16)
    saved_spec = pl.BlockSpec((hp, None, nq, blk, blk), lambda h, i: (h, i, 0, 0, 0))
    return _call(
        body, name="attn_fwd", grid=(G, nq),
        in_specs=[pl.BlockSpec((blk, W), lambda h, i: (i, h)),
                  pl.BlockSpec((S, W), lambda h, i: (0, G + h)),
                  pl.BlockSpec((S, W), lambda h, i: (0, 2 * G + h)),
                  pl.BlockSpec(u_incl.shape, lambda h, i: (0, 0))],
        out_specs=[pl.BlockSpec((blk, W), lambda h, i: (i, h)), saved_spec, saved_spec],
        out_shape=[_sds((S, n_heads * HEAD_DIM), BF16), saved, saved],
        args=(proj, proj, proj, u_incl), scratch=[pltpu.VMEM((hp, blk, blk), F32), pltpu.VMEM((hp, blk, CUMSUM_TERMS * blk), BF16)],
        sem=("parallel", "arbitrary"), comm=comm)


def _attn_bwd(proj, a_saved, b_saved, do, n_heads, blk, comm=None):
    S = proj.shape[0]
    nq = S // blk
    scale = HEAD_DIM ** -0.5
    lanes = blk // 128
    hp = 2 if n_heads % 2 == 0 else 1
    cols = [slice(h * HEAD_DIM, (h + 1) * HEAD_DIM) for h in range(hp)]
    l_incl = _tri(blk, lambda j, s: j <= s)

    def body(q_ref, k_ref, v_ref, a_ref, b_ref, do_ref, li_ref, dq_ref, dk_ref, dv_ref, dk_acc, dv_acc, g_buf, gl_buf):
        i = pl.program_id(1)

        @pl.when(i == 0)
        def _():
            dk_acc[...] = jnp.zeros_like(dk_acc)
            dv_acc[...] = jnp.zeros_like(dv_acc)

        qs = [q_ref[:, c].astype(BF16) for c in cols]
        dobs = [do_ref[:, c].astype(BF16) for c in cols]
        li = li_ref[...]

        def stage_products(j):
            ks = pl.multiple_of(j * blk, blk)
            for h in range(hp):
                vj = v_ref[pl.ds(ks, blk), cols[h]].astype(BF16)
                da = lax.dot_general(dobs[h], vj, (((1,), (1,)), ((), ())), preferred_element_type=F32)
                g = a_ref[h, j].astype(F32) * da
                g_buf[h] = g
                gl_buf[h] = _split(g)

        def grads(j, h, dq, run_g, causal):
            ks = pl.multiple_of(j * blk, blk)
            kj = k_ref[pl.ds(ks, blk), cols[h]].astype(BF16)
            gt = jnp.dot(gl_buf[h], li, preferred_element_type=F32)
            dz = g_buf[h] - b_ref[h, j].astype(F32) * (gt[:, :blk] + jnp.tile(run_g, (1, lanes)))
            if causal is not None:
                dz = jnp.where(causal, dz, 0.0)
            dzs = (dz * scale).astype(BF16)
            dk_acc[pl.ds(ks, blk), cols[h]] += lax.dot_general(dzs, qs[h], (((0,), (0,)), ((), ())),
                                                               preferred_element_type=F32)
            dv_acc[pl.ds(ks, blk), cols[h]] += lax.dot_general(a_ref[h, j], dobs[h], (((0,), (0,)), ((), ())),
                                                               preferred_element_type=F32)
            return dq + jnp.dot(dzs, kj, preferred_element_type=F32), run_g + gt[:, blk:]

        zero = jnp.zeros((blk, HEAD_DIM), F32)
        stage_products(0)

        def step(j, carry):
            out = tuple(grads(j, h, *carry[h], None) for h in range(hp))
            stage_products(j + 1)
            return out

        carry = lax.fori_loop(0, i, step, ((zero, zero),) * hp)
        causal = _causal(blk)
        carry = tuple(grads(i, h, *carry[h], causal) for h in range(hp))
        for h in range(hp):
            dq_ref[:, cols[h]] = carry[h][0].astype(BF16)

        @pl.when(i == nq - 1)
        def _():
            dk_ref[...] = dk_acc[...].astype(BF16)
            dv_ref[...] = dv_acc[...].astype(BF16)

    G = n_heads // hp
    W = hp * HEAD_DIM
    AW = n_heads * HEAD_DIM
    saved_spec = pl.BlockSpec((hp, None, nq, blk, blk), lambda h, i: (h, i, 0, 0, 0))
    return _call(
        body, name="attn_bwd", grid=(G, nq),
        in_specs=[pl.BlockSpec((blk, W), lambda h, i: (i, h)),
                  pl.BlockSpec((S, W), lambda h, i: (0, G + h)),
                  pl.BlockSpec((S, W), lambda h, i: (0, 2 * G + h)),
                  saved_spec, saved_spec,
                  pl.BlockSpec((blk, W), lambda h, i: (i, h)),
                  pl.BlockSpec(l_incl.shape, lambda h, i: (0, 0))],
        out_specs=[pl.BlockSpec((blk, W), lambda h, i: (i, h)),
                   pl.BlockSpec((S, W), lambda h, i: (0, h)),
                   pl.BlockSpec((S, W), lambda h, i: (0, h))],
        out_shape=[_sds((S, AW), BF16)] * 3, args=(proj, proj, proj, a_saved, b_saved, do, l_incl),
        scratch=[pltpu.VMEM((S, W), F32)] * 2 + [pltpu.VMEM((hp, blk, blk), F32),
                                                 pltpu.VMEM((hp, blk, CUMSUM_TERMS * blk), BF16)],
        sem=("parallel", "arbitrary"), comm=comm)


def _pool_count(r0, rows, w):
    t = r0 + lax.broadcasted_iota(jnp.int32, (rows, 1), 0)
    return jnp.minimum(t + 1, w).astype(F32)


def _pool_fwd(proj, col_blk, n_groups, width, chunk):
    S = proj.shape[0]
    H = POOL_HALO

    def body(u_ref, o_ref, pad_ref):
        g = pl.program_id(0)
        pad_ref[0:H, :] = jnp.zeros((H, width), F32)
        pad_ref[H:, :] = u_ref[...]
        for gi, w in enumerate(POOL_WINDOWS[:n_groups]):
            @pl.when(g == gi)
            def _(w=w):
                def one(c, _):
                    r0 = pl.multiple_of(c * chunk, chunk)
                    ext = pad_ref[pl.ds(r0, chunk + H), :]
                    s = ext
                    k = 1
                    while k < w:
                        s = s + pltpu.roll(s, k, 0)
                        k *= 2
                    o_ref[pl.ds(r0, chunk), :] = (s[H:] / _pool_count(r0, chunk, w) - ext[H:]).astype(BF16)
                    return 0

                lax.fori_loop(0, S // chunk, one, 0)

    return pl.pallas_call(
        body, name="pool_fwd", grid=(n_groups,),
        in_specs=[pl.BlockSpec((S, width), lambda g: (0, col_blk + g))],
        out_specs=pl.BlockSpec((S, width), lambda g: (0, g)), out_shape=_sds((S, n_groups * width), BF16),
        scratch_shapes=[pltpu.VMEM((S + H, width), F32)], compiler_params=_params("parallel"),
    )(proj)


def _pool_bwd(dpooled, n_groups, chunk):
    S = dpooled.shape[0]
    width = dpooled.shape[1] // n_groups
    H = POOL_HALO

    def body(d_ref, o_ref, pad_ref):
        g = pl.program_id(0)
        pad_ref[S:, :] = jnp.zeros((H, width), F32)
        for gi, w in enumerate(POOL_WINDOWS[:n_groups]):
            @pl.when(g == gi)
            def _(w=w):
                def fill(c, _):
                    r0 = pl.multiple_of(c * chunk, chunk)
                    pad_ref[pl.ds(r0, chunk), :] = d_ref[pl.ds(r0, chunk), :] / _pool_count(r0, chunk, w)
                    return 0

                lax.fori_loop(0, S // chunk, fill, 0)

                def one(c, _):
                    r0 = pl.multiple_of(c * chunk, chunk)
                    s = pad_ref[pl.ds(r0, chunk + H), :]
                    k = 1
                    while k < w:
                        s = s + pltpu.roll(s, chunk + H - k, 0)
                        k *= 2
                    o_ref[pl.ds(r0, chunk), :] = (s[:chunk] - d_ref[pl.ds(r0, chunk), :]).astype(BF16)
                    return 0

                lax.fori_loop(0, S // chunk, one, 0)

    return pl.pallas_call(
        body, name="pool_bwd", grid=(n_groups,),
        in_specs=[pl.BlockSpec((S, width), lambda g: (0, g))],
        out_specs=pl.BlockSpec((S, width), lambda g: (0, g)), out_shape=_sds((S, n_groups * width), BF16),
        scratch_shapes=[pltpu.VMEM((S + H, width), F32)], compiler_params=_params("parallel"),
    )(dpooled)


def _conv3(x_ext, w, b, shifted=None):
    x2, x1 = shifted if shifted is not None else (pltpu.roll(x_ext, 2, 0), pltpu.roll(x_ext, 1, 0))
    return b + x2 * w[0:1, :] + x1 * w[1:2, :] + x_ext * w[2:3, :]


def _gelu_parts(x):
    th = jnp.tanh(GELU_C0 * (x + GELU_C1 * (x * x * x)))
    return th, 0.5 * (1.0 + th)


def _conv_specs(S, F, cb):
    nb = F // cb
    return [pl.BlockSpec((S, cb), lambda j: (0, j)), pl.BlockSpec((S, cb), lambda j: (0, nb + j)),
            pl.BlockSpec((3, cb), lambda j: (0, j)), pl.BlockSpec((3, cb), lambda j: (0, nb + j)),
            pl.BlockSpec((1, cb), lambda j: (0, j)), pl.BlockSpec((1, cb), lambda j: (0, nb + j))]


def _conv_fwd(upre, cw, cb_, chunk):
    S, F2 = upre.shape
    F = F2 // 2
    cb = 128
    H = CONV_HALO

    def body(g_ref, v_ref, wg_ref, wv_ref, bg_ref, bv_ref, o_ref, pg_ref, pv_ref):
        pg_ref[0:H, :] = jnp.zeros((H, cb), F32)
        pv_ref[0:H, :] = jnp.zeros((H, cb), F32)
        pg_ref[H:, :] = g_ref[...]
        pv_ref[H:, :] = v_ref[...]
        wg, wv, bg, bv = wg_ref[...], wv_ref[...], bg_ref[...], bv_ref[...]

        def one(c, _):
            r0 = pl.multiple_of(c * chunk, chunk)
            up_g = _conv3(pg_ref[pl.ds(r0, chunk + H), :], wg, bg)[H:]
            up_v = _conv3(pv_ref[pl.ds(r0, chunk + H), :], wv, bv)[H:]
            _, cdf = _gelu_parts(up_g)
            o_ref[pl.ds(r0, chunk), :] = (up_g * cdf * up_v).astype(BF16)
            return 0

        lax.fori_loop(0, S // chunk, one, 0)

    return pl.pallas_call(
        body, name="conv_fwd", grid=(F // cb,), in_specs=_conv_specs(S, F, cb),
        out_specs=pl.BlockSpec((S, cb), lambda j: (0, j)), out_shape=_sds((S, F), BF16),
        scratch_shapes=[pltpu.VMEM((S + H, cb), F32)] * 2, compiler_params=_params("parallel"),
    )(upre, upre, cw, cw, cb_, cb_)


def _conv_bwd(upre, dact, cw, cb_, chunk, comm=None):
    S, F2 = upre.shape
    F = F2 // 2
    cb = 128
    H = CONV_HALO
    E = chunk + 2 * H

    def body(g_ref, v_ref, wg_ref, wv_ref, bg_ref, bv_ref, d_ref, dg_ref, dv_ref, dwg_ref, dwv_ref, dbg_ref, dbv_ref,
             pg_ref, pv_ref, pd_ref):
        for p, src in ((pg_ref, g_ref), (pv_ref, v_ref), (pd_ref, d_ref)):
            p[0:H, :] = jnp.zeros((H, cb), F32)
            p[H:S + H, :] = src[...]
            p[S + H:, :] = jnp.zeros((H, cb), F32)
        wg, wv, bg, bv = wg_ref[...], wv_ref[...], bg_ref[...], bv_ref[...]

        def taps_bwd(d, w):
            return d * w[2:3, :] + pltpu.roll(d, E - 1, 0) * w[1:2, :] + pltpu.roll(d, E - 2, 0) * w[0:1, :]

        def wsum(d, x, x2, x1):
            dc = d[H:H + chunk]
            return [jnp.sum(dc * x2[H:H + chunk], axis=0, keepdims=True),
                    jnp.sum(dc * x1[H:H + chunk], axis=0, keepdims=True),
                    jnp.sum(dc * x[H:H + chunk], axis=0, keepdims=True),
                    jnp.sum(dc, axis=0, keepdims=True)]

        def one(c, acc):
            r0 = pl.multiple_of(c * chunk, chunk)
            xg = pg_ref[pl.ds(r0, E), :]
            xv = pv_ref[pl.ds(r0, E), :]
            d = pd_ref[pl.ds(r0, E), :]
            sg = (pltpu.roll(xg, 2, 0), pltpu.roll(xg, 1, 0))
            sv = (pltpu.roll(xv, 2, 0), pltpu.roll(xv, 1, 0))
            up_g = _conv3(xg, wg, bg, sg)
            up_v = _conv3(xv, wv, bv, sv)
            th, cdf = _gelu_parts(up_g)
            dgelu = cdf + 0.5 * up_g * (1.0 - th * th) * (GELU_C0 * (1.0 + 3.0 * GELU_C1 * (up_g * up_g)))
            dgate = d * up_v * dgelu
            dval = d * (up_g * cdf)
            dg_ref[pl.ds(r0, chunk), :] = taps_bwd(dgate, wg)[H:H + chunk].astype(BF16)
            dv_ref[pl.ds(r0, chunk), :] = taps_bwd(dval, wv)[H:H + chunk].astype(BF16)
            return tuple(a + b for a, b in zip(acc, wsum(dgate, xg, *sg) + wsum(dval, xv, *sv)))

        zero = jnp.zeros((1, cb), F32)
        acc = lax.fori_loop(0, S // chunk, one, (zero,) * 8)
        dwg_ref[...] = jnp.concatenate(acc[0:3], axis=0)
        dbg_ref[...] = acc[3]
        dwv_ref[...] = jnp.concatenate(acc[4:7], axis=0)
        dbv_ref[...] = acc[7]

    col = lambda rows: pl.BlockSpec((rows, cb), lambda j: (0, j))
    return _call(
        body, name="conv_bwd", grid=(F // cb,), in_specs=_conv_specs(S, F, cb) + [col(S)],
        out_specs=[col(S), col(S), col(3), col(3), col(1), col(1)],
        out_shape=[_sds((S, F), BF16), _sds((S, F), BF16), _sds((3, F), F32), _sds((3, F), F32), _sds((1, F), F32),
                   _sds((1, F), F32)],
        args=(upre, upre, cw, cw, cb_, cb_, dact), scratch=[pltpu.VMEM((S + 2 * H, cb), F32)] * 3, sem=("parallel",),
        comm=comm)


def _position():
    x, y, c = lax.axis_index("x"), lax.axis_index("y"), lax.axis_index("c")
    return x, y, c, 4 * x + 2 * y + c


def _peer(x, y, c, d):
    px = 1 - x if d & 4 else x
    py = 1 - y if d & 2 else y
    pc = 1 - c if d & 1 else c
    return (px, py, pc), 4 * px + 2 * py + pc


def _all_gather_comm(tensors):
    nt = len(tensors)
    outs = [_sds((N_DEV,) + t.shape, t.dtype) for t in tensors]
    sems = [pltpu.SemaphoreType.DMA((7 * nt,)), pltpu.SemaphoreType.DMA((7 * nt,)), pltpu.SemaphoreType.DMA((nt,))]

    def parts(ins, outs_, sem_refs):
        send, recv, loc = sem_refs
        x, y, c, me = _position()
        chips = [(1 - x, y), (x, 1 - y), (1 - x, 1 - y)]

        def copy(t, k, block, to, src=None):
            slot = outs_[t].at[4 * block[0] + 2 * block[1] + block[2]]
            return pltpu.make_async_remote_copy(src_ref=slot if src is None else src, dst_ref=slot,
                                                send_sem=send.at[7 * t + k], recv_sem=recv.at[7 * t + k], device_id=to,
                                                device_id_type=MESH)

        def mine(t):
            return pltpu.make_async_copy(ins[t], outs_[t].at[me], loc.at[t])

        return (x, y, c), (x, y, 1 - c), chips, copy, mine

    def start(ins, outs_, sem_refs):
        me, sibling, chips, copy, mine = parts(ins, outs_, sem_refs)
        for t in range(nt):
            mine(t).start()
            copy(t, 0, me, sibling, src=ins[t]).start()
            for j, chip in enumerate(chips):
                copy(t, 1 + j, me, (*chip, me[2]), src=ins[t]).start()

    def finish(ins, outs_, sem_refs):
        me, sibling, chips, copy, mine = parts(ins, outs_, sem_refs)
        c = me[2]
        for t in range(nt):
            for j, chip in enumerate(chips):
                copy(t, 1 + j, (*chip, c), me).wait_recv()
                copy(t, 4 + j, (*chip, c), sibling).start()
        for t in range(nt):
            copy(t, 0, sibling, me).wait_recv()
            for j, chip in enumerate(chips):
                copy(t, 4 + j, (*chip, 1 - c), me).wait_recv()
        for t in range(nt):
            copy(t, 0, me, sibling, src=ins[t]).wait_send()
            for j, chip in enumerate(chips):
                copy(t, 1 + j, me, (*chip, c), src=ins[t]).wait_send()
                copy(t, 4 + j, (*chip, c), sibling).wait_send()
            mine(t).wait()

    return _Comm(tensors, outs, sems, start, finish)


def _reduce_scatter_comm(tensors):
    nt = len(tensors)
    outs = [_sds(t.shape, t.dtype) for t in tensors]
    sems = [pltpu.SemaphoreType.DMA((7 * nt,)), pltpu.SemaphoreType.DMA((7 * nt,)), pltpu.SemaphoreType.DMA((nt,))]

    def local(ins, outs_, sem_refs, t, me):
        return pltpu.make_async_copy(ins[t].at[me], outs_[t].at[me], sem_refs[2].at[t])

    def remote(ins, outs_, sem_refs, t, d, inbound):
        x, y, c, me = _position()
        peer, peer_idx = _peer(x, y, c, d)
        k = 7 * t + d - 1
        src, dst, to = (ins[t].at[me], outs_[t].at[peer_idx], (x, y, c)) if inbound else (ins[t].at[peer_idx], outs_[t].at[me], peer)
        return pltpu.make_async_remote_copy(src_ref=src, dst_ref=dst, send_sem=sem_refs[0].at[k], recv_sem=sem_refs[1].at[k],
                                            device_id=to, device_id_type=MESH)

    def start(ins, outs_, sem_refs):
        me = _position()[3]
        for t in range(nt):
            local(ins, outs_, sem_refs, t, me).start()
            for d in range(1, N_DEV):
                remote(ins, outs_, sem_refs, t, d, False).start()

    def finish(ins, outs_, sem_refs):
        me = _position()[3]
        for t in range(nt):
            for d in range(1, N_DEV):
                remote(ins, outs_, sem_refs, t, d, True).wait_recv()
        for t in range(nt):
            for d in range(1, N_DEV):
                remote(ins, outs_, sem_refs, t, d, False).wait_send()
            local(ins, outs_, sem_refs, t, me).wait()

    return _Comm(tensors, outs, sems, start, finish)


HBM_SPEC = pl.BlockSpec(memory_space=pltpu.HBM)
SEM_SPEC = pl.BlockSpec(memory_space=pltpu.SEMAPHORE)
DATAFLOW = pltpu.SideEffectType.DATAFLOW_SIDE_EFFECTING


def _scatter_copy(g_ref, land_ref, send_sems, recv_sems, d):
    x, y, c, me = _position()
    peer, peer_idx = _peer(x, y, c, d)
    return pltpu.make_async_remote_copy(src_ref=g_ref.at[peer_idx], dst_ref=land_ref.at[me], send_sem=send_sems.at[d - 1],
                                        recv_sem=recv_sems.at[d - 1], device_id=peer, device_id_type=MESH)


def _reduce_scatter_start(g, name):
    def body(g_ref, land_ref, send_sems, recv_sems, g_thru, land_thru, token):
        for d in range(1, N_DEV):
            _scatter_copy(g_ref, land_ref, send_sems, recv_sems, d).start()
        token[...] = jnp.zeros_like(token)

    return pl.pallas_call(
        body, name=name,
        out_shape=(pltpu.SemaphoreType.DMA((N_DEV - 1,)), pltpu.SemaphoreType.DMA((N_DEV - 1,)), pltpu.HBM(g.shape, g.dtype),
                   pltpu.HBM(g.shape, g.dtype), _sds((8, 128), F32)),
        in_specs=(HBM_SPEC, HBM_SPEC), out_specs=(SEM_SPEC, SEM_SPEC, HBM_SPEC, HBM_SPEC, pl.BlockSpec(memory_space=pltpu.VMEM)),
        input_output_aliases={0: 2, 1: 3}, compiler_params=pltpu.CompilerParams(has_side_effects=DATAFLOW),
    )(pltpu.with_memory_space_constraint(g, pltpu.HBM), pltpu.with_memory_space_constraint(lax.empty(g.shape, g.dtype), pltpu.HBM))


def _reduce_scatter_wait(send_sems, recv_sems, g_thru, land_thru, after, name):
    def body(g_ref, land_ref, send_sems, recv_sems, after_ref, g_out, land_out):
        for d in range(1, N_DEV):
            copy = _scatter_copy(g_ref, land_ref, send_sems, recv_sems, d)
            copy.wait_send()
            copy.wait_recv()

    return pl.pallas_call(
        body, name=name, out_shape=(pltpu.HBM(g_thru.shape, g_thru.dtype), pltpu.HBM(g_thru.shape, g_thru.dtype)),
        in_specs=(HBM_SPEC, HBM_SPEC, SEM_SPEC, SEM_SPEC, ANY), out_specs=(HBM_SPEC, HBM_SPEC), input_output_aliases={0: 0, 1: 1},
        compiler_params=pltpu.CompilerParams(has_side_effects=DATAFLOW),
    )(g_thru, land_thru, send_sems, recv_sems, after)


def _all_reduce_small(part):
    r, W = part.shape

    def body(p_ref, o_ref, g_ref, send_sems, recv_sems):
        x, y, c, me = _position()
        sends = []
        for d in range(1, N_DEV):
            peer, _ = _peer(x, y, c, d)
            cp = pltpu.make_async_remote_copy(src_ref=p_ref, dst_ref=g_ref.at[me], send_sem=send_sems.at[d - 1],
                                              recv_sem=recv_sems.at[d - 1], device_id=peer, device_id_type=MESH)
            cp.start()
            sends.append(cp)
        g_ref[me] = p_ref[...]
        for d in range(1, N_DEV):
            _, peer_idx = _peer(x, y, c, d)
            pltpu.make_async_remote_copy(src_ref=p_ref, dst_ref=g_ref.at[peer_idx], send_sem=send_sems.at[d - 1],
                                         recv_sem=recv_sems.at[d - 1], device_id=(x, y, c), device_id_type=MESH).wait_recv()
        for cp in sends:
            cp.wait_send()
        acc = g_ref[0]
        for i in range(1, N_DEV):
            acc = acc + g_ref[i]
        o_ref[...] = acc

    vmem = pl.BlockSpec(memory_space=pltpu.VMEM)
    return pl.pallas_call(
        body, name="all_reduce_small", in_specs=[vmem], out_specs=[vmem, vmem],
        out_shape=[_sds((r, W), F32), _sds((N_DEV, r, W), F32)],
        scratch_shapes=[pltpu.SemaphoreType.DMA((7,)), pltpu.SemaphoreType.DMA((7,))],
        compiler_params=pltpu.CompilerParams(has_side_effects=True, vmem_limit_bytes=VMEM_LIMIT),
    )(part)[0]


def _adamw_math(w, g, m, v):
    m = ADAM_B1 * m + (1.0 - ADAM_B1) * g
    v = ADAM_B2 * v + (1.0 - ADAM_B2) * (g * g)
    m_hat = m / (1.0 - ADAM_B1 ** ADAM_STEP)
    v_hat = v / (1.0 - ADAM_B2 ** ADAM_STEP)
    return -ADAM_LR * (m_hat / (jnp.sqrt(v_hat) + ADAM_EPS) + ADAM_WD * w), m, v


def _adamw(w, g, m, v, name):
    rows, cols = w.shape
    tr = _tile(rows, max(8, (2**18 // cols) // 8 * 8), 8)

    def body(w_ref, g_ref, m_ref, v_ref, d_ref, nm_ref, nv_ref):
        d_ref[...], nm_ref[...], nv_ref[...] = _adamw_math(w_ref[...], g_ref[...], m_ref[...], v_ref[...])

    spec = pl.BlockSpec((tr, cols), lambda i: (i, 0))
    return pl.pallas_call(
        body, name=name, grid=(rows // tr,), in_specs=[spec] * 4, out_specs=[spec] * 3,
        out_shape=[_sds((rows, cols), F32)] * 3, compiler_params=_params("parallel"),
    )(w, g, m, v)


def _adamw_sum(recv, w, m, v, name):
    n, rows, cols = recv.shape
    tr = _tile(rows, max(16, (2**17 // cols) // 16 * 16), 16)

    def body(r_ref, w_ref, m_ref, v_ref, g_ref, d_ref, nm_ref, nv_ref):
        g = r_ref[0].astype(F32)
        for i in range(1, n):
            g = g + r_ref[i].astype(F32)
        g_ref[...] = g
        d_ref[...], nm_ref[...], nv_ref[...] = _adamw_math(w_ref[...], g, m_ref[...], v_ref[...])

    spec = pl.BlockSpec((tr, cols), lambda i: (i, 0))
    return pl.pallas_call(
        body, name=name, grid=(rows // tr,), in_specs=[pl.BlockSpec((n, tr, cols), lambda i: (0, i, 0))] + [spec] * 3,
        out_specs=[spec] * 4, out_shape=[_sds((rows, cols), F32)] * 4, compiler_params=_params("parallel"),
    )(recv, w, m, v)


COLUMN_CUT = ("w_in", "w_attn_branch", "w_pool_branch", "w_up", "w_ple")
ROW_CUT = ("w_out", "w_down", "w_ple_gate")
REPLICATED = ("norm_mix_pre", "pool_scale", "norm_mix_post", "norm_ffn_pre", "conv_b", "norm_ffn_post", "norm_ple_post")
WEIGHTS = ("norm_mix_pre", "w_in", "w_attn_branch", "w_pool_group", "pool_scale", "w_pool_branch", "w_out", "norm_mix_post",
           "norm_ffn_pre", "w_up", "conv_w", "conv_b", "w_down", "norm_ffn_post", "w_ple", "w_ple_gate", "norm_ple_post")


def _size(shape):
    n = 1
    for s in shape:
        n *= s
    return n


def _pad_rows(flat, row_align):
    n = flat.shape[-1]
    per = PACK_W * row_align
    total = -(-n // per) * per
    return jnp.pad(flat, [(0, total - n)]).reshape(total // PACK_W, PACK_W)


def _natural(shard_major):
    n, r, c = shard_major.shape
    return shard_major.reshape(n * r, c)


def kernel(x, p, norm_mix_pre, w_in, w_attn_branch, w_pool_group, pool_scale, w_pool_branch, w_out, norm_mix_post, norm_ffn_pre, w_up, conv_w, conv_b, w_down, norm_ffn_post, w_ple, w_ple_gate, norm_ple_post, loss_target, m_norm_mix_pre, m_w_in, m_w_attn_branch, m_w_pool_group, m_pool_scale, m_w_pool_branch, m_w_out, m_norm_mix_post, m_norm_ffn_pre, m_w_up, m_conv_w, m_conv_b, m_w_down, m_norm_ffn_post, m_w_ple, m_w_ple_gate, m_norm_ple_post, v_norm_mix_pre, v_w_in, v_w_attn_branch, v_w_pool_group, v_pool_scale, v_w_pool_branch, v_w_out, v_norm_mix_post, v_norm_ffn_pre, v_w_up, v_conv_w, v_conv_b, v_w_down, v_norm_ffn_post, v_w_ple, v_w_ple_gate, v_norm_ple_post):
    given = dict(locals())
    wts = {n: given[n][0] for n in WEIGHTS}
    mom = {n: given["m_" + n][0] for n in WEIGHTS}
    var = {n: given["v_" + n][0] for n in WEIGHTS}
    xs = x[0]
    ps_in = p[0, 0]
    tgt = loss_target[0]
    S, D = xs.shape
    AW = wts["w_attn_branch"].shape[0]
    PW = wts["w_pool_branch"].shape[0]
    G = wts["w_pool_group"].shape[0]
    PGW = PW // G
    H = AW // HEAD_DIM
    F = wts["w_down"].shape[0] * N_DEV
    assert (3 * AW) % PGW == 0 and (3 * AW + PW) % D == 0 and PGW % 128 == 0 and F % 128 == 0
    tr = _tile(S, 256, 16)
    blk = _tile(S, 256, 128)
    chunk = _tile(S, 256, 8)
    me = 4 * lax.axis_index("x") + 2 * lax.axis_index("y") + lax.axis_index("c")

    cw_shape = wts["conv_w"].shape
    conv_b_row = wts["conv_b"].reshape(1, -1)
    g1, g2, g3, g4, g5 = (wts[n].reshape(1, D) for n in
                          ("norm_mix_pre", "norm_mix_post", "norm_ffn_pre", "norm_ffn_post", "norm_ple_post"))
    pscale = wts["pool_scale"].reshape(1, PW)
    big = 4096
    wb = {n: wts[n].astype(BF16) for n in COLUMN_CUT + ROW_CUT}
    wb["w_pool_group"] = wts["w_pool_group"].astype(BF16).reshape(G * PGW // N_DEV, PGW)

    h, w_in = _rms_fwd(xs, g1, tr, _all_gather_comm([wb["w_in"]]))
    proj, w_ab, w_pg, w_pb, w_out = _mm(
        h, w_in, b_sm=True, name="mm_in", tk=2048,
        comm=_all_gather_comm([wb["w_attn_branch"], wb["w_pool_group"], wb["w_pool_branch"], wb["w_out"]]))
    w_pg = jnp.moveaxis(w_pg.reshape(N_DEV, G, PGW // N_DEV, PGW), 0, 1).reshape(G, PGW, PGW)
    w_out = _natural(w_out)
    attn, a_saved, b_saved, w_up, conv_w_all = _attn_fwd(proj, H, blk, _all_gather_comm([wb["w_up"], wts["conv_w"]]))
    conv_w_full = jnp.moveaxis(conv_w_all, 0, 1).reshape(cw_shape[0], N_DEV * cw_shape[1])
    y_attn = _mm(attn, w_ab, b_sm=True, name="mm_attn_branch", tm=big, tn=256, tk=big)
    pooled = _pool_fwd(proj, 3 * AW // PGW, G, PGW, chunk)
    pg, ps = _pool_group_fwd(pooled, w_pg, pscale)
    y_pool = _mm(ps, w_pb, b_sm=True, name="mm_pool_branch", tm=big, tn=256, tk=big)
    gate_cb = (3 * AW + PW) // D
    mixed = _gate_mix(proj, y_attn, y_pool, gate_cb, tr)
    mo = _mm(mixed, w_out, name="mm_out", tk=2048)
    x1, h2 = _resid_rms2(xs, mo, g2, g3, tr)
    upre, w_down, w_ple, w_pleg = _mm(h2, w_up, b_sm=True, name="mm_up", tn=2048, tk=1024,
                                      comm=_all_gather_comm([wb["w_down"], wb["w_ple"], wb["w_ple_gate"]]))
    w_down, w_pleg = _natural(w_down), _natural(w_pleg)
    act = _conv_fwd(upre, conv_w_full, conv_b_row, chunk)
    yf = _mm(act, w_down, name="mm_down", tk=1408)
    x2, x2b = _resid_rms(x1, yf, g4, tr)
    e = _mm(ps_in, w_ple, b_sm=True, name="mm_ple", tm=big, tn=256, tk=big)
    gl = _mm(x2b, w_pleg, name="mm_ple_gate", tk=2048)
    loss_part, dx3, de, dgl, dg5 = _ple_loss(gl, e, x2, tgt, g5, tr)

    shards = lambda natural: natural.reshape((N_DEV, natural.shape[0] // N_DEV) + natural.shape[1:])
    recv = {}
    dw_ple = _mm(ps_in, de, ta=True, out_sm=True, out_dtype=BF16, name="mm_d_w_ple", tm=256, tn=256, tk=big)
    dw_pleg = shards(_mm(x2b, dgl, ta=True, out_dtype=BF16, name="mm_d_w_ple_gate"))
    dx2g = _mm(dgl, w_pleg, tb=True, name="mm_d_x2", tk=2048)
    dx2, dyf, dg4 = _rms_bwd_a(dx3, dx2g, yf, g4, tr)
    dw_down = shards(_mm(act, dyf, ta=True, out_dtype=BF16, name="mm_d_w_down"))
    dact, recv["w_ple"], recv["w_ple_gate"] = _mm(dyf, w_down, tb=True, name="mm_d_act", tn=1408, tk=1024,
                                                  comm=_reduce_scatter_comm([dw_ple, dw_pleg]))
    dup_g, dup_v, dcw_g, dcw_v, dcb_g, dcb_v, recv["w_down"] = _conv_bwd(upre, dact, conv_w_full, conv_b_row, chunk,
                                                                          _reduce_scatter_comm([dw_down]))
    dupre = jnp.concatenate([dup_g, dup_v], axis=1)
    dw_up = _mm(h2, dupre, ta=True, out_sm=True, out_dtype=BF16, name="mm_d_w_up", tn=2048)
    up_send, up_recv, dw_up, up_land, up_token = _reduce_scatter_start(dw_up, "rs_w_up_start")
    dupre, _ = lax.optimization_barrier((dupre, up_token))
    dh2 = _mm(dupre, w_up, tb=True, b_sm=True, name="mm_d_h2", tk=2048)
    dx1, dmo, dg3, dg2 = _rms_bwd_b(dx2, dh2, x1, g3, mo, g2, tr)
    dmixed = _mm(dmo, w_out, tb=True, name="mm_d_mixed", tk=2048)
    dw_out = shards(_mm(mixed, dmo, ta=True, out_dtype=BF16, name="mm_d_w_out"))
    dya, dyp, dga, dgp = _gate_bwd(dmixed, proj, y_attn, y_pool, gate_cb, tr)
    dps = _mm(dyp, w_pb, tb=True, b_sm=True, name="mm_d_ps", tm=2048, tk=256)
    dw_pb = _mm(ps, dyp, ta=True, out_sm=True, out_dtype=BF16, name="mm_d_w_pool_branch", tn=256, tk=big)
    dpg, dscale = _scale_bwd(dps, pg, pscale, tr)
    dpooled = _pool_group_bwd_x(dpg, w_pg)
    dw_pg = _pool_group_bwd_w(pooled, dpg, G)
    dw_pg = jnp.moveaxis(dw_pg.astype(BF16).reshape(G, N_DEV, PGW // N_DEV, PGW), 1, 0).reshape(N_DEV, G * PGW // N_DEV, PGW)
    du = _pool_bwd(dpooled, G, chunk)
    dattn = _mm(dya, w_ab, tb=True, b_sm=True, name="mm_d_attn", tm=2048, tk=256)
    dw_ab = _mm(attn, dya, ta=True, out_sm=True, out_dtype=BF16, name="mm_d_w_attn_branch", tn=256, tk=big)
    dq, dk, dv, recv["w_out"], recv["w_pool_branch"], recv["w_pool_group"], recv["w_attn_branch"] = _attn_bwd(
        proj, a_saved, b_saved, dattn, H, blk, _reduce_scatter_comm([dw_out, dw_pb, dw_pg, dw_ab]))
    dw_up, up_land = _reduce_scatter_wait(up_send, up_recv, dw_up, up_land, dq, "rs_w_up_wait")
    recv["w_up"] = lax.dynamic_update_slice_in_dim(up_land, lax.dynamic_index_in_dim(dw_up, me, 0, keepdims=True), me, 0)
    dproj = jnp.concatenate([dq, dk, dv, du, dga, dgp], axis=1)
    dw_in = _mm(h, dproj, ta=True, out_sm=True, out_dtype=BF16, name="mm_d_w_in")
    in_send, in_recv, dw_in, in_land, token = _reduce_scatter_start(dw_in, "rs_w_in_start")
    dproj, _ = lax.optimization_barrier((dproj, token))
    dh = _mm(dproj, w_in, tb=True, b_sm=True, name="mm_d_h", tm=big, tn=512, tk=1024)
    grad_x, dg1 = _rms_bwd_c(dx1, dh, xs, g1, tr)

    gshard, delta, new_m, new_v = {}, {}, {}, {}

    def adamw_cut(n):
        shp = wts[n].shape
        two_d = (_size(shp[:-1]), shp[-1])
        g_, d_, m_, v_ = _adamw_sum(recv[n].reshape((N_DEV,) + two_d), wts[n].reshape(two_d), mom[n].reshape(two_d),
                                    var[n].reshape(two_d), "adamw_" + n)
        gshard[n], delta[n], new_m[n], new_v[n] = g_.reshape(shp), d_.reshape(shp), m_.reshape(shp), v_.reshape(shp)

    for n in COLUMN_CUT[1:] + ROW_CUT + ("w_pool_group",):
        adamw_cut(n)

    assert COLUMN_CUT[0] == "w_in"
    done = lax.optimization_barrier((grad_x,) + tuple(delta[n] for n in delta))
    grad_x = done[0]
    for n, d_ in zip(list(delta), done[1:]):
        delta[n] = d_
    dw_in, in_land = _reduce_scatter_wait(in_send, in_recv, dw_in, in_land, grad_x, "rs_w_in_wait")

    dconv_w = jnp.concatenate([dcw_g, dcw_v], axis=1)
    dconv_b = jnp.concatenate([dcb_g, dcb_v], axis=1).reshape(-1)
    rep_parts = {"norm_mix_pre": dg1, "pool_scale": dscale, "norm_mix_post": dg2, "norm_ffn_pre": dg3, "conv_b": dconv_b,
                 "norm_ffn_post": dg4, "norm_ple_post": dg5}
    small = jnp.concatenate([rep_parts[n].reshape(-1) for n in REPLICATED] + [dconv_w.reshape(-1)])
    n_small = small.shape[0]
    small, in_land = lax.optimization_barrier((small, in_land))
    small_sum = _all_reduce_small(_pad_rows(small, 8)).reshape(-1)[:n_small]
    off = 0
    for n in REPLICATED:
        sz = _size(wts[n].shape)
        gshard[n] = small_sum[off:off + sz].reshape(wts[n].shape)
        off += sz
    dconv_w_sum = small_sum[off:off + 3 * 2 * F].reshape(3, 2 * F)
    gshard["conv_w"] = lax.dynamic_slice_in_dim(dconv_w_sum, me * cw_shape[1], cw_shape[1], axis=1)

    delta["conv_w"], new_m["conv_w"], new_v["conv_w"] = _adamw(wts["conv_w"], gshard["conv_w"], mom["conv_w"], var["conv_w"],
                                                               "adamw_conv_w")
    rep_sizes = [_size(wts[n].shape) for n in REPLICATED]
    n_rep = sum(rep_sizes)
    cat = lambda t: _pad_rows(jnp.concatenate([t[n].reshape(-1) for n in REPLICATED]), 8)
    d_, m_, v_ = _adamw(cat(wts), cat(gshard), cat(mom), cat(var), "adamw_replicated")
    off = 0
    for n, sz in zip(REPLICATED, rep_sizes):
        shp = wts[n].shape
        delta[n], new_m[n], new_v[n] = (t.reshape(-1)[off:off + sz].reshape(shp) for t in (d_, m_, v_))
        off += sz
    assert off == n_rep

    own = lax.dynamic_index_in_dim(dw_in, me, 0, keepdims=True)
    recv["w_in"] = lax.dynamic_update_slice_in_dim(in_land, own, me, 0)
    adamw_cut("w_in")

    loss = lax.psum(loss_part[0, 0], ("x", "y", "c"))
    lead = lambda t: t[None]
    return (loss, grad_x[None], *[lead(gshard[n]) for n in WEIGHTS], *[lead(delta[n]) for n in WEIGHTS],
            *[lead(new_m[n]) for n in WEIGHTS], *[lead(new_v[n]) for n in WEIGHTS])
```

```python
import functools

import jax
import jax.numpy as jnp
from jax import lax
from jax.experimental import pallas as pl
from jax.experimental.pallas import tpu as pltpu

F32 = jnp.float32
BF16 = jnp.bfloat16
MESH = pl.DeviceIdType.MESH

EPS = 1e-6
HEAD_DIM = 128
POOL_WINDOWS = (2, 4, 8, 16)
POOL_HALO = 16
CONV_HALO = 8
GELU_C0 = 0.7978845608028654
GELU_C1 = 0.044715
ADAM_LR = 0.001
ADAM_B1 = 0.9
ADAM_B2 = 0.999
ADAM_EPS = 1e-08
ADAM_WD = 0.01
ADAM_STEP = 10
N_DEV = 8
PACK_W = 1024
VMEM_LIMIT = 56 * 2**20
ANY = pl.BlockSpec(memory_space=pl.ANY)


def _params(*sem):
    return pltpu.CompilerParams(dimension_semantics=sem, vmem_limit_bytes=VMEM_LIMIT)


def _sds(shape, dtype):
    return jax.ShapeDtypeStruct(shape, dtype)


def _tile(dim, target, align):
    if dim <= target:
        return dim
    t = (target // align) * align
    while t >= align:
        if dim % t == 0:
            return t
        t -= align
    return dim


def _sigmoid(x):
    return 1.0 / (1.0 + jnp.exp(-x))


class _Comm:
    def __init__(self, ins, outs, sems, start, finish):
        self.ins, self.outs, self.sems, self.start, self.finish = list(ins), list(outs), list(sems), start, finish


def _call(body, *, name, grid, in_specs, out_specs, out_shape, args, scratch=(), sem=(), comm=None, after=()):
    in_specs, out_specs, out_shape, scratch = list(in_specs), list(out_specs), list(out_shape), list(scratch)
    if after:
        assert comm is None
        n_in = len(in_specs)
        return pl.pallas_call(lambda *refs: body(*refs[:n_in], *refs[n_in + len(after):]), name=name, grid=grid,
                              in_specs=in_specs + [ANY] * len(after), out_specs=out_specs, out_shape=out_shape,
                              scratch_shapes=scratch, compiler_params=_params(*sem))(*args, *after)
    if comm is None:
        return pl.pallas_call(body, name=name, grid=grid, in_specs=in_specs, out_specs=out_specs, out_shape=out_shape,
                              scratch_shapes=scratch, compiler_params=_params(*sem))(*args)
    n_in, n_out, n_scr, n_ci, n_co = len(in_specs), len(out_specs), len(scratch), len(comm.ins), len(comm.outs)

    def wrapped(*refs):
        ins, refs = refs[:n_in], refs[n_in:]
        c_ins, refs = refs[:n_ci], refs[n_ci:]
        outs, refs = refs[:n_out], refs[n_out:]
        c_outs, refs = refs[:n_co], refs[n_co:]
        scr, c_sems = refs[:n_scr], refs[n_scr:]
        first = last = None
        for axis, size in enumerate(grid):
            at_start, at_end = pl.program_id(axis) == 0, pl.program_id(axis) == size - 1
            first = at_start if first is None else jnp.logical_and(first, at_start)
            last = at_end if last is None else jnp.logical_and(last, at_end)
        if grid:
            pl.when(first)(lambda: comm.start(c_ins, c_outs, c_sems))
            body(*ins, *outs, *scr)
            pl.when(last)(lambda: comm.finish(c_ins, c_outs, c_sems))
        else:
            comm.start(c_ins, c_outs, c_sems)
            body(*ins, *outs, *scr)
            comm.finish(c_ins, c_outs, c_sems)

    return pl.pallas_call(
        wrapped, name=name, grid=grid, in_specs=in_specs + [ANY] * n_ci, out_specs=out_specs + [ANY] * n_co,
        out_shape=out_shape + comm.outs, scratch_shapes=scratch + comm.sems,
        compiler_params=pltpu.CompilerParams(dimension_semantics=("arbitrary",) * len(grid), vmem_limit_bytes=VMEM_LIMIT,
                                             has_side_effects=True),
    )(*args, *comm.ins)


def _mm(a, b, *, ta=False, tb=False, b_sm=False, out_sm=False, out_dtype=F32, name, tm=2048, tn=1024, tk=1024, comm=None,
        after=()):
    M, K = (a.shape[1], a.shape[0]) if ta else a.shape
    if b_sm:
        n_sl, rows, per = b.shape
        N = rows if tb else n_sl * per
        assert K == (n_sl * per if tb else rows)
    else:
        N = b.shape[0] if tb else b.shape[1]
    tm = _tile(M, tm, 128)
    tn = _tile(per if (b_sm and not tb) else N // N_DEV if out_sm else N, tn, 128)
    tk = _tile(per if (b_sm and tb) else K, tk, 128)
    nk = K // tk
    a_spec = pl.BlockSpec((tk, tm), lambda i, j, k: (k, i)) if ta else pl.BlockSpec((tm, tk), lambda i, j, k: (i, k))
    if not b_sm:
        b_spec = pl.BlockSpec((tn, tk), lambda i, j, k: (j, k)) if tb else pl.BlockSpec((tk, tn), lambda i, j, k: (k, j))
    elif tb:
        kp = per // tk
        b_spec = pl.BlockSpec((None, tn, tk), lambda i, j, k: (k // kp, j, k % kp))
    else:
        jp = per // tn
        b_spec = pl.BlockSpec((None, tk, tn), lambda i, j, k: (j // jp, k, j % jp))
    if out_sm:
        jo = (N // N_DEV) // tn
        o_spec = pl.BlockSpec((None, tm, tn), lambda i, j, k: (j // jo, i, j % jo))
        o_shape = _sds((N_DEV, M, N // N_DEV), out_dtype)
    else:
        o_spec = pl.BlockSpec((tm, tn), lambda i, j, k: (i, j))
        o_shape = _sds((M, N), out_dtype)
    dims = (((0 if ta else 1,), (1 if tb else 0,)), ((), ()))

    def product(a_ref, b_ref):
        return lax.dot_general(a_ref[...].astype(BF16), b_ref[...].astype(BF16), dims, preferred_element_type=F32)

    def body(a_ref, b_ref, o_ref, acc_ref):
        k = pl.program_id(2)

        @pl.when(k == 0)
        def _():
            acc_ref[...] = jnp.zeros_like(acc_ref)

        acc_ref[...] += product(a_ref, b_ref)

        @pl.when(k == nk - 1)
        def _():
            o_ref[...] = acc_ref[...].astype(o_ref.dtype)

    def body_one_step(a_ref, b_ref, o_ref):
        o_ref[...] = product(a_ref, b_ref).astype(o_ref.dtype)

    res = _call(body if nk > 1 else body_one_step, name=name, grid=(M // tm, N // tn, nk), in_specs=[a_spec, b_spec],
                out_specs=[o_spec], out_shape=[o_shape], args=(a, b), scratch=[pltpu.VMEM((tm, tn), F32)] if nk > 1 else [],
                sem=("parallel", "parallel", "arbitrary"), comm=comm, after=after)
    return res[0] if comm is None else res


def _pool_group_fwd(pooled, w_pg, scale):
    S = pooled.shape[0]
    G, C, C2 = w_pg.shape
    tm = _tile(S, 1024, 16)

    def body(a_ref, w_ref, s_ref, pg_ref, ps_ref):
        pg = jnp.dot(a_ref[...], w_ref[...], preferred_element_type=F32)
        pg_ref[...] = pg
        ps_ref[...] = (pg * s_ref[...]).astype(BF16)

    return pl.pallas_call(
        body, name="pool_group_fwd", grid=(G, S // tm),
        in_specs=[pl.BlockSpec((tm, C), lambda g, i: (i, g)), pl.BlockSpec((None, C, C2), lambda g, i: (g, 0, 0)),
                  pl.BlockSpec((1, C2), lambda g, i: (0, g))],
        out_specs=[pl.BlockSpec((tm, C2), lambda g, i: (i, g)), pl.BlockSpec((tm, C2), lambda g, i: (i, g))],
        out_shape=[jax.ShapeDtypeStruct((S, G * C2), F32), jax.ShapeDtypeStruct((S, G * C2), BF16)],
        compiler_params=_params("parallel", "parallel"),
    )(pooled, w_pg, scale)


def _pool_group_bwd_x(dpg, w_pg):
    S = dpg.shape[0]
    G, C, C2 = w_pg.shape
    tm = _tile(S, 1024, 16)

    def body(d_ref, w_ref, o_ref):
        o_ref[...] = lax.dot_general(d_ref[...], w_ref[...], (((1,), (1,)), ((), ())), preferred_element_type=F32)

    return pl.pallas_call(
        body, name="pool_group_bwd_x", grid=(G, S // tm),
        in_specs=[pl.BlockSpec((tm, C2), lambda g, i: (i, g)), pl.BlockSpec((None, C, C2), lambda g, i: (g, 0, 0))],
        out_specs=pl.BlockSpec((tm, C), lambda g, i: (i, g)), out_shape=jax.ShapeDtypeStruct((S, G * C), F32),
        compiler_params=_params("parallel", "parallel"),
    )(dpg, w_pg)


def _pool_group_bwd_w(pooled, dpg, G):
    S = pooled.shape[0]
    C, C2 = pooled.shape[1] // G, dpg.shape[1] // G
    tk = _tile(S, 1024, 16)

    def body(a_ref, d_ref, o_ref):
        @pl.when(pl.program_id(1) == 0)
        def _():
            o_ref[...] = jnp.zeros_like(o_ref)

        o_ref[...] += lax.dot_general(a_ref[...], d_ref[...], (((0,), (0,)), ((), ())), preferred_element_type=F32)

    return pl.pallas_call(
        body, name="pool_group_bwd_w", grid=(G, S // tk),
        in_specs=[pl.BlockSpec((tk, C), lambda g, k: (k, g)), pl.BlockSpec((tk, C2), lambda g, k: (k, g))],
        out_specs=pl.BlockSpec((None, C, C2), lambda g, k: (g, 0, 0)), out_shape=jax.ShapeDtypeStruct((G, C, C2), F32),
        compiler_params=_params("parallel", "arbitrary"),
    )(pooled, dpg)


def _rms(x, gain):
    r = lax.rsqrt(jnp.mean(x * x, axis=-1, keepdims=True) + EPS)
    return x * r * gain


def _rms_bwd(x, gain, dy):
    r = lax.rsqrt(jnp.mean(x * x, axis=-1, keepdims=True) + EPS)
    xh = x * r
    dgain = jnp.sum(dy * xh, axis=0, keepdims=True)
    dxh = dy * gain
    dx = r * (dxh - xh * jnp.mean(dxh * xh, axis=-1, keepdims=True))
    return dx, dgain


def _row_call(body, name, ins, outs, tr, *, comm=None):
    S = None
    in_specs, args = [], []
    for it in ins:
        arr, kind = it[0], it[1]
        if kind == "row":
            S = arr.shape[0]
            if len(it) == 4:
                width, cb = it[2], it[3]
                in_specs.append(pl.BlockSpec((tr, width), functools.partial(lambda i, cb: (i, cb), cb=cb)))
            else:
                in_specs.append(pl.BlockSpec((tr, arr.shape[1]), lambda i: (i, 0)))
        else:
            assert arr.ndim == 2
            in_specs.append(pl.BlockSpec(arr.shape, lambda i: (0, 0)))
        args.append(arr)
    out_specs, out_shape = [], []
    for sds, kind in outs:
        if kind == "row":
            out_specs.append(pl.BlockSpec((tr, sds.shape[1]), lambda i: (i, 0)))
        else:
            assert len(sds.shape) == 2
            out_specs.append(pl.BlockSpec(sds.shape, lambda i: (0, 0)))
        out_shape.append(sds)
    return _call(body, name=name, grid=(S // tr,), in_specs=in_specs, out_specs=out_specs, out_shape=out_shape, args=args,
                 sem=("arbitrary",), comm=comm)


def _first_step_zero(*refs):
    @pl.when(pl.program_id(0) == 0)
    def _():
        for r in refs:
            r[...] = jnp.zeros_like(r)


def _rms_fwd(x, gain, tr, comm):
    def body(x_ref, g_ref, o_ref):
        o_ref[...] = _rms(x_ref[...], g_ref[...]).astype(BF16)

    S, D = x.shape
    return _row_call(body, "rms_fwd", [(x, "row"), (gain, "full")], [(_sds((S, D), BF16), "row")], tr, comm=comm)


def _gate_mix(proj, ya, yp, gate_cb, tr):
    S, D = ya.shape

    def body(ga_ref, gp_ref, ya_ref, yp_ref, o_ref):
        o_ref[...] = (_sigmoid(ga_ref[...]) * ya_ref[...] + _sigmoid(gp_ref[...]) * yp_ref[...]).astype(BF16)

    return _row_call(body, "gate_mix", [(proj, "row", D, gate_cb), (proj, "row", D, gate_cb + 1), (ya, "row"), (yp, "row")],
                     [(_sds((S, D), BF16), "row")], tr)[0]


def _resid_rms2(x, mo, g2, g3, tr):
    S, D = x.shape

    def body(x_ref, mo_ref, g2_ref, g3_ref, x1_ref, h2_ref):
        x1 = x_ref[...] + _rms(mo_ref[...], g2_ref[...])
        x1_ref[...] = x1
        h2_ref[...] = _rms(x1, g3_ref[...]).astype(BF16)

    return _row_call(body, "resid_rms2", [(x, "row"), (mo, "row"), (g2, "full"), (g3, "full")],
                     [(_sds((S, D), F32), "row"), (_sds((S, D), BF16), "row")], tr)


def _resid_rms(x1, yf, g4, tr):
    S, D = x1.shape

    def body(x_ref, y_ref, g_ref, o_ref, ob_ref):
        x2 = x_ref[...] + _rms(y_ref[...], g_ref[...])
        o_ref[...] = x2
        ob_ref[...] = x2.astype(BF16)

    return _row_call(body, "resid_rms", [(x1, "row"), (yf, "row"), (g4, "full")],
                     [(_sds((S, D), F32), "row"), (_sds((S, D), BF16), "row")], tr)


def _ple_loss(gl, e, x2, tgt, g5, tr):
    S, D = x2.shape

    def body(gl_ref, e_ref, x2_ref, t_ref, g_ref, loss_ref, dx3_ref, de_ref, dgl_ref, dg_ref):
        _first_step_zero(loss_ref, dg_ref)
        s = _sigmoid(gl_ref[...])
        e_ = e_ref[...]
        t = s * e_
        gain = g_ref[...]
        err = x2_ref[...] + _rms(t, gain) - t_ref[...]
        row_loss = jnp.mean(err * err, axis=-1, keepdims=True)
        loss_ref[...] += 0.5 * jnp.sum(row_loss, axis=0, keepdims=True)
        dx3 = err * (1.0 / D)
        dx3_ref[...] = dx3
        dt, dgain = _rms_bwd(t, gain, dx3)
        dg_ref[...] += dgain
        de_ref[...] = (dt * s).astype(BF16)
        dgl_ref[...] = (dt * e_ * s * (1.0 - s)).astype(BF16)

    return _row_call(body, "ple_loss", [(gl, "row"), (e, "row"), (x2, "row"), (tgt, "row"), (g5, "full")],
                     [(_sds((1, 1), F32), "acc"), (_sds((S, D), F32), "row"), (_sds((S, D), BF16), "row"),
                      (_sds((S, D), BF16), "row"), (_sds((1, D), F32), "acc")], tr)


def _rms_bwd_a(dx3, dx2g, yf, g4, tr):
    S, D = yf.shape

    def body(a_ref, b_ref, y_ref, g_ref, dx_ref, dy_ref, dg_ref):
        _first_step_zero(dg_ref)
        dx2 = a_ref[...] + b_ref[...]
        dx_ref[...] = dx2
        dy, dgain = _rms_bwd(y_ref[...], g_ref[...], dx2)
        dy_ref[...] = dy.astype(BF16)
        dg_ref[...] += dgain

    return _row_call(body, "rms_bwd_a", [(dx3, "row"), (dx2g, "row"), (yf, "row"), (g4, "full")],
                     [(_sds((S, D), F32), "row"), (_sds((S, D), BF16), "row"), (_sds((1, D), F32), "acc")], tr)


def _rms_bwd_b(dx2, dh2, x1, g3, mo, g2, tr):
    S, D = x1.shape

    def body(dx2_ref, dh2_ref, x1_ref, g3_ref, mo_ref, g2_ref, dx1_ref, dmo_ref, dg3_ref, dg2_ref):
        _first_step_zero(dg3_ref, dg2_ref)
        d, dgain3 = _rms_bwd(x1_ref[...], g3_ref[...], dh2_ref[...])
        dx1 = dx2_ref[...] + d
        dx1_ref[...] = dx1
        dg3_ref[...] += dgain3
        dmo, dgain2 = _rms_bwd(mo_ref[...], g2_ref[...], dx1)
        dmo_ref[...] = dmo.astype(BF16)
        dg2_ref[...] += dgain2

    return _row_call(body, "rms_bwd_b", [(dx2, "row"), (dh2, "row"), (x1, "row"), (g3, "full"), (mo, "row"), (g2, "full")],
                     [(_sds((S, D), F32), "row"), (_sds((S, D), BF16), "row"), (_sds((1, D), F32), "acc"),
                      (_sds((1, D), F32), "acc")], tr)


def _rms_bwd_c(dx1, dh, x, g1, tr):
    S, D = x.shape

    def body(dx1_ref, dh_ref, x_ref, g_ref, o_ref, dg_ref):
        _first_step_zero(dg_ref)
        d, dgain = _rms_bwd(x_ref[...], g_ref[...], dh_ref[...])
        o_ref[...] = dx1_ref[...] + d
        dg_ref[...] += dgain

    return _row_call(body, "rms_bwd_c", [(dx1, "row"), (dh, "row"), (x, "row"), (g1, "full")],
                     [(_sds((S, D), F32), "row"), (_sds((1, D), F32), "acc")], tr)


def _gate_bwd(dmixed, proj, ya, yp, gate_cb, tr):
    S, D = ya.shape

    def body(dm_ref, ga_ref, gp_ref, ya_ref, yp_ref, dya_ref, dyp_ref, dga_ref, dgp_ref):
        dm = dm_ref[...]
        sa = _sigmoid(ga_ref[...])
        sp = _sigmoid(gp_ref[...])
        dya_ref[...] = (dm * sa).astype(BF16)
        dyp_ref[...] = (dm * sp).astype(BF16)
        dga_ref[...] = (dm * ya_ref[...] * sa * (1.0 - sa)).astype(BF16)
        dgp_ref[...] = (dm * yp_ref[...] * sp * (1.0 - sp)).astype(BF16)

    return _row_call(body, "gate_bwd",
                     [(dmixed, "row"), (proj, "row", D, gate_cb), (proj, "row", D, gate_cb + 1), (ya, "row"), (yp, "row")],
                     [(_sds((S, D), BF16), "row")] * 4, tr)


def _scale_bwd(dps, pg, scale, tr):
    S, W = dps.shape

    def body(d_ref, pg_ref, s_ref, o_ref, ds_ref):
        _first_step_zero(ds_ref)
        d = d_ref[...]
        o_ref[...] = (d * s_ref[...]).astype(BF16)
        ds_ref[...] += jnp.sum(d * pg_ref[...], axis=0, keepdims=True)

    return _row_call(body, "scale_bwd", [(dps, "row"), (pg, "row"), (scale, "full")],
                     [(_sds((S, W), BF16), "row"), (_sds((1, W), F32), "acc")], tr)


CUMSUM_TERMS = 2


def _tri(blk, cmp):
    j = lax.broadcasted_iota(jnp.int32, (blk, blk), 0)
    s = lax.broadcasted_iota(jnp.int32, (blk, blk), 1)
    one = jnp.concatenate([cmp(j, s).astype(BF16), jnp.ones((blk, 128), BF16)], axis=1)
    return jnp.concatenate([one] * CUMSUM_TERMS, axis=0)


def _split(x):
    terms = []
    for _ in range(CUMSUM_TERMS):
        t = x.astype(BF16)
        terms.append(t)
        x = x - t.astype(F32)
    return terms[0] if CUMSUM_TERMS == 1 else jnp.concatenate(terms, axis=1)


def _split_dot(x, u):
    return jnp.dot(_split(x), u, preferred_element_type=F32)


def _causal(blk):
    return lax.broadcasted_iota(jnp.int32, (blk, blk), 1) < lax.broadcasted_iota(jnp.int32, (blk, blk), 0)


def _scores(q, kj, scale, causal):
    z = lax.dot_general(q, kj, (((1,), (1,)), ((), ())), preferred_element_type=F32) * scale
    l1p = jnp.log(1.0 + jnp.exp(-jnp.abs(z)))
    lb = -(jnp.maximum(z, 0.0) + l1p)
    if causal is not None:
        lb = jnp.where(causal, lb, 0.0)
    return z, lb, jnp.minimum(z, 0.0) - l1p


def _attn_fwd(proj, n_heads, blk, comm=None):
    S = proj.shape[0]
    nq = S // blk
    scale = HEAD_DIM ** -0.5
    lanes = blk // 128
    hp = 2 if n_heads % 2 == 0 else 1
    cols = [slice(h * HEAD_DIM, (h + 1) * HEAD_DIM) for h in range(hp)]
    u_incl = _tri(blk, lambda j, s: j >= s)

    def body(q_ref, k_ref, v_ref, u_ref, o_ref, a_ref, b_ref, z_buf, hl_buf):
        i = pl.program_id(1)
        qs = [q_ref[:, c].astype(BF16) for c in cols]
        u = u_ref[...]

        def scores(j, h, causal):
            kj = k_ref[pl.ds(pl.multiple_of(j * blk, blk), blk), cols[h]].astype(BF16)
            z, lb, log_beta = _scores(qs[h], kj, scale, causal)
            b_ref[h, j] = jnp.exp(log_beta).astype(BF16)
            return z, _split(lb)

        def weigh(j, h, z, hl, acc, run, causal):
            vj = v_ref[pl.ds(pl.multiple_of(j * blk, blk), blk), cols[h]].astype(BF16)
            ct = jnp.dot(hl, u, preferred_element_type=F32)
            a = jnp.exp(z + ct[:, :blk] + jnp.tile(run, (1, lanes)))
            if causal is not None:
                a = jnp.where(causal, a, 0.0)
            a = a.astype(BF16)
            a_ref[h, j] = a
            return acc + jnp.dot(a, vj, preferred_element_type=F32), run + ct[:, blk:]

        def stage_scores(j):
            for h in range(hp):
                z_buf[h], hl_buf[h] = scores(j, h, None)

        zero = jnp.zeros((blk, HEAD_DIM), F32)
        causal = _causal(blk)
        carry = tuple(weigh(i, h, *scores(i, h, causal), zero, zero, causal) for h in range(hp))
        stage_scores(jnp.maximum(i - 1, 0))

        def step(t, carry):
            j = i - 1 - t
            out = tuple(weigh(j, h, z_buf[h], hl_buf[h], *carry[h], None) for h in range(hp))
            stage_scores(j - 1)
            return out

        carry = lax.fori_loop(0, jnp.maximum(i - 1, 0), step, carry)
        carry = lax.fori_loop(0, jnp.minimum(i, 1),
                              lambda t, c: tuple(weigh(t, h, z_buf[h], hl_buf[h], *c[h], None) for h in range(hp)), carry)
        for h in range(hp):
            o_ref[:, cols[h]] = carry[h][0].astype(BF16)

    G = n_heads // hp
    W = hp * HEAD_DIM
    saved = _sds((n_heads, nq, nq, blk, blk), BF16)
    saved_spec = pl.BlockSpec((hp, None, nq, blk, blk), lambda h, i: (h, i, 0, 0, 0))
    return _call(
        body, name="attn_fwd", grid=(G, nq),
        in_specs=[pl.BlockSpec((blk, W), lambda h, i: (i, h)),
                  pl.BlockSpec((S, W), lambda h, i: (0, G + h)),
                  pl.BlockSpec((S, W), lambda h, i: (0, 2 * G + h)),
                  pl.BlockSpec(u_incl.shape, lambda h, i: (0, 0))],
        out_specs=[pl.BlockSpec((blk, W), lambda h, i: (i, h)), saved_spec, saved_spec],
        out_shape=[_sds((S, n_heads * HEAD_DIM), BF16), saved, saved],
        args=(proj, proj, proj, u_incl), scratch=[pltpu.VMEM((hp, blk, blk), F32), pltpu.VMEM((hp, blk, CUMSUM_TERMS * blk), BF16)],
        sem=("parallel", "arbitrary"), comm=comm)


def _attn_bwd(proj, a_saved, b_saved, do, n_heads, blk, comm=None):
    S = proj.shape[0]
    nq = S // blk
    scale = HEAD_DIM ** -0.5
    lanes = blk // 128
    hp = 2 if n_heads % 2 == 0 else 1
    cols = [slice(h * HEAD_DIM, (h + 1) * HEAD_DIM) for h in range(hp)]
    l_incl = _tri(blk, lambda j, s: j <= s)

    def body(q_ref, k_ref, v_ref, a_ref, b_ref, do_ref, li_ref, dq_ref, dk_ref, dv_ref, dk_acc, dv_acc, g_buf, gl_buf):
        i = pl.program_id(1)

        @pl.when(i == 0)
        def _():
            dk_acc[...] = jnp.zeros_like(dk_acc)
            dv_acc[...] = jnp.zeros_like(dv_acc)

        qs = [q_ref[:, c].astype(BF16) for c in cols]
        dobs = [do_ref[:, c].astype(BF16) for c in cols]
        li = li_ref[...]

        def stage_products(j):
            ks = pl.multiple_of(j * blk, blk)
            for h in range(hp):
                vj = v_ref[pl.ds(ks, blk), cols[h]].astype(BF16)
                da = lax.dot_general(dobs[h], vj, (((1,), (1,)), ((), ())), preferred_element_type=F32)
                g = a_ref[h, j].astype(F32) * da
                g_buf[h] = g
                gl_buf[h] = _split(g)

        def grads(j, h, dq, run_g, causal):
            ks = pl.multiple_of(j * blk, blk)
            kj = k_ref[pl.ds(ks, blk), cols[h]].astype(BF16)
            gt = jnp.dot(gl_buf[h], li, preferred_element_type=F32)
            dz = g_buf[h] - b_ref[h, j].astype(F32) * (gt[:, :blk] + jnp.tile(run_g, (1, lanes)))
            if causal is not None:
                dz = jnp.where(causal, dz, 0.0)
            dzs = (dz * scale).astype(BF16)
            dk_acc[pl.ds(ks, blk), cols[h]] += lax.dot_general(dzs, qs[h], (((0,), (0,)), ((), ())),
                                                               preferred_element_type=F32)
            dv_acc[pl.ds(ks, blk), cols[h]] += lax.dot_general(a_ref[h, j], dobs[h], (((0,), (0,)), ((), ())),
                                                               preferred_element_type=F32)
            return dq + jnp.dot(dzs, kj, preferred_element_type=F32), run_g + gt[:, blk:]

        zero = jnp.zeros((blk, HEAD_DIM), F32)
        stage_products(0)

        def step(j, carry):
            out = tuple(grads(j, h, *carry[h], None) for h in range(hp))
            stage_products(j + 1)
            return out

        carry = lax.fori_loop(0, i, step, ((zero, zero),) * hp)
        causal = _causal(blk)
        carry = tuple(grads(i, h, *carry[h], causal) for h in range(hp))
        for h in range(hp):
            dq_ref[:, cols[h]] = carry[h][0].astype(BF16)

        @pl.when(i == nq - 1)
        def _():
            dk_ref[...] = dk_acc[...].astype(BF16)
            dv_ref[...] = dv_acc[...].astype(BF16)

    G = n_heads // hp
    W = hp * HEAD_DIM
    AW = n_heads * HEAD_DIM
    saved_spec = pl.BlockSpec((hp, None, nq, blk, blk), lambda h, i: (h, i, 0, 0, 0))
    return _call(
        body, name="attn_bwd", grid=(G, nq),
        in_specs=[pl.BlockSpec((blk, W), lambda h, i: (i, h)),
                  pl.BlockSpec((S, W), lambda h, i: (0, G + h)),
                  pl.BlockSpec((S, W), lambda h, i: (0, 2 * G + h)),
                  saved_spec, saved_spec,
                  pl.BlockSpec((blk, W), lambda h, i: (i, h)),
                  pl.BlockSpec(l_incl.shape, lambda h, i: (0, 0))],
        out_specs=[pl.BlockSpec((blk, W), lambda h, i: (i, h)),
                   pl.BlockSpec((S, W), lambda h, i: (0, h)),
                   pl.BlockSpec((S, W), lambda h, i: (0, h))],
        out_shape=[_sds((S, AW), BF16)] * 3, args=(proj, proj, proj, a_saved, b_saved, do, l_incl),
        scratch=[pltpu.VMEM((S, W), F32)] * 2 + [pltpu.VMEM((hp, blk, blk), F32),
                                                 pltpu.VMEM((hp, blk, CUMSUM_TERMS * blk), BF16)],
        sem=("parallel", "arbitrary"), comm=comm)


def _pool_count(r0, rows, w):
    t = r0 + lax.broadcasted_iota(jnp.int32, (rows, 1), 0)
    return jnp.minimum(t + 1, w).astype(F32)


def _pool_fwd(proj, col_blk, n_groups, width, chunk):
    S = proj.shape[0]
    H = POOL_HALO

    def body(u_ref, o_ref, pad_ref):
        g = pl.program_id(0)
        pad_ref[0:H, :] = jnp.zeros((H, width), F32)
        pad_ref[H:, :] = u_ref[...]
        for gi, w in enumerate(POOL_WINDOWS[:n_groups]):
            @pl.when(g == gi)
            def _(w=w):
                def one(c, _):
                    r0 = pl.multiple_of(c * chunk, chunk)
                    ext = pad_ref[pl.ds(r0, chunk + H), :]
                    s = ext
                    k = 1
                    while k < w:
                        s = s + pltpu.roll(s, k, 0)
                        k *= 2
                    o_ref[pl.ds(r0, chunk), :] = (s[H:] / _pool_count(r0, chunk, w) - ext[H:]).astype(BF16)
                    return 0

                lax.fori_loop(0, S // chunk, one, 0)

    return pl.pallas_call(
        body, name="pool_fwd", grid=(n_groups,),
        in_specs=[pl.BlockSpec((S, width), lambda g: (0, col_blk + g))],
        out_specs=pl.BlockSpec((S, width), lambda g: (0, g)), out_shape=_sds((S, n_groups * width), BF16),
        scratch_shapes=[pltpu.VMEM((S + H, width), F32)], compiler_params=_params("parallel"),
    )(proj)


def _pool_bwd(dpooled, n_groups, chunk):
    S = dpooled.shape[0]
    width = dpooled.shape[1] // n_groups
    H = POOL_HALO

    def body(d_ref, o_ref, pad_ref):
        g = pl.program_id(0)
        pad_ref[S:, :] = jnp.zeros((H, width), F32)
        for gi, w in enumerate(POOL_WINDOWS[:n_groups]):
            @pl.when(g == gi)
            def _(w=w):
                def fill(c, _):
                    r0 = pl.multiple_of(c * chunk, chunk)
                    pad_ref[pl.ds(r0, chunk), :] = d_ref[pl.ds(r0, chunk), :] / _pool_count(r0, chunk, w)
                    return 0

                lax.fori_loop(0, S // chunk, fill, 0)

                def one(c, _):
                    r0 = pl.multiple_of(c * chunk, chunk)
                    s = pad_ref[pl.ds(r0, chunk + H), :]
                    k = 1
                    while k < w:
                        s = s + pltpu.roll(s, chunk + H - k, 0)
                        k *= 2
                    o_ref[pl.ds(r0, chunk), :] = (s[:chunk] - d_ref[pl.ds(r0, chunk), :]).astype(BF16)
                    return 0

                lax.fori_loop(0, S // chunk, one, 0)

    return pl.pallas_call(
        body, name="pool_bwd", grid=(n_groups,),
        in_specs=[pl.BlockSpec((S, width), lambda g: (0, g))],
        out_specs=pl.BlockSpec((S, width), lambda g: (0, g)), out_shape=_sds((S, n_groups * width), BF16),
        scratch_shapes=[pltpu.VMEM((S + H, width), F32)], compiler_params=_params("parallel"),
    )(dpooled)


def _conv3(x_ext, w, b, shifted=None):
    x2, x1 = shifted if shifted is not None else (pltpu.roll(x_ext, 2, 0), pltpu.roll(x_ext, 1, 0))
    return b + x2 * w[0:1, :] + x1 * w[1:2, :] + x_ext * w[2:3, :]


def _gelu_parts(x):
    th = jnp.tanh(GELU_C0 * (x + GELU_C1 * (x * x * x)))
    return th, 0.5 * (1.0 + th)


def _conv_specs(S, F, cb):
    nb = F // cb
    return [pl.BlockSpec((S, cb), lambda j: (0, j)), pl.BlockSpec((S, cb), lambda j: (0, nb + j)),
            pl.BlockSpec((3, cb), lambda j: (0, j)), pl.BlockSpec((3, cb), lambda j: (0, nb + j)),
            pl.BlockSpec((1, cb), lambda j: (0, j)), pl.BlockSpec((1, cb), lambda j: (0, nb + j))]


def _conv_fwd(upre, cw, cb_, chunk):
    S, F2 = upre.shape
    F = F2 // 2
    cb = 128
    H = CONV_HALO

    def body(g_ref, v_ref, wg_ref, wv_ref, bg_ref, bv_ref, o_ref, pg_ref, pv_ref):
        pg_ref[0:H, :] = jnp.zeros((H, cb), F32)
        pv_ref[0:H, :] = jnp.zeros((H, cb), F32)
        pg_ref[H:, :] = g_ref[...]
        pv_ref[H:, :] = v_ref[...]
        wg, wv, bg, bv = wg_ref[...], wv_ref[...], bg_ref[...], bv_ref[...]

        def one(c, _):
            r0 = pl.multiple_of(c * chunk, chunk)
            up_g = _conv3(pg_ref[pl.ds(r0, chunk + H), :], wg, bg)[H:]
            up_v = _conv3(pv_ref[pl.ds(r0, chunk + H), :], wv, bv)[H:]
            _, cdf = _gelu_parts(up_g)
            o_ref[pl.ds(r0, chunk), :] = (up_g * cdf * up_v).astype(BF16)
            return 0

        lax.fori_loop(0, S // chunk, one, 0)

    return pl.pallas_call(
        body, name="conv_fwd", grid=(F // cb,), in_specs=_conv_specs(S, F, cb),
        out_specs=pl.BlockSpec((S, cb), lambda j: (0, j)), out_shape=_sds((S, F), BF16),
        scratch_shapes=[pltpu.VMEM((S + H, cb), F32)] * 2, compiler_params=_params("parallel"),
    )(upre, upre, cw, cw, cb_, cb_)


def _conv_bwd(upre, dact, cw, cb_, chunk, comm=None):
    S, F2 = upre.shape
    F = F2 // 2
    cb = 128
    H = CONV_HALO
    E = chunk + 2 * H

    def body(g_ref, v_ref, wg_ref, wv_ref, bg_ref, bv_ref, d_ref, dg_ref, dv_ref, dwg_ref, dwv_ref, dbg_ref, dbv_ref,
             pg_ref, pv_ref, pd_ref):
        for p, src in ((pg_ref, g_ref), (pv_ref, v_ref), (pd_ref, d_ref)):
            p[0:H, :] = jnp.zeros((H, cb), F32)
            p[H:S + H, :] = src[...]
            p[S + H:, :] = jnp.zeros((H, cb), F32)
        wg, wv, bg, bv = wg_ref[...], wv_ref[...], bg_ref[...], bv_ref[...]

        def taps_bwd(d, w):
            return d * w[2:3, :] + pltpu.roll(d, E - 1, 0) * w[1:2, :] + pltpu.roll(d, E - 2, 0) * w[0:1, :]

        def wsum(d, x, x2, x1):
            dc = d[H:H + chunk]
            return [jnp.sum(dc * x2[H:H + chunk], axis=0, keepdims=True),
                    jnp.sum(dc * x1[H:H + chunk], axis=0, keepdims=True),
                    jnp.sum(dc * x[H:H + chunk], axis=0, keepdims=True),
                    jnp.sum(dc, axis=0, keepdims=True)]

        def one(c, acc):
            r0 = pl.multiple_of(c * chunk, chunk)
            xg = pg_ref[pl.ds(r0, E), :]
            xv = pv_ref[pl.ds(r0, E), :]
            d = pd_ref[pl.ds(r0, E), :]
            sg = (pltpu.roll(xg, 2, 0), pltpu.roll(xg, 1, 0))
            sv = (pltpu.roll(xv, 2, 0), pltpu.roll(xv, 1, 0))
            up_g = _conv3(xg, wg, bg, sg)
            up_v = _conv3(xv, wv, bv, sv)
            th, cdf = _gelu_parts(up_g)
            dgelu = cdf + 0.5 * up_g * (1.0 - th * th) * (GELU_C0 * (1.0 + 3.0 * GELU_C1 * (up_g * up_g)))
            dgate = d * up_v * dgelu
            dval = d * (up_g * cdf)
            dg_ref[pl.ds(r0, chunk), :] = taps_bwd(dgate, wg)[H:H + chunk].astype(BF16)
            dv_ref[pl.ds(r0, chunk), :] = taps_bwd(dval, wv)[H:H + chunk].astype(BF16)
            return tuple(a + b for a, b in zip(acc, wsum(dgate, xg, *sg) + wsum(dval, xv, *sv)))

        zero = jnp.zeros((1, cb), F32)
        acc = lax.fori_loop(0, S // chunk, one, (zero,) * 8)
        dwg_ref[...] = jnp.concatenate(acc[0:3], axis=0)
        dbg_ref[...] = acc[3]
        dwv_ref[...] = jnp.concatenate(acc[4:7], axis=0)
        dbv_ref[...] = acc[7]

    col = lambda rows: pl.BlockSpec((rows, cb), lambda j: (0, j))
    return _call(
        body, name="conv_bwd", grid=(F // cb,), in_specs=_conv_specs(S, F, cb) + [col(S)],
        out_specs=[col(S), col(S), col(3), col(3), col(1), col(1)],
        out_shape=[_sds((S, F), BF16), _sds((S, F), BF16), _sds((3, F), F32), _sds((3, F), F32), _sds((1, F), F32),
                   _sds((1, F), F32)],
        args=(upre, upre, cw, cw, cb_, cb_, dact), scratch=[pltpu.VMEM((S + 2 * H, cb), F32)] * 3, sem=("parallel",),
        comm=comm)


def _position():
    x, y, c = lax.axis_index("x"), lax.axis_index("y"), lax.axis_index("c")
    return x, y, c, 4 * x + 2 * y + c


def _peer(x, y, c, d):
    px = 1 - x if d & 4 else x
    py = 1 - y if d & 2 else y
    pc = 1 - c if d & 1 else c
    return (px, py, pc), 4 * px + 2 * py + pc


def _all_gather_comm(tensors):
    nt = len(tensors)
    outs = [_sds((N_DEV,) + t.shape, t.dtype) for t in tensors]
    sems = [pltpu.SemaphoreType.DMA((7 * nt,)), pltpu.SemaphoreType.DMA((7 * nt,)), pltpu.SemaphoreType.DMA((nt,))]

    def parts(ins, outs_, sem_refs):
        send, recv, loc = sem_refs
        x, y, c, me = _position()
        chips = [(1 - x, y), (x, 1 - y), (1 - x, 1 - y)]

        def copy(t, k, block, to, src=None):
            slot = outs_[t].at[4 * block[0] + 2 * block[1] + block[2]]
            return pltpu.make_async_remote_copy(src_ref=slot if src is None else src, dst_ref=slot,
                                                send_sem=send.at[7 * t + k], recv_sem=recv.at[7 * t + k], device_id=to,
                                                device_id_type=MESH)

        def mine(t):
            return pltpu.make_async_copy(ins[t], outs_[t].at[me], loc.at[t])

        return (x, y, c), (x, y, 1 - c), chips, copy, mine

    def start(ins, outs_, sem_refs):
        me, sibling, chips, copy, mine = parts(ins, outs_, sem_refs)
        for t in range(nt):
            mine(t).start()
            copy(t, 0, me, sibling, src=ins[t]).start()
            for j, chip in enumerate(chips):
                copy(t, 1 + j, me, (*chip, me[2]), src=ins[t]).start()

    def finish(ins, outs_, sem_refs):
        me, sibling, chips, copy, mine = parts(ins, outs_, sem_refs)
        c = me[2]
        for t in range(nt):
            for j, chip in enumerate(chips):
                copy(t, 1 + j, (*chip, c), me).wait_recv()
                copy(t, 4 + j, (*chip, c), sibling).start()
        for t in range(nt):
            copy(t, 0, sibling, me).wait_recv()
            for j, chip in enumerate(chips):
                copy(t, 4 + j, (*chip, 1 - c), me).wait_recv()
        for t in range(nt):
            copy(t, 0, me, sibling, src=ins[t]).wait_send()
            for j, chip in enumerate(chips):
                copy(t, 1 + j, me, (*chip, c), src=ins[t]).wait_send()
                copy(t, 4 + j, (*chip, c), sibling).wait_send()
            mine(t).wait()

    return _Comm(tensors, outs, sems, start, finish)


def _reduce_scatter_comm(tensors):
    nt = len(tensors)
    outs = [_sds(t.shape, t.dtype) for t in tensors]
    sems = [pltpu.SemaphoreType.DMA((7 * nt,)), pltpu.SemaphoreType.DMA((7 * nt,)), pltpu.SemaphoreType.DMA((nt,))]

    def local(ins, outs_, sem_refs, t, me):
        return pltpu.make_async_copy(ins[t].at[me], outs_[t].at[me], sem_refs[2].at[t])

    def remote(ins, outs_, sem_refs, t, d, inbound):
        x, y, c, me = _position()
        peer, peer_idx = _peer(x, y, c, d)
        k = 7 * t + d - 1
        src, dst, to = (ins[t].at[me], outs_[t].at[peer_idx], (x, y, c)) if inbound else (ins[t].at[peer_idx], outs_[t].at[me], peer)
        return pltpu.make_async_remote_copy(src_ref=src, dst_ref=dst, send_sem=sem_refs[0].at[k], recv_sem=sem_refs[1].at[k],
                                            device_id=to, device_id_type=MESH)

    def start(ins, outs_, sem_refs):
        me = _position()[3]
        for t in range(nt):
            local(ins, outs_, sem_refs, t, me).start()
            for d in range(1, N_DEV):
                remote(ins, outs_, sem_refs, t, d, False).start()

    def finish(ins, outs_, sem_refs):
        me = _position()[3]
        for t in range(nt):
            for d in range(1, N_DEV):
                remote(ins, outs_, sem_refs, t, d, True).wait_recv()
        for t in range(nt):
            for d in range(1, N_DEV):
                remote(ins, outs_, sem_refs, t, d, False).wait_send()
            local(ins, outs_, sem_refs, t, me).wait()

    return _Comm(tensors, outs, sems, start, finish)


HBM_SPEC = pl.BlockSpec(memory_space=pltpu.HBM)
SEM_SPEC = pl.BlockSpec(memory_space=pltpu.SEMAPHORE)
DATAFLOW = pltpu.SideEffectType.DATAFLOW_SIDE_EFFECTING


def _scatter_copy(g_ref, land_ref, send_sems, recv_sems, d):
    x, y, c, me = _position()
    peer, peer_idx = _peer(x, y, c, d)
    return pltpu.make_async_remote_copy(src_ref=g_ref.at[peer_idx], dst_ref=land_ref.at[me], send_sem=send_sems.at[d - 1],
                                        recv_sem=recv_sems.at[d - 1], device_id=peer, device_id_type=MESH)


def _reduce_scatter_start(g, name):
    def body(g_ref, land_ref, send_sems, recv_sems, g_thru, land_thru, token):
        for d in range(1, N_DEV):
            _scatter_copy(g_ref, land_ref, send_sems, recv_sems, d).start()
        token[...] = jnp.zeros_like(token)

    return pl.pallas_call(
        body, name=name,
        out_shape=(pltpu.SemaphoreType.DMA((N_DEV - 1,)), pltpu.SemaphoreType.DMA((N_DEV - 1,)), pltpu.HBM(g.shape, g.dtype),
                   pltpu.HBM(g.shape, g.dtype), _sds((8, 128), F32)),
        in_specs=(HBM_SPEC, HBM_SPEC), out_specs=(SEM_SPEC, SEM_SPEC, HBM_SPEC, HBM_SPEC, pl.BlockSpec(memory_space=pltpu.VMEM)),
        input_output_aliases={0: 2, 1: 3}, compiler_params=pltpu.CompilerParams(has_side_effects=DATAFLOW),
    )(pltpu.with_memory_space_constraint(g, pltpu.HBM), pltpu.with_memory_space_constraint(lax.empty(g.shape, g.dtype), pltpu.HBM))


def _reduce_scatter_wait(send_sems, recv_sems, g_thru, land_thru, after, name):
    def body(g_ref, land_ref, send_sems, recv_sems, *rest):
        for d in range(1, N_DEV):
            copy = _scatter_copy(g_ref, land_ref, send_sems, recv_sems, d)
            copy.wait_send()
            copy.wait_recv()

    return pl.pallas_call(
        body, name=name, out_shape=(pltpu.HBM(g_thru.shape, g_thru.dtype), pltpu.HBM(g_thru.shape, g_thru.dtype)),
        in_specs=(HBM_SPEC, HBM_SPEC, SEM_SPEC, SEM_SPEC) + (ANY,) * len(after), out_specs=(HBM_SPEC, HBM_SPEC),
        input_output_aliases={0: 0, 1: 1}, compiler_params=pltpu.CompilerParams(has_side_effects=DATAFLOW),
    )(g_thru, land_thru, send_sems, recv_sems, *after)


def _all_reduce_small(part, after):
    r, W = part.shape

    def body(p_ref, after_ref, o_ref, g_ref, send_sems, recv_sems):
        x, y, c, me = _position()
        sends = []
        for d in range(1, N_DEV):
            peer, _ = _peer(x, y, c, d)
            cp = pltpu.make_async_remote_copy(src_ref=p_ref, dst_ref=g_ref.at[me], send_sem=send_sems.at[d - 1],
                                              recv_sem=recv_sems.at[d - 1], device_id=peer, device_id_type=MESH)
            cp.start()
            sends.append(cp)
        g_ref[me] = p_ref[...]
        for d in range(1, N_DEV):
            _, peer_idx = _peer(x, y, c, d)
            pltpu.make_async_remote_copy(src_ref=p_ref, dst_ref=g_ref.at[peer_idx], send_sem=send_sems.at[d - 1],
                                         recv_sem=recv_sems.at[d - 1], device_id=(x, y, c), device_id_type=MESH).wait_recv()
        for cp in sends:
            cp.wait_send()
        acc = g_ref[0]
        for i in range(1, N_DEV):
            acc = acc + g_ref[i]
        o_ref[...] = acc

    vmem = pl.BlockSpec(memory_space=pltpu.VMEM)
    return pl.pallas_call(
        body, name="all_reduce_small", in_specs=[vmem, ANY], out_specs=[vmem, vmem],
        out_shape=[_sds((r, W), F32), _sds((N_DEV, r, W), F32)],
        scratch_shapes=[pltpu.SemaphoreType.DMA((7,)), pltpu.SemaphoreType.DMA((7,))],
        compiler_params=pltpu.CompilerParams(has_side_effects=True, vmem_limit_bytes=VMEM_LIMIT),
    )(part, after)[0]


def _adamw_math(w, g, m, v):
    m = ADAM_B1 * m + (1.0 - ADAM_B1) * g
    v = ADAM_B2 * v + (1.0 - ADAM_B2) * (g * g)
    m_hat = m / (1.0 - ADAM_B1 ** ADAM_STEP)
    v_hat = v / (1.0 - ADAM_B2 ** ADAM_STEP)
    return -ADAM_LR * (m_hat / (jnp.sqrt(v_hat) + ADAM_EPS) + ADAM_WD * w), m, v


def _adamw(w, g, m, v, name):
    rows, cols = w.shape
    tr = _tile(rows, max(8, (2**18 // cols) // 8 * 8), 8)

    def body(w_ref, g_ref, m_ref, v_ref, d_ref, nm_ref, nv_ref):
        d_ref[...], nm_ref[...], nv_ref[...] = _adamw_math(w_ref[...], g_ref[...], m_ref[...], v_ref[...])

    spec = pl.BlockSpec((tr, cols), lambda i: (i, 0))
    return pl.pallas_call(
        body, name=name, grid=(rows // tr,), in_specs=[spec] * 4, out_specs=[spec] * 3,
        out_shape=[_sds((rows, cols), F32)] * 3, compiler_params=_params("parallel"),
    )(w, g, m, v)


def _adamw_sum(recv, w, m, v, name, after=()):
    n, rows, cols = recv.shape
    tr = _tile(rows, max(16, (2**17 // cols) // 16 * 16), 16)

    def body(r_ref, w_ref, m_ref, v_ref, g_ref, d_ref, nm_ref, nv_ref):
        g = r_ref[0].astype(F32)
        for i in range(1, n):
            g = g + r_ref[i].astype(F32)
        g_ref[...] = g
        d_ref[...], nm_ref[...], nv_ref[...] = _adamw_math(w_ref[...], g, m_ref[...], v_ref[...])

    spec = pl.BlockSpec((tr, cols), lambda i: (i, 0))
    return _call(body, name=name, grid=(rows // tr,), in_specs=[pl.BlockSpec((n, tr, cols), lambda i: (0, i, 0))] + [spec] * 3,
                 out_specs=[spec] * 4, out_shape=[_sds((rows, cols), F32)] * 4, args=(recv, w, m, v), sem=("parallel",),
                 after=after)


COLUMN_CUT = ("w_in", "w_attn_branch", "w_pool_branch", "w_up", "w_ple")
ROW_CUT = ("w_out", "w_down", "w_ple_gate")
REPLICATED = ("norm_mix_pre", "pool_scale", "norm_mix_post", "norm_ffn_pre", "conv_b", "norm_ffn_post", "norm_ple_post")
WEIGHTS = ("norm_mix_pre", "w_in", "w_attn_branch", "w_pool_group", "pool_scale", "w_pool_branch", "w_out", "norm_mix_post",
           "norm_ffn_pre", "w_up", "conv_w", "conv_b", "w_down", "norm_ffn_post", "w_ple", "w_ple_gate", "norm_ple_post")


def _size(shape):
    n = 1
    for s in shape:
        n *= s
    return n


def _pad_rows(flat, row_align):
    n = flat.shape[-1]
    per = PACK_W * row_align
    total = -(-n // per) * per
    return jnp.pad(flat, [(0, total - n)]).reshape(total // PACK_W, PACK_W)


def _natural(shard_major):
    n, r, c = shard_major.shape
    return shard_major.reshape(n * r, c)


def kernel(x, p, norm_mix_pre, w_in, w_attn_branch, w_pool_group, pool_scale, w_pool_branch, w_out, norm_mix_post, norm_ffn_pre, w_up, conv_w, conv_b, w_down, norm_ffn_post, w_ple, w_ple_gate, norm_ple_post, loss_target, m_norm_mix_pre, m_w_in, m_w_attn_branch, m_w_pool_group, m_pool_scale, m_w_pool_branch, m_w_out, m_norm_mix_post, m_norm_ffn_pre, m_w_up, m_conv_w, m_conv_b, m_w_down, m_norm_ffn_post, m_w_ple, m_w_ple_gate, m_norm_ple_post, v_norm_mix_pre, v_w_in, v_w_attn_branch, v_w_pool_group, v_pool_scale, v_w_pool_branch, v_w_out, v_norm_mix_post, v_norm_ffn_pre, v_w_up, v_conv_w, v_conv_b, v_w_down, v_norm_ffn_post, v_w_ple, v_w_ple_gate, v_norm_ple_post):
    given = dict(locals())
    wts = {n: given[n][0] for n in WEIGHTS}
    mom = {n: given["m_" + n][0] for n in WEIGHTS}
    var = {n: given["v_" + n][0] for n in WEIGHTS}
    xs = x[0]
    ps_in = p[0, 0]
    tgt = loss_target[0]
    S, D = xs.shape
    AW = wts["w_attn_branch"].shape[0]
    PW = wts["w_pool_branch"].shape[0]
    G = wts["w_pool_group"].shape[0]
    PGW = PW // G
    H = AW // HEAD_DIM
    F = wts["w_down"].shape[0] * N_DEV
    assert (3 * AW) % PGW == 0 and (3 * AW + PW) % D == 0 and PGW % 128 == 0 and F % 128 == 0
    tr = _tile(S, 256, 16)
    blk = _tile(S, 256, 128)
    chunk = _tile(S, 256, 8)
    me = 4 * lax.axis_index("x") + 2 * lax.axis_index("y") + lax.axis_index("c")

    cw_shape = wts["conv_w"].shape
    conv_b_row = wts["conv_b"].reshape(1, -1)
    g1, g2, g3, g4, g5 = (wts[n].reshape(1, D) for n in
                          ("norm_mix_pre", "norm_mix_post", "norm_ffn_pre", "norm_ffn_post", "norm_ple_post"))
    pscale = wts["pool_scale"].reshape(1, PW)
    big = 4096
    wb = {n: wts[n].astype(BF16) for n in COLUMN_CUT + ROW_CUT}
    wb["w_pool_group"] = wts["w_pool_group"].astype(BF16).reshape(G * PGW // N_DEV, PGW)

    h, w_in = _rms_fwd(xs, g1, tr, _all_gather_comm([wb["w_in"]]))
    proj, w_ab, w_pg, w_pb, w_out = _mm(
        h, w_in, b_sm=True, name="mm_in", tk=2048,
        comm=_all_gather_comm([wb["w_attn_branch"], wb["w_pool_group"], wb["w_pool_branch"], wb["w_out"]]))
    w_pg = jnp.moveaxis(w_pg.reshape(N_DEV, G, PGW // N_DEV, PGW), 0, 1).reshape(G, PGW, PGW)
    w_out = _natural(w_out)
    attn, a_saved, b_saved, w_up, conv_w_all = _attn_fwd(proj, H, blk, _all_gather_comm([wb["w_up"], wts["conv_w"]]))
    conv_w_full = jnp.moveaxis(conv_w_all, 0, 1).reshape(cw_shape[0], N_DEV * cw_shape[1])
    y_attn = _mm(attn, w_ab, b_sm=True, name="mm_attn_branch", tm=big, tn=256, tk=big)
    pooled = _pool_fwd(proj, 3 * AW // PGW, G, PGW, chunk)
    pg, ps = _pool_group_fwd(pooled, w_pg, pscale)
    y_pool = _mm(ps, w_pb, b_sm=True, name="mm_pool_branch", tm=big, tn=256, tk=big)
    gate_cb = (3 * AW + PW) // D
    mixed = _gate_mix(proj, y_attn, y_pool, gate_cb, tr)
    mo = _mm(mixed, w_out, name="mm_out", tk=2048)
    x1, h2 = _resid_rms2(xs, mo, g2, g3, tr)
    upre, w_down, w_ple, w_pleg = _mm(h2, w_up, b_sm=True, name="mm_up", tn=2048, tk=1024,
                                      comm=_all_gather_comm([wb["w_down"], wb["w_ple"], wb["w_ple_gate"]]))
    w_down, w_pleg = _natural(w_down), _natural(w_pleg)
    act = _conv_fwd(upre, conv_w_full, conv_b_row, chunk)
    yf = _mm(act, w_down, name="mm_down", tk=1408)
    x2, x2b = _resid_rms(x1, yf, g4, tr)
    e = _mm(ps_in, w_ple, b_sm=True, name="mm_ple", tm=big, tn=256, tk=big)
    gl = _mm(x2b, w_pleg, name="mm_ple_gate", tk=2048)
    loss_part, dx3, de, dgl, dg5 = _ple_loss(gl, e, x2, tgt, g5, tr)

    shards = lambda natural: natural.reshape((N_DEV, natural.shape[0] // N_DEV) + natural.shape[1:])
    recv = {}
    dw_ple = _mm(ps_in, de, ta=True, out_sm=True, out_dtype=BF16, name="mm_d_w_ple", tm=256, tn=256, tk=big)
    dw_pleg = shards(_mm(x2b, dgl, ta=True, out_dtype=BF16, name="mm_d_w_ple_gate"))
    dx2g = _mm(dgl, w_pleg, tb=True, name="mm_d_x2", tk=2048)
    dx2, dyf, dg4 = _rms_bwd_a(dx3, dx2g, yf, g4, tr)
    dw_down = shards(_mm(act, dyf, ta=True, out_dtype=BF16, name="mm_d_w_down"))
    dact, recv["w_ple"], recv["w_ple_gate"] = _mm(dyf, w_down, tb=True, name="mm_d_act", tn=1408, tk=1024,
                                                  comm=_reduce_scatter_comm([dw_ple, dw_pleg]))
    dup_g, dup_v, dcw_g, dcw_v, dcb_g, dcb_v, recv["w_down"] = _conv_bwd(upre, dact, conv_w_full, conv_b_row, chunk,
                                                                          _reduce_scatter_comm([dw_down]))
    dupre = jnp.concatenate([dup_g, dup_v], axis=1)
    dw_up = _mm(h2, dupre, ta=True, out_sm=True, out_dtype=BF16, name="mm_d_w_up", tn=2048)
    up_send, up_recv, dw_up, up_land, up_token = _reduce_scatter_start(dw_up, "rs_w_up_start")
    dh2 = _mm(dupre, w_up, tb=True, b_sm=True, name="mm_d_h2", tk=2048, after=(up_token,))
    dx1, dmo, dg3, dg2 = _rms_bwd_b(dx2, dh2, x1, g3, mo, g2, tr)
    dmixed = _mm(dmo, w_out, tb=True, name="mm_d_mixed", tk=2048)
    dw_out = shards(_mm(mixed, dmo, ta=True, out_dtype=BF16, name="mm_d_w_out"))
    dya, dyp, dga, dgp = _gate_bwd(dmixed, proj, y_attn, y_pool, gate_cb, tr)
    dps = _mm(dyp, w_pb, tb=True, b_sm=True, name="mm_d_ps", tm=2048, tk=256)
    dw_pb = _mm(ps, dyp, ta=True, out_sm=True, out_dtype=BF16, name="mm_d_w_pool_branch", tn=256, tk=big)
    dpg, dscale = _scale_bwd(dps, pg, pscale, tr)
    dpooled = _pool_group_bwd_x(dpg, w_pg)
    dw_pg = _pool_group_bwd_w(pooled, dpg, G)
    dw_pg = jnp.moveaxis(dw_pg.astype(BF16).reshape(G, N_DEV, PGW // N_DEV, PGW), 1, 0).reshape(N_DEV, G * PGW // N_DEV, PGW)
    du = _pool_bwd(dpooled, G, chunk)
    dattn = _mm(dya, w_ab, tb=True, b_sm=True, name="mm_d_attn", tm=2048, tk=256)
    dw_ab = _mm(attn, dya, ta=True, out_sm=True, out_dtype=BF16, name="mm_d_w_attn_branch", tn=256, tk=big)
    dq, dk, dv, recv["w_out"], recv["w_pool_branch"], recv["w_pool_group"], recv["w_attn_branch"] = _attn_bwd(
        proj, a_saved, b_saved, dattn, H, blk, _reduce_scatter_comm([dw_out, dw_pb, dw_pg, dw_ab]))
    dw_up, up_land = _reduce_scatter_wait(up_send, up_recv, dw_up, up_land, (dq,), "rs_w_up_wait")
    recv["w_up"] = lax.dynamic_update_slice_in_dim(up_land, lax.dynamic_index_in_dim(dw_up, me, 0, keepdims=True), me, 0)
    dproj = jnp.concatenate([dq, dk, dv, du, dga, dgp], axis=1)
    dw_in = _mm(h, dproj, ta=True, out_sm=True, out_dtype=BF16, name="mm_d_w_in")
    in_send, in_recv, dw_in, in_land, token = _reduce_scatter_start(dw_in, "rs_w_in_start")
    dh = _mm(dproj, w_in, tb=True, b_sm=True, name="mm_d_h", tm=big, tn=512, tk=1024, after=(token,))
    grad_x, dg1 = _rms_bwd_c(dx1, dh, xs, g1, tr)

    gshard, delta, new_m, new_v = {}, {}, {}, {}

    def adamw_cut(n, after=()):
        shp = wts[n].shape
        two_d = (_size(shp[:-1]), shp[-1])
        g_, d_, m_, v_ = _adamw_sum(recv[n].reshape((N_DEV,) + two_d), wts[n].reshape(two_d), mom[n].reshape(two_d),
                                    var[n].reshape(two_d), "adamw_" + n, after)
        gshard[n], delta[n], new_m[n], new_v[n] = g_.reshape(shp), d_.reshape(shp), m_.reshape(shp), v_.reshape(shp)

    for n in COLUMN_CUT[1:] + ROW_CUT + ("w_pool_group",):
        adamw_cut(n, (token,))

    assert COLUMN_CUT[0] == "w_in"
    dw_in, in_land = _reduce_scatter_wait(in_send, in_recv, dw_in, in_land, (grad_x,) + tuple(delta[n] for n in delta),
                                          "rs_w_in_wait")

    dconv_w = jnp.concatenate([dcw_g, dcw_v], axis=1)
    dconv_b = jnp.concatenate([dcb_g, dcb_v], axis=1).reshape(-1)
    rep_parts = {"norm_mix_pre": dg1, "pool_scale": dscale, "norm_mix_post": dg2, "norm_ffn_pre": dg3, "conv_b": dconv_b,
                 "norm_ffn_post": dg4, "norm_ple_post": dg5}
    small = jnp.concatenate([rep_parts[n].reshape(-1) for n in REPLICATED] + [dconv_w.reshape(-1)])
    n_small = small.shape[0]
    small_sum = _all_reduce_small(_pad_rows(small, 8), in_land).reshape(-1)[:n_small]
    off = 0
    for n in REPLICATED:
        sz = _size(wts[n].shape)
        gshard[n] = small_sum[off:off + sz].reshape(wts[n].shape)
        off += sz
    dconv_w_sum = small_sum[off:off + 3 * 2 * F].reshape(3, 2 * F)
    gshard["conv_w"] = lax.dynamic_slice_in_dim(dconv_w_sum, me * cw_shape[1], cw_shape[1], axis=1)

    delta["conv_w"], new_m["conv_w"], new_v["conv_w"] = _adamw(wts["conv_w"], gshard["conv_w"], mom["conv_w"], var["conv_w"],
                                                               "adamw_conv_w")
    rep_sizes = [_size(wts[n].shape) for n in REPLICATED]
    n_rep = sum(rep_sizes)
    cat = lambda t: _pad_rows(jnp.concatenate([t[n].reshape(-1) for n in REPLICATED]), 8)
    d_, m_, v_ = _adamw(cat(wts), cat(gshard), cat(mom), cat(var), "adamw_replicated")
    off = 0
    for n, sz in zip(REPLICATED, rep_sizes):
        shp = wts[n].shape
        delta[n], new_m[n], new_v[n] = (t.reshape(-1)[off:off + sz].reshape(shp) for t in (d_, m_, v_))
        off += sz
    assert off == n_rep

    own = lax.dynamic_index_in_dim(dw_in, me, 0, keepdims=True)
    recv["w_in"] = lax.dynamic_update_slice_in_dim(in_land, own, me, 0)
    adamw_cut("w_in")

    loss = lax.psum(loss_part[0, 0], ("x", "y", "c"))
    lead = lambda t: t[None]
    return (loss, grad_x[None], *[lead(gshard[n]) for n in WEIGHTS], *[lead(delta[n]) for n in WEIGHTS],
            *[lead(new_m[n]) for n in WEIGHTS], *[lead(new_v[n]) for n in WEIGHTS])
```

```python
import functools

import jax
import jax.numpy as jnp
from jax import lax
from jax.experimental import pallas as pl
from jax.experimental.pallas import tpu as pltpu

F32 = jnp.float32
BF16 = jnp.bfloat16
MESH = pl.DeviceIdType.MESH

EPS = 1e-6
HEAD_DIM = 128
POOL_WINDOWS = (2, 4, 8, 16)
POOL_HALO = 16
CONV_HALO = 8
GELU_C0 = 0.7978845608028654
GELU_C1 = 0.044715
ADAM_LR = 0.001
ADAM_B1 = 0.9
ADAM_B2 = 0.999
ADAM_EPS = 1e-08
ADAM_WD = 0.01
ADAM_STEP = 10
N_DEV = 8
PACK_W = 1024
VMEM_LIMIT = 56 * 2**20
ANY = pl.BlockSpec(memory_space=pl.ANY)


def _params(*sem):
    return pltpu.CompilerParams(dimension_semantics=sem, vmem_limit_bytes=VMEM_LIMIT)


def _sds(shape, dtype):
    return jax.ShapeDtypeStruct(shape, dtype)


def _tile(dim, target, align):
    if dim <= target:
        return dim
    t = (target // align) * align
    while t >= align:
        if dim % t == 0:
            return t
        t -= align
    return dim


def _sigmoid(x):
    return 1.0 / (1.0 + jnp.exp(-x))


class _Comm:
    def __init__(self, ins, outs, sems, start, finish):
        self.ins, self.outs, self.sems, self.start, self.finish = list(ins), list(outs), list(sems), start, finish


def _call(body, *, name, grid, in_specs, out_specs, out_shape, args, scratch=(), sem=(), comm=None, after=()):
    in_specs, out_specs, out_shape, scratch = list(in_specs), list(out_specs), list(out_shape), list(scratch)
    if after:
        assert comm is None
        n_in = len(in_specs)
        return pl.pallas_call(lambda *refs: body(*refs[:n_in], *refs[n_in + len(after):]), name=name, grid=grid,
                              in_specs=in_specs + [ANY] * len(after), out_specs=out_specs, out_shape=out_shape,
                              scratch_shapes=scratch, compiler_params=_params(*sem))(*args, *after)
    if comm is None:
        return pl.pallas_call(body, name=name, grid=grid, in_specs=in_specs, out_specs=out_specs, out_shape=out_shape,
                              scratch_shapes=scratch, compiler_params=_params(*sem))(*args)
    n_in, n_out, n_scr, n_ci, n_co = len(in_specs), len(out_specs), len(scratch), len(comm.ins), len(comm.outs)

    def wrapped(*refs):
        ins, refs = refs[:n_in], refs[n_in:]
        c_ins, refs = refs[:n_ci], refs[n_ci:]
        outs, refs = refs[:n_out], refs[n_out:]
        c_outs, refs = refs[:n_co], refs[n_co:]
        scr, c_sems = refs[:n_scr], refs[n_scr:]
        first = last = None
        for axis, size in enumerate(grid):
            at_start, at_end = pl.program_id(axis) == 0, pl.program_id(axis) == size - 1
            first = at_start if first is None else jnp.logical_and(first, at_start)
            last = at_end if last is None else jnp.logical_and(last, at_end)
        if grid:
            pl.when(first)(lambda: comm.start(c_ins, c_outs, c_sems))
            body(*ins, *outs, *scr)
            pl.when(last)(lambda: comm.finish(c_ins, c_outs, c_sems))
        else:
            comm.start(c_ins, c_outs, c_sems)
            body(*ins, *outs, *scr)
            comm.finish(c_ins, c_outs, c_sems)

    return pl.pallas_call(
        wrapped, name=name, grid=grid, in_specs=in_specs + [ANY] * n_ci, out_specs=out_specs + [ANY] * n_co,
        out_shape=out_shape + comm.outs, scratch_shapes=scratch + comm.sems,
        compiler_params=pltpu.CompilerParams(dimension_semantics=("arbitrary",) * len(grid), vmem_limit_bytes=VMEM_LIMIT,
                                             has_side_effects=True),
    )(*args, *comm.ins)


def _mm(a, b, *, ta=False, tb=False, b_sm=False, out_sm=False, out_dtype=F32, name, tm=2048, tn=1024, tk=1024, comm=None,
        after=()):
    M, K = (a.shape[1], a.shape[0]) if ta else a.shape
    if b_sm:
        n_sl, rows, per = b.shape
        N = rows if tb else n_sl * per
        assert K == (n_sl * per if tb else rows)
    else:
        N = b.shape[0] if tb else b.shape[1]
    tm = _tile(M, tm, 128)
    tn = _tile(per if (b_sm and not tb) else N // N_DEV if out_sm else N, tn, 128)
    tk = _tile(per if (b_sm and tb) else K, tk, 128)
    nk = K // tk
    a_spec = pl.BlockSpec((tk, tm), lambda i, j, k: (k, i)) if ta else pl.BlockSpec((tm, tk), lambda i, j, k: (i, k))
    if not b_sm:
        b_spec = pl.BlockSpec((tn, tk), lambda i, j, k: (j, k)) if tb else pl.BlockSpec((tk, tn), lambda i, j, k: (k, j))
    elif tb:
        kp = per // tk
        b_spec = pl.BlockSpec((None, tn, tk), lambda i, j, k: (k // kp, j, k % kp))
    else:
        jp = per // tn
        b_spec = pl.BlockSpec((None, tk, tn), lambda i, j, k: (j // jp, k, j % jp))
    if out_sm:
        jo = (N // N_DEV) // tn
        o_spec = pl.BlockSpec((None, tm, tn), lambda i, j, k: (j // jo, i, j % jo))
        o_shape = _sds((N_DEV, M, N // N_DEV), out_dtype)
    else:
        o_spec = pl.BlockSpec((tm, tn), lambda i, j, k: (i, j))
        o_shape = _sds((M, N), out_dtype)
    dims = (((0 if ta else 1,), (1 if tb else 0,)), ((), ()))

    def product(a_ref, b_ref):
        return lax.dot_general(a_ref[...].astype(BF16), b_ref[...].astype(BF16), dims, preferred_element_type=F32)

    def body(a_ref, b_ref, o_ref, acc_ref):
        k = pl.program_id(2)

        @pl.when(k == 0)
        def _():
            acc_ref[...] = jnp.zeros_like(acc_ref)

        acc_ref[...] += product(a_ref, b_ref)

        @pl.when(k == nk - 1)
        def _():
            o_ref[...] = acc_ref[...].astype(o_ref.dtype)

    def body_one_step(a_ref, b_ref, o_ref):
        o_ref[...] = product(a_ref, b_ref).astype(o_ref.dtype)

    res = _call(body if nk > 1 else body_one_step, name=name, grid=(M // tm, N // tn, nk), in_specs=[a_spec, b_spec],
                out_specs=[o_spec], out_shape=[o_shape], args=(a, b), scratch=[pltpu.VMEM((tm, tn), F32)] if nk > 1 else [],
                sem=("parallel", "parallel", "arbitrary"), comm=comm, after=after)
    return res[0] if comm is None else res


def _pool_group_fwd(pooled, w_pg, scale):
    S = pooled.shape[0]
    G, C, C2 = w_pg.shape
    tm = _tile(S, 1024, 16)

    def body(a_ref, w_ref, s_ref, pg_ref, ps_ref):
        pg = jnp.dot(a_ref[...], w_ref[...], preferred_element_type=F32)
        pg_ref[...] = pg
        ps_ref[...] = (pg * s_ref[...]).astype(BF16)

    return pl.pallas_call(
        body, name="pool_group_fwd", grid=(G, S // tm),
        in_specs=[pl.BlockSpec((tm, C), lambda g, i: (i, g)), pl.BlockSpec((None, C, C2), lambda g, i: (g, 0, 0)),
                  pl.BlockSpec((1, C2), lambda g, i: (0, g))],
        out_specs=[pl.BlockSpec((tm, C2), lambda g, i: (i, g)), pl.BlockSpec((tm, C2), lambda g, i: (i, g))],
        out_shape=[jax.ShapeDtypeStruct((S, G * C2), F32), jax.ShapeDtypeStruct((S, G * C2), BF16)],
        compiler_params=_params("parallel", "parallel"),
    )(pooled, w_pg, scale)


def _pool_group_bwd_x(dpg, w_pg):
    S = dpg.shape[0]
    G, C, C2 = w_pg.shape
    tm = _tile(S, 1024, 16)

    def body(d_ref, w_ref, o_ref):
        o_ref[...] = lax.dot_general(d_ref[...], w_ref[...], (((1,), (1,)), ((), ())), preferred_element_type=F32)

    return pl.pallas_call(
        body, name="pool_group_bwd_x", grid=(G, S // tm),
        in_specs=[pl.BlockSpec((tm, C2), lambda g, i: (i, g)), pl.BlockSpec((None, C, C2), lambda g, i: (g, 0, 0))],
        out_specs=pl.BlockSpec((tm, C), lambda g, i: (i, g)), out_shape=jax.ShapeDtypeStruct((S, G * C), F32),
        compiler_params=_params("parallel", "parallel"),
    )(dpg, w_pg)


def _pool_group_bwd_w(pooled, dpg, G):
    S = pooled.shape[0]
    C, C2 = pooled.shape[1] // G, dpg.shape[1] // G
    tk = _tile(S, 1024, 16)

    def body(a_ref, d_ref, o_ref):
        @pl.when(pl.program_id(1) == 0)
        def _():
            o_ref[...] = jnp.zeros_like(o_ref)

        o_ref[...] += lax.dot_general(a_ref[...], d_ref[...], (((0,), (0,)), ((), ())), preferred_element_type=F32)

    return pl.pallas_call(
        body, name="pool_group_bwd_w", grid=(G, S // tk),
        in_specs=[pl.BlockSpec((tk, C), lambda g, k: (k, g)), pl.BlockSpec((tk, C2), lambda g, k: (k, g))],
        out_specs=pl.BlockSpec((None, C, C2), lambda g, k: (g, 0, 0)), out_shape=jax.ShapeDtypeStruct((G, C, C2), F32),
        compiler_params=_params("parallel", "arbitrary"),
    )(pooled, dpg)


def _rms(x, gain):
    r = lax.rsqrt(jnp.mean(x * x, axis=-1, keepdims=True) + EPS)
    return x * r * gain


def _rms_bwd(x, gain, dy):
    r = lax.rsqrt(jnp.mean(x * x, axis=-1, keepdims=True) + EPS)
    xh = x * r
    dgain = jnp.sum(dy * xh, axis=0, keepdims=True)
    dxh = dy * gain
    dx = r * (dxh - xh * jnp.mean(dxh * xh, axis=-1, keepdims=True))
    return dx, dgain


def _row_call(body, name, ins, outs, tr, *, comm=None):
    S = None
    in_specs, args = [], []
    for it in ins:
        arr, kind = it[0], it[1]
        if kind == "row":
            S = arr.shape[0]
            if len(it) == 4:
                width, cb = it[2], it[3]
                in_specs.append(pl.BlockSpec((tr, width), functools.partial(lambda i, cb: (i, cb), cb=cb)))
            else:
                in_specs.append(pl.BlockSpec((tr, arr.shape[1]), lambda i: (i, 0)))
        else:
            assert arr.ndim == 2
            in_specs.append(pl.BlockSpec(arr.shape, lambda i: (0, 0)))
        args.append(arr)
    out_specs, out_shape = [], []
    for sds, kind in outs:
        if kind == "row":
            out_specs.append(pl.BlockSpec((tr, sds.shape[1]), lambda i: (i, 0)))
        else:
            assert len(sds.shape) == 2
            out_specs.append(pl.BlockSpec(sds.shape, lambda i: (0, 0)))
        out_shape.append(sds)
    return _call(body, name=name, grid=(S // tr,), in_specs=in_specs, out_specs=out_specs, out_shape=out_shape, args=args,
                 sem=("arbitrary",), comm=comm)


def _first_step_zero(*refs):
    @pl.when(pl.program_id(0) == 0)
    def _():
        for r in refs:
            r[...] = jnp.zeros_like(r)


def _rms_fwd(x, gain, tr, comm):
    def body(x_ref, g_ref, o_ref):
        o_ref[...] = _rms(x_ref[...], g_ref[...]).astype(BF16)

    S, D = x.shape
    return _row_call(body, "rms_fwd", [(x, "row"), (gain, "full")], [(_sds((S, D), BF16), "row")], tr, comm=comm)


def _gate_mix(proj, ya, yp, gate_cb, tr):
    S, D = ya.shape

    def body(ga_ref, gp_ref, ya_ref, yp_ref, o_ref):
        o_ref[...] = (_sigmoid(ga_ref[...]) * ya_ref[...] + _sigmoid(gp_ref[...]) * yp_ref[...]).astype(BF16)

    return _row_call(body, "gate_mix", [(proj, "row", D, gate_cb), (proj, "row", D, gate_cb + 1), (ya, "row"), (yp, "row")],
                     [(_sds((S, D), BF16), "row")], tr)[0]


def _resid_rms2(x, mo, g2, g3, tr):
    S, D = x.shape

    def body(x_ref, mo_ref, g2_ref, g3_ref, x1_ref, h2_ref):
        x1 = x_ref[...] + _rms(mo_ref[...], g2_ref[...])
        x1_ref[...] = x1
        h2_ref[...] = _rms(x1, g3_ref[...]).astype(BF16)

    return _row_call(body, "resid_rms2", [(x, "row"), (mo, "row"), (g2, "full"), (g3, "full")],
                     [(_sds((S, D), F32), "row"), (_sds((S, D), BF16), "row")], tr)


def _resid_rms(x1, yf, g4, tr):
    S, D = x1.shape

    def body(x_ref, y_ref, g_ref, o_ref, ob_ref):
        x2 = x_ref[...] + _rms(y_ref[...], g_ref[...])
        o_ref[...] = x2
        ob_ref[...] = x2.astype(BF16)

    return _row_call(body, "resid_rms", [(x1, "row"), (yf, "row"), (g4, "full")],
                     [(_sds((S, D), F32), "row"), (_sds((S, D), BF16), "row")], tr)


def _ple_loss(gl, e, x2, tgt, g5, tr):
    S, D = x2.shape

    def body(gl_ref, e_ref, x2_ref, t_ref, g_ref, loss_ref, dx3_ref, de_ref, dgl_ref, dg_ref):
        _first_step_zero(loss_ref, dg_ref)
        s = _sigmoid(gl_ref[...])
        e_ = e_ref[...]
        t = s * e_
        gain = g_ref[...]
        err = x2_ref[...] + _rms(t, gain) - t_ref[...]
        row_loss = jnp.mean(err * err, axis=-1, keepdims=True)
        loss_ref[...] += 0.5 * jnp.sum(row_loss, axis=0, keepdims=True)
        dx3 = err * (1.0 / D)
        dx3_ref[...] = dx3
        dt, dgain = _rms_bwd(t, gain, dx3)
        dg_ref[...] += dgain
        de_ref[...] = (dt * s).astype(BF16)
        dgl_ref[...] = (dt * e_ * s * (1.0 - s)).astype(BF16)

    return _row_call(body, "ple_loss", [(gl, "row"), (e, "row"), (x2, "row"), (tgt, "row"), (g5, "full")],
                     [(_sds((1, 1), F32), "acc"), (_sds((S, D), F32), "row"), (_sds((S, D), BF16), "row"),
                      (_sds((S, D), BF16), "row"), (_sds((1, D), F32), "acc")], tr)


def _rms_bwd_a(dx3, dx2g, yf, g4, tr):
    S, D = yf.shape

    def body(a_ref, b_ref, y_ref, g_ref, dx_ref, dy_ref, dg_ref):
        _first_step_zero(dg_ref)
        dx2 = a_ref[...] + b_ref[...]
        dx_ref[...] = dx2
        dy, dgain = _rms_bwd(y_ref[...], g_ref[...], dx2)
        dy_ref[...] = dy.astype(BF16)
        dg_ref[...] += dgain

    return _row_call(body, "rms_bwd_a", [(dx3, "row"), (dx2g, "row"), (yf, "row"), (g4, "full")],
                     [(_sds((S, D), F32), "row"), (_sds((S, D), BF16), "row"), (_sds((1, D), F32), "acc")], tr)


def _rms_bwd_b(dx2, dh2, x1, g3, mo, g2, tr):
    S, D = x1.shape

    def body(dx2_ref, dh2_ref, x1_ref, g3_ref, mo_ref, g2_ref, dx1_ref, dmo_ref, dg3_ref, dg2_ref):
        _first_step_zero(dg3_ref, dg2_ref)
        d, dgain3 = _rms_bwd(x1_ref[...], g3_ref[...], dh2_ref[...])
        dx1 = dx2_ref[...] + d
        dx1_ref[...] = dx1
        dg3_ref[...] += dgain3
        dmo, dgain2 = _rms_bwd(mo_ref[...], g2_ref[...], dx1)
        dmo_ref[...] = dmo.astype(BF16)
        dg2_ref[...] += dgain2

    return _row_call(body, "rms_bwd_b", [(dx2, "row"), (dh2, "row"), (x1, "row"), (g3, "full"), (mo, "row"), (g2, "full")],
                     [(_sds((S, D), F32), "row"), (_sds((S, D), BF16), "row"), (_sds((1, D), F32), "acc"),
                      (_sds((1, D), F32), "acc")], tr)


def _rms_bwd_c(dx1, dh, x, g1, tr):
    S, D = x.shape

    def body(dx1_ref, dh_ref, x_ref, g_ref, o_ref, dg_ref):
        _first_step_zero(dg_ref)
        d, dgain = _rms_bwd(x_ref[...], g_ref[...], dh_ref[...])
        o_ref[...] = dx1_ref[...] + d
        dg_ref[...] += dgain

    return _row_call(body, "rms_bwd_c", [(dx1, "row"), (dh, "row"), (x, "row"), (g1, "full")],
                     [(_sds((S, D), F32), "row"), (_sds((1, D), F32), "acc")], tr)


def _gate_bwd(dmixed, proj, ya, yp, gate_cb, tr):
    S, D = ya.shape

    def body(dm_ref, ga_ref, gp_ref, ya_ref, yp_ref, dya_ref, dyp_ref, dga_ref, dgp_ref):
        dm = dm_ref[...]
        sa = _sigmoid(ga_ref[...])
        sp = _sigmoid(gp_ref[...])
        dya_ref[...] = (dm * sa).astype(BF16)
        dyp_ref[...] = (dm * sp).astype(BF16)
        dga_ref[...] = (dm * ya_ref[...] * sa * (1.0 - sa)).astype(BF16)
        dgp_ref[...] = (dm * yp_ref[...] * sp * (1.0 - sp)).astype(BF16)

    return _row_call(body, "gate_bwd",
                     [(dmixed, "row"), (proj, "row", D, gate_cb), (proj, "row", D, gate_cb + 1), (ya, "row"), (yp, "row")],
                     [(_sds((S, D), BF16), "row")] * 4, tr)


def _scale_bwd(dps, pg, scale, tr):
    S, W = dps.shape

    def body(d_ref, pg_ref, s_ref, o_ref, ds_ref):
        _first_step_zero(ds_ref)
        d = d_ref[...]
        o_ref[...] = (d * s_ref[...]).astype(BF16)
        ds_ref[...] += jnp.sum(d * pg_ref[...], axis=0, keepdims=True)

    return _row_call(body, "scale_bwd", [(dps, "row"), (pg, "row"), (scale, "full")],
                     [(_sds((S, W), BF16), "row"), (_sds((1, W), F32), "acc")], tr)


CUMSUM_TERMS = 2
GRAD_TERMS = 1


def _tri(blk, cmp, terms=CUMSUM_TERMS):
    j = lax.broadcasted_iota(jnp.int32, (blk, blk), 0)
    s = lax.broadcasted_iota(jnp.int32, (blk, blk), 1)
    one = jnp.concatenate([cmp(j, s).astype(BF16), jnp.ones((blk, 128), BF16)], axis=1)
    return jnp.concatenate([one] * terms, axis=0)


def _split(x, terms=CUMSUM_TERMS):
    parts = []
    for _ in range(terms):
        t = x.astype(BF16)
        parts.append(t)
        x = x - t.astype(F32)
    return parts[0] if terms == 1 else jnp.concatenate(parts, axis=1)


def _causal(blk):
    return lax.broadcasted_iota(jnp.int32, (blk, blk), 1) < lax.broadcasted_iota(jnp.int32, (blk, blk), 0)


def _scores(q, kj, scale, causal):
    z = lax.dot_general(q, kj, (((1,), (1,)), ((), ())), preferred_element_type=F32) * scale
    l1p = jnp.log(1.0 + jnp.exp(-jnp.abs(z)))
    lb = -(jnp.maximum(z, 0.0) + l1p)
    if causal is not None:
        lb = jnp.where(causal, lb, 0.0)
    return z, lb, jnp.minimum(z, 0.0) - l1p


def _attn_fwd(proj, n_heads, blk, comm=None):
    S = proj.shape[0]
    nq = S // blk
    scale = HEAD_DIM ** -0.5
    lanes = blk // 128
    hp = 2 if n_heads % 2 == 0 else 1
    cols = [slice(h * HEAD_DIM, (h + 1) * HEAD_DIM) for h in range(hp)]
    u_incl = _tri(blk, lambda j, s: j >= s)

    def body(q_ref, k_ref, v_ref, u_ref, o_ref, a_ref, b_ref, z_buf, hl_buf):
        i = pl.program_id(1)
        qs = [q_ref[:, c].astype(BF16) for c in cols]
        u = u_ref[...]

        def scores(j, h, causal):
            kj = k_ref[pl.ds(pl.multiple_of(j * blk, blk), blk), cols[h]].astype(BF16)
            z, lb, log_beta = _scores(qs[h], kj, scale, causal)
            b_ref[h, j] = jnp.exp(log_beta).astype(BF16)
            return z, _split(lb)

        def weigh(j, h, z, hl, acc, run, causal):
            vj = v_ref[pl.ds(pl.multiple_of(j * blk, blk), blk), cols[h]].astype(BF16)
            ct = jnp.dot(hl, u, preferred_element_type=F32)
            a = jnp.exp(z + ct[:, :blk] + jnp.tile(run, (1, lanes)))
            if causal is not None:
                a = jnp.where(causal, a, 0.0)
            a = a.astype(BF16)
            a_ref[h, j] = a
            return acc + jnp.dot(a, vj, preferred_element_type=F32), run + ct[:, blk:]

        def stage_scores(j):
            for h in range(hp):
                z_buf[h], hl_buf[h] = scores(j, h, None)

        zero = jnp.zeros((blk, HEAD_DIM), F32)
        causal = _causal(blk)
        carry = tuple(weigh(i, h, *scores(i, h, causal), zero, zero, causal) for h in range(hp))
        stage_scores(jnp.maximum(i - 1, 0))

        def step(t, carry):
            j = i - 1 - t
            out = tuple(weigh(j, h, z_buf[h], hl_buf[h], *carry[h], None) for h in range(hp))
            stage_scores(j - 1)
            return out

        carry = lax.fori_loop(0, jnp.maximum(i - 1, 0), step, carry)
        carry = lax.fori_loop(0, jnp.minimum(i, 1),
                              lambda t, c: tuple(weigh(t, h, z_buf[h], hl_buf[h], *c[h], None) for h in range(hp)), carry)
        for h in range(hp):
            o_ref[:, cols[h]] = carry[h][0].astype(BF16)

    G = n_heads // hp
    W = hp * HEAD_DIM
    saved = _sds((n_heads, nq, nq, blk, blk), BF16)
    saved_spec = pl.BlockSpec((hp, None, nq, blk, blk), lambda h, i: (h, i, 0, 0, 0))
    return _call(
        body, name="attn_fwd", grid=(G, nq),
        in_specs=[pl.BlockSpec((blk, W), lambda h, i: (i, h)),
                  pl.BlockSpec((S, W), lambda h, i: (0, G + h)),
                  pl.BlockSpec((S, W), lambda h, i: (0, 2 * G + h)),
                  pl.BlockSpec(u_incl.shape, lambda h, i: (0, 0))],
        out_specs=[pl.BlockSpec((blk, W), lambda h, i: (i, h)), saved_spec, saved_spec],
        out_shape=[_sds((S, n_heads * HEAD_DIM), BF16), saved, saved],
        args=(proj, proj, proj, u_incl), scratch=[pltpu.VMEM((hp, blk, blk), F32), pltpu.VMEM((hp, blk, CUMSUM_TERMS * blk), BF16)],
        sem=("parallel", "arbitrary"), comm=comm)


def _attn_bwd(proj, a_saved, b_saved, do, n_heads, blk, comm=None):
    S = proj.shape[0]
    nq = S // blk
    scale = HEAD_DIM ** -0.5
    lanes = blk // 128
    hp = 2 if n_heads % 2 == 0 else 1
    cols = [slice(h * HEAD_DIM, (h + 1) * HEAD_DIM) for h in range(hp)]
    l_incl = _tri(blk, lambda j, s: j <= s, GRAD_TERMS)

    def body(q_ref, k_ref, v_ref, a_ref, b_ref, do_ref, li_ref, dq_ref, dk_ref, dv_ref, dk_acc, dv_acc, g_buf, gl_buf):
        i = pl.program_id(1)

        @pl.when(i == 0)
        def _():
            dk_acc[...] = jnp.zeros_like(dk_acc)
            dv_acc[...] = jnp.zeros_like(dv_acc)

        qs = [q_ref[:, c].astype(BF16) for c in cols]
        dobs = [do_ref[:, c].astype(BF16) for c in cols]
        li = li_ref[...]

        def stage_products(j):
            ks = pl.multiple_of(j * blk, blk)
            for h in range(hp):
                vj = v_ref[pl.ds(ks, blk), cols[h]].astype(BF16)
                da = lax.dot_general(dobs[h], vj, (((1,), (1,)), ((), ())), preferred_element_type=F32)
                g = a_ref[h, j].astype(F32) * da
                g_buf[h] = g
                gl_buf[h] = _split(g, GRAD_TERMS)

        def grads(j, h, dq, run_g, causal):
            ks = pl.multiple_of(j * blk, blk)
            kj = k_ref[pl.ds(ks, blk), cols[h]].astype(BF16)
            gt = jnp.dot(gl_buf[h], li, preferred_element_type=F32)
            dz = g_buf[h] - b_ref[h, j].astype(F32) * (gt[:, :blk] + jnp.tile(run_g, (1, lanes)))
            if causal is not None:
                dz = jnp.where(causal, dz, 0.0)
            dzs = (dz * scale).astype(BF16)
            dk_acc[pl.ds(ks, blk), cols[h]] += lax.dot_general(dzs, qs[h], (((0,), (0,)), ((), ())),
                                                               preferred_element_type=F32)
            dv_acc[pl.ds(ks, blk), cols[h]] += lax.dot_general(a_ref[h, j], dobs[h], (((0,), (0,)), ((), ())),
                                                               preferred_element_type=F32)
            return dq + jnp.dot(dzs, kj, preferred_element_type=F32), run_g + gt[:, blk:]

        zero = jnp.zeros((blk, HEAD_DIM), F32)
        stage_products(0)

        def step(j, carry):
            out = tuple(grads(j, h, *carry[h], None) for h in range(hp))
            stage_products(j + 1)
            return out

        carry = lax.fori_loop(0, i, step, ((zero, zero),) * hp)
        causal = _causal(blk)
        carry = tuple(grads(i, h, *carry[h], causal) for h in range(hp))
        for h in range(hp):
            dq_ref[:, cols[h]] = carry[h][0].astype(BF16)

        @pl.when(i == nq - 1)
        def _():
            dk_ref[...] = dk_acc[...].astype(BF16)
            dv_ref[...] = dv_acc[...].astype(BF16)

    G = n_heads // hp
    W = hp * HEAD_DIM
    AW = n_heads * HEAD_DIM
    saved_spec = pl.BlockSpec((hp, None, nq, blk, blk), lambda h, i: (h, i, 0, 0, 0))
    return _call(
        body, name="attn_bwd", grid=(G, nq),
        in_specs=[pl.BlockSpec((blk, W), lambda h, i: (i, h)),
                  pl.BlockSpec((S, W), lambda h, i: (0, G + h)),
                  pl.BlockSpec((S, W), lambda h, i: (0, 2 * G + h)),
                  saved_spec, saved_spec,
                  pl.BlockSpec((blk, W), lambda h, i: (i, h)),
                  pl.BlockSpec(l_incl.shape, lambda h, i: (0, 0))],
        out_specs=[pl.BlockSpec((blk, W), lambda h, i: (i, h)),
                   pl.BlockSpec((S, W), lambda h, i: (0, h)),
                   pl.BlockSpec((S, W), lambda h, i: (0, h))],
        out_shape=[_sds((S, AW), BF16)] * 3, args=(proj, proj, proj, a_saved, b_saved, do, l_incl),
        scratch=[pltpu.VMEM((S, W), F32)] * 2 + [pltpu.VMEM((hp, blk, blk), F32),
                                                 pltpu.VMEM((hp, blk, GRAD_TERMS * blk), BF16)],
        sem=("parallel", "arbitrary"), comm=comm)


def _pool_count(r0, rows, w):
    t = r0 + lax.broadcasted_iota(jnp.int32, (rows, 1), 0)
    return jnp.minimum(t + 1, w).astype(F32)


def _pool_fwd(proj, col_blk, n_groups, width, chunk):
    S = proj.shape[0]
    H = POOL_HALO

    def body(u_ref, o_ref, pad_ref):
        g = pl.program_id(0)
        pad_ref[0:H, :] = jnp.zeros((H, width), F32)
        pad_ref[H:, :] = u_ref[...]
        for gi, w in enumerate(POOL_WINDOWS[:n_groups]):
            @pl.when(g == gi)
            def _(w=w):
                def one(c, _):
                    r0 = pl.multiple_of(c * chunk, chunk)
                    ext = pad_ref[pl.ds(r0, chunk + H), :]
                    s = ext
                    k = 1
                    while k < w:
                        s = s + pltpu.roll(s, k, 0)
                        k *= 2
                    o_ref[pl.ds(r0, chunk), :] = (s[H:] / _pool_count(r0, chunk, w) - ext[H:]).astype(BF16)
                    return 0

                lax.fori_loop(0, S // chunk, one, 0)

    return pl.pallas_call(
        body, name="pool_fwd", grid=(n_groups,),
        in_specs=[pl.BlockSpec((S, width), lambda g: (0, col_blk + g))],
        out_specs=pl.BlockSpec((S, width), lambda g: (0, g)), out_shape=_sds((S, n_groups * width), BF16),
        scratch_shapes=[pltpu.VMEM((S + H, width), F32)], compiler_params=_params("parallel"),
    )(proj)


def _pool_bwd(dpooled, n_groups, chunk):
    S = dpooled.shape[0]
    width = dpooled.shape[1] // n_groups
    H = POOL_HALO

    def body(d_ref, o_ref, pad_ref):
        g = pl.program_id(0)
        pad_ref[S:, :] = jnp.zeros((H, width), F32)
        for gi, w in enumerate(POOL_WINDOWS[:n_groups]):
            @pl.when(g == gi)
            def _(w=w):
                def fill(c, _):
                    r0 = pl.multiple_of(c * chunk, chunk)
                    pad_ref[pl.ds(r0, chunk), :] = d_ref[pl.ds(r0, chunk), :] / _pool_count(r0, chunk, w)
                    return 0

                lax.fori_loop(0, S // chunk, fill, 0)

                def one(c, _):
                    r0 = pl.multiple_of(c * chunk, chunk)
                    s = pad_ref[pl.ds(r0, chunk + H), :]
                    k = 1
                    while k < w:
                        s = s + pltpu.roll(s, chunk + H - k, 0)
                        k *= 2
                    o_ref[pl.ds(r0, chunk), :] = (s[:chunk] - d_ref[pl.ds(r0, chunk), :]).astype(BF16)
                    return 0

                lax.fori_loop(0, S // chunk, one, 0)

    return pl.pallas_call(
        body, name="pool_bwd", grid=(n_groups,),
        in_specs=[pl.BlockSpec((S, width), lambda g: (0, g))],
        out_specs=pl.BlockSpec((S, width), lambda g: (0, g)), out_shape=_sds((S, n_groups * width), BF16),
        scratch_shapes=[pltpu.VMEM((S + H, width), F32)], compiler_params=_params("parallel"),
    )(dpooled)


def _conv3(x_ext, w, b, shifted=None):
    x2, x1 = shifted if shifted is not None else (pltpu.roll(x_ext, 2, 0), pltpu.roll(x_ext, 1, 0))
    return b + x2 * w[0:1, :] + x1 * w[1:2, :] + x_ext * w[2:3, :]


def _gelu_parts(x):
    th = jnp.tanh(GELU_C0 * (x + GELU_C1 * (x * x * x)))
    return th, 0.5 * (1.0 + th)


def _conv_specs(S, F, cb):
    nb = F // cb
    return [pl.BlockSpec((S, cb), lambda j: (0, j)), pl.BlockSpec((S, cb), lambda j: (0, nb + j)),
            pl.BlockSpec((3, cb), lambda j: (0, j)), pl.BlockSpec((3, cb), lambda j: (0, nb + j)),
            pl.BlockSpec((1, cb), lambda j: (0, j)), pl.BlockSpec((1, cb), lambda j: (0, nb + j))]


def _conv_fwd(upre, cw, cb_, chunk):
    S, F2 = upre.shape
    F = F2 // 2
    cb = 128
    H = CONV_HALO

    def body(g_ref, v_ref, wg_ref, wv_ref, bg_ref, bv_ref, o_ref, pg_ref, pv_ref):
        pg_ref[0:H, :] = jnp.zeros((H, cb), F32)
        pv_ref[0:H, :] = jnp.zeros((H, cb), F32)
        pg_ref[H:, :] = g_ref[...]
        pv_ref[H:, :] = v_ref[...]
        wg, wv, bg, bv = wg_ref[...], wv_ref[...], bg_ref[...], bv_ref[...]

        def one(c, _):
            r0 = pl.multiple_of(c * chunk, chunk)
            up_g = _conv3(pg_ref[pl.ds(r0, chunk + H), :], wg, bg)[H:]
            up_v = _conv3(pv_ref[pl.ds(r0, chunk + H), :], wv, bv)[H:]
            _, cdf = _gelu_parts(up_g)
            o_ref[pl.ds(r0, chunk), :] = (up_g * cdf * up_v).astype(BF16)
            return 0

        lax.fori_loop(0, S // chunk, one, 0)

    return pl.pallas_call(
        body, name="conv_fwd", grid=(F // cb,), in_specs=_conv_specs(S, F, cb),
        out_specs=pl.BlockSpec((S, cb), lambda j: (0, j)), out_shape=_sds((S, F), BF16),
        scratch_shapes=[pltpu.VMEM((S + H, cb), F32)] * 2, compiler_params=_params("parallel"),
    )(upre, upre, cw, cw, cb_, cb_)


def _conv_bwd(upre, dact, cw, cb_, chunk, comm=None):
    S, F2 = upre.shape
    F = F2 // 2
    cb = 128
    H = CONV_HALO
    E = chunk + 2 * H

    def body(g_ref, v_ref, wg_ref, wv_ref, bg_ref, bv_ref, d_ref, dg_ref, dv_ref, dwg_ref, dwv_ref, dbg_ref, dbv_ref,
             pg_ref, pv_ref, pd_ref):
        for p, src in ((pg_ref, g_ref), (pv_ref, v_ref), (pd_ref, d_ref)):
            p[0:H, :] = jnp.zeros((H, cb), F32)
            p[H:S + H, :] = src[...]
            p[S + H:, :] = jnp.zeros((H, cb), F32)
        wg, wv, bg, bv = wg_ref[...], wv_ref[...], bg_ref[...], bv_ref[...]

        def taps_bwd(d, w):
            return d * w[2:3, :] + pltpu.roll(d, E - 1, 0) * w[1:2, :] + pltpu.roll(d, E - 2, 0) * w[0:1, :]

        def wsum(d, x, x2, x1):
            dc = d[H:H + chunk]
            return [jnp.sum(dc * x2[H:H + chunk], axis=0, keepdims=True),
                    jnp.sum(dc * x1[H:H + chunk], axis=0, keepdims=True),
                    jnp.sum(dc * x[H:H + chunk], axis=0, keepdims=True),
                    jnp.sum(dc, axis=0, keepdims=True)]

        def one(c, acc):
            r0 = pl.multiple_of(c * chunk, chunk)
            xg = pg_ref[pl.ds(r0, E), :]
            xv = pv_ref[pl.ds(r0, E), :]
            d = pd_ref[pl.ds(r0, E), :]
            sg = (pltpu.roll(xg, 2, 0), pltpu.roll(xg, 1, 0))
            sv = (pltpu.roll(xv, 2, 0), pltpu.roll(xv, 1, 0))
            up_g = _conv3(xg, wg, bg, sg)
            up_v = _conv3(xv, wv, bv, sv)
            th, cdf = _gelu_parts(up_g)
            dgelu = cdf + 0.5 * up_g * (1.0 - th * th) * (GELU_C0 * (1.0 + 3.0 * GELU_C1 * (up_g * up_g)))
            dgate = d * up_v * dgelu
            dval = d * (up_g * cdf)
            dg_ref[pl.ds(r0, chunk), :] = taps_bwd(dgate, wg)[H:H + chunk].astype(BF16)
            dv_ref[pl.ds(r0, chunk), :] = taps_bwd(dval, wv)[H:H + chunk].astype(BF16)
            return tuple(a + b for a, b in zip(acc, wsum(dgate, xg, *sg) + wsum(dval, xv, *sv)))

        zero = jnp.zeros((1, cb), F32)
        acc = lax.fori_loop(0, S // chunk, one, (zero,) * 8)
        dwg_ref[...] = jnp.concatenate(acc[0:3], axis=0)
        dbg_ref[...] = acc[3]
        dwv_ref[...] = jnp.concatenate(acc[4:7], axis=0)
        dbv_ref[...] = acc[7]

    col = lambda rows: pl.BlockSpec((rows, cb), lambda j: (0, j))
    return _call(
        body, name="conv_bwd", grid=(F // cb,), in_specs=_conv_specs(S, F, cb) + [col(S)],
        out_specs=[col(S), col(S), col(3), col(3), col(1), col(1)],
        out_shape=[_sds((S, F), BF16), _sds((S, F), BF16), _sds((3, F), F32), _sds((3, F), F32), _sds((1, F), F32),
                   _sds((1, F), F32)],
        args=(upre, upre, cw, cw, cb_, cb_, dact), scratch=[pltpu.VMEM((S + 2 * H, cb), F32)] * 3, sem=("parallel",),
        comm=comm)


def _position():
    x, y, c = lax.axis_index("x"), lax.axis_index("y"), lax.axis_index("c")
    return x, y, c, 4 * x + 2 * y + c


def _peer(x, y, c, d):
    px = 1 - x if d & 4 else x
    py = 1 - y if d & 2 else y
    pc = 1 - c if d & 1 else c
    return (px, py, pc), 4 * px + 2 * py + pc


def _all_gather_comm(tensors):
    nt = len(tensors)
    outs = [_sds((N_DEV,) + t.shape, t.dtype) for t in tensors]
    sems = [pltpu.SemaphoreType.DMA((7 * nt,)), pltpu.SemaphoreType.DMA((7 * nt,)), pltpu.SemaphoreType.DMA((nt,))]

    def parts(ins, outs_, sem_refs):
        send, recv, loc = sem_refs
        x, y, c, me = _position()
        chips = [(1 - x, y), (x, 1 - y), (1 - x, 1 - y)]

        def copy(t, k, block, to, src=None):
            slot = outs_[t].at[4 * block[0] + 2 * block[1] + block[2]]
            return pltpu.make_async_remote_copy(src_ref=slot if src is None else src, dst_ref=slot,
                                                send_sem=send.at[7 * t + k], recv_sem=recv.at[7 * t + k], device_id=to,
                                                device_id_type=MESH)

        def mine(t):
            return pltpu.make_async_copy(ins[t], outs_[t].at[me], loc.at[t])

        return (x, y, c), (x, y, 1 - c), chips, copy, mine

    def start(ins, outs_, sem_refs):
        me, sibling, chips, copy, mine = parts(ins, outs_, sem_refs)
        for t in range(nt):
            mine(t).start()
            copy(t, 0, me, sibling, src=ins[t]).start()
            for j, chip in enumerate(chips):
                copy(t, 1 + j, me, (*chip, me[2]), src=ins[t]).start()

    def finish(ins, outs_, sem_refs):
        me, sibling, chips, copy, mine = parts(ins, outs_, sem_refs)
        c = me[2]
        for t in range(nt):
            for j, chip in enumerate(chips):
                copy(t, 1 + j, (*chip, c), me).wait_recv()
                copy(t, 4 + j, (*chip, c), sibling).start()
        for t in range(nt):
            copy(t, 0, sibling, me).wait_recv()
            for j, chip in enumerate(chips):
                copy(t, 4 + j, (*chip, 1 - c), me).wait_recv()
        for t in range(nt):
            copy(t, 0, me, sibling, src=ins[t]).wait_send()
            for j, chip in enumerate(chips):
                copy(t, 1 + j, me, (*chip, c), src=ins[t]).wait_send()
                copy(t, 4 + j, (*chip, c), sibling).wait_send()
            mine(t).wait()

    return _Comm(tensors, outs, sems, start, finish)


def _reduce_scatter_comm(tensors):
    nt = len(tensors)
    outs = [_sds(t.shape, t.dtype) for t in tensors]
    sems = [pltpu.SemaphoreType.DMA((7 * nt,)), pltpu.SemaphoreType.DMA((7 * nt,)), pltpu.SemaphoreType.DMA((nt,))]

    def local(ins, outs_, sem_refs, t, me):
        return pltpu.make_async_copy(ins[t].at[me], outs_[t].at[me], sem_refs[2].at[t])

    def remote(ins, outs_, sem_refs, t, d, inbound):
        x, y, c, me = _position()
        peer, peer_idx = _peer(x, y, c, d)
        k = 7 * t + d - 1
        src, dst, to = (ins[t].at[me], outs_[t].at[peer_idx], (x, y, c)) if inbound else (ins[t].at[peer_idx], outs_[t].at[me], peer)
        return pltpu.make_async_remote_copy(src_ref=src, dst_ref=dst, send_sem=sem_refs[0].at[k], recv_sem=sem_refs[1].at[k],
                                            device_id=to, device_id_type=MESH)

    def start(ins, outs_, sem_refs):
        me = _position()[3]
        for t in range(nt):
            local(ins, outs_, sem_refs, t, me).start()
            for d in range(1, N_DEV):
                remote(ins, outs_, sem_refs, t, d, False).start()

    def finish(ins, outs_, sem_refs):
        me = _position()[3]
        for t in range(nt):
            for d in range(1, N_DEV):
                remote(ins, outs_, sem_refs, t, d, True).wait_recv()
        for t in range(nt):
            for d in range(1, N_DEV):
                remote(ins, outs_, sem_refs, t, d, False).wait_send()
            local(ins, outs_, sem_refs, t, me).wait()

    return _Comm(tensors, outs, sems, start, finish)


HBM_SPEC = pl.BlockSpec(memory_space=pltpu.HBM)
SEM_SPEC = pl.BlockSpec(memory_space=pltpu.SEMAPHORE)
DATAFLOW = pltpu.SideEffectType.DATAFLOW_SIDE_EFFECTING


def _scatter_copy(g_ref, land_ref, send_sems, recv_sems, d):
    x, y, c, me = _position()
    peer, peer_idx = _peer(x, y, c, d)
    return pltpu.make_async_remote_copy(src_ref=g_ref.at[peer_idx], dst_ref=land_ref.at[me], send_sem=send_sems.at[d - 1],
                                        recv_sem=recv_sems.at[d - 1], device_id=peer, device_id_type=MESH)


def _reduce_scatter_start(g, name):
    def body(g_ref, land_ref, send_sems, recv_sems, g_thru, land_thru, token):
        for d in range(1, N_DEV):
            _scatter_copy(g_ref, land_ref, send_sems, recv_sems, d).start()
        token[...] = jnp.zeros_like(token)

    return pl.pallas_call(
        body, name=name,
        out_shape=(pltpu.SemaphoreType.DMA((N_DEV - 1,)), pltpu.SemaphoreType.DMA((N_DEV - 1,)), pltpu.HBM(g.shape, g.dtype),
                   pltpu.HBM(g.shape, g.dtype), _sds((8, 128), F32)),
        in_specs=(HBM_SPEC, HBM_SPEC), out_specs=(SEM_SPEC, SEM_SPEC, HBM_SPEC, HBM_SPEC, pl.BlockSpec(memory_space=pltpu.VMEM)),
        input_output_aliases={0: 2, 1: 3}, compiler_params=pltpu.CompilerParams(has_side_effects=DATAFLOW),
    )(pltpu.with_memory_space_constraint(g, pltpu.HBM), pltpu.with_memory_space_constraint(lax.empty(g.shape, g.dtype), pltpu.HBM))


def _reduce_scatter_wait(send_sems, recv_sems, g_thru, land_thru, after, name):
    def body(g_ref, land_ref, send_sems, recv_sems, *rest):
        for d in range(1, N_DEV):
            copy = _scatter_copy(g_ref, land_ref, send_sems, recv_sems, d)
            copy.wait_send()
            copy.wait_recv()

    return pl.pallas_call(
        body, name=name, out_shape=(pltpu.HBM(g_thru.shape, g_thru.dtype), pltpu.HBM(g_thru.shape, g_thru.dtype)),
        in_specs=(HBM_SPEC, HBM_SPEC, SEM_SPEC, SEM_SPEC) + (ANY,) * len(after), out_specs=(HBM_SPEC, HBM_SPEC),
        input_output_aliases={0: 0, 1: 1}, compiler_params=pltpu.CompilerParams(has_side_effects=DATAFLOW),
    )(g_thru, land_thru, send_sems, recv_sems, *after)


def _all_reduce_small(part, after):
    r, W = part.shape

    def body(p_ref, after_ref, o_ref, g_ref, send_sems, recv_sems):
        x, y, c, me = _position()
        sends = []
        for d in range(1, N_DEV):
            peer, _ = _peer(x, y, c, d)
            cp = pltpu.make_async_remote_copy(src_ref=p_ref, dst_ref=g_ref.at[me], send_sem=send_sems.at[d - 1],
                                              recv_sem=recv_sems.at[d - 1], device_id=peer, device_id_type=MESH)
            cp.start()
            sends.append(cp)
        g_ref[me] = p_ref[...]
        for d in range(1, N_DEV):
            _, peer_idx = _peer(x, y, c, d)
            pltpu.make_async_remote_copy(src_ref=p_ref, dst_ref=g_ref.at[peer_idx], send_sem=send_sems.at[d - 1],
                                         recv_sem=recv_sems.at[d - 1], device_id=(x, y, c), device_id_type=MESH).wait_recv()
        for cp in sends:
            cp.wait_send()
        acc = g_ref[0]
        for i in range(1, N_DEV):
            acc = acc + g_ref[i]
        o_ref[...] = acc

    vmem = pl.BlockSpec(memory_space=pltpu.VMEM)
    return pl.pallas_call(
        body, name="all_reduce_small", in_specs=[vmem, ANY], out_specs=[vmem, vmem],
        out_shape=[_sds((r, W), F32), _sds((N_DEV, r, W), F32)],
        scratch_shapes=[pltpu.SemaphoreType.DMA((7,)), pltpu.SemaphoreType.DMA((7,))],
        compiler_params=pltpu.CompilerParams(has_side_effects=True, vmem_limit_bytes=VMEM_LIMIT),
    )(part, after)[0]


def _adamw_math(w, g, m, v):
    m = ADAM_B1 * m + (1.0 - ADAM_B1) * g
    v = ADAM_B2 * v + (1.0 - ADAM_B2) * (g * g)
    m_hat = m / (1.0 - ADAM_B1 ** ADAM_STEP)
    v_hat = v / (1.0 - ADAM_B2 ** ADAM_STEP)
    return -ADAM_LR * (m_hat / (jnp.sqrt(v_hat) + ADAM_EPS) + ADAM_WD * w), m, v


def _adamw(w, g, m, v, name):
    rows, cols = w.shape
    tr = _tile(rows, max(8, (2**18 // cols) // 8 * 8), 8)

    def body(w_ref, g_ref, m_ref, v_ref, d_ref, nm_ref, nv_ref):
        d_ref[...], nm_ref[...], nv_ref[...] = _adamw_math(w_ref[...], g_ref[...], m_ref[...], v_ref[...])

    spec = pl.BlockSpec((tr, cols), lambda i: (i, 0))
    return pl.pallas_call(
        body, name=name, grid=(rows // tr,), in_specs=[spec] * 4, out_specs=[spec] * 3,
        out_shape=[_sds((rows, cols), F32)] * 3, compiler_params=_params("parallel"),
    )(w, g, m, v)


def _adamw_sum(recv, w, m, v, name, after=()):
    n, rows, cols = recv.shape
    tr = _tile(rows, max(16, (2**17 // cols) // 16 * 16), 16)

    def body(r_ref, w_ref, m_ref, v_ref, g_ref, d_ref, nm_ref, nv_ref):
        g = r_ref[0].astype(F32)
        for i in range(1, n):
            g = g + r_ref[i].astype(F32)
        g_ref[...] = g
        d_ref[...], nm_ref[...], nv_ref[...] = _adamw_math(w_ref[...], g, m_ref[...], v_ref[...])

    spec = pl.BlockSpec((tr, cols), lambda i: (i, 0))
    return _call(body, name=name, grid=(rows // tr,), in_specs=[pl.BlockSpec((n, tr, cols), lambda i: (0, i, 0))] + [spec] * 3,
                 out_specs=[spec] * 4, out_shape=[_sds((rows, cols), F32)] * 4, args=(recv, w, m, v), sem=("parallel",),
                 after=after)


COLUMN_CUT = ("w_in", "w_attn_branch", "w_pool_branch", "w_up", "w_ple")
ROW_CUT = ("w_out", "w_down", "w_ple_gate")
REPLICATED = ("norm_mix_pre", "pool_scale", "norm_mix_post", "norm_ffn_pre", "conv_b", "norm_ffn_post", "norm_ple_post")
WEIGHTS = ("norm_mix_pre", "w_in", "w_attn_branch", "w_pool_group", "pool_scale", "w_pool_branch", "w_out", "norm_mix_post",
           "norm_ffn_pre", "w_up", "conv_w", "conv_b", "w_down", "norm_ffn_post", "w_ple", "w_ple_gate", "norm_ple_post")


def _size(shape):
    n = 1
    for s in shape:
        n *= s
    return n


def _pad_rows(flat, row_align):
    n = flat.shape[-1]
    per = PACK_W * row_align
    total = -(-n // per) * per
    return jnp.pad(flat, [(0, total - n)]).reshape(total // PACK_W, PACK_W)


def _natural(shard_major):
    n, r, c = shard_major.shape
    return shard_major.reshape(n * r, c)


def kernel(x, p, norm_mix_pre, w_in, w_attn_branch, w_pool_group, pool_scale, w_pool_branch, w_out, norm_mix_post, norm_ffn_pre, w_up, conv_w, conv_b, w_down, norm_ffn_post, w_ple, w_ple_gate, norm_ple_post, loss_target, m_norm_mix_pre, m_w_in, m_w_attn_branch, m_w_pool_group, m_pool_scale, m_w_pool_branch, m_w_out, m_norm_mix_post, m_norm_ffn_pre, m_w_up, m_conv_w, m_conv_b, m_w_down, m_norm_ffn_post, m_w_ple, m_w_ple_gate, m_norm_ple_post, v_norm_mix_pre, v_w_in, v_w_attn_branch, v_w_pool_group, v_pool_scale, v_w_pool_branch, v_w_out, v_norm_mix_post, v_norm_ffn_pre, v_w_up, v_conv_w, v_conv_b, v_w_down, v_norm_ffn_post, v_w_ple, v_w_ple_gate, v_norm_ple_post):
    given = dict(locals())
    wts = {n: given[n][0] for n in WEIGHTS}
    mom = {n: given["m_" + n][0] for n in WEIGHTS}
    var = {n: given["v_" + n][0] for n in WEIGHTS}
    xs = x[0]
    ps_in = p[0, 0]
    tgt = loss_target[0]
    S, D = xs.shape
    AW = wts["w_attn_branch"].shape[0]
    PW = wts["w_pool_branch"].shape[0]
    G = wts["w_pool_group"].shape[0]
    PGW = PW // G
    H = AW // HEAD_DIM
    F = wts["w_down"].shape[0] * N_DEV
    assert (3 * AW) % PGW == 0 and (3 * AW + PW) % D == 0 and PGW % 128 == 0 and F % 128 == 0
    tr = _tile(S, 256, 16)
    blk = _tile(S, 256, 128)
    chunk = _tile(S, 256, 8)
    me = 4 * lax.axis_index("x") + 2 * lax.axis_index("y") + lax.axis_index("c")

    cw_shape = wts["conv_w"].shape
    conv_b_row = wts["conv_b"].reshape(1, -1)
    g1, g2, g3, g4, g5 = (wts[n].reshape(1, D) for n in
                          ("norm_mix_pre", "norm_mix_post", "norm_ffn_pre", "norm_ffn_post", "norm_ple_post"))
    pscale = wts["pool_scale"].reshape(1, PW)
    big = 4096
    wb = {n: wts[n].astype(BF16) for n in COLUMN_CUT + ROW_CUT}
    wb["w_pool_group"] = wts["w_pool_group"].astype(BF16).reshape(G * PGW // N_DEV, PGW)

    h, w_in = _rms_fwd(xs, g1, tr, _all_gather_comm([wb["w_in"]]))
    proj, w_ab, w_pg, w_pb, w_out = _mm(
        h, w_in, b_sm=True, name="mm_in", tk=2048,
        comm=_all_gather_comm([wb["w_attn_branch"], wb["w_pool_group"], wb["w_pool_branch"], wb["w_out"]]))
    w_pg = jnp.moveaxis(w_pg.reshape(N_DEV, G, PGW // N_DEV, PGW), 0, 1).reshape(G, PGW, PGW)
    w_out = _natural(w_out)
    attn, a_saved, b_saved, w_up, conv_w_all = _attn_fwd(proj, H, blk, _all_gather_comm([wb["w_up"], wts["conv_w"]]))
    conv_w_full = jnp.moveaxis(conv_w_all, 0, 1).reshape(cw_shape[0], N_DEV * cw_shape[1])
    y_attn = _mm(attn, w_ab, b_sm=True, name="mm_attn_branch", tm=big, tn=256, tk=big)
    pooled = _pool_fwd(proj, 3 * AW // PGW, G, PGW, chunk)
    pg, ps = _pool_group_fwd(pooled, w_pg, pscale)
    y_pool = _mm(ps, w_pb, b_sm=True, name="mm_pool_branch", tm=big, tn=256, tk=big)
    gate_cb = (3 * AW + PW) // D
    mixed = _gate_mix(proj, y_attn, y_pool, gate_cb, tr)
    mo = _mm(mixed, w_out, name="mm_out", tk=2048)
    x1, h2 = _resid_rms2(xs, mo, g2, g3, tr)
    upre, w_down, w_ple, w_pleg = _mm(h2, w_up, b_sm=True, name="mm_up", tn=2048, tk=1024,
                                      comm=_all_gather_comm([wb["w_down"], wb["w_ple"], wb["w_ple_gate"]]))
    w_down, w_pleg = _natural(w_down), _natural(w_pleg)
    act = _conv_fwd(upre, conv_w_full, conv_b_row, chunk)
    yf = _mm(act, w_down, name="mm_down", tk=1408)
    x2, x2b = _resid_rms(x1, yf, g4, tr)
    e = _mm(ps_in, w_ple, b_sm=True, name="mm_ple", tm=big, tn=256, tk=big)
    gl = _mm(x2b, w_pleg, name="mm_ple_gate", tk=2048)
    loss_part, dx3, de, dgl, dg5 = _ple_loss(gl, e, x2, tgt, g5, tr)

    shards = lambda natural: natural.reshape((N_DEV, natural.shape[0] // N_DEV) + natural.shape[1:])
    recv = {}
    dw_ple = _mm(ps_in, de, ta=True, out_sm=True, out_dtype=BF16, name="mm_d_w_ple", tm=256, tn=256, tk=big)
    dw_pleg = shards(_mm(x2b, dgl, ta=True, out_dtype=BF16, name="mm_d_w_ple_gate"))
    dx2g = _mm(dgl, w_pleg, tb=True, name="mm_d_x2", tk=2048)
    dx2, dyf, dg4 = _rms_bwd_a(dx3, dx2g, yf, g4, tr)
    dw_down = shards(_mm(act, dyf, ta=True, out_dtype=BF16, name="mm_d_w_down"))
    dact, recv["w_ple"], recv["w_ple_gate"] = _mm(dyf, w_down, tb=True, name="mm_d_act", tn=1408, tk=1024,
                                                  comm=_reduce_scatter_comm([dw_ple, dw_pleg]))
    dup_g, dup_v, dcw_g, dcw_v, dcb_g, dcb_v, recv["w_down"] = _conv_bwd(upre, dact, conv_w_full, conv_b_row, chunk,
                                                                          _reduce_scatter_comm([dw_down]))
    dupre = jnp.concatenate([dup_g, dup_v], axis=1)
    dw_up = _mm(h2, dupre, ta=True, out_sm=True, out_dtype=BF16, name="mm_d_w_up", tn=2048)
    up_send, up_recv, dw_up, up_land, up_token = _reduce_scatter_start(dw_up, "rs_w_up_start")
    dh2 = _mm(dupre, w_up, tb=True, b_sm=True, name="mm_d_h2", tk=2048, after=(up_token,))
    dx1, dmo, dg3, dg2 = _rms_bwd_b(dx2, dh2, x1, g3, mo, g2, tr)
    dmixed = _mm(dmo, w_out, tb=True, name="mm_d_mixed", tk=2048)
    dw_out = shards(_mm(mixed, dmo, ta=True, out_dtype=BF16, name="mm_d_w_out"))
    dya, dyp, dga, dgp = _gate_bwd(dmixed, proj, y_attn, y_pool, gate_cb, tr)
    dps = _mm(dyp, w_pb, tb=True, b_sm=True, name="mm_d_ps", tm=2048, tk=256)
    dw_pb = _mm(ps, dyp, ta=True, out_sm=True, out_dtype=BF16, name="mm_d_w_pool_branch", tn=256, tk=big)
    dpg, dscale = _scale_bwd(dps, pg, pscale, tr)
    dpooled = _pool_group_bwd_x(dpg, w_pg)
    dw_pg = _pool_group_bwd_w(pooled, dpg, G)
    dw_pg = jnp.moveaxis(dw_pg.astype(BF16).reshape(G, N_DEV, PGW // N_DEV, PGW), 1, 0).reshape(N_DEV, G * PGW // N_DEV, PGW)
    du = _pool_bwd(dpooled, G, chunk)
    dattn = _mm(dya, w_ab, tb=True, b_sm=True, name="mm_d_attn", tm=2048, tk=256)
    dw_ab = _mm(attn, dya, ta=True, out_sm=True, out_dtype=BF16, name="mm_d_w_attn_branch", tn=256, tk=big)
    dq, dk, dv, recv["w_out"], recv["w_pool_branch"], recv["w_pool_group"], recv["w_attn_branch"] = _attn_bwd(
        proj, a_saved, b_saved, dattn, H, blk, _reduce_scatter_comm([dw_out, dw_pb, dw_pg, dw_ab]))
    dw_up, up_land = _reduce_scatter_wait(up_send, up_recv, dw_up, up_land, (dq,), "rs_w_up_wait")
    recv["w_up"] = lax.dynamic_update_slice_in_dim(up_land, lax.dynamic_index_in_dim(dw_up, me, 0, keepdims=True), me, 0)
    dproj = jnp.concatenate([dq, dk, dv, du, dga, dgp], axis=1)
    dw_in = _mm(h, dproj, ta=True, out_sm=True, out_dtype=BF16, name="mm_d_w_in")
    in_send, in_recv, dw_in, in_land, token = _reduce_scatter_start(dw_in, "rs_w_in_start")
    dh = _mm(dproj, w_in, tb=True, b_sm=True, name="mm_d_h", tm=big, tn=512, tk=1024, after=(token,))
    grad_x, dg1 = _rms_bwd_c(dx1, dh, xs, g1, tr)

    gshard, delta, new_m, new_v = {}, {}, {}, {}

    def adamw_cut(n, after=()):
        shp = wts[n].shape
        two_d = (_size(shp[:-1]), shp[-1])
        g_, d_, m_, v_ = _adamw_sum(recv[n].reshape((N_DEV,) + two_d), wts[n].reshape(two_d), mom[n].reshape(two_d),
                                    var[n].reshape(two_d), "adamw_" + n, after)
        gshard[n], delta[n], new_m[n], new_v[n] = g_.reshape(shp), d_.reshape(shp), m_.reshape(shp), v_.reshape(shp)

    for n in COLUMN_CUT[1:] + ROW_CUT + ("w_pool_group",):
        adamw_cut(n, (token,))

    assert COLUMN_CUT[0] == "w_in"
    dw_in, in_land = _reduce_scatter_wait(in_send, in_recv, dw_in, in_land, (grad_x,) + tuple(delta[n] for n in delta),
                                          "rs_w_in_wait")

    dconv_w = jnp.concatenate([dcw_g, dcw_v], axis=1)
    dconv_b = jnp.concatenate([dcb_g, dcb_v], axis=1).reshape(-1)
    rep_parts = {"norm_mix_pre": dg1, "pool_scale": dscale, "norm_mix_post": dg2, "norm_ffn_pre": dg3, "conv_b": dconv_b,
                 "norm_ffn_post": dg4, "norm_ple_post": dg5}
    small = jnp.concatenate([rep_parts[n].reshape(-1) for n in REPLICATED] + [dconv_w.reshape(-1)])
    n_small = small.shape[0]
    small_sum = _all_reduce_small(_pad_rows(small, 8), in_land).reshape(-1)[:n_small]
    off = 0
    for n in REPLICATED:
        sz = _size(wts[n].shape)
        gshard[n] = small_sum[off:off + sz].reshape(wts[n].shape)
        off += sz
    dconv_w_sum = small_sum[off:off + 3 * 2 * F].reshape(3, 2 * F)
    gshard["conv_w"] = lax.dynamic_slice_in_dim(dconv_w_sum, me * cw_shape[1], cw_shape[1], axis=1)

    delta["conv_w"], new_m["conv_w"], new_v["conv_w"] = _adamw(wts["conv_w"], gshard["conv_w"], mom["conv_w"], var["conv_w"],
                                                               "adamw_conv_w")
    rep_sizes = [_size(wts[n].shape) for n in REPLICATED]
    n_rep = sum(rep_sizes)
    cat = lambda t: _pad_rows(jnp.concatenate([t[n].reshape(-1) for n in REPLICATED]), 8)
    d_, m_, v_ = _adamw(cat(wts), cat(gshard), cat(mom), cat(var), "adamw_replicated")
    off = 0
    for n, sz in zip(REPLICATED, rep_sizes):
        shp = wts[n].shape
        delta[n], new_m[n], new_v[n] = (t.reshape(-1)[off:off + sz].reshape(shp) for t in (d_, m_, v_))
        off += sz
    assert off == n_rep

    own = lax.dynamic_index_in_dim(dw_in, me, 0, keepdims=True)
    recv["w_in"] = lax.dynamic_update_slice_in_dim(in_land, own, me, 0)
    adamw_cut("w_in")

    loss = lax.psum(loss_part[0, 0], ("x", "y", "c"))
    lead = lambda t: t[None]
    return (loss, grad_x[None], *[lead(gshard[n]) for n in WEIGHTS], *[lead(delta[n]) for n in WEIGHTS],
            *[lead(new_m[n]) for n in WEIGHTS], *[lead(new_v[n]) for n in WEIGHTS])
```

```python
import functools

import jax
import jax.numpy as jnp
from jax import lax
from jax.experimental import pallas as pl
from jax.experimental.pallas import tpu as pltpu

F32 = jnp.float32
BF16 = jnp.bfloat16
MESH = pl.DeviceIdType.MESH

EPS = 1e-6
HEAD_DIM = 128
POOL_WINDOWS = (2, 4, 8, 16)
POOL_HALO = 16
CONV_HALO = 8
GELU_C0 = 0.7978845608028654
GELU_C1 = 0.044715
ADAM_LR = 0.001
ADAM_B1 = 0.9
ADAM_B2 = 0.999
ADAM_EPS = 1e-08
ADAM_WD = 0.01
ADAM_STEP = 10
N_DEV = 8
PACK_W = 1024
VMEM_LIMIT = 56 * 2**20
ANY = pl.BlockSpec(memory_space=pl.ANY)


def _params(*sem):
    return pltpu.CompilerParams(dimension_semantics=sem, vmem_limit_bytes=VMEM_LIMIT)


def _sds(shape, dtype):
    return jax.ShapeDtypeStruct(shape, dtype)


def _tile(dim, target, align):
    if dim <= target:
        return dim
    t = (target // align) * align
    while t >= align:
        if dim % t == 0:
            return t
        t -= align
    return dim


def _sigmoid(x):
    return 1.0 / (1.0 + jnp.exp(-x))


class _Comm:
    def __init__(self, ins, outs, sems, start, finish):
        self.ins, self.outs, self.sems, self.start, self.finish = list(ins), list(outs), list(sems), start, finish


def _call(body, *, name, grid, in_specs, out_specs, out_shape, args, scratch=(), sem=(), comm=None, after=()):
    in_specs, out_specs, out_shape, scratch = list(in_specs), list(out_specs), list(out_shape), list(scratch)
    if after:
        assert comm is None
        n_in = len(in_specs)
        return pl.pallas_call(lambda *refs: body(*refs[:n_in], *refs[n_in + len(after):]), name=name, grid=grid,
                              in_specs=in_specs + [ANY] * len(after), out_specs=out_specs, out_shape=out_shape,
                              scratch_shapes=scratch, compiler_params=_params(*sem))(*args, *after)
    if comm is None:
        return pl.pallas_call(body, name=name, grid=grid, in_specs=in_specs, out_specs=out_specs, out_shape=out_shape,
                              scratch_shapes=scratch, compiler_params=_params(*sem))(*args)
    n_in, n_out, n_scr, n_ci, n_co = len(in_specs), len(out_specs), len(scratch), len(comm.ins), len(comm.outs)

    def wrapped(*refs):
        ins, refs = refs[:n_in], refs[n_in:]
        c_ins, refs = refs[:n_ci], refs[n_ci:]
        outs, refs = refs[:n_out], refs[n_out:]
        c_outs, refs = refs[:n_co], refs[n_co:]
        scr, c_sems = refs[:n_scr], refs[n_scr:]
        first = last = None
        for axis, size in enumerate(grid):
            at_start, at_end = pl.program_id(axis) == 0, pl.program_id(axis) == size - 1
            first = at_start if first is None else jnp.logical_and(first, at_start)
            last = at_end if last is None else jnp.logical_and(last, at_end)
        if grid:
            pl.when(first)(lambda: comm.start(c_ins, c_outs, c_sems))
            body(*ins, *outs, *scr)
            pl.when(last)(lambda: comm.finish(c_ins, c_outs, c_sems))
        else:
            comm.start(c_ins, c_outs, c_sems)
            body(*ins, *outs, *scr)
            comm.finish(c_ins, c_outs, c_sems)

    return pl.pallas_call(
        wrapped, name=name, grid=grid, in_specs=in_specs + [ANY] * n_ci, out_specs=out_specs + [ANY] * n_co,
        out_shape=out_shape + comm.outs, scratch_shapes=scratch + comm.sems,
        compiler_params=pltpu.CompilerParams(dimension_semantics=("arbitrary",) * len(grid), vmem_limit_bytes=VMEM_LIMIT,
                                             has_side_effects=True),
    )(*args, *comm.ins)


def _mm(a, b, *, ta=False, tb=False, b_sm=False, out_sm=False, out_dtype=F32, name, tm=2048, tn=1024, tk=1024, comm=None,
        after=(), a2=None, b2=None):
    M, K = (a.shape[1], a.shape[0]) if ta else a.shape
    if a2 is not None:
        assert not ta and a2.shape == a.shape
        K = 2 * K
    if b_sm:
        n_sl, rows, per = b.shape
        N = rows if tb else n_sl * per
        assert K == (n_sl * per if tb else rows)
    else:
        N = b.shape[0] if tb else b.shape[1]
    if b2 is not None:
        assert not tb and not b_sm and b2.shape == b.shape
        N = 2 * N
    tm = _tile(M, tm, 128)
    tn = _tile(per if (b_sm and not tb) else N // N_DEV if out_sm else N // 2 if b2 is not None else N, tn, 128)
    tk = _tile(per if (b_sm and tb) else K // 2 if a2 is not None else K, tk, 128)
    nk = K // tk
    kh, jh = nk // 2, (N // tn) // 2
    a_spec = pl.BlockSpec((tk, tm), lambda i, j, k: (k, i)) if ta else pl.BlockSpec((tm, tk), lambda i, j, k: (i, k))
    if a2 is not None:
        a_spec = pl.BlockSpec((tm, tk), lambda i, j, k: (i, jnp.minimum(k, kh - 1)))
        a2_spec = pl.BlockSpec((tm, tk), lambda i, j, k: (i, jnp.maximum(k - kh, 0)))
    if b2 is not None:
        b_spec = pl.BlockSpec((tk, tn), lambda i, j, k: (jnp.where(j < jh, k, 0), jnp.minimum(j, jh - 1)))
        b2_spec = pl.BlockSpec((tk, tn), lambda i, j, k: (jnp.where(j < jh, 0, k), jnp.maximum(j - jh, 0)))
    elif not b_sm:
        b_spec = pl.BlockSpec((tn, tk), lambda i, j, k: (j, k)) if tb else pl.BlockSpec((tk, tn), lambda i, j, k: (k, j))
    elif tb:
        kp = per // tk
        b_spec = pl.BlockSpec((None, tn, tk), lambda i, j, k: (k // kp, j, k % kp))
    else:
        jp = per // tn
        b_spec = pl.BlockSpec((None, tk, tn), lambda i, j, k: (j // jp, k, j % jp))
    if out_sm:
        jo = (N // N_DEV) // tn
        o_spec = pl.BlockSpec((None, tm, tn), lambda i, j, k: (j // jo, i, j % jo))
        o_shape = _sds((N_DEV, M, N // N_DEV), out_dtype)
    else:
        o_spec = pl.BlockSpec((tm, tn), lambda i, j, k: (i, j))
        o_shape = _sds((M, N), out_dtype)
    dims = (((0 if ta else 1,), (1 if tb else 0,)), ((), ()))

    def product(a_ref, b_ref):
        return lax.dot_general(a_ref[...].astype(BF16), b_ref[...].astype(BF16), dims, preferred_element_type=F32)

    def body(a_ref, b_ref, o_ref, acc_ref):
        k = pl.program_id(2)

        @pl.when(k == 0)
        def _():
            acc_ref[...] = jnp.zeros_like(acc_ref)

        acc_ref[...] += product(a_ref, b_ref)

        @pl.when(k == nk - 1)
        def _():
            o_ref[...] = acc_ref[...].astype(o_ref.dtype)

    def body_one_step(a_ref, b_ref, o_ref):
        o_ref[...] = product(a_ref, b_ref).astype(o_ref.dtype)

    def body_split(a_ref, x2_ref, b_ref, o_ref, acc_ref):
        j, k = pl.program_id(1), pl.program_id(2)
        second = k >= kh if a2 is not None else j >= jh

        @pl.when(k == 0)
        def _():
            acc_ref[...] = jnp.zeros_like(acc_ref)

        @pl.when(jnp.logical_not(second))
        def _():
            acc_ref[...] += product(a_ref, b_ref)

        @pl.when(second)
        def _():
            acc_ref[...] += product(x2_ref, b_ref) if a2 is not None else product(a_ref, x2_ref)

        @pl.when(k == nk - 1)
        def _():
            o_ref[...] = acc_ref[...].astype(o_ref.dtype)

    in_specs, args, kernel_body = [a_spec, b_spec], (a, b), body if nk > 1 else body_one_step
    split = a2 is not None or b2 is not None
    if split:
        assert a2 is None or b2 is None
        kernel_body = body_split
        if a2 is not None:
            in_specs, args = [a_spec, a2_spec, b_spec], (a, a2, b)
        else:
            in_specs, args = [a_spec, b_spec, b2_spec], (a, b, b2)
            kernel_body = lambda a_ref, b_ref, b2_ref, o_ref, acc_ref: body_split(a_ref, b2_ref, b_ref, o_ref, acc_ref)
    res = _call(kernel_body, name=name, grid=(M // tm, N // tn, nk), in_specs=in_specs, out_specs=[o_spec],
                out_shape=[o_shape], args=args, scratch=[pltpu.VMEM((tm, tn), F32)] if nk > 1 or split else [],
                sem=("parallel", "parallel", "arbitrary"), comm=comm, after=after)
    return res[0] if comm is None else res


def _pool_group_fwd(pooled, w_pg, scale):
    S = pooled.shape[0]
    G, C, C2 = w_pg.shape
    tm = _tile(S, 1024, 16)

    def body(a_ref, w_ref, s_ref, pg_ref, ps_ref):
        pg = jnp.dot(a_ref[...], w_ref[...], preferred_element_type=F32)
        pg_ref[...] = pg
        ps_ref[...] = (pg * s_ref[...]).astype(BF16)

    return pl.pallas_call(
        body, name="pool_group_fwd", grid=(G, S // tm),
        in_specs=[pl.BlockSpec((tm, C), lambda g, i: (i, g)), pl.BlockSpec((None, C, C2), lambda g, i: (g, 0, 0)),
                  pl.BlockSpec((1, C2), lambda g, i: (0, g))],
        out_specs=[pl.BlockSpec((tm, C2), lambda g, i: (i, g)), pl.BlockSpec((tm, C2), lambda g, i: (i, g))],
        out_shape=[jax.ShapeDtypeStruct((S, G * C2), F32), jax.ShapeDtypeStruct((S, G * C2), BF16)],
        compiler_params=_params("parallel", "parallel"),
    )(pooled, w_pg, scale)


def _pool_group_bwd_x(dpg, w_pg):
    S = dpg.shape[0]
    G, C, C2 = w_pg.shape
    tm = _tile(S, 1024, 16)

    def body(d_ref, w_ref, o_ref):
        o_ref[...] = lax.dot_general(d_ref[...], w_ref[...], (((1,), (1,)), ((), ())), preferred_element_type=F32)

    return pl.pallas_call(
        body, name="pool_group_bwd_x", grid=(G, S // tm),
        in_specs=[pl.BlockSpec((tm, C2), lambda g, i: (i, g)), pl.BlockSpec((None, C, C2), lambda g, i: (g, 0, 0))],
        out_specs=pl.BlockSpec((tm, C), lambda g, i: (i, g)), out_shape=jax.ShapeDtypeStruct((S, G * C), F32),
        compiler_params=_params("parallel", "parallel"),
    )(dpg, w_pg)


def _pool_group_bwd_w(pooled, dpg, G):
    S = pooled.shape[0]
    C, C2 = pooled.shape[1] // G, dpg.shape[1] // G
    tk = _tile(S, 1024, 16)

    def body(a_ref, d_ref, o_ref):
        @pl.when(pl.program_id(1) == 0)
        def _():
            o_ref[...] = jnp.zeros_like(o_ref)

        o_ref[...] += lax.dot_general(a_ref[...], d_ref[...], (((0,), (0,)), ((), ())), preferred_element_type=F32)

    return pl.pallas_call(
        body, name="pool_group_bwd_w", grid=(G, S // tk),
        in_specs=[pl.BlockSpec((tk, C), lambda g, k: (k, g)), pl.BlockSpec((tk, C2), lambda g, k: (k, g))],
        out_specs=pl.BlockSpec((None, C, C2), lambda g, k: (g, 0, 0)), out_shape=jax.ShapeDtypeStruct((G, C, C2), F32),
        compiler_params=_params("parallel", "arbitrary"),
    )(pooled, dpg)


def _rms(x, gain):
    r = lax.rsqrt(jnp.mean(x * x, axis=-1, keepdims=True) + EPS)
    return x * r * gain


def _rms_bwd(x, gain, dy):
    r = lax.rsqrt(jnp.mean(x * x, axis=-1, keepdims=True) + EPS)
    xh = x * r
    dgain = jnp.sum(dy * xh, axis=0, keepdims=True)
    dxh = dy * gain
    dx = r * (dxh - xh * jnp.mean(dxh * xh, axis=-1, keepdims=True))
    return dx, dgain


def _row_call(body, name, ins, outs, tr, *, comm=None):
    S = None
    in_specs, args = [], []
    for it in ins:
        arr, kind = it[0], it[1]
        if kind == "row":
            S = arr.shape[0]
            if len(it) == 4:
                width, cb = it[2], it[3]
                in_specs.append(pl.BlockSpec((tr, width), functools.partial(lambda i, cb: (i, cb), cb=cb)))
            else:
                in_specs.append(pl.BlockSpec((tr, arr.shape[1]), lambda i: (i, 0)))
        else:
            assert arr.ndim == 2
            in_specs.append(pl.BlockSpec(arr.shape, lambda i: (0, 0)))
        args.append(arr)
    out_specs, out_shape = [], []
    for sds, kind in outs:
        if kind == "row":
            out_specs.append(pl.BlockSpec((tr, sds.shape[1]), lambda i: (i, 0)))
        else:
            assert len(sds.shape) == 2
            out_specs.append(pl.BlockSpec(sds.shape, lambda i: (0, 0)))
        out_shape.append(sds)
    return _call(body, name=name, grid=(S // tr,), in_specs=in_specs, out_specs=out_specs, out_shape=out_shape, args=args,
                 sem=("arbitrary",), comm=comm)


def _first_step_zero(*refs):
    @pl.when(pl.program_id(0) == 0)
    def _():
        for r in refs:
            r[...] = jnp.zeros_like(r)


def _rms_fwd(x, gain, tr, comm):
    def body(x_ref, g_ref, o_ref):
        o_ref[...] = _rms(x_ref[...], g_ref[...]).astype(BF16)

    S, D = x.shape
    return _row_call(body, "rms_fwd", [(x, "row"), (gain, "full")], [(_sds((S, D), BF16), "row")], tr, comm=comm)


def _gate_mix(proj, ya, yp, gate_cb, tr):
    S, D = ya.shape

    def body(ga_ref, gp_ref, ya_ref, yp_ref, o_ref):
        o_ref[...] = (_sigmoid(ga_ref[...]) * ya_ref[...] + _sigmoid(gp_ref[...]) * yp_ref[...]).astype(BF16)

    return _row_call(body, "gate_mix", [(proj, "row", D, gate_cb), (proj, "row", D, gate_cb + 1), (ya, "row"), (yp, "row")],
                     [(_sds((S, D), BF16), "row")], tr)[0]


def _resid_rms2(x, mo, g2, g3, tr):
    S, D = x.shape

    def body(x_ref, mo_ref, g2_ref, g3_ref, x1_ref, h2_ref):
        x1 = x_ref[...] + _rms(mo_ref[...], g2_ref[...])
        x1_ref[...] = x1
        h2_ref[...] = _rms(x1, g3_ref[...]).astype(BF16)

    return _row_call(body, "resid_rms2", [(x, "row"), (mo, "row"), (g2, "full"), (g3, "full")],
                     [(_sds((S, D), F32), "row"), (_sds((S, D), BF16), "row")], tr)


def _resid_rms(x1, yf, g4, tr):
    S, D = x1.shape

    def body(x_ref, y_ref, g_ref, o_ref, ob_ref):
        x2 = x_ref[...] + _rms(y_ref[...], g_ref[...])
        o_ref[...] = x2
        ob_ref[...] = x2.astype(BF16)

    return _row_call(body, "resid_rms", [(x1, "row"), (yf, "row"), (g4, "full")],
                     [(_sds((S, D), F32), "row"), (_sds((S, D), BF16), "row")], tr)


def _ple_loss(gl, e, x2, tgt, g5, tr):
    S, D = x2.shape

    def body(gl_ref, e_ref, x2_ref, t_ref, g_ref, loss_ref, dx3_ref, de_ref, dgl_ref, dg_ref):
        _first_step_zero(loss_ref, dg_ref)
        s = _sigmoid(gl_ref[...])
        e_ = e_ref[...]
        t = s * e_
        gain = g_ref[...]
        err = x2_ref[...] + _rms(t, gain) - t_ref[...]
        row_loss = jnp.mean(err * err, axis=-1, keepdims=True)
        loss_ref[...] += 0.5 * jnp.sum(row_loss, axis=0, keepdims=True)
        dx3 = err * (1.0 / D)
        dx3_ref[...] = dx3
        dt, dgain = _rms_bwd(t, gain, dx3)
        dg_ref[...] += dgain
        de_ref[...] = (dt * s).astype(BF16)
        dgl_ref[...] = (dt * e_ * s * (1.0 - s)).astype(BF16)

    return _row_call(body, "ple_loss", [(gl, "row"), (e, "row"), (x2, "row"), (tgt, "row"), (g5, "full")],
                     [(_sds((1, 1), F32), "acc"), (_sds((S, D), F32), "row"), (_sds((S, D), BF16), "row"),
                      (_sds((S, D), BF16), "row"), (_sds((1, D), F32), "acc")], tr)


def _rms_bwd_a(dx3, dx2g, yf, g4, tr):
    S, D = yf.shape

    def body(a_ref, b_ref, y_ref, g_ref, dx_ref, dy_ref, dg_ref):
        _first_step_zero(dg_ref)
        dx2 = a_ref[...] + b_ref[...]
        dx_ref[...] = dx2
        dy, dgain = _rms_bwd(y_ref[...], g_ref[...], dx2)
        dy_ref[...] = dy.astype(BF16)
        dg_ref[...] += dgain

    return _row_call(body, "rms_bwd_a", [(dx3, "row"), (dx2g, "row"), (yf, "row"), (g4, "full")],
                     [(_sds((S, D), F32), "row"), (_sds((S, D), BF16), "row"), (_sds((1, D), F32), "acc")], tr)


def _rms_bwd_b(dx2, dh2, x1, g3, mo, g2, tr):
    S, D = x1.shape

    def body(dx2_ref, dh2_ref, x1_ref, g3_ref, mo_ref, g2_ref, dx1_ref, dmo_ref, dg3_ref, dg2_ref):
        _first_step_zero(dg3_ref, dg2_ref)
        d, dgain3 = _rms_bwd(x1_ref[...], g3_ref[...], dh2_ref[...])
        dx1 = dx2_ref[...] + d
        dx1_ref[...] = dx1
        dg3_ref[...] += dgain3
        dmo, dgain2 = _rms_bwd(mo_ref[...], g2_ref[...], dx1)
        dmo_ref[...] = dmo.astype(BF16)
        dg2_ref[...] += dgain2

    return _row_call(body, "rms_bwd_b", [(dx2, "row"), (dh2, "row"), (x1, "row"), (g3, "full"), (mo, "row"), (g2, "full")],
                     [(_sds((S, D), F32), "row"), (_sds((S, D), BF16), "row"), (_sds((1, D), F32), "acc"),
                      (_sds((1, D), F32), "acc")], tr)


def _rms_bwd_c(dx1, dh, x, g1, tr):
    S, D = x.shape

    def body(dx1_ref, dh_ref, x_ref, g_ref, o_ref, dg_ref):
        _first_step_zero(dg_ref)
        d, dgain = _rms_bwd(x_ref[...], g_ref[...], dh_ref[...])
        o_ref[...] = dx1_ref[...] + d
        dg_ref[...] += dgain

    return _row_call(body, "rms_bwd_c", [(dx1, "row"), (dh, "row"), (x, "row"), (g1, "full")],
                     [(_sds((S, D), F32), "row"), (_sds((1, D), F32), "acc")], tr)


def _gate_bwd(dmixed, proj, ya, yp, gate_cb, tr):
    S, D = ya.shape

    def body(dm_ref, ga_ref, gp_ref, ya_ref, yp_ref, dya_ref, dyp_ref, dga_ref, dgp_ref):
        dm = dm_ref[...]
        sa = _sigmoid(ga_ref[...])
        sp = _sigmoid(gp_ref[...])
        dya_ref[...] = (dm * sa).astype(BF16)
        dyp_ref[...] = (dm * sp).astype(BF16)
        dga_ref[...] = (dm * ya_ref[...] * sa * (1.0 - sa)).astype(BF16)
        dgp_ref[...] = (dm * yp_ref[...] * sp * (1.0 - sp)).astype(BF16)

    return _row_call(body, "gate_bwd",
                     [(dmixed, "row"), (proj, "row", D, gate_cb), (proj, "row", D, gate_cb + 1), (ya, "row"), (yp, "row")],
                     [(_sds((S, D), BF16), "row")] * 4, tr)


def _scale_bwd(dps, pg, scale, tr):
    S, W = dps.shape

    def body(d_ref, pg_ref, s_ref, o_ref, ds_ref):
        _first_step_zero(ds_ref)
        d = d_ref[...]
        o_ref[...] = (d * s_ref[...]).astype(BF16)
        ds_ref[...] += jnp.sum(d * pg_ref[...], axis=0, keepdims=True)

    return _row_call(body, "scale_bwd", [(dps, "row"), (pg, "row"), (scale, "full")],
                     [(_sds((S, W), BF16), "row"), (_sds((1, W), F32), "acc")], tr)


CUMSUM_TERMS = 2
GRAD_TERMS = 1


def _tri(blk, cmp, terms=CUMSUM_TERMS):
    j = lax.broadcasted_iota(jnp.int32, (blk, blk), 0)
    s = lax.broadcasted_iota(jnp.int32, (blk, blk), 1)
    one = jnp.concatenate([cmp(j, s).astype(BF16), jnp.ones((blk, 128), BF16)], axis=1)
    return jnp.concatenate([one] * terms, axis=0)


def _split(x, terms=CUMSUM_TERMS):
    parts = []
    for _ in range(terms):
        t = x.astype(BF16)
        parts.append(t)
        x = x - t.astype(F32)
    return parts[0] if terms == 1 else jnp.concatenate(parts, axis=1)


def _causal(blk):
    return lax.broadcasted_iota(jnp.int32, (blk, blk), 1) < lax.broadcasted_iota(jnp.int32, (blk, blk), 0)


def _scores(q, kj, scale, causal):
    z = lax.dot_general(q, kj, (((1,), (1,)), ((), ())), preferred_element_type=F32) * scale
    l1p = jnp.log(1.0 + jnp.exp(-jnp.abs(z)))
    lb = -(jnp.maximum(z, 0.0) + l1p)
    if causal is not None:
        lb = jnp.where(causal, lb, 0.0)
    return z, lb, jnp.minimum(z, 0.0) - l1p


def _attn_fwd(proj, n_heads, blk, comm=None):
    S = proj.shape[0]
    nq = S // blk
    scale = HEAD_DIM ** -0.5
    lanes = blk // 128
    hp = 2 if n_heads % 2 == 0 else 1
    cols = [slice(h * HEAD_DIM, (h + 1) * HEAD_DIM) for h in range(hp)]
    u_incl = _tri(blk, lambda j, s: j >= s)

    def body(q_ref, k_ref, v_ref, u_ref, o_ref, a_ref, b_ref, z_buf, hl_buf):
        i = pl.program_id(1)
        qs = [q_ref[:, c].astype(BF16) for c in cols]
        u = u_ref[...]

        def scores(j, h, causal):
            kj = k_ref[pl.ds(pl.multiple_of(j * blk, blk), blk), cols[h]].astype(BF16)
            z, lb, log_beta = _scores(qs[h], kj, scale, causal)
            b_ref[h, j] = jnp.exp(log_beta).astype(BF16)
            return z, _split(lb)

        def weigh(j, h, z, hl, acc, run, causal):
            vj = v_ref[pl.ds(pl.multiple_of(j * blk, blk), blk), cols[h]].astype(BF16)
            ct = jnp.dot(hl, u, preferred_element_type=F32)
            a = jnp.exp(z + ct[:, :blk] + jnp.tile(run, (1, lanes)))
            if causal is not None:
                a = jnp.where(causal, a, 0.0)
            a = a.astype(BF16)
            a_ref[h, j] = a
            return acc + jnp.dot(a, vj, preferred_element_type=F32), run + ct[:, blk:]

        def stage_scores(j):
            for h in range(hp):
                z_buf[h], hl_buf[h] = scores(j, h, None)

        zero = jnp.zeros((blk, HEAD_DIM), F32)
        causal = _causal(blk)
        carry = tuple(weigh(i, h, *scores(i, h, causal), zero, zero, causal) for h in range(hp))
        stage_scores(jnp.maximum(i - 1, 0))

        def step(t, carry):
            j = i - 1 - t
            out = tuple(weigh(j, h, z_buf[h], hl_buf[h], *carry[h], None) for h in range(hp))
            stage_scores(j - 1)
            return out

        carry = lax.fori_loop(0, jnp.maximum(i - 1, 0), step, carry)
        carry = lax.fori_loop(0, jnp.minimum(i, 1),
                              lambda t, c: tuple(weigh(t, h, z_buf[h], hl_buf[h], *c[h], None) for h in range(hp)), carry)
        for h in range(hp):
            o_ref[:, cols[h]] = carry[h][0].astype(BF16)

    G = n_heads // hp
    W = hp * HEAD_DIM
    saved = _sds((n_heads, nq, nq, blk, blk), BF16)
    saved_spec = pl.BlockSpec((hp, None, nq, blk, blk), lambda h, i: (h, i, 0, 0, 0))
    return _call(
        body, name="attn_fwd", grid=(G, nq),
        in_specs=[pl.BlockSpec((blk, W), lambda h, i: (i, h)),
                  pl.BlockSpec((S, W), lambda h, i: (0, G + h)),
                  pl.BlockSpec((S, W), lambda h, i: (0, 2 * G + h)),
                  pl.BlockSpec(u_incl.shape, lambda h, i: (0, 0))],
        out_specs=[pl.BlockSpec((blk, W), lambda h, i: (i, h)), saved_spec, saved_spec],
        out_shape=[_sds((S, n_heads * HEAD_DIM), BF16), saved, saved],
        args=(proj, proj, proj, u_incl), scratch=[pltpu.VMEM((hp, blk, blk), F32), pltpu.VMEM((hp, blk, CUMSUM_TERMS * blk), BF16)],
        sem=("parallel", "arbitrary"), comm=comm)


def _attn_bwd(proj, a_saved, b_saved, do, n_heads, blk, comm=None):
    S = proj.shape[0]
    nq = S // blk
    scale = HEAD_DIM ** -0.5
    lanes = blk // 128
    hp = 2 if n_heads % 2 == 0 else 1
    cols = [slice(h * HEAD_DIM, (h + 1) * HEAD_DIM) for h in range(hp)]
    l_incl = _tri(blk, lambda j, s: j <= s, GRAD_TERMS)

    def body(q_ref, k_ref, v_ref, a_ref, b_ref, do_ref, li_ref, dq_ref, dk_ref, dv_ref, dk_acc, dv_acc, g_buf, gl_buf):
        i = pl.program_id(1)

        @pl.when(i == 0)
        def _():
            dk_acc[...] = jnp.zeros_like(dk_acc)
            dv_acc[...] = jnp.zeros_like(dv_acc)

        qs = [q_ref[:, c].astype(BF16) for c in cols]
        dobs = [do_ref[:, c].astype(BF16) for c in cols]
        li = li_ref[...]

        def stage_products(j):
            ks = pl.multiple_of(j * blk, blk)
            for h in range(hp):
                vj = v_ref[pl.ds(ks, blk), cols[h]].astype(BF16)
                da = lax.dot_general(dobs[h], vj, (((1,), (1,)), ((), ())), preferred_element_type=F32)
                g = a_ref[h, j].astype(F32) * da
                g_buf[h] = g
                gl_buf[h] = _split(g, GRAD_TERMS)

        def grads(j, h, dq, run_g, causal):
            ks = pl.multiple_of(j * blk, blk)
            kj = k_ref[pl.ds(ks, blk), cols[h]].astype(BF16)
            gt = jnp.dot(gl_buf[h], li, preferred_element_type=F32)
            dz = g_buf[h] - b_ref[h, j].astype(F32) * (gt[:, :blk] + jnp.tile(run_g, (1, lanes)))
            if causal is not None:
                dz = jnp.where(causal, dz, 0.0)
            dzs = (dz * scale).astype(BF16)
            dk_acc[pl.ds(ks, blk), cols[h]] += lax.dot_general(dzs, qs[h], (((0,), (0,)), ((), ())),
                                                               preferred_element_type=F32)
            dv_acc[pl.ds(ks, blk), cols[h]] += lax.dot_general(a_ref[h, j], dobs[h], (((0,), (0,)), ((), ())),
                                                               preferred_element_type=F32)
            return dq + jnp.dot(dzs, kj, preferred_element_type=F32), run_g + gt[:, blk:]

        zero = jnp.zeros((blk, HEAD_DIM), F32)
        stage_products(0)

        def step(j, carry):
            out = tuple(grads(j, h, *carry[h], None) for h in range(hp))
            stage_products(j + 1)
            return out

        carry = lax.fori_loop(0, i, step, ((zero, zero),) * hp)
        causal = _causal(blk)
        carry = tuple(grads(i, h, *carry[h], causal) for h in range(hp))
        for h in range(hp):
            dq_ref[:, cols[h]] = carry[h][0].astype(BF16)

        @pl.when(i == nq - 1)
        def _():
            dk_ref[...] = dk_acc[...].astype(BF16)
            dv_ref[...] = dv_acc[...].astype(BF16)

    G = n_heads // hp
    W = hp * HEAD_DIM
    AW = n_heads * HEAD_DIM
    saved_spec = pl.BlockSpec((hp, None, nq, blk, blk), lambda h, i: (h, i, 0, 0, 0))
    return _call(
        body, name="attn_bwd", grid=(G, nq),
        in_specs=[pl.BlockSpec((blk, W), lambda h, i: (i, h)),
                  pl.BlockSpec((S, W), lambda h, i: (0, G + h)),
                  pl.BlockSpec((S, W), lambda h, i: (0, 2 * G + h)),
                  saved_spec, saved_spec,
                  pl.BlockSpec((blk, W), lambda h, i: (i, h)),
                  pl.BlockSpec(l_incl.shape, lambda h, i: (0, 0))],
        out_specs=[pl.BlockSpec((blk, W), lambda h, i: (i, h)),
                   pl.BlockSpec((S, W), lambda h, i: (0, h)),
                   pl.BlockSpec((S, W), lambda h, i: (0, h))],
        out_shape=[_sds((S, AW), BF16)] * 3, args=(proj, proj, proj, a_saved, b_saved, do, l_incl),
        scratch=[pltpu.VMEM((S, W), F32)] * 2 + [pltpu.VMEM((hp, blk, blk), F32),
                                                 pltpu.VMEM((hp, blk, GRAD_TERMS * blk), BF16)],
        sem=("parallel", "arbitrary"), comm=comm)


def _pool_count(r0, rows, w):
    t = r0 + lax.broadcasted_iota(jnp.int32, (rows, 1), 0)
    return jnp.minimum(t + 1, w).astype(F32)


def _pool_fwd(proj, col_blk, n_groups, width, chunk):
    S = proj.shape[0]
    H = POOL_HALO

    def body(u_ref, o_ref, pad_ref):
        g = pl.program_id(0)
        pad_ref[0:H, :] = jnp.zeros((H, width), F32)
        pad_ref[H:, :] = u_ref[...]
        for gi, w in enumerate(POOL_WINDOWS[:n_groups]):
            @pl.when(g == gi)
            def _(w=w):
                def one(c, _):
                    r0 = pl.multiple_of(c * chunk, chunk)
                    ext = pad_ref[pl.ds(r0, chunk + H), :]
                    s = ext
                    k = 1
                    while k < w:
                        s = s + pltpu.roll(s, k, 0)
                        k *= 2
                    o_ref[pl.ds(r0, chunk), :] = (s[H:] / _pool_count(r0, chunk, w) - ext[H:]).astype(BF16)
                    return 0

                lax.fori_loop(0, S // chunk, one, 0)

    return pl.pallas_call(
        body, name="pool_fwd", grid=(n_groups,),
        in_specs=[pl.BlockSpec((S, width), lambda g: (0, col_blk + g))],
        out_specs=pl.BlockSpec((S, width), lambda g: (0, g)), out_shape=_sds((S, n_groups * width), BF16),
        scratch_shapes=[pltpu.VMEM((S + H, width), F32)], compiler_params=_params("parallel"),
    )(proj)


def _pool_bwd(dpooled, n_groups, chunk):
    S = dpooled.shape[0]
    width = dpooled.shape[1] // n_groups
    H = POOL_HALO

    def body(d_ref, o_ref, pad_ref):
        g = pl.program_id(0)
        pad_ref[S:, :] = jnp.zeros((H, width), F32)
        for gi, w in enumerate(POOL_WINDOWS[:n_groups]):
            @pl.when(g == gi)
            def _(w=w):
                def fill(c, _):
                    r0 = pl.multiple_of(c * chunk, chunk)
                    pad_ref[pl.ds(r0, chunk), :] = d_ref[pl.ds(r0, chunk), :] / _pool_count(r0, chunk, w)
                    return 0

                lax.fori_loop(0, S // chunk, fill, 0)

                def one(c, _):
                    r0 = pl.multiple_of(c * chunk, chunk)
                    s = pad_ref[pl.ds(r0, chunk + H), :]
                    k = 1
                    while k < w:
                        s = s + pltpu.roll(s, chunk + H - k, 0)
                        k *= 2
                    o_ref[pl.ds(r0, chunk), :] = (s[:chunk] - d_ref[pl.ds(r0, chunk), :]).astype(BF16)
                    return 0

                lax.fori_loop(0, S // chunk, one, 0)

    return pl.pallas_call(
        body, name="pool_bwd", grid=(n_groups,),
        in_specs=[pl.BlockSpec((S, width), lambda g: (0, g))],
        out_specs=pl.BlockSpec((S, width), lambda g: (0, g)), out_shape=_sds((S, n_groups * width), BF16),
        scratch_shapes=[pltpu.VMEM((S + H, width), F32)], compiler_params=_params("parallel"),
    )(dpooled)


def _conv3(x_ext, w, b, shifted=None):
    x2, x1 = shifted if shifted is not None else (pltpu.roll(x_ext, 2, 0), pltpu.roll(x_ext, 1, 0))
    return b + x2 * w[0:1, :] + x1 * w[1:2, :] + x_ext * w[2:3, :]


def _gelu_parts(x):
    th = jnp.tanh(GELU_C0 * (x + GELU_C1 * (x * x * x)))
    return th, 0.5 * (1.0 + th)


def _conv_specs(S, F, cb):
    nb = F // cb
    return [pl.BlockSpec((S, cb), lambda j: (0, j)), pl.BlockSpec((S, cb), lambda j: (0, nb + j)),
            pl.BlockSpec((3, cb), lambda j: (0, j)), pl.BlockSpec((3, cb), lambda j: (0, nb + j)),
            pl.BlockSpec((1, cb), lambda j: (0, j)), pl.BlockSpec((1, cb), lambda j: (0, nb + j))]


def _conv_fwd(upre, cw, cb_, chunk):
    S, F2 = upre.shape
    F = F2 // 2
    cb = 128
    H = CONV_HALO

    def body(g_ref, v_ref, wg_ref, wv_ref, bg_ref, bv_ref, o_ref, pg_ref, pv_ref):
        pg_ref[0:H, :] = jnp.zeros((H, cb), F32)
        pv_ref[0:H, :] = jnp.zeros((H, cb), F32)
        pg_ref[H:, :] = g_ref[...]
        pv_ref[H:, :] = v_ref[...]
        wg, wv, bg, bv = wg_ref[...], wv_ref[...], bg_ref[...], bv_ref[...]

        def one(c, _):
            r0 = pl.multiple_of(c * chunk, chunk)
            up_g = _conv3(pg_ref[pl.ds(r0, chunk + H), :], wg, bg)[H:]
            up_v = _conv3(pv_ref[pl.ds(r0, chunk + H), :], wv, bv)[H:]
            _, cdf = _gelu_parts(up_g)
            o_ref[pl.ds(r0, chunk), :] = (up_g * cdf * up_v).astype(BF16)
            return 0

        lax.fori_loop(0, S // chunk, one, 0)

    return pl.pallas_call(
        body, name="conv_fwd", grid=(F // cb,), in_specs=_conv_specs(S, F, cb),
        out_specs=pl.BlockSpec((S, cb), lambda j: (0, j)), out_shape=_sds((S, F), BF16),
        scratch_shapes=[pltpu.VMEM((S + H, cb), F32)] * 2, compiler_params=_params("parallel"),
    )(upre, upre, cw, cw, cb_, cb_)


def _conv_bwd(upre, dact, cw, cb_, chunk, comm=None):
    S, F2 = upre.shape
    F = F2 // 2
    cb = 128
    H = CONV_HALO
    E = chunk + 2 * H

    def body(g_ref, v_ref, wg_ref, wv_ref, bg_ref, bv_ref, d_ref, dg_ref, dv_ref, dwg_ref, dwv_ref, dbg_ref, dbv_ref,
             pg_ref, pv_ref, pd_ref):
        for p, src in ((pg_ref, g_ref), (pv_ref, v_ref), (pd_ref, d_ref)):
            p[0:H, :] = jnp.zeros((H, cb), F32)
            p[H:S + H, :] = src[...]
            p[S + H:, :] = jnp.zeros((H, cb), F32)
        wg, wv, bg, bv = wg_ref[...], wv_ref[...], bg_ref[...], bv_ref[...]

        def taps_bwd(d, w):
            return d * w[2:3, :] + pltpu.roll(d, E - 1, 0) * w[1:2, :] + pltpu.roll(d, E - 2, 0) * w[0:1, :]

        def wsum(d, x, x2, x1):
            dc = d[H:H + chunk]
            return [jnp.sum(dc * x2[H:H + chunk], axis=0, keepdims=True),
                    jnp.sum(dc * x1[H:H + chunk], axis=0, keepdims=True),
                    jnp.sum(dc * x[H:H + chunk], axis=0, keepdims=True),
                    jnp.sum(dc, axis=0, keepdims=True)]

        def one(c, acc):
            r0 = pl.multiple_of(c * chunk, chunk)
            xg = pg_ref[pl.ds(r0, E), :]
            xv = pv_ref[pl.ds(r0, E), :]
            d = pd_ref[pl.ds(r0, E), :]
            sg = (pltpu.roll(xg, 2, 0), pltpu.roll(xg, 1, 0))
            sv = (pltpu.roll(xv, 2, 0), pltpu.roll(xv, 1, 0))
            up_g = _conv3(xg, wg, bg, sg)
            up_v = _conv3(xv, wv, bv, sv)
            th, cdf = _gelu_parts(up_g)
            dgelu = cdf + 0.5 * up_g * (1.0 - th * th) * (GELU_C0 * (1.0 + 3.0 * GELU_C1 * (up_g * up_g)))
            dgate = d * up_v * dgelu
            dval = d * (up_g * cdf)
            dg_ref[pl.ds(r0, chunk), :] = taps_bwd(dgate, wg)[H:H + chunk].astype(BF16)
            dv_ref[pl.ds(r0, chunk), :] = taps_bwd(dval, wv)[H:H + chunk].astype(BF16)
            return tuple(a + b for a, b in zip(acc, wsum(dgate, xg, *sg) + wsum(dval, xv, *sv)))

        zero = jnp.zeros((1, cb), F32)
        acc = lax.fori_loop(0, S // chunk, one, (zero,) * 8)
        dwg_ref[...] = jnp.concatenate(acc[0:3], axis=0)
        dbg_ref[...] = acc[3]
        dwv_ref[...] = jnp.concatenate(acc[4:7], axis=0)
        dbv_ref[...] = acc[7]

    col = lambda rows: pl.BlockSpec((rows, cb), lambda j: (0, j))
    return _call(
        body, name="conv_bwd", grid=(F // cb,), in_specs=_conv_specs(S, F, cb) + [col(S)],
        out_specs=[col(S), col(S), col(3), col(3), col(1), col(1)],
        out_shape=[_sds((S, F), BF16), _sds((S, F), BF16), _sds((3, F), F32), _sds((3, F), F32), _sds((1, F), F32),
                   _sds((1, F), F32)],
        args=(upre, upre, cw, cw, cb_, cb_, dact), scratch=[pltpu.VMEM((S + 2 * H, cb), F32)] * 3, sem=("parallel",),
        comm=comm)


def _position():
    x, y, c = lax.axis_index("x"), lax.axis_index("y"), lax.axis_index("c")
    return x, y, c, 4 * x + 2 * y + c


def _peer(x, y, c, d):
    px = 1 - x if d & 4 else x
    py = 1 - y if d & 2 else y
    pc = 1 - c if d & 1 else c
    return (px, py, pc), 4 * px + 2 * py + pc


def _all_gather_comm(tensors):
    nt = len(tensors)
    outs = [_sds((N_DEV,) + t.shape, t.dtype) for t in tensors]
    sems = [pltpu.SemaphoreType.DMA((7 * nt,)), pltpu.SemaphoreType.DMA((7 * nt,)), pltpu.SemaphoreType.DMA((nt,))]

    def parts(ins, outs_, sem_refs):
        send, recv, loc = sem_refs
        x, y, c, me = _position()
        chips = [(1 - x, y), (x, 1 - y), (1 - x, 1 - y)]

        def copy(t, k, block, to, src=None):
            slot = outs_[t].at[4 * block[0] + 2 * block[1] + block[2]]
            return pltpu.make_async_remote_copy(src_ref=slot if src is None else src, dst_ref=slot,
                                                send_sem=send.at[7 * t + k], recv_sem=recv.at[7 * t + k], device_id=to,
                                                device_id_type=MESH)

        def mine(t):
            return pltpu.make_async_copy(ins[t], outs_[t].at[me], loc.at[t])

        return (x, y, c), (x, y, 1 - c), chips, copy, mine

    def start(ins, outs_, sem_refs):
        me, sibling, chips, copy, mine = parts(ins, outs_, sem_refs)
        for t in range(nt):
            mine(t).start()
            copy(t, 0, me, sibling, src=ins[t]).start()
            for j, chip in enumerate(chips):
                copy(t, 1 + j, me, (*chip, me[2]), src=ins[t]).start()

    def finish(ins, outs_, sem_refs):
        me, sibling, chips, copy, mine = parts(ins, outs_, sem_refs)
        c = me[2]
        for t in range(nt):
            for j, chip in enumerate(chips):
                copy(t, 1 + j, (*chip, c), me).wait_recv()
                copy(t, 4 + j, (*chip, c), sibling).start()
        for t in range(nt):
            copy(t, 0, sibling, me).wait_recv()
            for j, chip in enumerate(chips):
                copy(t, 4 + j, (*chip, 1 - c), me).wait_recv()
        for t in range(nt):
            copy(t, 0, me, sibling, src=ins[t]).wait_send()
            for j, chip in enumerate(chips):
                copy(t, 1 + j, me, (*chip, c), src=ins[t]).wait_send()
                copy(t, 4 + j, (*chip, c), sibling).wait_send()
            mine(t).wait()

    return _Comm(tensors, outs, sems, start, finish)


def _reduce_scatter_comm(tensors):
    nt = len(tensors)
    outs = [_sds(t.shape, t.dtype) for t in tensors]
    sems = [pltpu.SemaphoreType.DMA((7 * nt,)), pltpu.SemaphoreType.DMA((7 * nt,)), pltpu.SemaphoreType.DMA((nt,))]

    def local(ins, outs_, sem_refs, t, me):
        return pltpu.make_async_copy(ins[t].at[me], outs_[t].at[me], sem_refs[2].at[t])

    def remote(ins, outs_, sem_refs, t, d, inbound):
        x, y, c, me = _position()
        peer, peer_idx = _peer(x, y, c, d)
        k = 7 * t + d - 1
        src, dst, to = (ins[t].at[me], outs_[t].at[peer_idx], (x, y, c)) if inbound else (ins[t].at[peer_idx], outs_[t].at[me], peer)
        return pltpu.make_async_remote_copy(src_ref=src, dst_ref=dst, send_sem=sem_refs[0].at[k], recv_sem=sem_refs[1].at[k],
                                            device_id=to, device_id_type=MESH)

    def start(ins, outs_, sem_refs):
        me = _position()[3]
        for t in range(nt):
            local(ins, outs_, sem_refs, t, me).start()
            for d in range(1, N_DEV):
                remote(ins, outs_, sem_refs, t, d, False).start()

    def finish(ins, outs_, sem_refs):
        me = _position()[3]
        for t in range(nt):
            for d in range(1, N_DEV):
                remote(ins, outs_, sem_refs, t, d, True).wait_recv()
        for t in range(nt):
            for d in range(1, N_DEV):
                remote(ins, outs_, sem_refs, t, d, False).wait_send()
            local(ins, outs_, sem_refs, t, me).wait()

    return _Comm(tensors, outs, sems, start, finish)


HBM_SPEC = pl.BlockSpec(memory_space=pltpu.HBM)
SEM_SPEC = pl.BlockSpec(memory_space=pltpu.SEMAPHORE)
DATAFLOW = pltpu.SideEffectType.DATAFLOW_SIDE_EFFECTING


def _scatter_copy(g_ref, land_ref, send_sems, recv_sems, d):
    x, y, c, me = _position()
    peer, peer_idx = _peer(x, y, c, d)
    return pltpu.make_async_remote_copy(src_ref=g_ref.at[peer_idx], dst_ref=land_ref.at[me], send_sem=send_sems.at[d - 1],
                                        recv_sem=recv_sems.at[d - 1], device_id=peer, device_id_type=MESH)


def _reduce_scatter_start(g, name):
    def body(g_ref, land_ref, send_sems, recv_sems, g_thru, land_thru, token):
        for d in range(1, N_DEV):
            _scatter_copy(g_ref, land_ref, send_sems, recv_sems, d).start()
        token[...] = jnp.zeros_like(token)

    return pl.pallas_call(
        body, name=name,
        out_shape=(pltpu.SemaphoreType.DMA((N_DEV - 1,)), pltpu.SemaphoreType.DMA((N_DEV - 1,)), pltpu.HBM(g.shape, g.dtype),
                   pltpu.HBM(g.shape, g.dtype), _sds((8, 128), F32)),
        in_specs=(HBM_SPEC, HBM_SPEC), out_specs=(SEM_SPEC, SEM_SPEC, HBM_SPEC, HBM_SPEC, pl.BlockSpec(memory_space=pltpu.VMEM)),
        input_output_aliases={0: 2, 1: 3}, compiler_params=pltpu.CompilerParams(has_side_effects=DATAFLOW),
    )(pltpu.with_memory_space_constraint(g, pltpu.HBM), pltpu.with_memory_space_constraint(lax.empty(g.shape, g.dtype), pltpu.HBM))


def _reduce_scatter_wait(send_sems, recv_sems, g_thru, land_thru, after, name):
    def body(g_ref, land_ref, send_sems, recv_sems, *rest):
        for d in range(1, N_DEV):
            copy = _scatter_copy(g_ref, land_ref, send_sems, recv_sems, d)
            copy.wait_send()
            copy.wait_recv()

    return pl.pallas_call(
        body, name=name, out_shape=(pltpu.HBM(g_thru.shape, g_thru.dtype), pltpu.HBM(g_thru.shape, g_thru.dtype)),
        in_specs=(HBM_SPEC, HBM_SPEC, SEM_SPEC, SEM_SPEC) + (ANY,) * len(after), out_specs=(HBM_SPEC, HBM_SPEC),
        input_output_aliases={0: 0, 1: 1}, compiler_params=pltpu.CompilerParams(has_side_effects=DATAFLOW),
    )(g_thru, land_thru, send_sems, recv_sems, *after)


def _all_reduce_small(part, after):
    r, W = part.shape

    def body(p_ref, after_ref, o_ref, g_ref, send_sems, recv_sems):
        x, y, c, me = _position()
        sends = []
        for d in range(1, N_DEV):
            peer, _ = _peer(x, y, c, d)
            cp = pltpu.make_async_remote_copy(src_ref=p_ref, dst_ref=g_ref.at[me], send_sem=send_sems.at[d - 1],
                                              recv_sem=recv_sems.at[d - 1], device_id=peer, device_id_type=MESH)
            cp.start()
            sends.append(cp)
        g_ref[me] = p_ref[...]
        for d in range(1, N_DEV):
            _, peer_idx = _peer(x, y, c, d)
            pltpu.make_async_remote_copy(src_ref=p_ref, dst_ref=g_ref.at[peer_idx], send_sem=send_sems.at[d - 1],
                                         recv_sem=recv_sems.at[d - 1], device_id=(x, y, c), device_id_type=MESH).wait_recv()
        for cp in sends:
            cp.wait_send()
        acc = g_ref[0]
        for i in range(1, N_DEV):
            acc = acc + g_ref[i]
        o_ref[...] = acc

    vmem = pl.BlockSpec(memory_space=pltpu.VMEM)
    return pl.pallas_call(
        body, name="all_reduce_small", in_specs=[vmem, ANY], out_specs=[vmem, vmem],
        out_shape=[_sds((r, W), F32), _sds((N_DEV, r, W), F32)],
        scratch_shapes=[pltpu.SemaphoreType.DMA((7,)), pltpu.SemaphoreType.DMA((7,))],
        compiler_params=pltpu.CompilerParams(has_side_effects=True, vmem_limit_bytes=VMEM_LIMIT),
    )(part, after)[0]


def _adamw_math(w, g, m, v):
    m = ADAM_B1 * m + (1.0 - ADAM_B1) * g
    v = ADAM_B2 * v + (1.0 - ADAM_B2) * (g * g)
    m_hat = m / (1.0 - ADAM_B1 ** ADAM_STEP)
    v_hat = v / (1.0 - ADAM_B2 ** ADAM_STEP)
    return -ADAM_LR * (m_hat / (jnp.sqrt(v_hat) + ADAM_EPS) + ADAM_WD * w), m, v


def _adamw(w, g, m, v, name):
    rows, cols = w.shape
    tr = _tile(rows, max(8, (2**18 // cols) // 8 * 8), 8)

    def body(w_ref, g_ref, m_ref, v_ref, d_ref, nm_ref, nv_ref):
        d_ref[...], nm_ref[...], nv_ref[...] = _adamw_math(w_ref[...], g_ref[...], m_ref[...], v_ref[...])

    spec = pl.BlockSpec((tr, cols), lambda i: (i, 0))
    return pl.pallas_call(
        body, name=name, grid=(rows // tr,), in_specs=[spec] * 4, out_specs=[spec] * 3,
        out_shape=[_sds((rows, cols), F32)] * 3, compiler_params=_params("parallel"),
    )(w, g, m, v)


def _adamw_sum(recv, w, m, v, name, after=()):
    n, rows, cols = recv.shape
    tr = _tile(rows, max(16, (2**17 // cols) // 16 * 16), 16)

    def body(r_ref, w_ref, m_ref, v_ref, g_ref, d_ref, nm_ref, nv_ref):
        g = r_ref[0].astype(F32)
        for i in range(1, n):
            g = g + r_ref[i].astype(F32)
        g_ref[...] = g
        d_ref[...], nm_ref[...], nv_ref[...] = _adamw_math(w_ref[...], g, m_ref[...], v_ref[...])

    spec = pl.BlockSpec((tr, cols), lambda i: (i, 0))
    return _call(body, name=name, grid=(rows // tr,), in_specs=[pl.BlockSpec((n, tr, cols), lambda i: (0, i, 0))] + [spec] * 3,
                 out_specs=[spec] * 4, out_shape=[_sds((rows, cols), F32)] * 4, args=(recv, w, m, v), sem=("parallel",),
                 after=after)


COLUMN_CUT = ("w_in", "w_attn_branch", "w_pool_branch", "w_up", "w_ple")
ROW_CUT = ("w_out", "w_down", "w_ple_gate")
REPLICATED = ("norm_mix_pre", "pool_scale", "norm_mix_post", "norm_ffn_pre", "conv_b", "norm_ffn_post", "norm_ple_post")
WEIGHTS = ("norm_mix_pre", "w_in", "w_attn_branch", "w_pool_group", "pool_scale", "w_pool_branch", "w_out", "norm_mix_post",
           "norm_ffn_pre", "w_up", "conv_w", "conv_b", "w_down", "norm_ffn_post", "w_ple", "w_ple_gate", "norm_ple_post")


def _size(shape):
    n = 1
    for s in shape:
        n *= s
    return n


def _pad_rows(flat, row_align):
    n = flat.shape[-1]
    per = PACK_W * row_align
    total = -(-n // per) * per
    return jnp.pad(flat, [(0, total - n)]).reshape(total // PACK_W, PACK_W)


def _natural(shard_major):
    n, r, c = shard_major.shape
    return shard_major.reshape(n * r, c)


def kernel(x, p, norm_mix_pre, w_in, w_attn_branch, w_pool_group, pool_scale, w_pool_branch, w_out, norm_mix_post, norm_ffn_pre, w_up, conv_w, conv_b, w_down, norm_ffn_post, w_ple, w_ple_gate, norm_ple_post, loss_target, m_norm_mix_pre, m_w_in, m_w_attn_branch, m_w_pool_group, m_pool_scale, m_w_pool_branch, m_w_out, m_norm_mix_post, m_norm_ffn_pre, m_w_up, m_conv_w, m_conv_b, m_w_down, m_norm_ffn_post, m_w_ple, m_w_ple_gate, m_norm_ple_post, v_norm_mix_pre, v_w_in, v_w_attn_branch, v_w_pool_group, v_pool_scale, v_w_pool_branch, v_w_out, v_norm_mix_post, v_norm_ffn_pre, v_w_up, v_conv_w, v_conv_b, v_w_down, v_norm_ffn_post, v_w_ple, v_w_ple_gate, v_norm_ple_post):
    given = dict(locals())
    wts = {n: given[n][0] for n in WEIGHTS}
    mom = {n: given["m_" + n][0] for n in WEIGHTS}
    var = {n: given["v_" + n][0] for n in WEIGHTS}
    xs = x[0]
    ps_in = p[0, 0]
    tgt = loss_target[0]
    S, D = xs.shape
    AW = wts["w_attn_branch"].shape[0]
    PW = wts["w_pool_branch"].shape[0]
    G = wts["w_pool_group"].shape[0]
    PGW = PW // G
    H = AW // HEAD_DIM
    F = wts["w_down"].shape[0] * N_DEV
    assert (3 * AW) % PGW == 0 and (3 * AW + PW) % D == 0 and PGW % 128 == 0 and F % 128 == 0
    tr = _tile(S, 256, 16)
    blk = _tile(S, 256, 128)
    chunk = _tile(S, 256, 8)
    me = 4 * lax.axis_index("x") + 2 * lax.axis_index("y") + lax.axis_index("c")

    cw_shape = wts["conv_w"].shape
    conv_b_row = wts["conv_b"].reshape(1, -1)
    g1, g2, g3, g4, g5 = (wts[n].reshape(1, D) for n in
                          ("norm_mix_pre", "norm_mix_post", "norm_ffn_pre", "norm_ffn_post", "norm_ple_post"))
    pscale = wts["pool_scale"].reshape(1, PW)
    big = 4096
    wb = {n: wts[n].astype(BF16) for n in COLUMN_CUT + ROW_CUT}
    wb["w_pool_group"] = wts["w_pool_group"].astype(BF16).reshape(G * PGW // N_DEV, PGW)

    h, w_in = _rms_fwd(xs, g1, tr, _all_gather_comm([wb["w_in"]]))
    proj, w_ab, w_pg, w_pb, w_out = _mm(
        h, w_in, b_sm=True, name="mm_in", tk=2048,
        comm=_all_gather_comm([wb["w_attn_branch"], wb["w_pool_group"], wb["w_pool_branch"], wb["w_out"]]))
    w_pg = jnp.moveaxis(w_pg.reshape(N_DEV, G, PGW // N_DEV, PGW), 0, 1).reshape(G, PGW, PGW)
    w_out = _natural(w_out)
    attn, a_saved, b_saved, w_up, conv_w_all = _attn_fwd(proj, H, blk, _all_gather_comm([wb["w_up"], wts["conv_w"]]))
    conv_w_full = jnp.moveaxis(conv_w_all, 0, 1).reshape(cw_shape[0], N_DEV * cw_shape[1])
    y_attn = _mm(attn, w_ab, b_sm=True, name="mm_attn_branch", tm=big, tn=256, tk=big)
    pooled = _pool_fwd(proj, 3 * AW // PGW, G, PGW, chunk)
    pg, ps = _pool_group_fwd(pooled, w_pg, pscale)
    y_pool = _mm(ps, w_pb, b_sm=True, name="mm_pool_branch", tm=big, tn=256, tk=big)
    gate_cb = (3 * AW + PW) // D
    mixed = _gate_mix(proj, y_attn, y_pool, gate_cb, tr)
    mo = _mm(mixed, w_out, name="mm_out", tk=2048)
    x1, h2 = _resid_rms2(xs, mo, g2, g3, tr)
    upre, w_down, w_ple, w_pleg = _mm(h2, w_up, b_sm=True, name="mm_up", tn=2048, tk=1024,
                                      comm=_all_gather_comm([wb["w_down"], wb["w_ple"], wb["w_ple_gate"]]))
    w_down, w_pleg = _natural(w_down), _natural(w_pleg)
    act = _conv_fwd(upre, conv_w_full, conv_b_row, chunk)
    yf = _mm(act, w_down, name="mm_down", tk=1408)
    x2, x2b = _resid_rms(x1, yf, g4, tr)
    e = _mm(ps_in, w_ple, b_sm=True, name="mm_ple", tm=big, tn=256, tk=big)
    gl = _mm(x2b, w_pleg, name="mm_ple_gate", tk=2048)
    loss_part, dx3, de, dgl, dg5 = _ple_loss(gl, e, x2, tgt, g5, tr)

    shards = lambda natural: natural.reshape((N_DEV, natural.shape[0] // N_DEV) + natural.shape[1:])
    recv = {}
    dw_ple = _mm(ps_in, de, ta=True, out_sm=True, out_dtype=BF16, name="mm_d_w_ple", tm=256, tn=256, tk=big)
    dw_pleg = shards(_mm(x2b, dgl, ta=True, out_dtype=BF16, name="mm_d_w_ple_gate"))
    dx2g = _mm(dgl, w_pleg, tb=True, name="mm_d_x2", tk=2048)
    dx2, dyf, dg4 = _rms_bwd_a(dx3, dx2g, yf, g4, tr)
    dw_down = shards(_mm(act, dyf, ta=True, out_dtype=BF16, name="mm_d_w_down"))
    dact, recv["w_ple"], recv["w_ple_gate"] = _mm(dyf, w_down, tb=True, name="mm_d_act", tn=1408, tk=1024,
                                                  comm=_reduce_scatter_comm([dw_ple, dw_pleg]))
    dup_g, dup_v, dcw_g, dcw_v, dcb_g, dcb_v, recv["w_down"] = _conv_bwd(upre, dact, conv_w_full, conv_b_row, chunk,
                                                                          _reduce_scatter_comm([dw_down]))
    dw_up = _mm(h2, dup_g, b2=dup_v, ta=True, out_sm=True, out_dtype=BF16, name="mm_d_w_up", tn=2048)
    up_send, up_recv, dw_up, up_land, up_token = _reduce_scatter_start(dw_up, "rs_w_up_start")
    dh2 = _mm(dup_g, w_up, a2=dup_v, tb=True, b_sm=True, name="mm_d_h2", tk=2048, after=(up_token,))
    dx1, dmo, dg3, dg2 = _rms_bwd_b(dx2, dh2, x1, g3, mo, g2, tr)
    dmixed = _mm(dmo, w_out, tb=True, name="mm_d_mixed", tk=2048)
    dw_out = shards(_mm(mixed, dmo, ta=True, out_dtype=BF16, name="mm_d_w_out"))
    dya, dyp, dga, dgp = _gate_bwd(dmixed, proj, y_attn, y_pool, gate_cb, tr)
    dps = _mm(dyp, w_pb, tb=True, b_sm=True, name="mm_d_ps", tm=2048, tk=256)
    dw_pb = _mm(ps, dyp, ta=True, out_sm=True, out_dtype=BF16, name="mm_d_w_pool_branch", tn=256, tk=big)
    dpg, dscale = _scale_bwd(dps, pg, pscale, tr)
    dpooled = _pool_group_bwd_x(dpg, w_pg)
    dw_pg = _pool_group_bwd_w(pooled, dpg, G)
    dw_pg = jnp.moveaxis(dw_pg.astype(BF16).reshape(G, N_DEV, PGW // N_DEV, PGW), 1, 0).reshape(N_DEV, G * PGW // N_DEV, PGW)
    du = _pool_bwd(dpooled, G, chunk)
    dattn = _mm(dya, w_ab, tb=True, b_sm=True, name="mm_d_attn", tm=2048, tk=256)
    dw_ab = _mm(attn, dya, ta=True, out_sm=True, out_dtype=BF16, name="mm_d_w_attn_branch", tn=256, tk=big)
    dq, dk, dv, recv["w_out"], recv["w_pool_branch"], recv["w_pool_group"], recv["w_attn_branch"] = _attn_bwd(
        proj, a_saved, b_saved, dattn, H, blk, _reduce_scatter_comm([dw_out, dw_pb, dw_pg, dw_ab]))
    dw_up, up_land = _reduce_scatter_wait(up_send, up_recv, dw_up, up_land, (dq,), "rs_w_up_wait")
    recv["w_up"] = lax.dynamic_update_slice_in_dim(up_land, lax.dynamic_index_in_dim(dw_up, me, 0, keepdims=True), me, 0)
    dproj = jnp.concatenate([dq, dk, dv, du, dga, dgp], axis=1)
    dw_in = _mm(h, dproj, ta=True, out_sm=True, out_dtype=BF16, name="mm_d_w_in")
    in_send, in_recv, dw_in, in_land, token = _reduce_scatter_start(dw_in, "rs_w_in_start")
    dh = _mm(dproj, w_in, tb=True, b_sm=True, name="mm_d_h", tm=big, tn=512, tk=1024, after=(token,))
    grad_x, dg1 = _rms_bwd_c(dx1, dh, xs, g1, tr)

    gshard, delta, new_m, new_v = {}, {}, {}, {}

    def adamw_cut(n, after=()):
        shp = wts[n].shape
        two_d = (_size(shp[:-1]), shp[-1])
        g_, d_, m_, v_ = _adamw_sum(recv[n].reshape((N_DEV,) + two_d), wts[n].reshape(two_d), mom[n].reshape(two_d),
                                    var[n].reshape(two_d), "adamw_" + n, after)
        gshard[n], delta[n], new_m[n], new_v[n] = g_.reshape(shp), d_.reshape(shp), m_.reshape(shp), v_.reshape(shp)

    for n in COLUMN_CUT[1:] + ROW_CUT + ("w_pool_group",):
        adamw_cut(n, (token,))

    assert COLUMN_CUT[0] == "w_in"
    dw_in, in_land = _reduce_scatter_wait(in_send, in_recv, dw_in, in_land, (grad_x,) + tuple(delta[n] for n in delta),
                                          "rs_w_in_wait")

    dconv_w = jnp.concatenate([dcw_g, dcw_v], axis=1)
    dconv_b = jnp.concatenate([dcb_g, dcb_v], axis=1).reshape(-1)
    rep_parts = {"norm_mix_pre": dg1, "pool_scale": dscale, "norm_mix_post": dg2, "norm_ffn_pre": dg3, "conv_b": dconv_b,
                 "norm_ffn_post": dg4, "norm_ple_post": dg5}
    small = jnp.concatenate([rep_parts[n].reshape(-1) for n in REPLICATED] + [dconv_w.reshape(-1)])
    n_small = small.shape[0]
    small_sum = _all_reduce_small(_pad_rows(small, 8), in_land).reshape(-1)[:n_small]
    off = 0
    for n in REPLICATED:
        sz = _size(wts[n].shape)
        gshard[n] = small_sum[off:off + sz].reshape(wts[n].shape)
        off += sz
    dconv_w_sum = small_sum[off:off + 3 * 2 * F].reshape(3, 2 * F)
    gshard["conv_w"] = lax.dynamic_slice_in_dim(dconv_w_sum, me * cw_shape[1], cw_shape[1], axis=1)

    delta["conv_w"], new_m["conv_w"], new_v["conv_w"] = _adamw(wts["conv_w"], gshard["conv_w"], mom["conv_w"], var["conv_w"],
                                                               "adamw_conv_w")
    rep_sizes = [_size(wts[n].shape) for n in REPLICATED]
    n_rep = sum(rep_sizes)
    cat = lambda t: _pad_rows(jnp.concatenate([t[n].reshape(-1) for n in REPLICATED]), 8)
    d_, m_, v_ = _adamw(cat(wts), cat(gshard), cat(mom), cat(var), "adamw_replicated")
    off = 0
    for n, sz in zip(REPLICATED, rep_sizes):
        shp = wts[n].shape
        delta[n], new_m[n], new_v[n] = (t.reshape(-1)[off:off + sz].reshape(shp) for t in (d_, m_, v_))
        off += sz
    assert off == n_rep

    own = lax.dynamic_index_in_dim(dw_in, me, 0, keepdims=True)
    recv["w_in"] = lax.dynamic_update_slice_in_dim(in_land, own, me, 0)
    adamw_cut("w_in")

    loss = lax.psum(loss_part[0, 0], ("x", "y", "c"))
    lead = lambda t: t[None]
    return (loss, grad_x[None], *[lead(gshard[n]) for n in WEIGHTS], *[lead(delta[n]) for n in WEIGHTS],
            *[lead(new_m[n]) for n in WEIGHTS], *[lead(new_v[n]) for n in WEIGHTS])
```

```python
import functools

import jax
import jax.numpy as jnp
from jax import lax
from jax.experimental import pallas as pl
from jax.experimental.pallas import tpu as pltpu

F32 = jnp.float32
BF16 = jnp.bfloat16
MESH = pl.DeviceIdType.MESH

EPS = 1e-6
HEAD_DIM = 128
POOL_WINDOWS = (2, 4, 8, 16)
POOL_HALO = 16
CONV_HALO = 8
GELU_C0 = 0.7978845608028654
GELU_C1 = 0.044715
ADAM_LR = 0.001
ADAM_B1 = 0.9
ADAM_B2 = 0.999
ADAM_EPS = 1e-08
ADAM_WD = 0.01
ADAM_STEP = 10
N_DEV = 8
PACK_W = 1024
VMEM_LIMIT = 56 * 2**20
ANY = pl.BlockSpec(memory_space=pl.ANY)


def _params(*sem):
    return pltpu.CompilerParams(dimension_semantics=sem, vmem_limit_bytes=VMEM_LIMIT)


def _sds(shape, dtype):
    return jax.ShapeDtypeStruct(shape, dtype)


def _tile(dim, target, align):
    if dim <= target:
        return dim
    t = (target // align) * align
    while t >= align:
        if dim % t == 0:
            return t
        t -= align
    return dim


def _sigmoid(x):
    return 1.0 / (1.0 + jnp.exp(-x))


class _Comm:
    def __init__(self, ins, outs, sems, start, finish):
        self.ins, self.outs, self.sems, self.start, self.finish = list(ins), list(outs), list(sems), start, finish


def _call(body, *, name, grid, in_specs, out_specs, out_shape, args, scratch=(), sem=(), comm=None, after=()):
    in_specs, out_specs, out_shape, scratch = list(in_specs), list(out_specs), list(out_shape), list(scratch)
    if after:
        assert comm is None
        n_in = len(in_specs)
        return pl.pallas_call(lambda *refs: body(*refs[:n_in], *refs[n_in + len(after):]), name=name, grid=grid,
                              in_specs=in_specs + [ANY] * len(after), out_specs=out_specs, out_shape=out_shape,
                              scratch_shapes=scratch, compiler_params=_params(*sem))(*args, *after)
    if comm is None:
        return pl.pallas_call(body, name=name, grid=grid, in_specs=in_specs, out_specs=out_specs, out_shape=out_shape,
                              scratch_shapes=scratch, compiler_params=_params(*sem))(*args)
    n_in, n_out, n_scr, n_ci, n_co = len(in_specs), len(out_specs), len(scratch), len(comm.ins), len(comm.outs)

    def wrapped(*refs):
        ins, refs = refs[:n_in], refs[n_in:]
        c_ins, refs = refs[:n_ci], refs[n_ci:]
        outs, refs = refs[:n_out], refs[n_out:]
        c_outs, refs = refs[:n_co], refs[n_co:]
        scr, c_sems = refs[:n_scr], refs[n_scr:]
        first = last = None
        for axis, size in enumerate(grid):
            at_start, at_end = pl.program_id(axis) == 0, pl.program_id(axis) == size - 1
            first = at_start if first is None else jnp.logical_and(first, at_start)
            last = at_end if last is None else jnp.logical_and(last, at_end)
        if grid:
            pl.when(first)(lambda: comm.start(c_ins, c_outs, c_sems))
            body(*ins, *outs, *scr)
            pl.when(last)(lambda: comm.finish(c_ins, c_outs, c_sems))
        else:
            comm.start(c_ins, c_outs, c_sems)
            body(*ins, *outs, *scr)
            comm.finish(c_ins, c_outs, c_sems)

    return pl.pallas_call(
        wrapped, name=name, grid=grid, in_specs=in_specs + [ANY] * n_ci, out_specs=out_specs + [ANY] * n_co,
        out_shape=out_shape + comm.outs, scratch_shapes=scratch + comm.sems,
        compiler_params=pltpu.CompilerParams(dimension_semantics=("arbitrary",) * len(grid), vmem_limit_bytes=VMEM_LIMIT,
                                             has_side_effects=True),
    )(*args, *comm.ins)


def _mm(a, b, *, ta=False, tb=False, b_sm=False, out_sm=False, out_dtype=F32, name, tm=2048, tn=1024, tk=1024, comm=None,
        after=(), a2=None, b2=None):
    M, K = (a.shape[1], a.shape[0]) if ta else a.shape
    if a2 is not None:
        assert not ta and a2.shape == a.shape
        K = 2 * K
    if b_sm:
        n_sl, rows, per = b.shape
        N = rows if tb else n_sl * per
        assert K == (n_sl * per if tb else rows)
    else:
        N = b.shape[0] if tb else b.shape[1]
    if b2 is not None:
        assert not tb and not b_sm and b2.shape == b.shape
        N = 2 * N
    tm = _tile(M, tm, 128)
    tn = _tile(per if (b_sm and not tb) else N // N_DEV if out_sm else N // 2 if b2 is not None else N, tn, 128)
    tk = _tile(per if (b_sm and tb) else K // 2 if a2 is not None else K, tk, 128)
    nk = K // tk
    kh, jh = nk // 2, (N // tn) // 2
    a_spec = pl.BlockSpec((tk, tm), lambda i, j, k: (k, i)) if ta else pl.BlockSpec((tm, tk), lambda i, j, k: (i, k))
    if a2 is not None:
        a_spec = pl.BlockSpec((tm, tk), lambda i, j, k: (i, jnp.minimum(k, kh - 1)))
        a2_spec = pl.BlockSpec((tm, tk), lambda i, j, k: (i, jnp.maximum(k - kh, 0)))
    if b2 is not None:
        b_spec = pl.BlockSpec((tk, tn), lambda i, j, k: (jnp.where(j < jh, k, 0), jnp.minimum(j, jh - 1)))
        b2_spec = pl.BlockSpec((tk, tn), lambda i, j, k: (jnp.where(j < jh, 0, k), jnp.maximum(j - jh, 0)))
    elif not b_sm:
        b_spec = pl.BlockSpec((tn, tk), lambda i, j, k: (j, k)) if tb else pl.BlockSpec((tk, tn), lambda i, j, k: (k, j))
    elif tb:
        kp = per // tk
        b_spec = pl.BlockSpec((None, tn, tk), lambda i, j, k: (k // kp, j, k % kp))
    else:
        jp = per // tn
        b_spec = pl.BlockSpec((None, tk, tn), lambda i, j, k: (j // jp, k, j % jp))
    if out_sm:
        jo = (N // N_DEV) // tn
        o_spec = pl.BlockSpec((None, tm, tn), lambda i, j, k: (j // jo, i, j % jo))
        o_shape = _sds((N_DEV, M, N // N_DEV), out_dtype)
    else:
        o_spec = pl.BlockSpec((tm, tn), lambda i, j, k: (i, j))
        o_shape = _sds((M, N), out_dtype)
    dims = (((0 if ta else 1,), (1 if tb else 0,)), ((), ()))

    def product(a_ref, b_ref):
        return lax.dot_general(a_ref[...].astype(BF16), b_ref[...].astype(BF16), dims, preferred_element_type=F32)

    def body(a_ref, b_ref, o_ref, acc_ref):
        k = pl.program_id(2)

        @pl.when(k == 0)
        def _():
            acc_ref[...] = jnp.zeros_like(acc_ref)

        acc_ref[...] += product(a_ref, b_ref)

        @pl.when(k == nk - 1)
        def _():
            o_ref[...] = acc_ref[...].astype(o_ref.dtype)

    def body_one_step(a_ref, b_ref, o_ref):
        o_ref[...] = product(a_ref, b_ref).astype(o_ref.dtype)

    def body_split(a_ref, x2_ref, b_ref, o_ref, acc_ref):
        j, k = pl.program_id(1), pl.program_id(2)
        second = k >= kh if a2 is not None else j >= jh

        @pl.when(k == 0)
        def _():
            acc_ref[...] = jnp.zeros_like(acc_ref)

        @pl.when(jnp.logical_not(second))
        def _():
            acc_ref[...] += product(a_ref, b_ref)

        @pl.when(second)
        def _():
            acc_ref[...] += product(x2_ref, b_ref) if a2 is not None else product(a_ref, x2_ref)

        @pl.when(k == nk - 1)
        def _():
            o_ref[...] = acc_ref[...].astype(o_ref.dtype)

    in_specs, args, kernel_body = [a_spec, b_spec], (a, b), body if nk > 1 else body_one_step
    split = a2 is not None or b2 is not None
    if split:
        assert a2 is None or b2 is None
        kernel_body = body_split
        if a2 is not None:
            in_specs, args = [a_spec, a2_spec, b_spec], (a, a2, b)
        else:
            in_specs, args = [a_spec, b_spec, b2_spec], (a, b, b2)
            kernel_body = lambda a_ref, b_ref, b2_ref, o_ref, acc_ref: body_split(a_ref, b2_ref, b_ref, o_ref, acc_ref)
    res = _call(kernel_body, name=name, grid=(M // tm, N // tn, nk), in_specs=in_specs, out_specs=[o_spec],
                out_shape=[o_shape], args=args, scratch=[pltpu.VMEM((tm, tn), F32)] if nk > 1 or split else [],
                sem=("parallel", "parallel", "arbitrary"), comm=comm, after=after)
    return res[0] if comm is None else res


def _pool_group_fwd(pooled, w_pg, scale):
    S = pooled.shape[0]
    G, C, C2 = w_pg.shape
    tm = _tile(S, 1024, 16)

    def body(a_ref, w_ref, s_ref, pg_ref, ps_ref):
        pg = jnp.dot(a_ref[...], w_ref[...], preferred_element_type=F32)
        pg_ref[...] = pg
        ps_ref[...] = (pg * s_ref[...]).astype(BF16)

    return pl.pallas_call(
        body, name="pool_group_fwd", grid=(G, S // tm),
        in_specs=[pl.BlockSpec((tm, C), lambda g, i: (i, g)), pl.BlockSpec((None, C, C2), lambda g, i: (g, 0, 0)),
                  pl.BlockSpec((1, C2), lambda g, i: (0, g))],
        out_specs=[pl.BlockSpec((tm, C2), lambda g, i: (i, g)), pl.BlockSpec((tm, C2), lambda g, i: (i, g))],
        out_shape=[jax.ShapeDtypeStruct((S, G * C2), F32), jax.ShapeDtypeStruct((S, G * C2), BF16)],
        compiler_params=_params("parallel", "parallel"),
    )(pooled, w_pg, scale)


def _pool_group_bwd_x(dpg, w_pg):
    S = dpg.shape[0]
    G, C, C2 = w_pg.shape
    tm = _tile(S, 1024, 16)

    def body(d_ref, w_ref, o_ref):
        o_ref[...] = lax.dot_general(d_ref[...], w_ref[...], (((1,), (1,)), ((), ())), preferred_element_type=F32)

    return pl.pallas_call(
        body, name="pool_group_bwd_x", grid=(G, S // tm),
        in_specs=[pl.BlockSpec((tm, C2), lambda g, i: (i, g)), pl.BlockSpec((None, C, C2), lambda g, i: (g, 0, 0))],
        out_specs=pl.BlockSpec((tm, C), lambda g, i: (i, g)), out_shape=jax.ShapeDtypeStruct((S, G * C), F32),
        compiler_params=_params("parallel", "parallel"),
    )(dpg, w_pg)


def _pool_group_bwd_w(pooled, dpg, G):
    S = pooled.shape[0]
    C, C2 = pooled.shape[1] // G, dpg.shape[1] // G
    tk = _tile(S, 1024, 16)

    def body(a_ref, d_ref, o_ref):
        @pl.when(pl.program_id(1) == 0)
        def _():
            o_ref[...] = jnp.zeros_like(o_ref)

        o_ref[...] += lax.dot_general(a_ref[...], d_ref[...], (((0,), (0,)), ((), ())), preferred_element_type=F32)

    return pl.pallas_call(
        body, name="pool_group_bwd_w", grid=(G, S // tk),
        in_specs=[pl.BlockSpec((tk, C), lambda g, k: (k, g)), pl.BlockSpec((tk, C2), lambda g, k: (k, g))],
        out_specs=pl.BlockSpec((None, C, C2), lambda g, k: (g, 0, 0)), out_shape=jax.ShapeDtypeStruct((G, C, C2), F32),
        compiler_params=_params("parallel", "arbitrary"),
    )(pooled, dpg)


def _rms(x, gain):
    r = lax.rsqrt(jnp.mean(x * x, axis=-1, keepdims=True) + EPS)
    return x * r * gain


def _rms_bwd(x, gain, dy):
    r = lax.rsqrt(jnp.mean(x * x, axis=-1, keepdims=True) + EPS)
    xh = x * r
    dgain = jnp.sum(dy * xh, axis=0, keepdims=True)
    dxh = dy * gain
    dx = r * (dxh - xh * jnp.mean(dxh * xh, axis=-1, keepdims=True))
    return dx, dgain


def _row_call(body, name, ins, outs, tr, *, comm=None):
    S = None
    in_specs, args = [], []
    for it in ins:
        arr, kind = it[0], it[1]
        if kind == "row":
            S = arr.shape[0]
            if len(it) == 4:
                width, cb = it[2], it[3]
                in_specs.append(pl.BlockSpec((tr, width), functools.partial(lambda i, cb: (i, cb), cb=cb)))
            else:
                in_specs.append(pl.BlockSpec((tr, arr.shape[1]), lambda i: (i, 0)))
        else:
            assert arr.ndim == 2
            in_specs.append(pl.BlockSpec(arr.shape, lambda i: (0, 0)))
        args.append(arr)
    out_specs, out_shape = [], []
    for sds, kind in outs:
        if kind == "row":
            out_specs.append(pl.BlockSpec((tr, sds.shape[1]), lambda i: (i, 0)))
        else:
            assert len(sds.shape) == 2
            out_specs.append(pl.BlockSpec(sds.shape, lambda i: (0, 0)))
        out_shape.append(sds)
    return _call(body, name=name, grid=(S // tr,), in_specs=in_specs, out_specs=out_specs, out_shape=out_shape, args=args,
                 sem=("arbitrary",), comm=comm)


def _first_step_zero(*refs):
    @pl.when(pl.program_id(0) == 0)
    def _():
        for r in refs:
            r[...] = jnp.zeros_like(r)


def _rms_fwd(x, gain, tr, comm):
    def body(x_ref, g_ref, o_ref):
        o_ref[...] = _rms(x_ref[...], g_ref[...]).astype(BF16)

    S, D = x.shape
    return _row_call(body, "rms_fwd", [(x, "row"), (gain, "full")], [(_sds((S, D), BF16), "row")], tr, comm=comm)


def _gate_mix(proj, ya, yp, gate_cb, tr):
    S, D = ya.shape

    def body(ga_ref, gp_ref, ya_ref, yp_ref, o_ref):
        o_ref[...] = (_sigmoid(ga_ref[...]) * ya_ref[...] + _sigmoid(gp_ref[...]) * yp_ref[...]).astype(BF16)

    return _row_call(body, "gate_mix", [(proj, "row", D, gate_cb), (proj, "row", D, gate_cb + 1), (ya, "row"), (yp, "row")],
                     [(_sds((S, D), BF16), "row")], tr)[0]


def _resid_rms2(x, mo, g2, g3, tr):
    S, D = x.shape

    def body(x_ref, mo_ref, g2_ref, g3_ref, x1_ref, h2_ref):
        x1 = x_ref[...] + _rms(mo_ref[...], g2_ref[...])
        x1_ref[...] = x1
        h2_ref[...] = _rms(x1, g3_ref[...]).astype(BF16)

    return _row_call(body, "resid_rms2", [(x, "row"), (mo, "row"), (g2, "full"), (g3, "full")],
                     [(_sds((S, D), F32), "row"), (_sds((S, D), BF16), "row")], tr)


def _resid_rms(x1, yf, g4, tr):
    S, D = x1.shape

    def body(x_ref, y_ref, g_ref, o_ref, ob_ref):
        x2 = x_ref[...] + _rms(y_ref[...], g_ref[...])
        o_ref[...] = x2
        ob_ref[...] = x2.astype(BF16)

    return _row_call(body, "resid_rms", [(x1, "row"), (yf, "row"), (g4, "full")],
                     [(_sds((S, D), F32), "row"), (_sds((S, D), BF16), "row")], tr)


def _ple_loss(gl, e, x2, tgt, g5, tr):
    S, D = x2.shape

    def body(gl_ref, e_ref, x2_ref, t_ref, g_ref, loss_ref, dx3_ref, de_ref, dgl_ref, dg_ref):
        _first_step_zero(loss_ref, dg_ref)
        s = _sigmoid(gl_ref[...])
        e_ = e_ref[...]
        t = s * e_
        gain = g_ref[...]
        err = x2_ref[...] + _rms(t, gain) - t_ref[...]
        row_loss = jnp.mean(err * err, axis=-1, keepdims=True)
        loss_ref[...] += 0.5 * jnp.sum(row_loss, axis=0, keepdims=True)
        dx3 = err * (1.0 / D)
        dx3_ref[...] = dx3
        dt, dgain = _rms_bwd(t, gain, dx3)
        dg_ref[...] += dgain
        de_ref[...] = (dt * s).astype(BF16)
        dgl_ref[...] = (dt * e_ * s * (1.0 - s)).astype(BF16)

    return _row_call(body, "ple_loss", [(gl, "row"), (e, "row"), (x2, "row"), (tgt, "row"), (g5, "full")],
                     [(_sds((1, 1), F32), "acc"), (_sds((S, D), F32), "row"), (_sds((S, D), BF16), "row"),
                      (_sds((S, D), BF16), "row"), (_sds((1, D), F32), "acc")], tr)


def _rms_bwd_a(dx3, dx2g, yf, g4, tr):
    S, D = yf.shape

    def body(a_ref, b_ref, y_ref, g_ref, dx_ref, dy_ref, dg_ref):
        _first_step_zero(dg_ref)
        dx2 = a_ref[...] + b_ref[...]
        dx_ref[...] = dx2
        dy, dgain = _rms_bwd(y_ref[...], g_ref[...], dx2)
        dy_ref[...] = dy.astype(BF16)
        dg_ref[...] += dgain

    return _row_call(body, "rms_bwd_a", [(dx3, "row"), (dx2g, "row"), (yf, "row"), (g4, "full")],
                     [(_sds((S, D), F32), "row"), (_sds((S, D), BF16), "row"), (_sds((1, D), F32), "acc")], tr)


def _rms_bwd_b(dx2, dh2, x1, g3, mo, g2, tr):
    S, D = x1.shape

    def body(dx2_ref, dh2_ref, x1_ref, g3_ref, mo_ref, g2_ref, dx1_ref, dmo_ref, dg3_ref, dg2_ref):
        _first_step_zero(dg3_ref, dg2_ref)
        d, dgain3 = _rms_bwd(x1_ref[...], g3_ref[...], dh2_ref[...])
        dx1 = dx2_ref[...] + d
        dx1_ref[...] = dx1
        dg3_ref[...] += dgain3
        dmo, dgain2 = _rms_bwd(mo_ref[...], g2_ref[...], dx1)
        dmo_ref[...] = dmo.astype(BF16)
        dg2_ref[...] += dgain2

    return _row_call(body, "rms_bwd_b", [(dx2, "row"), (dh2, "row"), (x1, "row"), (g3, "full"), (mo, "row"), (g2, "full")],
                     [(_sds((S, D), F32), "row"), (_sds((S, D), BF16), "row"), (_sds((1, D), F32), "acc"),
                      (_sds((1, D), F32), "acc")], tr)


def _rms_bwd_c(dx1, dh, x, g1, tr):
    S, D = x.shape

    def body(dx1_ref, dh_ref, x_ref, g_ref, o_ref, dg_ref):
        _first_step_zero(dg_ref)
        d, dgain = _rms_bwd(x_ref[...], g_ref[...], dh_ref[...])
        o_ref[...] = dx1_ref[...] + d
        dg_ref[...] += dgain

    return _row_call(body, "rms_bwd_c", [(dx1, "row"), (dh, "row"), (x, "row"), (g1, "full")],
                     [(_sds((S, D), F32), "row"), (_sds((1, D), F32), "acc")], tr)


def _gate_bwd(dmixed, proj, ya, yp, gate_cb, tr):
    S, D = ya.shape

    def body(dm_ref, ga_ref, gp_ref, ya_ref, yp_ref, dya_ref, dyp_ref, dga_ref, dgp_ref):
        dm = dm_ref[...]
        sa = _sigmoid(ga_ref[...])
        sp = _sigmoid(gp_ref[...])
        dya_ref[...] = (dm * sa).astype(BF16)
        dyp_ref[...] = (dm * sp).astype(BF16)
        dga_ref[...] = (dm * ya_ref[...] * sa * (1.0 - sa)).astype(BF16)
        dgp_ref[...] = (dm * yp_ref[...] * sp * (1.0 - sp)).astype(BF16)

    return _row_call(body, "gate_bwd",
                     [(dmixed, "row"), (proj, "row", D, gate_cb), (proj, "row", D, gate_cb + 1), (ya, "row"), (yp, "row")],
                     [(_sds((S, D), BF16), "row")] * 4, tr)


def _scale_bwd(dps, pg, scale, tr):
    S, W = dps.shape

    def body(d_ref, pg_ref, s_ref, o_ref, ds_ref):
        _first_step_zero(ds_ref)
        d = d_ref[...]
        o_ref[...] = (d * s_ref[...]).astype(BF16)
        ds_ref[...] += jnp.sum(d * pg_ref[...], axis=0, keepdims=True)

    return _row_call(body, "scale_bwd", [(dps, "row"), (pg, "row"), (scale, "full")],
                     [(_sds((S, W), BF16), "row"), (_sds((1, W), F32), "acc")], tr)


CUMSUM_TERMS = 2
GRAD_TERMS = 1


def _tri(blk, cmp, terms=CUMSUM_TERMS):
    j = lax.broadcasted_iota(jnp.int32, (blk, blk), 0)
    s = lax.broadcasted_iota(jnp.int32, (blk, blk), 1)
    one = jnp.concatenate([cmp(j, s).astype(BF16), jnp.ones((blk, 128), BF16)], axis=1)
    return jnp.concatenate([one] * terms, axis=0)


def _split(x, terms=CUMSUM_TERMS):
    parts = []
    for _ in range(terms):
        t = x.astype(BF16)
        parts.append(t)
        x = x - t.astype(F32)
    return parts[0] if terms == 1 else jnp.concatenate(parts, axis=1)


def _causal(blk):
    return lax.broadcasted_iota(jnp.int32, (blk, blk), 1) < lax.broadcasted_iota(jnp.int32, (blk, blk), 0)


def _scores(q, kj, scale, causal):
    z = lax.dot_general(q, kj, (((1,), (1,)), ((), ())), preferred_element_type=F32) * scale
    l1p = jnp.log(1.0 + jnp.exp(-jnp.abs(z)))
    lb = -(jnp.maximum(z, 0.0) + l1p)
    if causal is not None:
        lb = jnp.where(causal, lb, 0.0)
    return z, lb, jnp.minimum(z, 0.0) - l1p


def _attn_fwd(proj, n_heads, blk, comm=None):
    S = proj.shape[0]
    nq = S // blk
    scale = HEAD_DIM ** -0.5
    lanes = blk // 128
    hp = 2 if n_heads % 2 == 0 else 1
    cols = [slice(h * HEAD_DIM, (h + 1) * HEAD_DIM) for h in range(hp)]
    u_incl = _tri(blk, lambda j, s: j >= s)

    def body(q_ref, k_ref, v_ref, u_ref, o_ref, a_ref, b_ref, z_buf, hl_buf):
        i = pl.program_id(1)
        qs = [q_ref[:, c].astype(BF16) for c in cols]
        u = u_ref[...]

        def scores(j, h, causal):
            kj = k_ref[pl.ds(pl.multiple_of(j * blk, blk), blk), cols[h]].astype(BF16)
            z, lb, log_beta = _scores(qs[h], kj, scale, causal)
            b_ref[h, j] = jnp.exp(log_beta).astype(BF16)
            return z, _split(lb)

        def weigh(j, h, z, hl, acc, run, causal):
            vj = v_ref[pl.ds(pl.multiple_of(j * blk, blk), blk), cols[h]].astype(BF16)
            ct = jnp.dot(hl, u, preferred_element_type=F32)
            a = jnp.exp(z + ct[:, :blk] + jnp.tile(run, (1, lanes)))
            if causal is not None:
                a = jnp.where(causal, a, 0.0)
            a = a.astype(BF16)
            a_ref[h, j] = a
            return acc + jnp.dot(a, vj, preferred_element_type=F32), run + ct[:, blk:]

        def stage_scores(j):
            for h in range(hp):
                z_buf[h], hl_buf[h] = scores(j, h, None)

        zero = jnp.zeros((blk, HEAD_DIM), F32)
        causal = _causal(blk)
        carry = tuple(weigh(i, h, *scores(i, h, causal), zero, zero, causal) for h in range(hp))
        stage_scores(jnp.maximum(i - 1, 0))

        def step(t, carry):
            j = i - 1 - t
            out = tuple(weigh(j, h, z_buf[h], hl_buf[h], *carry[h], None) for h in range(hp))
            stage_scores(j - 1)
            return out

        carry = lax.fori_loop(0, jnp.maximum(i - 1, 0), step, carry)
        carry = lax.fori_loop(0, jnp.minimum(i, 1),
                              lambda t, c: tuple(weigh(t, h, z_buf[h], hl_buf[h], *c[h], None) for h in range(hp)), carry)
        for h in range(hp):
            o_ref[:, cols[h]] = carry[h][0].astype(BF16)

    G = n_heads // hp
    W = hp * HEAD_DIM
    saved = _sds((n_heads, nq, nq, blk, blk), BF16)
    saved_spec = pl.BlockSpec((hp, None, nq, blk, blk), lambda h, i: (h, i, 0, 0, 0))
    return _call(
        body, name="attn_fwd", grid=(G, nq),
        in_specs=[pl.BlockSpec((blk, W), lambda h, i: (i, h)),
                  pl.BlockSpec((S, W), lambda h, i: (0, G + h)),
                  pl.BlockSpec((S, W), lambda h, i: (0, 2 * G + h)),
                  pl.BlockSpec(u_incl.shape, lambda h, i: (0, 0))],
        out_specs=[pl.BlockSpec((blk, W), lambda h, i: (i, h)), saved_spec, saved_spec],
        out_shape=[_sds((S, n_heads * HEAD_DIM), BF16), saved, saved],
        args=(proj, proj, proj, u_incl), scratch=[pltpu.VMEM((hp, blk, blk), F32), pltpu.VMEM((hp, blk, CUMSUM_TERMS * blk), BF16)],
        sem=("parallel", "arbitrary"), comm=comm)


def _attn_bwd(proj, a_saved, b_saved, do, n_heads, blk, comm=None):
    S = proj.shape[0]
    nq = S // blk
    scale = HEAD_DIM ** -0.5
    lanes = blk // 128
    hp = 2 if n_heads % 2 == 0 else 1
    cols = [slice(h * HEAD_DIM, (h + 1) * HEAD_DIM) for h in range(hp)]
    l_incl = _tri(blk, lambda j, s: j <= s, GRAD_TERMS)

    def body(q_ref, k_ref, v_ref, a_ref, b_ref, do_ref, li_ref, dq_ref, dk_ref, dv_ref, dk_acc, dv_acc, g_buf, gl_buf):
        i = pl.program_id(1)

        @pl.when(i == 0)
        def _():
            dk_acc[...] = jnp.zeros_like(dk_acc)
            dv_acc[...] = jnp.zeros_like(dv_acc)

        qs = [q_ref[:, c].astype(BF16) for c in cols]
        dobs = [do_ref[:, c].astype(BF16) for c in cols]
        li = li_ref[...]

        def stage_products(j):
            ks = pl.multiple_of(j * blk, blk)
            for h in range(hp):
                vj = v_ref[pl.ds(ks, blk), cols[h]].astype(BF16)
                da = lax.dot_general(dobs[h], vj, (((1,), (1,)), ((), ())), preferred_element_type=F32)
                g = a_ref[h, j].astype(F32) * da
                g_buf[h] = g
                gl_buf[h] = _split(g, GRAD_TERMS)

        def grads(j, h, dq, run_g, causal):
            ks = pl.multiple_of(j * blk, blk)
            kj = k_ref[pl.ds(ks, blk), cols[h]].astype(BF16)
            gt = jnp.dot(gl_buf[h], li, preferred_element_type=F32)
            dz = g_buf[h] - b_ref[h, j].astype(F32) * (gt[:, :blk] + jnp.tile(run_g, (1, lanes)))
            if causal is not None:
                dz = jnp.where(causal, dz, 0.0)
            dzs = (dz * scale).astype(BF16)
            dk_acc[pl.ds(ks, blk), cols[h]] += lax.dot_general(dzs, qs[h], (((0,), (0,)), ((), ())),
                                                               preferred_element_type=F32)
            dv_acc[pl.ds(ks, blk), cols[h]] += lax.dot_general(a_ref[h, j], dobs[h], (((0,), (0,)), ((), ())),
                                                               preferred_element_type=F32)
            return dq + jnp.dot(dzs, kj, preferred_element_type=F32), run_g + gt[:, blk:]

        zero = jnp.zeros((blk, HEAD_DIM), F32)
        stage_products(0)

        def step(j, carry):
            out = tuple(grads(j, h, *carry[h], None) for h in range(hp))
            stage_products(j + 1)
            return out

        carry = lax.fori_loop(0, i, step, ((zero, zero),) * hp)
        causal = _causal(blk)
        carry = tuple(grads(i, h, *carry[h], causal) for h in range(hp))
        for h in range(hp):
            dq_ref[:, cols[h]] = carry[h][0].astype(BF16)

        @pl.when(i == nq - 1)
        def _():
            dk_ref[...] = dk_acc[...].astype(BF16)
            dv_ref[...] = dv_acc[...].astype(BF16)

    G = n_heads // hp
    W = hp * HEAD_DIM
    AW = n_heads * HEAD_DIM
    saved_spec = pl.BlockSpec((hp, None, nq, blk, blk), lambda h, i: (h, i, 0, 0, 0))
    return _call(
        body, name="attn_bwd", grid=(G, nq),
        in_specs=[pl.BlockSpec((blk, W), lambda h, i: (i, h)),
                  pl.BlockSpec((S, W), lambda h, i: (0, G + h)),
                  pl.BlockSpec((S, W), lambda h, i: (0, 2 * G + h)),
                  saved_spec, saved_spec,
                  pl.BlockSpec((blk, W), lambda h, i: (i, h)),
                  pl.BlockSpec(l_incl.shape, lambda h, i: (0, 0))],
        out_specs=[pl.BlockSpec((blk, W), lambda h, i: (i, h)),
                   pl.BlockSpec((S, W), lambda h, i: (0, h)),
                   pl.BlockSpec((S, W), lambda h, i: (0, h))],
        out_shape=[_sds((S, AW), BF16)] * 3, args=(proj, proj, proj, a_saved, b_saved, do, l_incl),
        scratch=[pltpu.VMEM((S, W), F32)] * 2 + [pltpu.VMEM((hp, blk, blk), F32),
                                                 pltpu.VMEM((hp, blk, GRAD_TERMS * blk), BF16)],
        sem=("parallel", "arbitrary"), comm=comm)


def _pool_count(r0, rows, w):
    t = r0 + lax.broadcasted_iota(jnp.int32, (rows, 1), 0)
    return jnp.minimum(t + 1, w).astype(F32)


def _pool_fwd(proj, col_blk, n_groups, width, chunk):
    S = proj.shape[0]
    H = POOL_HALO

    def body(u_ref, o_ref, pad_ref):
        g = pl.program_id(0)
        pad_ref[0:H, :] = jnp.zeros((H, width), F32)
        pad_ref[H:, :] = u_ref[...]
        for gi, w in enumerate(POOL_WINDOWS[:n_groups]):
            @pl.when(g == gi)
            def _(w=w):
                def one(c, _):
                    r0 = pl.multiple_of(c * chunk, chunk)
                    ext = pad_ref[pl.ds(r0, chunk + H), :]
                    s = ext
                    k = 1
                    while k < w:
                        s = s + pltpu.roll(s, k, 0)
                        k *= 2
                    o_ref[pl.ds(r0, chunk), :] = (s[H:] / _pool_count(r0, chunk, w) - ext[H:]).astype(BF16)
                    return 0

                lax.fori_loop(0, S // chunk, one, 0)

    return pl.pallas_call(
        body, name="pool_fwd", grid=(n_groups,),
        in_specs=[pl.BlockSpec((S, width), lambda g: (0, col_blk + g))],
        out_specs=pl.BlockSpec((S, width), lambda g: (0, g)), out_shape=_sds((S, n_groups * width), BF16),
        scratch_shapes=[pltpu.VMEM((S + H, width), F32)], compiler_params=_params("parallel"),
    )(proj)


def _pool_bwd(dpooled, n_groups, chunk):
    S = dpooled.shape[0]
    width = dpooled.shape[1] // n_groups
    H = POOL_HALO

    def body(d_ref, o_ref, pad_ref):
        g = pl.program_id(0)
        pad_ref[S:, :] = jnp.zeros((H, width), F32)
        for gi, w in enumerate(POOL_WINDOWS[:n_groups]):
            @pl.when(g == gi)
            def _(w=w):
                def fill(c, _):
                    r0 = pl.multiple_of(c * chunk, chunk)
                    pad_ref[pl.ds(r0, chunk), :] = d_ref[pl.ds(r0, chunk), :] / _pool_count(r0, chunk, w)
                    return 0

                lax.fori_loop(0, S // chunk, fill, 0)

                def one(c, _):
                    r0 = pl.multiple_of(c * chunk, chunk)
                    s = pad_ref[pl.ds(r0, chunk + H), :]
                    k = 1
                    while k < w:
                        s = s + pltpu.roll(s, chunk + H - k, 0)
                        k *= 2
                    o_ref[pl.ds(r0, chunk), :] = (s[:chunk] - d_ref[pl.ds(r0, chunk), :]).astype(BF16)
                    return 0

                lax.fori_loop(0, S // chunk, one, 0)

    return pl.pallas_call(
        body, name="pool_bwd", grid=(n_groups,),
        in_specs=[pl.BlockSpec((S, width), lambda g: (0, g))],
        out_specs=pl.BlockSpec((S, width), lambda g: (0, g)), out_shape=_sds((S, n_groups * width), BF16),
        scratch_shapes=[pltpu.VMEM((S + H, width), F32)], compiler_params=_params("parallel"),
    )(dpooled)


def _conv3(x_ext, w, b, shifted=None):
    x2, x1 = shifted if shifted is not None else (pltpu.roll(x_ext, 2, 0), pltpu.roll(x_ext, 1, 0))
    return b + x2 * w[0:1, :] + x1 * w[1:2, :] + x_ext * w[2:3, :]


def _gelu_parts(x):
    th = jnp.tanh(GELU_C0 * (x + GELU_C1 * (x * x * x)))
    return th, 0.5 * (1.0 + th)


def _conv_specs(S, F, cb):
    nb = F // cb
    return [pl.BlockSpec((S, cb), lambda j: (0, j)), pl.BlockSpec((S, cb), lambda j: (0, nb + j)),
            pl.BlockSpec((3, cb), lambda j: (0, j)), pl.BlockSpec((3, cb), lambda j: (0, nb + j)),
            pl.BlockSpec((1, cb), lambda j: (0, j)), pl.BlockSpec((1, cb), lambda j: (0, nb + j))]


def _conv_fwd(upre, cw, cb_, chunk):
    S, F2 = upre.shape
    F = F2 // 2
    cb = 128
    H = CONV_HALO

    def body(g_ref, v_ref, wg_ref, wv_ref, bg_ref, bv_ref, o_ref, pg_ref, pv_ref):
        pg_ref[0:H, :] = jnp.zeros((H, cb), F32)
        pv_ref[0:H, :] = jnp.zeros((H, cb), F32)
        pg_ref[H:, :] = g_ref[...]
        pv_ref[H:, :] = v_ref[...]
        wg, wv, bg, bv = wg_ref[...], wv_ref[...], bg_ref[...], bv_ref[...]

        def one(c, _):
            r0 = pl.multiple_of(c * chunk, chunk)
            up_g = _conv3(pg_ref[pl.ds(r0, chunk + H), :], wg, bg)[H:]
            up_v = _conv3(pv_ref[pl.ds(r0, chunk + H), :], wv, bv)[H:]
            _, cdf = _gelu_parts(up_g)
            o_ref[pl.ds(r0, chunk), :] = (up_g * cdf * up_v).astype(BF16)
            return 0

        lax.fori_loop(0, S // chunk, one, 0)

    return pl.pallas_call(
        body, name="conv_fwd", grid=(F // cb,), in_specs=_conv_specs(S, F, cb),
        out_specs=pl.BlockSpec((S, cb), lambda j: (0, j)), out_shape=_sds((S, F), BF16),
        scratch_shapes=[pltpu.VMEM((S + H, cb), F32)] * 2, compiler_params=_params("parallel"),
    )(upre, upre, cw, cw, cb_, cb_)


def _conv_bwd(upre, dact, cw, cb_, chunk, comm=None):
    S, F2 = upre.shape
    F = F2 // 2
    cb = 128
    H = CONV_HALO
    E = chunk + 2 * H

    def body(g_ref, v_ref, wg_ref, wv_ref, bg_ref, bv_ref, d_ref, dg_ref, dv_ref, dwg_ref, dwv_ref, dbg_ref, dbv_ref,
             pg_ref, pv_ref, pd_ref):
        for p, src in ((pg_ref, g_ref), (pv_ref, v_ref), (pd_ref, d_ref)):
            p[0:H, :] = jnp.zeros((H, cb), F32)
            p[H:S + H, :] = src[...]
            p[S + H:, :] = jnp.zeros((H, cb), F32)
        wg, wv, bg, bv = wg_ref[...], wv_ref[...], bg_ref[...], bv_ref[...]

        def taps_bwd(d, w):
            return d * w[2:3, :] + pltpu.roll(d, E - 1, 0) * w[1:2, :] + pltpu.roll(d, E - 2, 0) * w[0:1, :]

        def wsum(d, x, x2, x1):
            dc = d[H:H + chunk]
            return [jnp.sum(dc * x2[H:H + chunk], axis=0, keepdims=True),
                    jnp.sum(dc * x1[H:H + chunk], axis=0, keepdims=True),
                    jnp.sum(dc * x[H:H + chunk], axis=0, keepdims=True),
                    jnp.sum(dc, axis=0, keepdims=True)]

        def one(c, acc):
            r0 = pl.multiple_of(c * chunk, chunk)
            xg = pg_ref[pl.ds(r0, E), :]
            xv = pv_ref[pl.ds(r0, E), :]
            d = pd_ref[pl.ds(r0, E), :]
            sg = (pltpu.roll(xg, 2, 0), pltpu.roll(xg, 1, 0))
            sv = (pltpu.roll(xv, 2, 0), pltpu.roll(xv, 1, 0))
            up_g = _conv3(xg, wg, bg, sg)
            up_v = _conv3(xv, wv, bv, sv)
            th, cdf = _gelu_parts(up_g)
            dgelu = cdf + 0.5 * up_g * (1.0 - th * th) * (GELU_C0 * (1.0 + 3.0 * GELU_C1 * (up_g * up_g)))
            dgate = d * up_v * dgelu
            dval = d * (up_g * cdf)
            dg_ref[pl.ds(r0, chunk), :] = taps_bwd(dgate, wg)[H:H + chunk].astype(BF16)
            dv_ref[pl.ds(r0, chunk), :] = taps_bwd(dval, wv)[H:H + chunk].astype(BF16)
            return tuple(a + b for a, b in zip(acc, wsum(dgate, xg, *sg) + wsum(dval, xv, *sv)))

        zero = jnp.zeros((1, cb), F32)
        acc = lax.fori_loop(0, S // chunk, one, (zero,) * 8)
        dwg_ref[...] = jnp.concatenate(acc[0:3], axis=0)
        dbg_ref[...] = acc[3]
        dwv_ref[...] = jnp.concatenate(acc[4:7], axis=0)
        dbv_ref[...] = acc[7]

    col = lambda rows: pl.BlockSpec((rows, cb), lambda j: (0, j))
    return _call(
        body, name="conv_bwd", grid=(F // cb,), in_specs=_conv_specs(S, F, cb) + [col(S)],
        out_specs=[col(S), col(S), col(3), col(3), col(1), col(1)],
        out_shape=[_sds((S, F), BF16), _sds((S, F), BF16), _sds((3, F), F32), _sds((3, F), F32), _sds((1, F), F32),
                   _sds((1, F), F32)],
        args=(upre, upre, cw, cw, cb_, cb_, dact), scratch=[pltpu.VMEM((S + 2 * H, cb), F32)] * 3, sem=("parallel",),
        comm=comm)


def _position():
    x, y, c = lax.axis_index("x"), lax.axis_index("y"), lax.axis_index("c")
    return x, y, c, 4 * x + 2 * y + c


def _peer(x, y, c, d):
    px = 1 - x if d & 4 else x
    py = 1 - y if d & 2 else y
    pc = 1 - c if d & 1 else c
    return (px, py, pc), 4 * px + 2 * py + pc


def _all_gather_comm(tensors):
    nt = len(tensors)
    outs = [_sds((N_DEV,) + t.shape, t.dtype) for t in tensors]
    sems = [pltpu.SemaphoreType.DMA((7 * nt,)), pltpu.SemaphoreType.DMA((7 * nt,)), pltpu.SemaphoreType.DMA((nt,))]

    def parts(ins, outs_, sem_refs):
        send, recv, loc = sem_refs
        x, y, c, me = _position()
        chips = [(1 - x, y), (x, 1 - y), (1 - x, 1 - y)]

        def copy(t, k, block, to, src=None):
            slot = outs_[t].at[4 * block[0] + 2 * block[1] + block[2]]
            return pltpu.make_async_remote_copy(src_ref=slot if src is None else src, dst_ref=slot,
                                                send_sem=send.at[7 * t + k], recv_sem=recv.at[7 * t + k], device_id=to,
                                                device_id_type=MESH)

        def mine(t):
            return pltpu.make_async_copy(ins[t], outs_[t].at[me], loc.at[t])

        return (x, y, c), (x, y, 1 - c), chips, copy, mine

    def start(ins, outs_, sem_refs):
        me, sibling, chips, copy, mine = parts(ins, outs_, sem_refs)
        for t in range(nt):
            mine(t).start()
            copy(t, 0, me, sibling, src=ins[t]).start()
            for j, chip in enumerate(chips):
                copy(t, 1 + j, me, (*chip, me[2]), src=ins[t]).start()

    def finish(ins, outs_, sem_refs):
        me, sibling, chips, copy, mine = parts(ins, outs_, sem_refs)
        c = me[2]
        for t in range(nt):
            for j, chip in enumerate(chips):
                copy(t, 1 + j, (*chip, c), me).wait_recv()
                copy(t, 4 + j, (*chip, c), sibling).start()
        for t in range(nt):
            copy(t, 0, sibling, me).wait_recv()
            for j, chip in enumerate(chips):
                copy(t, 4 + j, (*chip, 1 - c), me).wait_recv()
        for t in range(nt):
            copy(t, 0, me, sibling, src=ins[t]).wait_send()
            for j, chip in enumerate(chips):
                copy(t, 1 + j, me, (*chip, c), src=ins[t]).wait_send()
                copy(t, 4 + j, (*chip, c), sibling).wait_send()
            mine(t).wait()

    return _Comm(tensors, outs, sems, start, finish)


def _reduce_scatter_comm(tensors):
    nt = len(tensors)
    outs = [_sds(t.shape, t.dtype) for t in tensors]
    sems = [pltpu.SemaphoreType.DMA((7 * nt,)), pltpu.SemaphoreType.DMA((7 * nt,)), pltpu.SemaphoreType.DMA((nt,))]

    def local(ins, outs_, sem_refs, t, me):
        return pltpu.make_async_copy(ins[t].at[me], outs_[t].at[me], sem_refs[2].at[t])

    def remote(ins, outs_, sem_refs, t, d, inbound):
        x, y, c, me = _position()
        peer, peer_idx = _peer(x, y, c, d)
        k = 7 * t + d - 1
        src, dst, to = (ins[t].at[me], outs_[t].at[peer_idx], (x, y, c)) if inbound else (ins[t].at[peer_idx], outs_[t].at[me], peer)
        return pltpu.make_async_remote_copy(src_ref=src, dst_ref=dst, send_sem=sem_refs[0].at[k], recv_sem=sem_refs[1].at[k],
                                            device_id=to, device_id_type=MESH)

    def start(ins, outs_, sem_refs):
        me = _position()[3]
        for t in range(nt):
            local(ins, outs_, sem_refs, t, me).start()
            for d in range(1, N_DEV):
                remote(ins, outs_, sem_refs, t, d, False).start()

    def finish(ins, outs_, sem_refs):
        me = _position()[3]
        for t in range(nt):
            for d in range(1, N_DEV):
                remote(ins, outs_, sem_refs, t, d, True).wait_recv()
        for t in range(nt):
            for d in range(1, N_DEV):
                remote(ins, outs_, sem_refs, t, d, False).wait_send()
            local(ins, outs_, sem_refs, t, me).wait()

    return _Comm(tensors, outs, sems, start, finish)


HBM_SPEC = pl.BlockSpec(memory_space=pltpu.HBM)
SEM_SPEC = pl.BlockSpec(memory_space=pltpu.SEMAPHORE)
DATAFLOW = pltpu.SideEffectType.DATAFLOW_SIDE_EFFECTING


def _scatter_copy(g_ref, land_ref, send_sems, recv_sems, d):
    x, y, c, me = _position()
    peer, peer_idx = _peer(x, y, c, d)
    return pltpu.make_async_remote_copy(src_ref=g_ref.at[peer_idx], dst_ref=land_ref.at[me], send_sem=send_sems.at[d - 1],
                                        recv_sem=recv_sems.at[d - 1], device_id=peer, device_id_type=MESH)


def _reduce_scatter_start(g, name):
    def body(g_ref, land_ref, send_sems, recv_sems, g_thru, land_thru, token):
        for d in range(1, N_DEV):
            _scatter_copy(g_ref, land_ref, send_sems, recv_sems, d).start()
        token[...] = jnp.zeros_like(token)

    return pl.pallas_call(
        body, name=name,
        out_shape=(pltpu.SemaphoreType.DMA((N_DEV - 1,)), pltpu.SemaphoreType.DMA((N_DEV - 1,)), pltpu.HBM(g.shape, g.dtype),
                   pltpu.HBM(g.shape, g.dtype), _sds((8, 128), F32)),
        in_specs=(HBM_SPEC, HBM_SPEC), out_specs=(SEM_SPEC, SEM_SPEC, HBM_SPEC, HBM_SPEC, pl.BlockSpec(memory_space=pltpu.VMEM)),
        input_output_aliases={0: 2, 1: 3}, compiler_params=pltpu.CompilerParams(has_side_effects=DATAFLOW),
    )(pltpu.with_memory_space_constraint(g, pltpu.HBM), pltpu.with_memory_space_constraint(lax.empty(g.shape, g.dtype), pltpu.HBM))


def _reduce_scatter_wait(send_sems, recv_sems, g_thru, land_thru, after, name):
    def body(g_ref, land_ref, send_sems, recv_sems, *rest):
        for d in range(1, N_DEV):
            copy = _scatter_copy(g_ref, land_ref, send_sems, recv_sems, d)
            copy.wait_send()
            copy.wait_recv()

    return pl.pallas_call(
        body, name=name, out_shape=(pltpu.HBM(g_thru.shape, g_thru.dtype), pltpu.HBM(g_thru.shape, g_thru.dtype)),
        in_specs=(HBM_SPEC, HBM_SPEC, SEM_SPEC, SEM_SPEC) + (ANY,) * len(after), out_specs=(HBM_SPEC, HBM_SPEC),
        input_output_aliases={0: 0, 1: 1}, compiler_params=pltpu.CompilerParams(has_side_effects=DATAFLOW),
    )(g_thru, land_thru, send_sems, recv_sems, *after)


def _all_reduce_small(part, after):
    r, W = part.shape

    def body(p_ref, after_ref, o_ref, g_ref, send_sems, recv_sems):
        x, y, c, me = _position()
        sends = []
        for d in range(1, N_DEV):
            peer, _ = _peer(x, y, c, d)
            cp = pltpu.make_async_remote_copy(src_ref=p_ref, dst_ref=g_ref.at[me], send_sem=send_sems.at[d - 1],
                                              recv_sem=recv_sems.at[d - 1], device_id=peer, device_id_type=MESH)
            cp.start()
            sends.append(cp)
        g_ref[me] = p_ref[...]
        for d in range(1, N_DEV):
            _, peer_idx = _peer(x, y, c, d)
            pltpu.make_async_remote_copy(src_ref=p_ref, dst_ref=g_ref.at[peer_idx], send_sem=send_sems.at[d - 1],
                                         recv_sem=recv_sems.at[d - 1], device_id=(x, y, c), device_id_type=MESH).wait_recv()
        for cp in sends:
            cp.wait_send()
        acc = g_ref[0]
        for i in range(1, N_DEV):
            acc = acc + g_ref[i]
        o_ref[...] = acc

    vmem = pl.BlockSpec(memory_space=pltpu.VMEM)
    return pl.pallas_call(
        body, name="all_reduce_small", in_specs=[vmem, ANY], out_specs=[vmem, vmem],
        out_shape=[_sds((r, W), F32), _sds((N_DEV, r, W), F32)],
        scratch_shapes=[pltpu.SemaphoreType.DMA((7,)), pltpu.SemaphoreType.DMA((7,))],
        compiler_params=pltpu.CompilerParams(has_side_effects=True, vmem_limit_bytes=VMEM_LIMIT),
    )(part, after)[0]


def _adamw_math(w, g, m, v):
    m = ADAM_B1 * m + (1.0 - ADAM_B1) * g
    v = ADAM_B2 * v + (1.0 - ADAM_B2) * (g * g)
    m_hat = m / (1.0 - ADAM_B1 ** ADAM_STEP)
    v_hat = v / (1.0 - ADAM_B2 ** ADAM_STEP)
    return -ADAM_LR * (m_hat / (jnp.sqrt(v_hat) + ADAM_EPS) + ADAM_WD * w), m, v


def _adamw(w, g, m, v, name):
    rows, cols = w.shape
    tr = _tile(rows, max(8, (2**18 // cols) // 8 * 8), 8)

    def body(w_ref, g_ref, m_ref, v_ref, d_ref, nm_ref, nv_ref):
        d_ref[...], nm_ref[...], nv_ref[...] = _adamw_math(w_ref[...], g_ref[...], m_ref[...], v_ref[...])

    spec = pl.BlockSpec((tr, cols), lambda i: (i, 0))
    return pl.pallas_call(
        body, name=name, grid=(rows // tr,), in_specs=[spec] * 4, out_specs=[spec] * 3,
        out_shape=[_sds((rows, cols), F32)] * 3, compiler_params=_params("parallel"),
    )(w, g, m, v)


def _adamw_sum(recv, w, m, v, name, after=()):
    n, rows, cols = recv.shape
    tr = _tile(rows, max(16, (2**17 // cols) // 16 * 16), 16)

    def body(r_ref, w_ref, m_ref, v_ref, g_ref, d_ref, nm_ref, nv_ref):
        g = r_ref[0].astype(F32)
        for i in range(1, n):
            g = g + r_ref[i].astype(F32)
        g_ref[...] = g
        d_ref[...], nm_ref[...], nv_ref[...] = _adamw_math(w_ref[...], g, m_ref[...], v_ref[...])

    spec = pl.BlockSpec((tr, cols), lambda i: (i, 0))
    return _call(body, name=name, grid=(rows // tr,), in_specs=[pl.BlockSpec((n, tr, cols), lambda i: (0, i, 0))] + [spec] * 3,
                 out_specs=[spec] * 4, out_shape=[_sds((rows, cols), F32)] * 4, args=(recv, w, m, v), sem=("parallel",),
                 after=after)


COLUMN_CUT = ("w_in", "w_attn_branch", "w_pool_branch", "w_up", "w_ple")
ROW_CUT = ("w_out", "w_down", "w_ple_gate")
REPLICATED = ("norm_mix_pre", "pool_scale", "norm_mix_post", "norm_ffn_pre", "conv_b", "norm_ffn_post", "norm_ple_post")
WEIGHTS = ("norm_mix_pre", "w_in", "w_attn_branch", "w_pool_group", "pool_scale", "w_pool_branch", "w_out", "norm_mix_post",
           "norm_ffn_pre", "w_up", "conv_w", "conv_b", "w_down", "norm_ffn_post", "w_ple", "w_ple_gate", "norm_ple_post")


def _size(shape):
    n = 1
    for s in shape:
        n *= s
    return n


def _pad_rows(flat, row_align):
    n = flat.shape[-1]
    per = PACK_W * row_align
    total = -(-n // per) * per
    return jnp.pad(flat, [(0, total - n)]).reshape(total // PACK_W, PACK_W)


def _natural(shard_major):
    n, r, c = shard_major.shape
    return shard_major.reshape(n * r, c)


def kernel(x, p, norm_mix_pre, w_in, w_attn_branch, w_pool_group, pool_scale, w_pool_branch, w_out, norm_mix_post, norm_ffn_pre, w_up, conv_w, conv_b, w_down, norm_ffn_post, w_ple, w_ple_gate, norm_ple_post, loss_target, m_norm_mix_pre, m_w_in, m_w_attn_branch, m_w_pool_group, m_pool_scale, m_w_pool_branch, m_w_out, m_norm_mix_post, m_norm_ffn_pre, m_w_up, m_conv_w, m_conv_b, m_w_down, m_norm_ffn_post, m_w_ple, m_w_ple_gate, m_norm_ple_post, v_norm_mix_pre, v_w_in, v_w_attn_branch, v_w_pool_group, v_pool_scale, v_w_pool_branch, v_w_out, v_norm_mix_post, v_norm_ffn_pre, v_w_up, v_conv_w, v_conv_b, v_w_down, v_norm_ffn_post, v_w_ple, v_w_ple_gate, v_norm_ple_post):
    given = dict(locals())
    wts = {n: given[n][0] for n in WEIGHTS}
    mom = {n: given["m_" + n][0] for n in WEIGHTS}
    var = {n: given["v_" + n][0] for n in WEIGHTS}
    xs = x[0]
    ps_in = p[0, 0]
    tgt = loss_target[0]
    S, D = xs.shape
    AW = wts["w_attn_branch"].shape[0]
    PW = wts["w_pool_branch"].shape[0]
    G = wts["w_pool_group"].shape[0]
    PGW = PW // G
    H = AW // HEAD_DIM
    F = wts["w_down"].shape[0] * N_DEV
    assert (3 * AW) % PGW == 0 and (3 * AW + PW) % D == 0 and PGW % 128 == 0 and F % 128 == 0
    tr = _tile(S, 256, 16)
    blk = _tile(S, 256, 128)
    chunk = _tile(S, 256, 8)
    me = 4 * lax.axis_index("x") + 2 * lax.axis_index("y") + lax.axis_index("c")

    cw_shape = wts["conv_w"].shape
    conv_b_row = wts["conv_b"].reshape(1, -1)
    g1, g2, g3, g4, g5 = (wts[n].reshape(1, D) for n in
                          ("norm_mix_pre", "norm_mix_post", "norm_ffn_pre", "norm_ffn_post", "norm_ple_post"))
    pscale = wts["pool_scale"].reshape(1, PW)
    big = 4096
    wb = {n: wts[n].astype(BF16) for n in COLUMN_CUT + ROW_CUT}
    wb["w_pool_group"] = wts["w_pool_group"].astype(BF16).reshape(G * PGW // N_DEV, PGW)

    h, w_in = _rms_fwd(xs, g1, tr, _all_gather_comm([wb["w_in"]]))
    proj, w_ab, w_pg, w_pb, w_out = _mm(
        h, w_in, b_sm=True, name="mm_in", tk=2048,
        comm=_all_gather_comm([wb["w_attn_branch"], wb["w_pool_group"], wb["w_pool_branch"], wb["w_out"]]))
    w_pg = jnp.moveaxis(w_pg.reshape(N_DEV, G, PGW // N_DEV, PGW), 0, 1).reshape(G, PGW, PGW)
    w_out = _natural(w_out)
    attn, a_saved, b_saved, w_up, conv_w_all, w_down, w_ple, w_pleg = _attn_fwd(
        proj, H, blk, _all_gather_comm([wb["w_up"], wts["conv_w"], wb["w_down"], wb["w_ple"], wb["w_ple_gate"]]))
    conv_w_full = jnp.moveaxis(conv_w_all, 0, 1).reshape(cw_shape[0], N_DEV * cw_shape[1])
    y_attn = _mm(attn, w_ab, b_sm=True, name="mm_attn_branch", tm=big, tn=256, tk=big)
    pooled = _pool_fwd(proj, 3 * AW // PGW, G, PGW, chunk)
    pg, ps = _pool_group_fwd(pooled, w_pg, pscale)
    y_pool = _mm(ps, w_pb, b_sm=True, name="mm_pool_branch", tm=big, tn=256, tk=big)
    gate_cb = (3 * AW + PW) // D
    mixed = _gate_mix(proj, y_attn, y_pool, gate_cb, tr)
    mo = _mm(mixed, w_out, name="mm_out", tk=2048)
    x1, h2 = _resid_rms2(xs, mo, g2, g3, tr)
    upre = _mm(h2, w_up, b_sm=True, name="mm_up", tn=2048, tk=1024)
    w_down, w_pleg = _natural(w_down), _natural(w_pleg)
    act = _conv_fwd(upre, conv_w_full, conv_b_row, chunk)
    yf = _mm(act, w_down, name="mm_down", tk=1408)
    x2, x2b = _resid_rms(x1, yf, g4, tr)
    e = _mm(ps_in, w_ple, b_sm=True, name="mm_ple", tm=big, tn=256, tk=big)
    gl = _mm(x2b, w_pleg, name="mm_ple_gate", tk=2048)
    loss_part, dx3, de, dgl, dg5 = _ple_loss(gl, e, x2, tgt, g5, tr)

    shards = lambda natural: natural.reshape((N_DEV, natural.shape[0] // N_DEV) + natural.shape[1:])
    recv = {}
    dw_ple = _mm(ps_in, de, ta=True, out_sm=True, out_dtype=BF16, name="mm_d_w_ple", tm=256, tn=256, tk=big)
    dw_pleg = shards(_mm(x2b, dgl, ta=True, out_dtype=BF16, name="mm_d_w_ple_gate"))
    dx2g = _mm(dgl, w_pleg, tb=True, name="mm_d_x2", tk=2048)
    dx2, dyf, dg4 = _rms_bwd_a(dx3, dx2g, yf, g4, tr)
    dw_down = shards(_mm(act, dyf, ta=True, out_dtype=BF16, name="mm_d_w_down"))
    dact, recv["w_ple"], recv["w_ple_gate"] = _mm(dyf, w_down, tb=True, name="mm_d_act", tn=1408, tk=1024,
                                                  comm=_reduce_scatter_comm([dw_ple, dw_pleg]))
    dup_g, dup_v, dcw_g, dcw_v, dcb_g, dcb_v, recv["w_down"] = _conv_bwd(upre, dact, conv_w_full, conv_b_row, chunk,
                                                                          _reduce_scatter_comm([dw_down]))
    dw_up = _mm(h2, dup_g, b2=dup_v, ta=True, out_sm=True, out_dtype=BF16, name="mm_d_w_up", tn=2048)
    up_send, up_recv, dw_up, up_land, up_token = _reduce_scatter_start(dw_up, "rs_w_up_start")
    dh2 = _mm(dup_g, w_up, a2=dup_v, tb=True, b_sm=True, name="mm_d_h2", tk=2048, after=(up_token,))
    dx1, dmo, dg3, dg2 = _rms_bwd_b(dx2, dh2, x1, g3, mo, g2, tr)
    dmixed = _mm(dmo, w_out, tb=True, name="mm_d_mixed", tk=2048)
    dw_out = shards(_mm(mixed, dmo, ta=True, out_dtype=BF16, name="mm_d_w_out"))
    dya, dyp, dga, dgp = _gate_bwd(dmixed, proj, y_attn, y_pool, gate_cb, tr)
    dps = _mm(dyp, w_pb, tb=True, b_sm=True, name="mm_d_ps", tm=2048, tk=256)
    dw_pb = _mm(ps, dyp, ta=True, out_sm=True, out_dtype=BF16, name="mm_d_w_pool_branch", tn=256, tk=big)
    dpg, dscale = _scale_bwd(dps, pg, pscale, tr)
    dpooled = _pool_group_bwd_x(dpg, w_pg)
    dw_pg = _pool_group_bwd_w(pooled, dpg, G)
    dw_pg = jnp.moveaxis(dw_pg.astype(BF16).reshape(G, N_DEV, PGW // N_DEV, PGW), 1, 0).reshape(N_DEV, G * PGW // N_DEV, PGW)
    du = _pool_bwd(dpooled, G, chunk)
    dattn = _mm(dya, w_ab, tb=True, b_sm=True, name="mm_d_attn", tm=2048, tk=256)
    dw_ab = _mm(attn, dya, ta=True, out_sm=True, out_dtype=BF16, name="mm_d_w_attn_branch", tn=256, tk=big)
    dq, dk, dv, recv["w_out"], recv["w_pool_branch"], recv["w_pool_group"], recv["w_attn_branch"] = _attn_bwd(
        proj, a_saved, b_saved, dattn, H, blk, _reduce_scatter_comm([dw_out, dw_pb, dw_pg, dw_ab]))
    dw_up, up_land = _reduce_scatter_wait(up_send, up_recv, dw_up, up_land, (dq,), "rs_w_up_wait")
    recv["w_up"] = lax.dynamic_update_slice_in_dim(up_land, lax.dynamic_index_in_dim(dw_up, me, 0, keepdims=True), me, 0)
    dproj = jnp.concatenate([dq, dk, dv, du, dga, dgp], axis=1)
    dw_in = _mm(h, dproj, ta=True, out_sm=True, out_dtype=BF16, name="mm_d_w_in")
    in_send, in_recv, dw_in, in_land, token = _reduce_scatter_start(dw_in, "rs_w_in_start")
    dh = _mm(dproj, w_in, tb=True, b_sm=True, name="mm_d_h", tm=big, tn=512, tk=1024, after=(token,))
    grad_x, dg1 = _rms_bwd_c(dx1, dh, xs, g1, tr)

    gshard, delta, new_m, new_v = {}, {}, {}, {}

    def adamw_cut(n, after=()):
        shp = wts[n].shape
        two_d = (_size(shp[:-1]), shp[-1])
        g_, d_, m_, v_ = _adamw_sum(recv[n].reshape((N_DEV,) + two_d), wts[n].reshape(two_d), mom[n].reshape(two_d),
                                    var[n].reshape(two_d), "adamw_" + n, after)
        gshard[n], delta[n], new_m[n], new_v[n] = g_.reshape(shp), d_.reshape(shp), m_.reshape(shp), v_.reshape(shp)

    for n in COLUMN_CUT[1:] + ROW_CUT + ("w_pool_group",):
        adamw_cut(n, (token,))

    assert COLUMN_CUT[0] == "w_in"
    dw_in, in_land = _reduce_scatter_wait(in_send, in_recv, dw_in, in_land, (grad_x,) + tuple(delta[n] for n in delta),
                                          "rs_w_in_wait")

    dconv_w = jnp.concatenate([dcw_g, dcw_v], axis=1)
    dconv_b = jnp.concatenate([dcb_g, dcb_v], axis=1).reshape(-1)
    rep_parts = {"norm_mix_pre": dg1, "pool_scale": dscale, "norm_mix_post": dg2, "norm_ffn_pre": dg3, "conv_b": dconv_b,
                 "norm_ffn_post": dg4, "norm_ple_post": dg5}
    small = jnp.concatenate([rep_parts[n].reshape(-1) for n in REPLICATED] + [dconv_w.reshape(-1)])
    n_small = small.shape[0]
    small_sum = _all_reduce_small(_pad_rows(small, 8), in_land).reshape(-1)[:n_small]
    off = 0
    for n in REPLICATED:
        sz = _size(wts[n].shape)
        gshard[n] = small_sum[off:off + sz].reshape(wts[n].shape)
        off += sz
    dconv_w_sum = small_sum[off:off + 3 * 2 * F].reshape(3, 2 * F)
    gshard["conv_w"] = lax.dynamic_slice_in_dim(dconv_w_sum, me * cw_shape[1], cw_shape[1], axis=1)

    delta["conv_w"], new_m["conv_w"], new_v["conv_w"] = _adamw(wts["conv_w"], gshard["conv_w"], mom["conv_w"], var["conv_w"],
                                                               "adamw_conv_w")
    rep_sizes = [_size(wts[n].shape) for n in REPLICATED]
    n_rep = sum(rep_sizes)
    cat = lambda t: _pad_rows(jnp.concatenate([t[n].reshape(-1) for n in REPLICATED]), 8)
    d_, m_, v_ = _adamw(cat(wts), cat(gshard), cat(mom), cat(var), "adamw_replicated")
    off = 0
    for n, sz in zip(REPLICATED, rep_sizes):
        shp = wts[n].shape
        delta[n], new_m[n], new_v[n] = (t.reshape(-1)[off:off + sz].reshape(shp) for t in (d_, m_, v_))
        off += sz
    assert off == n_rep

    own = lax.dynamic_index_in_dim(dw_in, me, 0, keepdims=True)
    recv["w_in"] = lax.dynamic_update_slice_in_dim(in_land, own, me, 0)
    adamw_cut("w_in")

    loss = lax.psum(loss_part[0, 0], ("x", "y", "c"))
    lead = lambda t: t[None]
    return (loss, grad_x[None], *[lead(gshard[n]) for n in WEIGHTS], *[lead(delta[n]) for n in WEIGHTS],
            *[lead(new_m[n]) for n in WEIGHTS], *[lead(new_v[n]) for n in WEIGHTS])
```

```python
import functools

import jax
import jax.numpy as jnp
from jax import lax
from jax.experimental import pallas as pl
from jax.experimental.pallas import tpu as pltpu

F32 = jnp.float32
BF16 = jnp.bfloat16
MESH = pl.DeviceIdType.MESH

EPS = 1e-6
HEAD_DIM = 128
POOL_WINDOWS = (2, 4, 8, 16)
POOL_HALO = 16
CONV_HALO = 8
GELU_C0 = 0.7978845608028654
GELU_C1 = 0.044715
ADAM_LR = 0.001
ADAM_B1 = 0.9
ADAM_B2 = 0.999
ADAM_EPS = 1e-08
ADAM_WD = 0.01
ADAM_STEP = 10
N_DEV = 8
PACK_W = 1024
VMEM_LIMIT = 56 * 2**20
ANY = pl.BlockSpec(memory_space=pl.ANY)


def _params(*sem):
    return pltpu.CompilerParams(dimension_semantics=sem, vmem_limit_bytes=VMEM_LIMIT)


def _sds(shape, dtype):
    return jax.ShapeDtypeStruct(shape, dtype)


def _tile(dim, target, align):
    if dim <= target:
        return dim
    t = (target // align) * align
    while t >= align:
        if dim % t == 0:
            return t
        t -= align
    return dim


def _sigmoid(x):
    return 1.0 / (1.0 + jnp.exp(-x))


class _Comm:
    def __init__(self, ins, outs, sems, start, finish):
        self.ins, self.outs, self.sems, self.start, self.finish = list(ins), list(outs), list(sems), start, finish


def _call(body, *, name, grid, in_specs, out_specs, out_shape, args, scratch=(), sem=(), comm=None, after=()):
    in_specs, out_specs, out_shape, scratch = list(in_specs), list(out_specs), list(out_shape), list(scratch)
    if after:
        assert comm is None
        n_in = len(in_specs)
        return pl.pallas_call(lambda *refs: body(*refs[:n_in], *refs[n_in + len(after):]), name=name, grid=grid,
                              in_specs=in_specs + [ANY] * len(after), out_specs=out_specs, out_shape=out_shape,
                              scratch_shapes=scratch, compiler_params=_params(*sem))(*args, *after)
    if comm is None:
        return pl.pallas_call(body, name=name, grid=grid, in_specs=in_specs, out_specs=out_specs, out_shape=out_shape,
                              scratch_shapes=scratch, compiler_params=_params(*sem))(*args)
    n_in, n_out, n_scr, n_ci, n_co = len(in_specs), len(out_specs), len(scratch), len(comm.ins), len(comm.outs)

    def wrapped(*refs):
        ins, refs = refs[:n_in], refs[n_in:]
        c_ins, refs = refs[:n_ci], refs[n_ci:]
        outs, refs = refs[:n_out], refs[n_out:]
        c_outs, refs = refs[:n_co], refs[n_co:]
        scr, c_sems = refs[:n_scr], refs[n_scr:]
        first = last = None
        for axis, size in enumerate(grid):
            at_start, at_end = pl.program_id(axis) == 0, pl.program_id(axis) == size - 1
            first = at_start if first is None else jnp.logical_and(first, at_start)
            last = at_end if last is None else jnp.logical_and(last, at_end)
        if grid:
            pl.when(first)(lambda: comm.start(c_ins, c_outs, c_sems))
            body(*ins, *outs, *scr)
            pl.when(last)(lambda: comm.finish(c_ins, c_outs, c_sems))
        else:
            comm.start(c_ins, c_outs, c_sems)
            body(*ins, *outs, *scr)
            comm.finish(c_ins, c_outs, c_sems)

    return pl.pallas_call(
        wrapped, name=name, grid=grid, in_specs=in_specs + [ANY] * n_ci, out_specs=out_specs + [ANY] * n_co,
        out_shape=out_shape + comm.outs, scratch_shapes=scratch + comm.sems,
        compiler_params=pltpu.CompilerParams(dimension_semantics=("arbitrary",) * len(grid), vmem_limit_bytes=VMEM_LIMIT,
                                             has_side_effects=True),
    )(*args, *comm.ins)


def _mm(a, b, *, ta=False, tb=False, b_sm=False, out_sm=False, out_dtype=F32, name, tm=2048, tn=1024, tk=1024, comm=None,
        after=(), a2=None, b2=None):
    M, K = (a.shape[1], a.shape[0]) if ta else a.shape
    if a2 is not None:
        assert not ta and a2.shape == a.shape
        K = 2 * K
    if b_sm:
        n_sl, rows, per = b.shape
        N = rows if tb else n_sl * per
        assert K == (n_sl * per if tb else rows)
    else:
        N = b.shape[0] if tb else b.shape[1]
    if b2 is not None:
        assert not tb and not b_sm and b2.shape == b.shape
        N = 2 * N
    tm = _tile(M, tm, 128)
    tn = _tile(per if (b_sm and not tb) else N // N_DEV if out_sm else N // 2 if b2 is not None else N, tn, 128)
    tk = _tile(per if (b_sm and tb) else K // 2 if a2 is not None else K, tk, 128)
    nk = K // tk
    kh, jh = nk // 2, (N // tn) // 2
    a_spec = pl.BlockSpec((tk, tm), lambda i, j, k: (k, i)) if ta else pl.BlockSpec((tm, tk), lambda i, j, k: (i, k))
    if a2 is not None:
        a_spec = pl.BlockSpec((tm, tk), lambda i, j, k: (i, jnp.minimum(k, kh - 1)))
        a2_spec = pl.BlockSpec((tm, tk), lambda i, j, k: (i, jnp.maximum(k - kh, 0)))
    if b2 is not None:
        b_spec = pl.BlockSpec((tk, tn), lambda i, j, k: (jnp.where(j < jh, k, 0), jnp.minimum(j, jh - 1)))
        b2_spec = pl.BlockSpec((tk, tn), lambda i, j, k: (jnp.where(j < jh, 0, k), jnp.maximum(j - jh, 0)))
    elif not b_sm:
        b_spec = pl.BlockSpec((tn, tk), lambda i, j, k: (j, k)) if tb else pl.BlockSpec((tk, tn), lambda i, j, k: (k, j))
    elif tb:
        kp = per // tk
        b_spec = pl.BlockSpec((None, tn, tk), lambda i, j, k: (k // kp, j, k % kp))
    else:
        jp = per // tn
        b_spec = pl.BlockSpec((None, tk, tn), lambda i, j, k: (j // jp, k, j % jp))
    if out_sm:
        jo = (N // N_DEV) // tn
        o_spec = pl.BlockSpec((None, tm, tn), lambda i, j, k: (j // jo, i, j % jo))
        o_shape = _sds((N_DEV, M, N // N_DEV), out_dtype)
    else:
        o_spec = pl.BlockSpec((tm, tn), lambda i, j, k: (i, j))
        o_shape = _sds((M, N), out_dtype)
    dims = (((0 if ta else 1,), (1 if tb else 0,)), ((), ()))

    def product(a_ref, b_ref):
        return lax.dot_general(a_ref[...].astype(BF16), b_ref[...].astype(BF16), dims, preferred_element_type=F32)

    def body(a_ref, b_ref, o_ref, acc_ref):
        k = pl.program_id(2)

        @pl.when(k == 0)
        def _():
            acc_ref[...] = jnp.zeros_like(acc_ref)

        acc_ref[...] += product(a_ref, b_ref)

        @pl.when(k == nk - 1)
        def _():
            o_ref[...] = acc_ref[...].astype(o_ref.dtype)

    def body_one_step(a_ref, b_ref, o_ref):
        o_ref[...] = product(a_ref, b_ref).astype(o_ref.dtype)

    def body_split(a_ref, x2_ref, b_ref, o_ref, acc_ref):
        j, k = pl.program_id(1), pl.program_id(2)
        second = k >= kh if a2 is not None else j >= jh

        @pl.when(k == 0)
        def _():
            acc_ref[...] = jnp.zeros_like(acc_ref)

        @pl.when(jnp.logical_not(second))
        def _():
            acc_ref[...] += product(a_ref, b_ref)

        @pl.when(second)
        def _():
            acc_ref[...] += product(x2_ref, b_ref) if a2 is not None else product(a_ref, x2_ref)

        @pl.when(k == nk - 1)
        def _():
            o_ref[...] = acc_ref[...].astype(o_ref.dtype)

    in_specs, args, kernel_body = [a_spec, b_spec], (a, b), body if nk > 1 else body_one_step
    split = a2 is not None or b2 is not None
    if split:
        assert a2 is None or b2 is None
        kernel_body = body_split
        if a2 is not None:
            in_specs, args = [a_spec, a2_spec, b_spec], (a, a2, b)
        else:
            in_specs, args = [a_spec, b_spec, b2_spec], (a, b, b2)
            kernel_body = lambda a_ref, b_ref, b2_ref, o_ref, acc_ref: body_split(a_ref, b2_ref, b_ref, o_ref, acc_ref)
    res = _call(kernel_body, name=name, grid=(M // tm, N // tn, nk), in_specs=in_specs, out_specs=[o_spec],
                out_shape=[o_shape], args=args, scratch=[pltpu.VMEM((tm, tn), F32)] if nk > 1 or split else [],
                sem=("parallel", "parallel", "arbitrary"), comm=comm, after=after)
    return res[0] if comm is None else res


def _pool_group_fwd(pooled, w_pg, scale):
    S = pooled.shape[0]
    G, C, C2 = w_pg.shape
    tm = _tile(S, 1024, 16)

    def body(a_ref, w_ref, s_ref, pg_ref, ps_ref):
        pg = jnp.dot(a_ref[...], w_ref[...], preferred_element_type=F32)
        pg_ref[...] = pg
        ps_ref[...] = (pg * s_ref[...]).astype(BF16)

    return pl.pallas_call(
        body, name="pool_group_fwd", grid=(G, S // tm),
        in_specs=[pl.BlockSpec((tm, C), lambda g, i: (i, g)), pl.BlockSpec((None, C, C2), lambda g, i: (g, 0, 0)),
                  pl.BlockSpec((1, C2), lambda g, i: (0, g))],
        out_specs=[pl.BlockSpec((tm, C2), lambda g, i: (i, g)), pl.BlockSpec((tm, C2), lambda g, i: (i, g))],
        out_shape=[jax.ShapeDtypeStruct((S, G * C2), F32), jax.ShapeDtypeStruct((S, G * C2), BF16)],
        compiler_params=_params("parallel", "parallel"),
    )(pooled, w_pg, scale)


def _pool_group_bwd_x(dpg, w_pg):
    S = dpg.shape[0]
    G, C, C2 = w_pg.shape
    tm = _tile(S, 1024, 16)

    def body(d_ref, w_ref, o_ref):
        o_ref[...] = lax.dot_general(d_ref[...], w_ref[...], (((1,), (1,)), ((), ())), preferred_element_type=F32)

    return pl.pallas_call(
        body, name="pool_group_bwd_x", grid=(G, S // tm),
        in_specs=[pl.BlockSpec((tm, C2), lambda g, i: (i, g)), pl.BlockSpec((None, C, C2), lambda g, i: (g, 0, 0))],
        out_specs=pl.BlockSpec((tm, C), lambda g, i: (i, g)), out_shape=jax.ShapeDtypeStruct((S, G * C), F32),
        compiler_params=_params("parallel", "parallel"),
    )(dpg, w_pg)


def _pool_group_bwd_w(pooled, dpg, G):
    S = pooled.shape[0]
    C, C2 = pooled.shape[1] // G, dpg.shape[1] // G
    tk = _tile(S, 1024, 16)

    def body(a_ref, d_ref, o_ref):
        @pl.when(pl.program_id(1) == 0)
        def _():
            o_ref[...] = jnp.zeros_like(o_ref)

        o_ref[...] += lax.dot_general(a_ref[...], d_ref[...], (((0,), (0,)), ((), ())), preferred_element_type=F32)

    return pl.pallas_call(
        body, name="pool_group_bwd_w", grid=(G, S // tk),
        in_specs=[pl.BlockSpec((tk, C), lambda g, k: (k, g)), pl.BlockSpec((tk, C2), lambda g, k: (k, g))],
        out_specs=pl.BlockSpec((None, C, C2), lambda g, k: (g, 0, 0)), out_shape=jax.ShapeDtypeStruct((G, C, C2), F32),
        compiler_params=_params("parallel", "arbitrary"),
    )(pooled, dpg)


def _rms(x, gain):
    r = lax.rsqrt(jnp.mean(x * x, axis=-1, keepdims=True) + EPS)
    return x * r * gain


def _rms_bwd(x, gain, dy):
    r = lax.rsqrt(jnp.mean(x * x, axis=-1, keepdims=True) + EPS)
    xh = x * r
    dgain = jnp.sum(dy * xh, axis=0, keepdims=True)
    dxh = dy * gain
    dx = r * (dxh - xh * jnp.mean(dxh * xh, axis=-1, keepdims=True))
    return dx, dgain


def _row_call(body, name, ins, outs, tr, *, comm=None):
    S = None
    in_specs, args = [], []
    for it in ins:
        arr, kind = it[0], it[1]
        if kind == "row":
            S = arr.shape[0]
            if len(it) == 4:
                width, cb = it[2], it[3]
                in_specs.append(pl.BlockSpec((tr, width), functools.partial(lambda i, cb: (i, cb), cb=cb)))
            else:
                in_specs.append(pl.BlockSpec((tr, arr.shape[1]), lambda i: (i, 0)))
        else:
            assert arr.ndim == 2
            in_specs.append(pl.BlockSpec(arr.shape, lambda i: (0, 0)))
        args.append(arr)
    out_specs, out_shape = [], []
    for sds, kind in outs:
        if kind == "row":
            out_specs.append(pl.BlockSpec((tr, sds.shape[1]), lambda i: (i, 0)))
        else:
            assert len(sds.shape) == 2
            out_specs.append(pl.BlockSpec(sds.shape, lambda i: (0, 0)))
        out_shape.append(sds)
    return _call(body, name=name, grid=(S // tr,), in_specs=in_specs, out_specs=out_specs, out_shape=out_shape, args=args,
                 sem=("arbitrary",), comm=comm)


def _first_step_zero(*refs):
    @pl.when(pl.program_id(0) == 0)
    def _():
        for r in refs:
            r[...] = jnp.zeros_like(r)


def _rms_fwd(x, gain, tr, comm):
    def body(x_ref, g_ref, o_ref):
        o_ref[...] = _rms(x_ref[...], g_ref[...]).astype(BF16)

    S, D = x.shape
    return _row_call(body, "rms_fwd", [(x, "row"), (gain, "full")], [(_sds((S, D), BF16), "row")], tr, comm=comm)


def _gate_mix(proj, ya, yp, gate_cb, tr):
    S, D = ya.shape

    def body(ga_ref, gp_ref, ya_ref, yp_ref, o_ref):
        o_ref[...] = (_sigmoid(ga_ref[...]) * ya_ref[...] + _sigmoid(gp_ref[...]) * yp_ref[...]).astype(BF16)

    return _row_call(body, "gate_mix", [(proj, "row", D, gate_cb), (proj, "row", D, gate_cb + 1), (ya, "row"), (yp, "row")],
                     [(_sds((S, D), BF16), "row")], tr)[0]


def _resid_rms2(x, mo, g2, g3, tr):
    S, D = x.shape

    def body(x_ref, mo_ref, g2_ref, g3_ref, x1_ref, h2_ref):
        x1 = x_ref[...] + _rms(mo_ref[...], g2_ref[...])
        x1_ref[...] = x1
        h2_ref[...] = _rms(x1, g3_ref[...]).astype(BF16)

    return _row_call(body, "resid_rms2", [(x, "row"), (mo, "row"), (g2, "full"), (g3, "full")],
                     [(_sds((S, D), F32), "row"), (_sds((S, D), BF16), "row")], tr)


def _resid_rms(x1, yf, g4, tr):
    S, D = x1.shape

    def body(x_ref, y_ref, g_ref, o_ref, ob_ref):
        x2 = x_ref[...] + _rms(y_ref[...], g_ref[...])
        o_ref[...] = x2
        ob_ref[...] = x2.astype(BF16)

    return _row_call(body, "resid_rms", [(x1, "row"), (yf, "row"), (g4, "full")],
                     [(_sds((S, D), F32), "row"), (_sds((S, D), BF16), "row")], tr)


def _ple_loss(gl, e, x2, tgt, g5, tr):
    S, D = x2.shape

    def body(gl_ref, e_ref, x2_ref, t_ref, g_ref, loss_ref, dx3_ref, de_ref, dgl_ref, dg_ref):
        _first_step_zero(loss_ref, dg_ref)
        s = _sigmoid(gl_ref[...])
        e_ = e_ref[...]
        t = s * e_
        gain = g_ref[...]
        err = x2_ref[...] + _rms(t, gain) - t_ref[...]
        row_loss = jnp.mean(err * err, axis=-1, keepdims=True)
        loss_ref[...] += 0.5 * jnp.sum(row_loss, axis=0, keepdims=True)
        dx3 = err * (1.0 / D)
        dx3_ref[...] = dx3
        dt, dgain = _rms_bwd(t, gain, dx3)
        dg_ref[...] += dgain
        de_ref[...] = (dt * s).astype(BF16)
        dgl_ref[...] = (dt * e_ * s * (1.0 - s)).astype(BF16)

    return _row_call(body, "ple_loss", [(gl, "row"), (e, "row"), (x2, "row"), (tgt, "row"), (g5, "full")],
                     [(_sds((1, 1), F32), "acc"), (_sds((S, D), F32), "row"), (_sds((S, D), BF16), "row"),
                      (_sds((S, D), BF16), "row"), (_sds((1, D), F32), "acc")], tr)


def _rms_bwd_a(dx3, dx2g, yf, g4, tr):
    S, D = yf.shape

    def body(a_ref, b_ref, y_ref, g_ref, dx_ref, dy_ref, dg_ref):
        _first_step_zero(dg_ref)
        dx2 = a_ref[...] + b_ref[...]
        dx_ref[...] = dx2
        dy, dgain = _rms_bwd(y_ref[...], g_ref[...], dx2)
        dy_ref[...] = dy.astype(BF16)
        dg_ref[...] += dgain

    return _row_call(body, "rms_bwd_a", [(dx3, "row"), (dx2g, "row"), (yf, "row"), (g4, "full")],
                     [(_sds((S, D), F32), "row"), (_sds((S, D), BF16), "row"), (_sds((1, D), F32), "acc")], tr)


def _rms_bwd_b(dx2, dh2, x1, g3, mo, g2, tr):
    S, D = x1.shape

    def body(dx2_ref, dh2_ref, x1_ref, g3_ref, mo_ref, g2_ref, dx1_ref, dmo_ref, dg3_ref, dg2_ref):
        _first_step_zero(dg3_ref, dg2_ref)
        d, dgain3 = _rms_bwd(x1_ref[...], g3_ref[...], dh2_ref[...])
        dx1 = dx2_ref[...] + d
        dx1_ref[...] = dx1
        dg3_ref[...] += dgain3
        dmo, dgain2 = _rms_bwd(mo_ref[...], g2_ref[...], dx1)
        dmo_ref[...] = dmo.astype(BF16)
        dg2_ref[...] += dgain2

    return _row_call(body, "rms_bwd_b", [(dx2, "row"), (dh2, "row"), (x1, "row"), (g3, "full"), (mo, "row"), (g2, "full")],
                     [(_sds((S, D), F32), "row"), (_sds((S, D), BF16), "row"), (_sds((1, D), F32), "acc"),
                      (_sds((1, D), F32), "acc")], tr)


def _rms_bwd_c(dx1, dh, x, g1, tr):
    S, D = x.shape

    def body(dx1_ref, dh_ref, x_ref, g_ref, o_ref, dg_ref):
        _first_step_zero(dg_ref)
        d, dgain = _rms_bwd(x_ref[...], g_ref[...], dh_ref[...])
        o_ref[...] = dx1_ref[...] + d
        dg_ref[...] += dgain

    return _row_call(body, "rms_bwd_c", [(dx1, "row"), (dh, "row"), (x, "row"), (g1, "full")],
                     [(_sds((S, D), F32), "row"), (_sds((1, D), F32), "acc")], tr)


def _gate_bwd(dmixed, proj, ya, yp, gate_cb, tr):
    S, D = ya.shape

    def body(dm_ref, ga_ref, gp_ref, ya_ref, yp_ref, dya_ref, dyp_ref, dga_ref, dgp_ref):
        dm = dm_ref[...]
        sa = _sigmoid(ga_ref[...])
        sp = _sigmoid(gp_ref[...])
        dya_ref[...] = (dm * sa).astype(BF16)
        dyp_ref[...] = (dm * sp).astype(BF16)
        dga_ref[...] = (dm * ya_ref[...] * sa * (1.0 - sa)).astype(BF16)
        dgp_ref[...] = (dm * yp_ref[...] * sp * (1.0 - sp)).astype(BF16)

    return _row_call(body, "gate_bwd",
                     [(dmixed, "row"), (proj, "row", D, gate_cb), (proj, "row", D, gate_cb + 1), (ya, "row"), (yp, "row")],
                     [(_sds((S, D), BF16), "row")] * 4, tr)


def _scale_bwd(dps, pg, scale, tr):
    S, W = dps.shape

    def body(d_ref, pg_ref, s_ref, o_ref, ds_ref):
        _first_step_zero(ds_ref)
        d = d_ref[...]
        o_ref[...] = (d * s_ref[...]).astype(BF16)
        ds_ref[...] += jnp.sum(d * pg_ref[...], axis=0, keepdims=True)

    return _row_call(body, "scale_bwd", [(dps, "row"), (pg, "row"), (scale, "full")],
                     [(_sds((S, W), BF16), "row"), (_sds((1, W), F32), "acc")], tr)


CUMSUM_TERMS = 2
GRAD_TERMS = 1


def _tri(blk, cmp, terms=CUMSUM_TERMS):
    j = lax.broadcasted_iota(jnp.int32, (blk, blk), 0)
    s = lax.broadcasted_iota(jnp.int32, (blk, blk), 1)
    one = jnp.concatenate([cmp(j, s).astype(BF16), jnp.ones((blk, 128), BF16)], axis=1)
    return jnp.concatenate([one] * terms, axis=0)


def _split(x, terms=CUMSUM_TERMS):
    parts = []
    for _ in range(terms):
        t = x.astype(BF16)
        parts.append(t)
        x = x - t.astype(F32)
    return parts[0] if terms == 1 else jnp.concatenate(parts, axis=1)


def _causal(blk):
    return lax.broadcasted_iota(jnp.int32, (blk, blk), 1) < lax.broadcasted_iota(jnp.int32, (blk, blk), 0)


def _scores(q, kj, scale, causal):
    z = lax.dot_general(q, kj, (((1,), (1,)), ((), ())), preferred_element_type=F32) * scale
    l1p = jnp.log(1.0 + jnp.exp(-jnp.abs(z)))
    lb = -(jnp.maximum(z, 0.0) + l1p)
    if causal is not None:
        lb = jnp.where(causal, lb, 0.0)
    return z, lb, jnp.minimum(z, 0.0) - l1p


def _attn_fwd(proj, n_heads, blk, comm=None):
    S = proj.shape[0]
    nq = S // blk
    scale = HEAD_DIM ** -0.5
    lanes = blk // 128
    hp = 4 if n_heads % 4 == 0 else 2 if n_heads % 2 == 0 else 1
    cols = [slice(h * HEAD_DIM, (h + 1) * HEAD_DIM) for h in range(hp)]
    u_incl = _tri(blk, lambda j, s: j >= s)

    def body(q_ref, k_ref, v_ref, u_ref, o_ref, a_ref, b_ref, z_buf, hl_buf):
        i = pl.program_id(1)
        qs = [q_ref[:, c].astype(BF16) for c in cols]
        u = u_ref[...]

        def scores(j, h, causal):
            kj = k_ref[pl.ds(pl.multiple_of(j * blk, blk), blk), cols[h]].astype(BF16)
            z, lb, log_beta = _scores(qs[h], kj, scale, causal)
            b_ref[h, j] = jnp.exp(log_beta).astype(BF16)
            return z, _split(lb)

        def weigh(j, h, z, hl, acc, run, causal):
            vj = v_ref[pl.ds(pl.multiple_of(j * blk, blk), blk), cols[h]].astype(BF16)
            ct = jnp.dot(hl, u, preferred_element_type=F32)
            a = jnp.exp(z + ct[:, :blk] + jnp.tile(run, (1, lanes)))
            if causal is not None:
                a = jnp.where(causal, a, 0.0)
            a = a.astype(BF16)
            a_ref[h, j] = a
            return acc + jnp.dot(a, vj, preferred_element_type=F32), run + ct[:, blk:]

        def stage_scores(j):
            for h in range(hp):
                z_buf[h], hl_buf[h] = scores(j, h, None)

        zero = jnp.zeros((blk, HEAD_DIM), F32)
        causal = _causal(blk)
        carry = tuple(weigh(i, h, *scores(i, h, causal), zero, zero, causal) for h in range(hp))
        stage_scores(jnp.maximum(i - 1, 0))

        def step(t, carry):
            j = i - 1 - t
            out = tuple(weigh(j, h, z_buf[h], hl_buf[h], *carry[h], None) for h in range(hp))
            stage_scores(j - 1)
            return out

        carry = lax.fori_loop(0, jnp.maximum(i - 1, 0), step, carry)
        carry = lax.fori_loop(0, jnp.minimum(i, 1),
                              lambda t, c: tuple(weigh(t, h, z_buf[h], hl_buf[h], *c[h], None) for h in range(hp)), carry)
        for h in range(hp):
            o_ref[:, cols[h]] = carry[h][0].astype(BF16)

    G = n_heads // hp
    W = hp * HEAD_DIM
    saved = _sds((n_heads, nq, nq, blk, blk), BF16)
    saved_spec = pl.BlockSpec((hp, None, nq, blk, blk), lambda h, i: (h, i, 0, 0, 0))
    return _call(
        body, name="attn_fwd", grid=(G, nq),
        in_specs=[pl.BlockSpec((blk, W), lambda h, i: (i, h)),
                  pl.BlockSpec((S, W), lambda h, i: (0, G + h)),
                  pl.BlockSpec((S, W), lambda h, i: (0, 2 * G + h)),
                  pl.BlockSpec(u_incl.shape, lambda h, i: (0, 0))],
        out_specs=[pl.BlockSpec((blk, W), lambda h, i: (i, h)), saved_spec, saved_spec],
        out_shape=[_sds((S, n_heads * HEAD_DIM), BF16), saved, saved],
        args=(proj, proj, proj, u_incl), scratch=[pltpu.VMEM((hp, blk, blk), F32), pltpu.VMEM((hp, blk, CUMSUM_TERMS * blk), BF16)],
        sem=("parallel", "arbitrary"), comm=comm)


def _attn_bwd(proj, a_saved, b_saved, do, n_heads, blk, comm=None):
    S = proj.shape[0]
    nq = S // blk
    scale = HEAD_DIM ** -0.5
    lanes = blk // 128
    hp = 2 if n_heads % 2 == 0 else 1
    cols = [slice(h * HEAD_DIM, (h + 1) * HEAD_DIM) for h in range(hp)]
    l_incl = _tri(blk, lambda j, s: j <= s, GRAD_TERMS)

    def body(q_ref, k_ref, v_ref, a_ref, b_ref, do_ref, li_ref, dq_ref, dk_ref, dv_ref, dk_acc, dv_acc, g_buf, gl_buf):
        i = pl.program_id(1)

        @pl.when(i == 0)
        def _():
            dk_acc[...] = jnp.zeros_like(dk_acc)
            dv_acc[...] = jnp.zeros_like(dv_acc)

        qs = [q_ref[:, c].astype(BF16) for c in cols]
        dobs = [do_ref[:, c].astype(BF16) for c in cols]
        li = li_ref[...]

        def stage_products(j):
            ks = pl.multiple_of(j * blk, blk)
            for h in range(hp):
                vj = v_ref[pl.ds(ks, blk), cols[h]].astype(BF16)
                da = lax.dot_general(dobs[h], vj, (((1,), (1,)), ((), ())), preferred_element_type=F32)
                g = a_ref[h, j].astype(F32) * da
                g_buf[h] = g
                gl_buf[h] = _split(g, GRAD_TERMS)

        def grads(j, h, dq, run_g, causal):
            ks = pl.multiple_of(j * blk, blk)
            kj = k_ref[pl.ds(ks, blk), cols[h]].astype(BF16)
            gt = jnp.dot(gl_buf[h], li, preferred_element_type=F32)
            dz = g_buf[h] - b_ref[h, j].astype(F32) * (gt[:, :blk] + jnp.tile(run_g, (1, lanes)))
            if causal is not None:
                dz = jnp.where(causal, dz, 0.0)
            dzs = (dz * scale).astype(BF16)
            dk_acc[pl.ds(ks, blk), cols[h]] += lax.dot_general(dzs, qs[h], (((0,), (0,)), ((), ())),
                                                               preferred_element_type=F32)
            dv_acc[pl.ds(ks, blk), cols[h]] += lax.dot_general(a_ref[h, j], dobs[h], (((0,), (0,)), ((), ())),
                                                               preferred_element_type=F32)
            return dq + jnp.dot(dzs, kj, preferred_element_type=F32), run_g + gt[:, blk:]

        zero = jnp.zeros((blk, HEAD_DIM), F32)
        stage_products(0)

        def step(j, carry):
            out = tuple(grads(j, h, *carry[h], None) for h in range(hp))
            stage_products(j + 1)
            return out

        carry = lax.fori_loop(0, i, step, ((zero, zero),) * hp)
        causal = _causal(blk)
        carry = tuple(grads(i, h, *carry[h], causal) for h in range(hp))
        for h in range(hp):
            dq_ref[:, cols[h]] = carry[h][0].astype(BF16)

        @pl.when(i == nq - 1)
        def _():
            dk_ref[...] = dk_acc[...].astype(BF16)
            dv_ref[...] = dv_acc[...].astype(BF16)

    G = n_heads // hp
    W = hp * HEAD_DIM
    AW = n_heads * HEAD_DIM
    saved_spec = pl.BlockSpec((hp, None, nq, blk, blk), lambda h, i: (h, i, 0, 0, 0))
    return _call(
        body, name="attn_bwd", grid=(G, nq),
        in_specs=[pl.BlockSpec((blk, W), lambda h, i: (i, h)),
                  pl.BlockSpec((S, W), lambda h, i: (0, G + h)),
                  pl.BlockSpec((S, W), lambda h, i: (0, 2 * G + h)),
                  saved_spec, saved_spec,
                  pl.BlockSpec((blk, W), lambda h, i: (i, h)),
                  pl.BlockSpec(l_incl.shape, lambda h, i: (0, 0))],
        out_specs=[pl.BlockSpec((blk, W), lambda h, i: (i, h)),
                   pl.BlockSpec((S, W), lambda h, i: (0, h)),
                   pl.BlockSpec((S, W), lambda h, i: (0, h))],
        out_shape=[_sds((S, AW), BF16)] * 3, args=(proj, proj, proj, a_saved, b_saved, do, l_incl),
        scratch=[pltpu.VMEM((S, W), F32)] * 2 + [pltpu.VMEM((hp, blk, blk), F32),
                                                 pltpu.VMEM((hp, blk, GRAD_TERMS * blk), BF16)],
        sem=("parallel", "arbitrary"), comm=comm)


def _pool_count(r0, rows, w):
    t = r0 + lax.broadcasted_iota(jnp.int32, (rows, 1), 0)
    return jnp.minimum(t + 1, w).astype(F32)


def _pool_fwd(proj, col_blk, n_groups, width, chunk):
    S = proj.shape[0]
    H = POOL_HALO

    def body(u_ref, o_ref, pad_ref):
        g = pl.program_id(0)
        pad_ref[0:H, :] = jnp.zeros((H, width), F32)
        pad_ref[H:, :] = u_ref[...]
        for gi, w in enumerate(POOL_WINDOWS[:n_groups]):
            @pl.when(g == gi)
            def _(w=w):
                def one(c, _):
                    r0 = pl.multiple_of(c * chunk, chunk)
                    ext = pad_ref[pl.ds(r0, chunk + H), :]
                    s = ext
                    k = 1
                    while k < w:
                        s = s + pltpu.roll(s, k, 0)
                        k *= 2
                    o_ref[pl.ds(r0, chunk), :] = (s[H:] / _pool_count(r0, chunk, w) - ext[H:]).astype(BF16)
                    return 0

                lax.fori_loop(0, S // chunk, one, 0)

    return pl.pallas_call(
        body, name="pool_fwd", grid=(n_groups,),
        in_specs=[pl.BlockSpec((S, width), lambda g: (0, col_blk + g))],
        out_specs=pl.BlockSpec((S, width), lambda g: (0, g)), out_shape=_sds((S, n_groups * width), BF16),
        scratch_shapes=[pltpu.VMEM((S + H, width), F32)], compiler_params=_params("parallel"),
    )(proj)


def _pool_bwd(dpooled, n_groups, chunk):
    S = dpooled.shape[0]
    width = dpooled.shape[1] // n_groups
    H = POOL_HALO

    def body(d_ref, o_ref, pad_ref):
        g = pl.program_id(0)
        pad_ref[S:, :] = jnp.zeros((H, width), F32)
        for gi, w in enumerate(POOL_WINDOWS[:n_groups]):
            @pl.when(g == gi)
            def _(w=w):
                def fill(c, _):
                    r0 = pl.multiple_of(c * chunk, chunk)
                    pad_ref[pl.ds(r0, chunk), :] = d_ref[pl.ds(r0, chunk), :] / _pool_count(r0, chunk, w)
                    return 0

                lax.fori_loop(0, S // chunk, fill, 0)

                def one(c, _):
                    r0 = pl.multiple_of(c * chunk, chunk)
                    s = pad_ref[pl.ds(r0, chunk + H), :]
                    k = 1
                    while k < w:
                        s = s + pltpu.roll(s, chunk + H - k, 0)
                        k *= 2
                    o_ref[pl.ds(r0, chunk), :] = (s[:chunk] - d_ref[pl.ds(r0, chunk), :]).astype(BF16)
                    return 0

                lax.fori_loop(0, S // chunk, one, 0)

    return pl.pallas_call(
        body, name="pool_bwd", grid=(n_groups,),
        in_specs=[pl.BlockSpec((S, width), lambda g: (0, g))],
        out_specs=pl.BlockSpec((S, width), lambda g: (0, g)), out_shape=_sds((S, n_groups * width), BF16),
        scratch_shapes=[pltpu.VMEM((S + H, width), F32)], compiler_params=_params("parallel"),
    )(dpooled)


def _conv3(x_ext, w, b, shifted=None):
    x2, x1 = shifted if shifted is not None else (pltpu.roll(x_ext, 2, 0), pltpu.roll(x_ext, 1, 0))
    return b + x2 * w[0:1, :] + x1 * w[1:2, :] + x_ext * w[2:3, :]


def _gelu_parts(x):
    th = jnp.tanh(GELU_C0 * (x + GELU_C1 * (x * x * x)))
    return th, 0.5 * (1.0 + th)


def _conv_specs(S, F, cb):
    nb = F // cb
    return [pl.BlockSpec((S, cb), lambda j: (0, j)), pl.BlockSpec((S, cb), lambda j: (0, nb + j)),
            pl.BlockSpec((3, cb), lambda j: (0, j)), pl.BlockSpec((3, cb), lambda j: (0, nb + j)),
            pl.BlockSpec((1, cb), lambda j: (0, j)), pl.BlockSpec((1, cb), lambda j: (0, nb + j))]


def _conv_fwd(upre, cw, cb_, chunk):
    S, F2 = upre.shape
    F = F2 // 2
    cb = 128
    H = CONV_HALO

    def body(g_ref, v_ref, wg_ref, wv_ref, bg_ref, bv_ref, o_ref, pg_ref, pv_ref):
        pg_ref[0:H, :] = jnp.zeros((H, cb), F32)
        pv_ref[0:H, :] = jnp.zeros((H, cb), F32)
        pg_ref[H:, :] = g_ref[...]
        pv_ref[H:, :] = v_ref[...]
        wg, wv, bg, bv = wg_ref[...], wv_ref[...], bg_ref[...], bv_ref[...]

        def one(c, _):
            r0 = pl.multiple_of(c * chunk, chunk)
            up_g = _conv3(pg_ref[pl.ds(r0, chunk + H), :], wg, bg)[H:]
            up_v = _conv3(pv_ref[pl.ds(r0, chunk + H), :], wv, bv)[H:]
            _, cdf = _gelu_parts(up_g)
            o_ref[pl.ds(r0, chunk), :] = (up_g * cdf * up_v).astype(BF16)
            return 0

        lax.fori_loop(0, S // chunk, one, 0)

    return pl.pallas_call(
        body, name="conv_fwd", grid=(F // cb,), in_specs=_conv_specs(S, F, cb),
        out_specs=pl.BlockSpec((S, cb), lambda j: (0, j)), out_shape=_sds((S, F), BF16),
        scratch_shapes=[pltpu.VMEM((S + H, cb), F32)] * 2, compiler_params=_params("parallel"),
    )(upre, upre, cw, cw, cb_, cb_)


def _conv_bwd(upre, dact, cw, cb_, chunk, comm=None):
    S, F2 = upre.shape
    F = F2 // 2
    cb = 128
    H = CONV_HALO
    E = chunk + 2 * H

    def body(g_ref, v_ref, wg_ref, wv_ref, bg_ref, bv_ref, d_ref, dg_ref, dv_ref, dwg_ref, dwv_ref, dbg_ref, dbv_ref,
             pg_ref, pv_ref, pd_ref):
        for p, src in ((pg_ref, g_ref), (pv_ref, v_ref), (pd_ref, d_ref)):
            p[0:H, :] = jnp.zeros((H, cb), F32)
            p[H:S + H, :] = src[...]
            p[S + H:, :] = jnp.zeros((H, cb), F32)
        wg, wv, bg, bv = wg_ref[...], wv_ref[...], bg_ref[...], bv_ref[...]

        def taps_bwd(d, w):
            return d * w[2:3, :] + pltpu.roll(d, E - 1, 0) * w[1:2, :] + pltpu.roll(d, E - 2, 0) * w[0:1, :]

        def wsum(d, x, x2, x1):
            dc = d[H:H + chunk]
            return [jnp.sum(dc * x2[H:H + chunk], axis=0, keepdims=True),
                    jnp.sum(dc * x1[H:H + chunk], axis=0, keepdims=True),
                    jnp.sum(dc * x[H:H + chunk], axis=0, keepdims=True),
                    jnp.sum(dc, axis=0, keepdims=True)]

        def one(c, acc):
            r0 = pl.multiple_of(c * chunk, chunk)
            xg = pg_ref[pl.ds(r0, E), :]
            xv = pv_ref[pl.ds(r0, E), :]
            d = pd_ref[pl.ds(r0, E), :]
            sg = (pltpu.roll(xg, 2, 0), pltpu.roll(xg, 1, 0))
            sv = (pltpu.roll(xv, 2, 0), pltpu.roll(xv, 1, 0))
            up_g = _conv3(xg, wg, bg, sg)
            up_v = _conv3(xv, wv, bv, sv)
            th, cdf = _gelu_parts(up_g)
            dgelu = cdf + 0.5 * up_g * (1.0 - th * th) * (GELU_C0 * (1.0 + 3.0 * GELU_C1 * (up_g * up_g)))
            dgate = d * up_v * dgelu
            dval = d * (up_g * cdf)
            dg_ref[pl.ds(r0, chunk), :] = taps_bwd(dgate, wg)[H:H + chunk].astype(BF16)
            dv_ref[pl.ds(r0, chunk), :] = taps_bwd(dval, wv)[H:H + chunk].astype(BF16)
            return tuple(a + b for a, b in zip(acc, wsum(dgate, xg, *sg) + wsum(dval, xv, *sv)))

        zero = jnp.zeros((1, cb), F32)
        acc = lax.fori_loop(0, S // chunk, one, (zero,) * 8)
        dwg_ref[...] = jnp.concatenate(acc[0:3], axis=0)
        dbg_ref[...] = acc[3]
        dwv_ref[...] = jnp.concatenate(acc[4:7], axis=0)
        dbv_ref[...] = acc[7]

    col = lambda rows: pl.BlockSpec((rows, cb), lambda j: (0, j))
    return _call(
        body, name="conv_bwd", grid=(F // cb,), in_specs=_conv_specs(S, F, cb) + [col(S)],
        out_specs=[col(S), col(S), col(3), col(3), col(1), col(1)],
        out_shape=[_sds((S, F), BF16), _sds((S, F), BF16), _sds((3, F), F32), _sds((3, F), F32), _sds((1, F), F32),
                   _sds((1, F), F32)],
        args=(upre, upre, cw, cw, cb_, cb_, dact), scratch=[pltpu.VMEM((S + 2 * H, cb), F32)] * 3, sem=("parallel",),
        comm=comm)


def _position():
    x, y, c = lax.axis_index("x"), lax.axis_index("y"), lax.axis_index("c")
    return x, y, c, 4 * x + 2 * y + c


def _peer(x, y, c, d):
    px = 1 - x if d & 4 else x
    py = 1 - y if d & 2 else y
    pc = 1 - c if d & 1 else c
    return (px, py, pc), 4 * px + 2 * py + pc


def _all_gather_comm(tensors):
    nt = len(tensors)
    outs = [_sds((N_DEV,) + t.shape, t.dtype) for t in tensors]
    sems = [pltpu.SemaphoreType.DMA((7 * nt,)), pltpu.SemaphoreType.DMA((7 * nt,)), pltpu.SemaphoreType.DMA((nt,))]

    def parts(ins, outs_, sem_refs):
        send, recv, loc = sem_refs
        x, y, c, me = _position()
        chips = [(1 - x, y), (x, 1 - y), (1 - x, 1 - y)]

        def copy(t, k, block, to, src=None):
            slot = outs_[t].at[4 * block[0] + 2 * block[1] + block[2]]
            return pltpu.make_async_remote_copy(src_ref=slot if src is None else src, dst_ref=slot,
                                                send_sem=send.at[7 * t + k], recv_sem=recv.at[7 * t + k], device_id=to,
                                                device_id_type=MESH)

        def mine(t):
            return pltpu.make_async_copy(ins[t], outs_[t].at[me], loc.at[t])

        return (x, y, c), (x, y, 1 - c), chips, copy, mine

    def start(ins, outs_, sem_refs):
        me, sibling, chips, copy, mine = parts(ins, outs_, sem_refs)
        for t in range(nt):
            mine(t).start()
            copy(t, 0, me, sibling, src=ins[t]).start()
            for j, chip in enumerate(chips):
                copy(t, 1 + j, me, (*chip, me[2]), src=ins[t]).start()

    def finish(ins, outs_, sem_refs):
        me, sibling, chips, copy, mine = parts(ins, outs_, sem_refs)
        c = me[2]
        for t in range(nt):
            for j, chip in enumerate(chips):
                copy(t, 1 + j, (*chip, c), me).wait_recv()
                copy(t, 4 + j, (*chip, c), sibling).start()
        for t in range(nt):
            copy(t, 0, sibling, me).wait_recv()
            for j, chip in enumerate(chips):
                copy(t, 4 + j, (*chip, 1 - c), me).wait_recv()
        for t in range(nt):
            copy(t, 0, me, sibling, src=ins[t]).wait_send()
            for j, chip in enumerate(chips):
                copy(t, 1 + j, me, (*chip, c), src=ins[t]).wait_send()
                copy(t, 4 + j, (*chip, c), sibling).wait_send()
            mine(t).wait()

    return _Comm(tensors, outs, sems, start, finish)


def _reduce_scatter_comm(tensors):
    nt = len(tensors)
    outs = [_sds(t.shape, t.dtype) for t in tensors]
    sems = [pltpu.SemaphoreType.DMA((7 * nt,)), pltpu.SemaphoreType.DMA((7 * nt,)), pltpu.SemaphoreType.DMA((nt,))]

    def local(ins, outs_, sem_refs, t, me):
        return pltpu.make_async_copy(ins[t].at[me], outs_[t].at[me], sem_refs[2].at[t])

    def remote(ins, outs_, sem_refs, t, d, inbound):
        x, y, c, me = _position()
        peer, peer_idx = _peer(x, y, c, d)
        k = 7 * t + d - 1
        src, dst, to = (ins[t].at[me], outs_[t].at[peer_idx], (x, y, c)) if inbound else (ins[t].at[peer_idx], outs_[t].at[me], peer)
        return pltpu.make_async_remote_copy(src_ref=src, dst_ref=dst, send_sem=sem_refs[0].at[k], recv_sem=sem_refs[1].at[k],
                                            device_id=to, device_id_type=MESH)

    def start(ins, outs_, sem_refs):
        me = _position()[3]
        for t in range(nt):
            local(ins, outs_, sem_refs, t, me).start()
            for d in range(1, N_DEV):
                remote(ins, outs_, sem_refs, t, d, False).start()

    def finish(ins, outs_, sem_refs):
        me = _position()[3]
        for t in range(nt):
            for d in range(1, N_DEV):
                remote(ins, outs_, sem_refs, t, d, True).wait_recv()
        for t in range(nt):
            for d in range(1, N_DEV):
                remote(ins, outs_, sem_refs, t, d, False).wait_send()
            local(ins, outs_, sem_refs, t, me).wait()

    return _Comm(tensors, outs, sems, start, finish)


HBM_SPEC = pl.BlockSpec(memory_space=pltpu.HBM)
SEM_SPEC = pl.BlockSpec(memory_space=pltpu.SEMAPHORE)
DATAFLOW = pltpu.SideEffectType.DATAFLOW_SIDE_EFFECTING


def _scatter_copy(g_ref, land_ref, send_sems, recv_sems, d):
    x, y, c, me = _position()
    peer, peer_idx = _peer(x, y, c, d)
    return pltpu.make_async_remote_copy(src_ref=g_ref.at[peer_idx], dst_ref=land_ref.at[me], send_sem=send_sems.at[d - 1],
                                        recv_sem=recv_sems.at[d - 1], device_id=peer, device_id_type=MESH)


def _reduce_scatter_start(g, name):
    def body(g_ref, land_ref, send_sems, recv_sems, g_thru, land_thru, token):
        for d in range(1, N_DEV):
            _scatter_copy(g_ref, land_ref, send_sems, recv_sems, d).start()
        token[...] = jnp.zeros_like(token)

    return pl.pallas_call(
        body, name=name,
        out_shape=(pltpu.SemaphoreType.DMA((N_DEV - 1,)), pltpu.SemaphoreType.DMA((N_DEV - 1,)), pltpu.HBM(g.shape, g.dtype),
                   pltpu.HBM(g.shape, g.dtype), _sds((8, 128), F32)),
        in_specs=(HBM_SPEC, HBM_SPEC), out_specs=(SEM_SPEC, SEM_SPEC, HBM_SPEC, HBM_SPEC, pl.BlockSpec(memory_space=pltpu.VMEM)),
        input_output_aliases={0: 2, 1: 3}, compiler_params=pltpu.CompilerParams(has_side_effects=DATAFLOW),
    )(pltpu.with_memory_space_constraint(g, pltpu.HBM), pltpu.with_memory_space_constraint(lax.empty(g.shape, g.dtype), pltpu.HBM))


def _reduce_scatter_wait(send_sems, recv_sems, g_thru, land_thru, after, name):
    def body(g_ref, land_ref, send_sems, recv_sems, *rest):
        for d in range(1, N_DEV):
            copy = _scatter_copy(g_ref, land_ref, send_sems, recv_sems, d)
            copy.wait_send()
            copy.wait_recv()

    return pl.pallas_call(
        body, name=name, out_shape=(pltpu.HBM(g_thru.shape, g_thru.dtype), pltpu.HBM(g_thru.shape, g_thru.dtype)),
        in_specs=(HBM_SPEC, HBM_SPEC, SEM_SPEC, SEM_SPEC) + (ANY,) * len(after), out_specs=(HBM_SPEC, HBM_SPEC),
        input_output_aliases={0: 0, 1: 1}, compiler_params=pltpu.CompilerParams(has_side_effects=DATAFLOW),
    )(g_thru, land_thru, send_sems, recv_sems, *after)


def _all_reduce_small(part, after):
    r, W = part.shape

    def body(p_ref, after_ref, o_ref, g_ref, send_sems, recv_sems):
        x, y, c, me = _position()
        sends = []
        for d in range(1, N_DEV):
            peer, _ = _peer(x, y, c, d)
            cp = pltpu.make_async_remote_copy(src_ref=p_ref, dst_ref=g_ref.at[me], send_sem=send_sems.at[d - 1],
                                              recv_sem=recv_sems.at[d - 1], device_id=peer, device_id_type=MESH)
            cp.start()
            sends.append(cp)
        g_ref[me] = p_ref[...]
        for d in range(1, N_DEV):
            _, peer_idx = _peer(x, y, c, d)
            pltpu.make_async_remote_copy(src_ref=p_ref, dst_ref=g_ref.at[peer_idx], send_sem=send_sems.at[d - 1],
                                         recv_sem=recv_sems.at[d - 1], device_id=(x, y, c), device_id_type=MESH).wait_recv()
        for cp in sends:
            cp.wait_send()
        acc = g_ref[0]
        for i in range(1, N_DEV):
            acc = acc + g_ref[i]
        o_ref[...] = acc

    vmem = pl.BlockSpec(memory_space=pltpu.VMEM)
    return pl.pallas_call(
        body, name="all_reduce_small", in_specs=[vmem, ANY], out_specs=[vmem, vmem],
        out_shape=[_sds((r, W), F32), _sds((N_DEV, r, W), F32)],
        scratch_shapes=[pltpu.SemaphoreType.DMA((7,)), pltpu.SemaphoreType.DMA((7,))],
        compiler_params=pltpu.CompilerParams(has_side_effects=True, vmem_limit_bytes=VMEM_LIMIT),
    )(part, after)[0]


def _adamw_math(w, g, m, v):
    m = ADAM_B1 * m + (1.0 - ADAM_B1) * g
    v = ADAM_B2 * v + (1.0 - ADAM_B2) * (g * g)
    m_hat = m / (1.0 - ADAM_B1 ** ADAM_STEP)
    v_hat = v / (1.0 - ADAM_B2 ** ADAM_STEP)
    return -ADAM_LR * (m_hat / (jnp.sqrt(v_hat) + ADAM_EPS) + ADAM_WD * w), m, v


def _adamw(w, g, m, v, name):
    rows, cols = w.shape
    tr = _tile(rows, max(8, (2**18 // cols) // 8 * 8), 8)

    def body(w_ref, g_ref, m_ref, v_ref, d_ref, nm_ref, nv_ref):
        d_ref[...], nm_ref[...], nv_ref[...] = _adamw_math(w_ref[...], g_ref[...], m_ref[...], v_ref[...])

    spec = pl.BlockSpec((tr, cols), lambda i: (i, 0))
    return pl.pallas_call(
        body, name=name, grid=(rows // tr,), in_specs=[spec] * 4, out_specs=[spec] * 3,
        out_shape=[_sds((rows, cols), F32)] * 3, compiler_params=_params("parallel"),
    )(w, g, m, v)


def _adamw_sum(recv, w, m, v, name, after=()):
    n, rows, cols = recv.shape
    tr = _tile(rows, max(16, (2**17 // cols) // 16 * 16), 16)

    def body(r_ref, w_ref, m_ref, v_ref, g_ref, d_ref, nm_ref, nv_ref):
        g = r_ref[0].astype(F32)
        for i in range(1, n):
            g = g + r_ref[i].astype(F32)
        g_ref[...] = g
        d_ref[...], nm_ref[...], nv_ref[...] = _adamw_math(w_ref[...], g, m_ref[...], v_ref[...])

    spec = pl.BlockSpec((tr, cols), lambda i: (i, 0))
    return _call(body, name=name, grid=(rows // tr,), in_specs=[pl.BlockSpec((n, tr, cols), lambda i: (0, i, 0))] + [spec] * 3,
                 out_specs=[spec] * 4, out_shape=[_sds((rows, cols), F32)] * 4, args=(recv, w, m, v), sem=("parallel",),
                 after=after)


COLUMN_CUT = ("w_in", "w_attn_branch", "w_pool_branch", "w_up", "w_ple")
ROW_CUT = ("w_out", "w_down", "w_ple_gate")
REPLICATED = ("norm_mix_pre", "pool_scale", "norm_mix_post", "norm_ffn_pre", "conv_b", "norm_ffn_post", "norm_ple_post")
WEIGHTS = ("norm_mix_pre", "w_in", "w_attn_branch", "w_pool_group", "pool_scale", "w_pool_branch", "w_out", "norm_mix_post",
           "norm_ffn_pre", "w_up", "conv_w", "conv_b", "w_down", "norm_ffn_post", "w_ple", "w_ple_gate", "norm_ple_post")


def _size(shape):
    n = 1
    for s in shape:
        n *= s
    return n


def _pad_rows(flat, row_align):
    n = flat.shape[-1]
    per = PACK_W * row_align
    total = -(-n // per) * per
    return jnp.pad(flat, [(0, total - n)]).reshape(total // PACK_W, PACK_W)


def _natural(shard_major):
    n, r, c = shard_major.shape
    return shard_major.reshape(n * r, c)


def kernel(x, p, norm_mix_pre, w_in, w_attn_branch, w_pool_group, pool_scale, w_pool_branch, w_out, norm_mix_post, norm_ffn_pre, w_up, conv_w, conv_b, w_down, norm_ffn_post, w_ple, w_ple_gate, norm_ple_post, loss_target, m_norm_mix_pre, m_w_in, m_w_attn_branch, m_w_pool_group, m_pool_scale, m_w_pool_branch, m_w_out, m_norm_mix_post, m_norm_ffn_pre, m_w_up, m_conv_w, m_conv_b, m_w_down, m_norm_ffn_post, m_w_ple, m_w_ple_gate, m_norm_ple_post, v_norm_mix_pre, v_w_in, v_w_attn_branch, v_w_pool_group, v_pool_scale, v_w_pool_branch, v_w_out, v_norm_mix_post, v_norm_ffn_pre, v_w_up, v_conv_w, v_conv_b, v_w_down, v_norm_ffn_post, v_w_ple, v_w_ple_gate, v_norm_ple_post):
    given = dict(locals())
    wts = {n: given[n][0] for n in WEIGHTS}
    mom = {n: given["m_" + n][0] for n in WEIGHTS}
    var = {n: given["v_" + n][0] for n in WEIGHTS}
    xs = x[0]
    ps_in = p[0, 0]
    tgt = loss_target[0]
    S, D = xs.shape
    AW = wts["w_attn_branch"].shape[0]
    PW = wts["w_pool_branch"].shape[0]
    G = wts["w_pool_group"].shape[0]
    PGW = PW // G
    H = AW // HEAD_DIM
    F = wts["w_down"].shape[0] * N_DEV
    assert (3 * AW) % PGW == 0 and (3 * AW + PW) % D == 0 and PGW % 128 == 0 and F % 128 == 0
    tr = _tile(S, 256, 16)
    blk = _tile(S, 256, 128)
    chunk = _tile(S, 256, 8)
    me = 4 * lax.axis_index("x") + 2 * lax.axis_index("y") + lax.axis_index("c")

    cw_shape = wts["conv_w"].shape
    conv_b_row = wts["conv_b"].reshape(1, -1)
    g1, g2, g3, g4, g5 = (wts[n].reshape(1, D) for n in
                          ("norm_mix_pre", "norm_mix_post", "norm_ffn_pre", "norm_ffn_post", "norm_ple_post"))
    pscale = wts["pool_scale"].reshape(1, PW)
    big = 4096
    wb = {n: wts[n].astype(BF16) for n in COLUMN_CUT + ROW_CUT}
    wb["w_pool_group"] = wts["w_pool_group"].astype(BF16).reshape(G * PGW // N_DEV, PGW)

    h, w_in = _rms_fwd(xs, g1, tr, _all_gather_comm([wb["w_in"]]))
    proj, w_ab, w_pg, w_pb, w_out = _mm(
        h, w_in, b_sm=True, name="mm_in", tk=2048,
        comm=_all_gather_comm([wb["w_attn_branch"], wb["w_pool_group"], wb["w_pool_branch"], wb["w_out"]]))
    w_pg = jnp.moveaxis(w_pg.reshape(N_DEV, G, PGW // N_DEV, PGW), 0, 1).reshape(G, PGW, PGW)
    w_out = _natural(w_out)
    qkv = proj[:, :3 * AW].astype(BF16)
    attn, a_saved, b_saved, w_up, conv_w_all = _attn_fwd(qkv, H, blk, _all_gather_comm([wb["w_up"], wts["conv_w"]]))
    conv_w_full = jnp.moveaxis(conv_w_all, 0, 1).reshape(cw_shape[0], N_DEV * cw_shape[1])
    y_attn = _mm(attn, w_ab, b_sm=True, name="mm_attn_branch", tm=big, tn=256, tk=big)
    pooled = _pool_fwd(proj, 3 * AW // PGW, G, PGW, chunk)
    pg, ps = _pool_group_fwd(pooled, w_pg, pscale)
    y_pool = _mm(ps, w_pb, b_sm=True, name="mm_pool_branch", tm=big, tn=256, tk=big)
    gate_cb = (3 * AW + PW) // D
    mixed = _gate_mix(proj, y_attn, y_pool, gate_cb, tr)
    mo = _mm(mixed, w_out, name="mm_out", tk=2048)
    x1, h2 = _resid_rms2(xs, mo, g2, g3, tr)
    upre, w_down, w_ple, w_pleg = _mm(h2, w_up, b_sm=True, name="mm_up", tn=2048, tk=1024,
                                      comm=_all_gather_comm([wb["w_down"], wb["w_ple"], wb["w_ple_gate"]]))
    w_down, w_pleg = _natural(w_down), _natural(w_pleg)
    act = _conv_fwd(upre, conv_w_full, conv_b_row, chunk)
    yf = _mm(act, w_down, name="mm_down", tk=1408)
    x2, x2b = _resid_rms(x1, yf, g4, tr)
    e = _mm(ps_in, w_ple, b_sm=True, name="mm_ple", tm=big, tn=256, tk=big)
    gl = _mm(x2b, w_pleg, name="mm_ple_gate", tk=2048)
    loss_part, dx3, de, dgl, dg5 = _ple_loss(gl, e, x2, tgt, g5, tr)

    shards = lambda natural: natural.reshape((N_DEV, natural.shape[0] // N_DEV) + natural.shape[1:])
    recv = {}
    dw_ple = _mm(ps_in, de, ta=True, out_sm=True, out_dtype=BF16, name="mm_d_w_ple", tm=256, tn=256, tk=big)
    dw_pleg = shards(_mm(x2b, dgl, ta=True, out_dtype=BF16, name="mm_d_w_ple_gate"))
    dx2g = _mm(dgl, w_pleg, tb=True, name="mm_d_x2", tk=2048)
    dx2, dyf, dg4 = _rms_bwd_a(dx3, dx2g, yf, g4, tr)
    dw_down = shards(_mm(act, dyf, ta=True, out_dtype=BF16, name="mm_d_w_down"))
    dact, recv["w_ple"], recv["w_ple_gate"] = _mm(dyf, w_down, tb=True, name="mm_d_act", tn=1408, tk=1024,
                                                  comm=_reduce_scatter_comm([dw_ple, dw_pleg]))
    dup_g, dup_v, dcw_g, dcw_v, dcb_g, dcb_v, recv["w_down"] = _conv_bwd(upre, dact, conv_w_full, conv_b_row, chunk,
                                                                          _reduce_scatter_comm([dw_down]))
    dw_up = _mm(h2, dup_g, b2=dup_v, ta=True, out_sm=True, out_dtype=BF16, name="mm_d_w_up", tn=2048)
    up_send, up_recv, dw_up, up_land, up_token = _reduce_scatter_start(dw_up, "rs_w_up_start")
    dh2 = _mm(dup_g, w_up, a2=dup_v, tb=True, b_sm=True, name="mm_d_h2", tk=2048, after=(up_token,))
    dx1, dmo, dg3, dg2 = _rms_bwd_b(dx2, dh2, x1, g3, mo, g2, tr)
    dmixed = _mm(dmo, w_out, tb=True, name="mm_d_mixed", tk=2048)
    dw_out = shards(_mm(mixed, dmo, ta=True, out_dtype=BF16, name="mm_d_w_out"))
    dya, dyp, dga, dgp = _gate_bwd(dmixed, proj, y_attn, y_pool, gate_cb, tr)
    dps = _mm(dyp, w_pb, tb=True, b_sm=True, name="mm_d_ps", tm=2048, tk=256)
    dw_pb = _mm(ps, dyp, ta=True, out_sm=True, out_dtype=BF16, name="mm_d_w_pool_branch", tn=256, tk=big)
    dpg, dscale = _scale_bwd(dps, pg, pscale, tr)
    dpooled = _pool_group_bwd_x(dpg, w_pg)
    dw_pg = _pool_group_bwd_w(pooled, dpg, G)
    dw_pg = jnp.moveaxis(dw_pg.astype(BF16).reshape(G, N_DEV, PGW // N_DEV, PGW), 1, 0).reshape(N_DEV, G * PGW // N_DEV, PGW)
    du = _pool_bwd(dpooled, G, chunk)
    dattn = _mm(dya, w_ab, tb=True, b_sm=True, name="mm_d_attn", tm=2048, tk=256)
    dw_ab = _mm(attn, dya, ta=True, out_sm=True, out_dtype=BF16, name="mm_d_w_attn_branch", tn=256, tk=big)
    dq, dk, dv, recv["w_out"], recv["w_pool_branch"], recv["w_pool_group"], recv["w_attn_branch"] = _attn_bwd(
        qkv, a_saved, b_saved, dattn, H, blk, _reduce_scatter_comm([dw_out, dw_pb, dw_pg, dw_ab]))
    dw_up, up_land = _reduce_scatter_wait(up_send, up_recv, dw_up, up_land, (dq,), "rs_w_up_wait")
    recv["w_up"] = lax.dynamic_update_slice_in_dim(up_land, lax.dynamic_index_in_dim(dw_up, me, 0, keepdims=True), me, 0)
    dproj = jnp.concatenate([dq, dk, dv, du, dga, dgp], axis=1)
    dw_in = _mm(h, dproj, ta=True, out_sm=True, out_dtype=BF16, name="mm_d_w_in")
    in_send, in_recv, dw_in, in_land, token = _reduce_scatter_start(dw_in, "rs_w_in_start")
    dh = _mm(dproj, w_in, tb=True, b_sm=True, name="mm_d_h", tm=big, tn=512, tk=1024, after=(token,))
    grad_x, dg1 = _rms_bwd_c(dx1, dh, xs, g1, tr)

    gshard, delta, new_m, new_v = {}, {}, {}, {}

    def adamw_cut(n, after=()):
        shp = wts[n].shape
        two_d = (_size(shp[:-1]), shp[-1])
        g_, d_, m_, v_ = _adamw_sum(recv[n].reshape((N_DEV,) + two_d), wts[n].reshape(two_d), mom[n].reshape(two_d),
                                    var[n].reshape(two_d), "adamw_" + n, after)
        gshard[n], delta[n], new_m[n], new_v[n] = g_.reshape(shp), d_.reshape(shp), m_.reshape(shp), v_.reshape(shp)

    for n in COLUMN_CUT[1:] + ROW_CUT + ("w_pool_group",):
        adamw_cut(n, (token,))

    assert COLUMN_CUT[0] == "w_in"
    dw_in, in_land = _reduce_scatter_wait(in_send, in_recv, dw_in, in_land, (grad_x,) + tuple(delta[n] for n in delta),
                                          "rs_w_in_wait")

    dconv_w = jnp.concatenate([dcw_g, dcw_v], axis=1)
    dconv_b = jnp.concatenate([dcb_g, dcb_v], axis=1).reshape(-1)
    rep_parts = {"norm_mix_pre": dg1, "pool_scale": dscale, "norm_mix_post": dg2, "norm_ffn_pre": dg3, "conv_b": dconv_b,
                 "norm_ffn_post": dg4, "norm_ple_post": dg5}
    small = jnp.concatenate([rep_parts[n].reshape(-1) for n in REPLICATED] + [dconv_w.reshape(-1)])
    n_small = small.shape[0]
    small_sum = _all_reduce_small(_pad_rows(small, 8), in_land).reshape(-1)[:n_small]
    off = 0
    for n in REPLICATED:
        sz = _size(wts[n].shape)
        gshard[n] = small_sum[off:off + sz].reshape(wts[n].shape)
        off += sz
    dconv_w_sum = small_sum[off:off + 3 * 2 * F].reshape(3, 2 * F)
    gshard["conv_w"] = lax.dynamic_slice_in_dim(dconv_w_sum, me * cw_shape[1], cw_shape[1], axis=1)

    delta["conv_w"], new_m["conv_w"], new_v["conv_w"] = _adamw(wts["conv_w"], gshard["conv_w"], mom["conv_w"], var["conv_w"],
                                                               "adamw_conv_w")
    rep_sizes = [_size(wts[n].shape) for n in REPLICATED]
    n_rep = sum(rep_sizes)
    cat = lambda t: _pad_rows(jnp.concatenate([t[n].reshape(-1) for n in REPLICATED]), 8)
    d_, m_, v_ = _adamw(cat(wts), cat(gshard), cat(mom), cat(var), "adamw_replicated")
    off = 0
    for n, sz in zip(REPLICATED, rep_sizes):
        shp = wts[n].shape
        delta[n], new_m[n], new_v[n] = (t.reshape(-1)[off:off + sz].reshape(shp) for t in (d_, m_, v_))
        off += sz
    assert off == n_rep

    own = lax.dynamic_index_in_dim(dw_in, me, 0, keepdims=True)
    recv["w_in"] = lax.dynamic_update_slice_in_dim(in_land, own, me, 0)
    adamw_cut("w_in")

    loss = lax.psum(loss_part[0, 0], ("x", "y", "c"))
    lead = lambda t: t[None]
    return (loss, grad_x[None], *[lead(gshard[n]) for n in WEIGHTS], *[lead(delta[n]) for n in WEIGHTS],
            *[lead(new_m[n]) for n in WEIGHTS], *[lead(new_v[n]) for n in WEIGHTS])
```
